```python
import math
import jax, jax.numpy as jnp
from jax import lax
import numpy as np

D_MODEL = 2048
BATCH = 1
SEQ = 8192
DEPTH = 1

CHUNK = 64
MIX_WIDTH = D_MODEL
ATTN_WIDTH = MIX_WIDTH // 2
HEAD_DIM = 128
N_ATTN_HEADS = ATTN_WIDTH // HEAD_DIM
SSM_WIDTH = MIX_WIDTH - ATTN_WIDTH
SSM_GROUP = 16
N_SSM_GROUPS = SSM_WIDTH // SSM_GROUP
SSM_STATE = 64
IN_WIDTH = 3 * ATTN_WIDTH + N_ATTN_HEADS + SSM_WIDTH
D_FF = 256 * ((8 * D_MODEL // 3 + 255) // 256)
CONV_WIDTH = 3
Q_BLOCK = 128
N_MOD = 6
EPS = 1e-6
DT_MIN = 1e-3
DT_MAX = 1e-1

kernel_name = "hybrid_fox_s5_convffn_adaln"


def rms_norm(x, g):
    xf = x.astype(jnp.float32)
    y = xf * lax.rsqrt(jnp.mean(xf * xf, axis=-1, keepdims=True) + EPS)
    return (y * g.astype(jnp.float32)).astype(x.dtype)


def modulate(h, shift, scale):
    return h * (1 + scale[:, None, :]) + shift[:, None, :]


def forgetting_attention(q, k, v, f_logit):
    B, S, H, Dh = q.shape
    log_f = jax.nn.log_sigmoid(f_logit.astype(jnp.float32))
    cum = jnp.transpose(jnp.cumsum(log_f, axis=1), (0, 2, 1))
    scale = Dh ** -0.5
    outs = []
    for blk in range(S // Q_BLOCK):
        q0 = blk * Q_BLOCK
        q1 = q0 + Q_BLOCK
        qb = q[:, q0:q1]
        kb = k[:, :q1]
        vb = v[:, :q1]
        s = jnp.einsum('bqhd,bkhd->bhqk', qb, kb, preferred_element_type=jnp.float32) * scale
        s = s + (cum[:, :, q0:q1, None] - cum[:, :, None, :q1])
        t_idx = q0 + jnp.arange(Q_BLOCK)
        s_idx = jnp.arange(q1)
        causal = s_idx[None, :] <= t_idx[:, None]
        s = jnp.where(causal, s, -jnp.inf)
        p = jax.nn.softmax(s, axis=-1)
        outs.append(jnp.einsum('bhqk,bkhd->bqhd', p.astype(vb.dtype), vb))
    return jnp.concatenate(outs, axis=1)


def s5_ssm(u, a_re, a_im, log_dt, b_re, b_im, c_re, c_im, d_skip):
    B, S, _ = u.shape
    uf = u.astype(jnp.float32).reshape(B, S, N_SSM_GROUPS, SSM_GROUP)
    a_re = a_re.astype(jnp.float32)
    a_im = a_im.astype(jnp.float32)
    dt = jnp.exp(log_dt.astype(jnp.float32))[:, None]
    mag = jnp.exp(dt * a_re)
    ab_re = mag * jnp.cos(dt * a_im)
    ab_im = mag * jnp.sin(dt * a_im)
    n_re = ab_re - 1
    n_im = ab_im
    den = a_re * a_re + a_im * a_im
    z_re = (n_re * a_re + n_im * a_im) / den
    z_im = (n_im * a_re - n_re * a_im) / den
    bu_re = jnp.einsum('bsgh,gph->bsgp', uf, b_re.astype(jnp.float32))
    bu_im = jnp.einsum('bsgh,gph->bsgp', uf, b_im.astype(jnp.float32))
    e_re = z_re * bu_re - z_im * bu_im
    e_im = z_re * bu_im + z_im * bu_re
    a_re_t = jnp.broadcast_to(ab_re, e_re.shape)
    a_im_t = jnp.broadcast_to(ab_im, e_im.shape)

    def combine(left, right):
        a1r, a1i, b1r, b1i = left
        a2r, a2i, b2r, b2i = right
        return (a2r * a1r - a2i * a1i,
                a2r * a1i + a2i * a1r,
                a2r * b1r - a2i * b1i + b2r,
                a2r * b1i + a2i * b1r + b2i)

    _, _, x_re, x_im = lax.associative_scan(combine, (a_re_t, a_im_t, e_re, e_im), axis=1)
    y = (jnp.einsum('bsgp,ghp->bsgh', x_re, c_re.astype(jnp.float32))
         - jnp.einsum('bsgp,ghp->bsgh', x_im, c_im.astype(jnp.float32))
         + d_skip.astype(jnp.float32) * uf)
    return y.reshape(B, S, SSM_WIDTH).astype(u.dtype)


def causal_dwconv(a, w, bias):
    S = a.shape[1]
    ap = jnp.pad(a, ((0, 0), (CONV_WIDTH - 1, 0), (0, 0)))
    out = bias
    for i in range(CONV_WIDTH):
        out = out + w[i] * ap[:, i:i + S]
    return out


def setup_inputs(seed: int = 0) -> dict:
    key = jax.random.key(seed)
    ks = jax.random.split(key, 32)
    L, D = DEPTH, D_MODEL
    G, P, Hc = N_SSM_GROUPS, SSM_STATE, SSM_GROUP
    nrm = lambda k, shape, s: jax.random.normal(k, shape, jnp.float32) * s
    gain = lambda k, shape: 1.0 + 0.02 * jax.random.normal(k, shape, jnp.float32)
    n_idx = jnp.arange(P, dtype=jnp.float32)
    return {
        "x": nrm(ks[0], (BATCH, SEQ, D), 1.0),
        "c": nrm(ks[1], (BATCH, D), 1.0),
        "w_ada": nrm(ks[2], (L, D, N_MOD * D), 0.5 * D ** -0.5),
        "b_ada": nrm(ks[3], (L, N_MOD * D), 0.02),
        "g_mix": gain(ks[4], (L, D)),
        "w_in": nrm(ks[5], (L, D, IN_WIDTH), D ** -0.5),
        "b_f": 2.0 + 0.5 * jax.random.normal(ks[6], (L, N_ATTN_HEADS), jnp.float32),
        "a_re": -0.5 + nrm(ks[7], (L, G, P), 0.01),
        "a_im": math.pi * n_idx + nrm(ks[8], (L, G, P), 0.01),
        "log_dt": jax.random.uniform(ks[9], (L, G), jnp.float32, math.log(DT_MIN), math.log(DT_MAX)),
        "ssm_b_re": nrm(ks[10], (L, G, P, Hc), (2 * Hc) ** -0.5),
        "ssm_b_im": nrm(ks[11], (L, G, P, Hc), (2 * Hc) ** -0.5),
        "ssm_c_re": nrm(ks[12], (L, G, Hc, P), (2 * P) ** -0.5 * 4.0),
        "ssm_c_im": nrm(ks[13], (L, G, Hc, P), (2 * P) ** -0.5 * 4.0),
        "ssm_d": nrm(ks[14], (L, G, Hc), 0.5),
        "w_glu": nrm(ks[15], (L, SSM_WIDTH, SSM_WIDTH), SSM_WIDTH ** -0.5),
        "b_glu": nrm(ks[16], (L, SSM_WIDTH), 0.02),
        "g_attn_out": gain(ks[17], (L, ATTN_WIDTH)),
        "g_ssm_out": gain(ks[18], (L, SSM_WIDTH)),
        "w_out": nrm(ks[19], (L, MIX_WIDTH, D), MIX_WIDTH ** -0.5),
        "g_ffn": gain(ks[20], (L, D)),
        "w_up": nrm(ks[21], (L, D, 2 * D_FF), D ** -0.5),
        "conv_w": nrm(ks[22], (L, CONV_WIDTH, D_FF), CONV_WIDTH ** -0.5),
        "conv_b": nrm(ks[23], (L, D_FF), 0.02),
        "w_down": nrm(ks[24], (L, D_FF, D), D_FF ** -0.5),
        "g_final": gain(ks[25], (D,)),
    }


def reference(x, c, w_ada, b_ada, g_mix, w_in, b_f, a_re, a_im, log_dt, ssm_b_re, ssm_b_im,
              ssm_c_re, ssm_c_im, ssm_d, w_glu, b_glu, g_attn_out, g_ssm_out, w_out, g_ffn,
              w_up, conv_w, conv_b, w_down, g_final):
    B, S, D = x.shape
    h = x
    cond = jax.nn.silu(c)
    splits = [ATTN_WIDTH, 2 * ATTN_WIDTH, 3 * ATTN_WIDTH, 3 * ATTN_WIDTH + N_ATTN_HEADS]
    for l in range(DEPTH):
        mod = cond @ w_ada[l] + b_ada[l]
        sh1, sc1, gt1, sh2, sc2, gt2 = jnp.split(mod, N_MOD, axis=-1)

        hn = modulate(rms_norm(h, g_mix[l]), sh1, sc1)
        proj = hn @ w_in[l]
        q, k, v, f_logit, u = jnp.split(proj, splits, axis=-1)
        q = q.reshape(B, S, N_ATTN_HEADS, HEAD_DIM)
        k = k.reshape(B, S, N_ATTN_HEADS, HEAD_DIM)
        v = v.reshape(B, S, N_ATTN_HEADS, HEAD_DIM)
        attn = forgetting_attention(q, k, v, f_logit + b_f[l]).reshape(B, S, ATTN_WIDTH)

        y = s5_ssm(u, a_re[l], a_im[l], log_dt[l], ssm_b_re[l], ssm_b_im[l],
                   ssm_c_re[l], ssm_c_im[l], ssm_d[l])
        y = jax.nn.gelu(y)
        y = y * jax.nn.sigmoid(y @ w_glu[l] + b_glu[l])

        mixed = jnp.concatenate([rms_norm(attn, g_attn_out[l]), rms_norm(y, g_ssm_out[l])], axis=-1)
        h = h + gt1[:, None, :] * (mixed @ w_out[l])

        hn = modulate(rms_norm(h, g_ffn[l]), sh2, sc2)
        up = hn @ w_up[l]
        a_br, b_br = jnp.split(up, 2, axis=-1)
        a_br = causal_dwconv(a_br, conv_w[l], conv_b[l])
        h = h + gt2[:, None, :] * ((jax.nn.silu(a_br) * b_br) @ w_down[l])
    return rms_norm(h, g_final)
```

```python
import functools
import math

import jax
import jax.numpy as jnp
from jax import lax
from jax.experimental import pallas as pl
from jax.experimental.pallas import tpu as pltpu

F32 = jnp.float32
BF16 = jnp.bfloat16

EPS = 1e-6
HEAD_DIM = 128
N_HEADS = 8
SSM_GROUP = 16
SSM_STATE = 64
N_MOD = 6
LANES = 128
SSM_T = 16
SSM_TW = SSM_T * SSM_GROUP
SSM_GB = 8

_MIB = 1024 * 1024


def _cparams(semantics, vmem_mib):
    return pltpu.CompilerParams(dimension_semantics=semantics, vmem_limit_bytes=vmem_mib * _MIB)


def _dot(a, b):
    return jnp.dot(a, b, preferred_element_type=F32)


def _dot_nt(a, b):
    return lax.dot_general(a, b, (((1,), (1,)), ((), ())), preferred_element_type=F32)


def _rms(x, g):
    return x * lax.rsqrt(jnp.mean(x * x, axis=-1, keepdims=True) + EPS) * g


def _adaln_kernel(c_ref, w_ref, b_ref, o_ref):
    c = c_ref[...]
    cond = c * jax.nn.sigmoid(c)
    cond8 = jnp.broadcast_to(cond, (8, c.shape[1])).astype(BF16)
    acc = _dot(cond8, w_ref[...].astype(BF16))
    o_ref[...] = acc[0:1, :] + b_ref[...]


def _adaln(c, w, b, tn=1536):
    d, n = w.shape
    return pl.pallas_call(
        _adaln_kernel,
        grid=(n // tn,),
        in_specs=[pl.BlockSpec((1, d), lambda j: (0, 0)),
                  pl.BlockSpec((d, tn), lambda j: (0, j)),
                  pl.BlockSpec((1, tn), lambda j: (0, j))],
        out_specs=pl.BlockSpec((1, tn), lambda j: (0, j)),
        out_shape=jax.ShapeDtypeStruct((1, n), F32),
        compiler_params=_cparams(("arbitrary",), 48),
        name="adaln",
    )(c, w, b.reshape(1, n))


def _inproj_kernel(x_ref, g_ref, sc_ref, sh_ref, w_ref, wf_ref,
                   q_ref, k_ref, v_ref, u_ref, f_ref, hn_ref, *, q_scale):
    @pl.when(pl.program_id(1) == 0)
    def _():
        hn = _rms(x_ref[...], g_ref[...]) * (1.0 + sc_ref[...]) + sh_ref[...]
        hnb = hn.astype(BF16)
        hn_ref[...] = hnb
        f_ref[...] = _dot(hnb, wf_ref[...])

    r = _dot(hn_ref[...], w_ref[0])
    q_ref[...] = (r[:, 0:LANES] * q_scale).astype(BF16)
    k_ref[...] = r[:, LANES:2 * LANES].astype(BF16)
    v_ref[...] = r[:, 2 * LANES:3 * LANES].astype(BF16)
    u_ref[...] = r[:, 3 * LANES:4 * LANES].astype(BF16)


def _inproj(x, g, sc, sh, w_tiles, w_f, tm=512):
    s, d = x.shape
    nt = w_tiles.shape[0]
    row = lambda i, j: (0, 0)
    col = pl.BlockSpec((tm, LANES), lambda i, j: (i, j))
    wide = jax.ShapeDtypeStruct((s, nt * LANES), BF16)
    return pl.pallas_call(
        functools.partial(_inproj_kernel, q_scale=HEAD_DIM ** -0.5),
        grid=(s // tm, nt),
        in_specs=[pl.BlockSpec((tm, d), lambda i, j: (i, 0)),
                  pl.BlockSpec((1, d), row), pl.BlockSpec((1, d), row), pl.BlockSpec((1, d), row),
                  pl.BlockSpec((1, d, 4 * LANES), lambda i, j: (j, 0, 0)),
                  pl.BlockSpec((d, LANES), row)],
        out_specs=[col, col, col, col, pl.BlockSpec((tm, LANES), lambda i, j: (i, 0))],
        out_shape=[wide, wide, wide, wide, jax.ShapeDtypeStruct((s, LANES), F32)],
        scratch_shapes=[pltpu.VMEM((tm, d), BF16)],
        compiler_params=_cparams(("arbitrary", "arbitrary"), 48),
        name="inproj",
    )(x, g, sc, sh, w_tiles, w_f)


def _cum_kernel(f_ref, b_ref, o_ref):
    z = f_ref[...] + b_ref[...]
    x = jnp.minimum(z, 0.0) - jnp.log1p(jnp.exp(-jnp.abs(z)))
    n = x.shape[1]
    lane = lax.broadcasted_iota(jnp.int32, x.shape, 1)
    shift = 1
    while shift < n:
        x = x + jnp.where(lane >= shift, pltpu.roll(x, shift, axis=1), 0.0)
        shift *= 2
    o_ref[...] = x


def _forget_cumsum(f_t, b_f):
    h, s = f_t.shape
    return pl.pallas_call(
        _cum_kernel,
        out_shape=jax.ShapeDtypeStruct((h, s), F32),
        name="forget_cumsum",
    )(f_t, b_f.reshape(h, 1))


def _attn_kernel(q_ref, k_ref, v_ref, cq_ref, ck_ref, o_ref, m_ref, l_ref, acc_ref, *, tq):
    i = pl.program_id(1)
    q = q_ref[...]
    cq = cq_ref[0]
    m_ref[...] = jnp.full(m_ref.shape, -jnp.inf, F32)
    l_ref[...] = jnp.zeros(l_ref.shape, F32)
    acc_ref[...] = jnp.zeros(acc_ref.shape, F32)

    def step(j, masked):
        ks = pl.multiple_of(j * tq, tq)
        kb = k_ref[pl.ds(ks, tq), :]
        vb = v_ref[pl.ds(ks, tq), :]
        ck = ck_ref[0, :, pl.ds(ks, tq)]
        s = _dot_nt(q, kb) + (cq - ck)
        if masked:
            row = lax.broadcasted_iota(jnp.int32, s.shape, 0)
            colm = lax.broadcasted_iota(jnp.int32, s.shape, 1)
            s = jnp.where(colm <= row, s, -jnp.inf)
        m_prev = m_ref[...]
        m_new = jnp.maximum(m_prev, jnp.max(s, axis=1, keepdims=True))
        alpha = jnp.exp(m_prev - m_new)
        p = jnp.exp(s - m_new)
        l_ref[...] = alpha * l_ref[...] + jnp.sum(p, axis=1, keepdims=True)
        acc_ref[...] = alpha * acc_ref[...] + _dot(p.astype(BF16), vb)
        m_ref[...] = m_new

    def body(j, carry):
        step(j, False)
        return carry

    lax.fori_loop(0, i, body, 0)
    step(i, True)
    o_ref[...] = (acc_ref[...] / l_ref[...]).astype(BF16)


def _attention(q, k, v, cum_col, cum_row, tq=256):
    s = q.shape[0]
    return pl.pallas_call(
        functools.partial(_attn_kernel, tq=tq),
        grid=(N_HEADS, s // tq),
        in_specs=[pl.BlockSpec((tq, HEAD_DIM), lambda h, i: (i, h)),
                  pl.BlockSpec((s, HEAD_DIM), lambda h, i: (0, h)),
                  pl.BlockSpec((s, HEAD_DIM), lambda h, i: (0, h)),
                  pl.BlockSpec((1, tq, 1), lambda h, i: (h, i, 0)),
                  pl.BlockSpec((1, 1, s), lambda h, i: (h, 0, 0))],
        out_specs=pl.BlockSpec((tq, HEAD_DIM), lambda h, i: (i, h)),
        out_shape=jax.ShapeDtypeStruct((s, N_HEADS * HEAD_DIM), BF16),
        scratch_shapes=[pltpu.VMEM((tq, 1), F32), pltpu.VMEM((tq, 1), F32),
                        pltpu.VMEM((tq, HEAD_DIM), F32)],
        compiler_params=_cparams(("arbitrary", "arbitrary"), 32),
        name="fox_attention",
    )(q, k, v, cum_col, cum_row)


def _cmul(are, aim, bre, bim):
    return are * bre - aim * bim, are * bim + aim * bre


def _ssm_prep_kernel(ldt_ref, are_ref, aim_ref, btr_ref, bti_ref, cr_ref, ci_ref, dm_ref, u_ref,
                     toep_ref, pmt_ref, a1_ref, a2_ref, v_ref, vs_ref,
                     cp_ref, q_ref, qs_ref):
    t_len = SSM_T
    first = lax.broadcasted_iota(jnp.int32, (SSM_GROUP, LANES), 1) < SSM_STATE
    first1 = first[0:1, :]
    dt = jnp.exp(ldt_ref[0])
    are, aim = are_ref[0], aim_ref[0]
    mag = jnp.exp(dt * are)
    abre, abim = mag * jnp.cos(dt * aim), mag * jnp.sin(dt * aim)
    nre, nim = abre - 1.0, abim
    den = are * are + aim * aim
    zre = (nre * are + nim * aim) / den
    zim = (nim * are - nre * aim) / den
    bbre, bbim = _cmul(zre, zim, btr_ref[0], bti_ref[0])
    bbcat = jnp.where(first, bbre, bbim)

    cpre, cpim = cr_ref[0], ci_ref[0]
    qre, qim = bbre, bbim
    pwre, pwim = jnp.ones_like(abre), jnp.zeros_like(abim)
    for t in range(t_len):
        rows = slice(t * SSM_GROUP, (t + 1) * SSM_GROUP)
        cp_ref[rows, :] = jnp.where(first, cpre, -cpim)
        if t > 0:
            prev = slice((t - 1) * SSM_GROUP, t * SSM_GROUP)
            pmt_ref[0, prev, :] = jnp.where(first, cpre, -cpim).astype(BF16)
        srows = slice((t_len - 1 - t) * SSM_GROUP, (t_len - t) * SSM_GROUP)
        q_ref[srows, :] = jnp.where(first, qre, qim).astype(BF16)
        qs_ref[srows, :] = jnp.where(first, qim, qre).astype(BF16)
        cpre, cpim = _cmul(cpre, cpim, abre, abim)
        qre, qim = _cmul(qre, qim, abre, abim)
        pwre, pwim = _cmul(pwre, pwim, abre, abim)
    last = slice((t_len - 1) * SSM_GROUP, t_len * SSM_GROUP)
    pmt_ref[0, last, :] = jnp.where(first, cpre, -cpim).astype(BF16)
    a1_ref[0] = pwre
    a2_ref[0] = jnp.where(first1, -pwim, pwim)

    krow = lax.dot_general(bbcat, cp_ref[...], (((1,), (1,)), ((), ())),
                           preferred_element_type=F32, precision=lax.Precision.HIGHEST)
    krow = krow + dm_ref[0]
    lane = lax.broadcasted_iota(jnp.int32, krow.shape, 1)
    for s in range(t_len):
        rows = slice(s * SSM_GROUP, (s + 1) * SSM_GROUP)
        shifted = krow if s == 0 else pltpu.roll(krow, s * SSM_GROUP, axis=1)
        toep_ref[0, rows, :] = jnp.where(lane >= s * SSM_GROUP, shifted, 0.0).astype(BF16)

    u = u_ref[0]
    v_ref[0] = _dot(u, q_ref[...])
    vs_ref[0] = _dot(u, qs_ref[...])


def _ssm_prep(ldt, are2, aim2, bt_re2, bt_im2, c_re2, c_im2, dmat, u_chunks):
    g, c, tw = u_chunks.shape
    per_g = lambda *shape: pl.BlockSpec((1,) + shape, lambda i: (i,) + (0,) * len(shape))
    return pl.pallas_call(
        _ssm_prep_kernel,
        grid=(g,),
        in_specs=[per_g(1, LANES), per_g(1, LANES), per_g(1, LANES),
                  per_g(SSM_GROUP, LANES), per_g(SSM_GROUP, LANES),
                  per_g(SSM_GROUP, LANES), per_g(SSM_GROUP, LANES),
                  per_g(SSM_GROUP, tw), per_g(c, tw)],
        out_specs=[per_g(tw, tw), per_g(tw, LANES), per_g(1, LANES), per_g(1, LANES),
                   per_g(c, LANES), per_g(c, LANES)],
        out_shape=[jax.ShapeDtypeStruct((g, tw, tw), BF16),
                   jax.ShapeDtypeStruct((g, tw, LANES), BF16),
                   jax.ShapeDtypeStruct((g, 1, LANES), F32),
                   jax.ShapeDtypeStruct((g, 1, LANES), F32),
                   jax.ShapeDtypeStruct((g, c, LANES), F32),
                   jax.ShapeDtypeStruct((g, c, LANES), F32)],
        scratch_shapes=[pltpu.VMEM((tw, LANES), F32), pltpu.VMEM((tw, LANES), BF16),
                        pltpu.VMEM((tw, LANES), BF16)],
        compiler_params=_cparams(("arbitrary",), 32),
        name="ssm_prep",
    )(ldt, are2, aim2, bt_re2, bt_im2, c_re2, c_im2, dmat, u_chunks)


def _ssm_scan_kernel(v_ref, vs_ref, a1_ref, a2_ref, x0_ref):
    n_chunks = v_ref.shape[0]
    a1, a2 = a1_ref[...], a2_ref[...]

    def body(c, carry):
        x, xs = carry
        x0_ref[c] = x
        xn = a1 * x + a2 * xs + v_ref[c]
        xsn = a1 * xs - a2 * x + vs_ref[c]
        return xn, xsn

    zero = jnp.zeros(a1.shape, F32)
    lax.fori_loop(0, n_chunks, body, (zero, zero))


def _ssm_scan(v_cg, vs_cg, a1, a2):
    return pl.pallas_call(
        _ssm_scan_kernel,
        out_shape=jax.ShapeDtypeStruct(v_cg.shape, F32),
        compiler_params=_cparams(None, 48),
        name="ssm_scan",
    )(v_cg, vs_cg, a1, a2)


def _ssm_out_kernel(u_ref, toep_ref, pmt_ref, x0_ref, y_ref):
    for gi in range(u_ref.shape[0]):
        y = _dot(u_ref[gi], toep_ref[gi]) + _dot_nt(x0_ref[gi].astype(BF16), pmt_ref[gi])
        y_ref[gi] = y.astype(BF16)


def _ssm_out(u_chunks, toep, pmt, x0):
    g, c, tw = u_chunks.shape
    blk = lambda *shape: pl.BlockSpec((SSM_GB,) + shape, lambda i: (i,) + (0,) * len(shape))
    return pl.pallas_call(
        _ssm_out_kernel,
        grid=(g // SSM_GB,),
        in_specs=[blk(c, tw), blk(tw, tw), blk(tw, LANES), blk(c, LANES)],
        out_specs=blk(c, tw),
        out_shape=jax.ShapeDtypeStruct((g, c, tw), BF16),
        compiler_params=_cparams(("arbitrary",), 48),
        name="ssm_out",
    )(u_chunks, toep, pmt, x0)


def _s5(u, a_re, a_im, log_dt, b_re, b_im, c_re, c_im, d_skip):
    s = u.shape[0]
    g, p = a_re.shape
    c = s // SSM_T
    dup = lambda a: jnp.concatenate([a, a], axis=-1)
    ldt = jnp.broadcast_to(log_dt.reshape(g, 1, 1), (g, 1, LANES))
    are2, aim2 = dup(a_re).reshape(g, 1, LANES), dup(a_im).reshape(g, 1, LANES)
    bt_re2 = dup(jnp.swapaxes(b_re, 1, 2))
    bt_im2 = dup(jnp.swapaxes(b_im, 1, 2))
    c_re2, c_im2 = dup(c_re), dup(c_im)
    dmat = jnp.pad(d_skip[:, None, :] * jnp.eye(SSM_GROUP, dtype=F32)[None],
                   ((0, 0), (0, 0), (0, SSM_TW - SSM_GROUP)))
    u_chunks = u.reshape(c, SSM_T, g, SSM_GROUP).transpose(2, 0, 1, 3).reshape(g, c, SSM_TW)
    toep, pmt, a1, a2, v, vs = _ssm_prep(ldt, are2, aim2, bt_re2, bt_im2, c_re2, c_im2,
                                         dmat, u_chunks)
    x0 = _ssm_scan(v.transpose(1, 0, 2), vs.transpose(1, 0, 2),
                   a1.reshape(g, LANES), a2.reshape(g, LANES))
    y_chunks = _ssm_out(u_chunks, toep, pmt, x0.transpose(1, 0, 2))
    return y_chunks.reshape(g, c, SSM_T, SSM_GROUP).transpose(1, 2, 0, 3).reshape(s, g * SSM_GROUP)


def _gelu_tanh(x):
    return 0.5 * x * (1.0 + jnp.tanh(math.sqrt(2.0 / math.pi) * (x + 0.044715 * (x * x * x))))


def _mixout_kernel(x_ref, attn_ref, y_ref, wglu_ref, bglu_ref, ga_ref, gs_ref, woa_ref, wos_ref,
                   gt_ref, gf_ref, sc_ref, sh_ref, h_ref, hn_ref):
    y = _gelu_tanh(y_ref[...].astype(F32))
    gate = jax.nn.sigmoid(_dot(y.astype(BF16), wglu_ref[...]) + bglu_ref[...])
    ns = _rms(y * gate, gs_ref[...]).astype(BF16)
    na = _rms(attn_ref[...].astype(F32), ga_ref[...]).astype(BF16)
    mixed = _dot(na, woa_ref[...]) + _dot(ns, wos_ref[...])
    h = x_ref[...] + gt_ref[...] * mixed
    h_ref[...] = h
    hn_ref[...] = (_rms(h, gf_ref[...]) * (1.0 + sc_ref[...]) + sh_ref[...]).astype(BF16)


def _mixout(x, attn, y, w_glu, b_glu, g_attn, g_ssm, w_out_a, w_out_s, gt1, g_ffn, sc2, sh2,
            tm=256):
    s, d = x.shape
    w = attn.shape[1]
    const = lambda r, c: pl.BlockSpec((r, c), lambda i: (0, 0))
    rows = lambda c: pl.BlockSpec((tm, c), lambda i: (i, 0))
    return pl.pallas_call(
        _mixout_kernel,
        grid=(s // tm,),
        in_specs=[rows(d), rows(w), rows(w), const(w, w), const(1, w), const(1, w), const(1, w),
                  const(w, d), const(w, d), const(1, d), const(1, d), const(1, d), const(1, d)],
        out_specs=[rows(d), rows(d)],
        out_shape=[jax.ShapeDtypeStruct((s, d), F32), jax.ShapeDtypeStruct((s, d), BF16)],
        compiler_params=_cparams(("arbitrary",), 48),
        name="mixer_out",
    )(x, attn, y, w_glu, b_glu, g_attn, g_ssm, w_out_a, w_out_s, gt1, g_ffn, sc2, sh2)


def _ffn_kernel(hn_ref, halo_ref, h_ref, wa_ref, wb_ref, cw_ref, cb_ref, wd_ref, gt_ref, gfin_ref,
                o_ref, acc_ref):
    i, j = pl.program_id(0), pl.program_id(1)

    @pl.when(j == 0)
    def _():
        acc_ref[...] = jnp.zeros(acc_ref.shape, F32)

    hn = hn_ref[...]
    a = _dot(hn, wa_ref[...])
    b = _dot(hn, wb_ref[...])
    halo = _dot(halo_ref[...], wa_ref[...]) * (i > 0).astype(F32)
    row = lax.broadcasted_iota(jnp.int32, a.shape, 0)
    prev1 = jnp.where(row == 0, halo[7:8, :], pltpu.roll(a, 1, axis=0))
    prev2 = jnp.where(row == 0, halo[6:7, :],
                      jnp.where(row == 1, halo[7:8, :], pltpu.roll(a, 2, axis=0)))
    cw = cw_ref[...]
    conv = cb_ref[...] + cw[0:1, :] * prev2 + cw[1:2, :] * prev1 + cw[2:3, :] * a
    act = (conv * jax.nn.sigmoid(conv) * b).astype(BF16)
    acc_ref[...] += _dot(act, wd_ref[...])

    @pl.when(j == pl.num_programs(1) - 1)
    def _():
        h = h_ref[...] + gt_ref[...] * acc_ref[...]
        o_ref[...] = _rms(h, gfin_ref[...])


def _ffn(hn, h, w_a, w_b, conv_w, conv_b, w_down, gt2, g_final, tm=512, tn=512):
    s, d = h.shape
    f = w_a.shape[1]
    halo_blocks = tm // 8
    return pl.pallas_call(
        _ffn_kernel,
        grid=(s // tm, f // tn),
        in_specs=[pl.BlockSpec((tm, d), lambda i, j: (i, 0)),
                  pl.BlockSpec((8, d), lambda i, j: (jnp.maximum(i * halo_blocks - 1, 0), 0)),
                  pl.BlockSpec((tm, d), lambda i, j: (i, 0)),
                  pl.BlockSpec((d, tn), lambda i, j: (0, j)),
                  pl.BlockSpec((d, tn), lambda i, j: (0, j)),
                  pl.BlockSpec((3, tn), lambda i, j: (0, j)),
                  pl.BlockSpec((1, tn), lambda i, j: (0, j)),
                  pl.BlockSpec((tn, d), lambda i, j: (j, 0)),
                  pl.BlockSpec((1, d), lambda i, j: (0, 0)),
                  pl.BlockSpec((1, d), lambda i, j: (0, 0))],
        out_specs=pl.BlockSpec((tm, d), lambda i, j: (i, 0)),
        out_shape=jax.ShapeDtypeStruct((s, d), F32),
        scratch_shapes=[pltpu.VMEM((tm, d), F32)],
        compiler_params=_cparams(("arbitrary", "arbitrary"), 56),
        name="conv_ffn",
    )(hn, hn, h, w_a, w_b, conv_w, conv_b, w_down, gt2, g_final)


def _layer(h, mod, g_mix, w_in, b_f, a_re, a_im, log_dt, ssm_b_re, ssm_b_im, ssm_c_re, ssm_c_im,
           ssm_d, w_glu, b_glu, g_attn_out, g_ssm_out, w_out, g_ffn, w_up, conv_w, conv_b, w_down):
    s, d = h.shape
    aw = N_HEADS * HEAD_DIM
    sw = a_re.shape[0] * SSM_GROUP
    d_ff = w_down.shape[0]
    sh1, sc1, gt1, sh2, sc2, gt2 = [mod[:, i * d:(i + 1) * d] for i in range(N_MOD)]

    wq, wk, wv = (w_in[:, i * aw:(i + 1) * aw].reshape(d, N_HEADS, HEAD_DIM) for i in range(3))
    wu = w_in[:, 3 * aw + N_HEADS:].reshape(d, sw // LANES, LANES)
    w_tiles = jnp.concatenate([wq, wk, wv, wu], axis=2).transpose(1, 0, 2).astype(BF16)
    w_f = jnp.pad(w_in[:, 3 * aw:3 * aw + N_HEADS], ((0, 0), (0, LANES - N_HEADS))).astype(BF16)

    row = lambda a: a.reshape(1, -1)
    q, k, v, u, f = _inproj(h, row(g_mix), sc1, sh1, w_tiles, w_f)

    cum = _forget_cumsum(f[:, :N_HEADS].T, b_f)
    attn = _attention(q, k, v, cum.reshape(N_HEADS, s, 1), cum.reshape(N_HEADS, 1, s))

    y = _s5(u, a_re, a_im, log_dt, ssm_b_re, ssm_b_im, ssm_c_re, ssm_c_im, ssm_d)

    h1, hn2 = _mixout(h, attn, y, w_glu.astype(BF16), row(b_glu), row(g_attn_out),
                      row(g_ssm_out), w_out[:aw].astype(BF16), w_out[aw:].astype(BF16),
                      gt1, row(g_ffn), sc2, sh2)
    return hn2, h1, (w_up[:, :d_ff].astype(BF16), w_up[:, d_ff:].astype(BF16), conv_w,
                     row(conv_b), w_down.astype(BF16), gt2)


def kernel(x, c, w_ada, b_ada, g_mix, w_in, b_f, a_re, a_im, log_dt, ssm_b_re, ssm_b_im, ssm_c_re,
           ssm_c_im, ssm_d, w_glu, b_glu, g_attn_out, g_ssm_out, w_out, g_ffn, w_up, conv_w,
           conv_b, w_down, g_final):
    batch, s, d = x.shape
    assert w_ada.shape[0] == 1, "only DEPTH == 1 is supported"
    l = 0
    outs = []
    for bi in range(batch):
        mod = _adaln(c[bi:bi + 1], w_ada[l], b_ada[l])
        hn2, h1, ffn_args = _layer(
            x[bi], mod, g_mix[l], w_in[l], b_f[l], a_re[l], a_im[l], log_dt[l], ssm_b_re[l],
            ssm_b_im[l], ssm_c_re[l], ssm_c_im[l], ssm_d[l], w_glu[l], b_glu[l],
            g_attn_out[l], g_ssm_out[l], w_out[l], g_ffn[l], w_up[l], conv_w[l], conv_b[l],
            w_down[l])
        outs.append(_ffn(hn2, h1, *ffn_args, g_final.reshape(1, d)))
    return jnp.stack(outs, axis=0)
```

```python
import functools
import math

import jax
import jax.numpy as jnp
from jax import lax
from jax.experimental import pallas as pl
from jax.experimental.pallas import tpu as pltpu

F32 = jnp.float32
BF16 = jnp.bfloat16

EPS = 1e-6
HEAD_DIM = 128
N_HEADS = 8
SSM_GROUP = 16
SSM_STATE = 64
N_MOD = 6
LANES = 128
SSM_T = 16
SSM_TW = SSM_T * SSM_GROUP
SSM_GB = 8
LOG2E = 1.4426950408889634
SKIP_LOG2 = 150.0
NORM_SLACK = 1.01

_MIB = 1024 * 1024


def _cparams(semantics, vmem_mib):
    return pltpu.CompilerParams(dimension_semantics=semantics, vmem_limit_bytes=vmem_mib * _MIB)


def _dot(a, b):
    return jnp.dot(a, b, preferred_element_type=F32)


def _dot_nt(a, b):
    return lax.dot_general(a, b, (((1,), (1,)), ((), ())), preferred_element_type=F32)


def _rms(x, g):
    return x * lax.rsqrt(jnp.mean(x * x, axis=-1, keepdims=True) + EPS) * g


def _adaln_kernel(c_ref, w_ref, b_ref, o_ref):
    c = c_ref[...]
    cond = c * jax.nn.sigmoid(c)
    cond8 = jnp.broadcast_to(cond, (8, c.shape[1])).astype(BF16)
    acc = _dot(cond8, w_ref[...].astype(BF16))
    o_ref[...] = acc[0:1, :] + b_ref[...]


def _adaln(c, w, b, tn=1536):
    d, n = w.shape
    return pl.pallas_call(
        _adaln_kernel,
        grid=(n // tn,),
        in_specs=[pl.BlockSpec((1, d), lambda j: (0, 0)),
                  pl.BlockSpec((d, tn), lambda j: (0, j)),
                  pl.BlockSpec((1, tn), lambda j: (0, j))],
        out_specs=pl.BlockSpec((1, tn), lambda j: (0, j)),
        out_shape=jax.ShapeDtypeStruct((1, n), F32),
        compiler_params=_cparams(("arbitrary",), 48),
        name="adaln",
    )(c, w, b.reshape(1, n))


def _inproj_kernel(x_ref, g_ref, sc_ref, sh_ref, w_ref, wf_ref,
                   q_ref, k_ref, v_ref, u_ref, f_ref, hn_ref, *, q_scale):
    @pl.when(pl.program_id(1) == 0)
    def _():
        hn = _rms(x_ref[...], g_ref[...]) * (1.0 + sc_ref[...]) + sh_ref[...]
        hnb = hn.astype(BF16)
        hn_ref[...] = hnb
        f_ref[...] = _dot(hnb, wf_ref[...])

    r = _dot(hn_ref[...], w_ref[0])
    q_ref[...] = (r[:, 0:LANES] * q_scale).astype(BF16)
    k_ref[...] = r[:, LANES:2 * LANES].astype(BF16)
    v_ref[:, 0:LANES] = r[:, 2 * LANES:3 * LANES].astype(BF16)
    v_ref[:, LANES:2 * LANES] = jnp.ones((r.shape[0], LANES), BF16)
    u_ref[...] = r[:, 3 * LANES:4 * LANES].astype(BF16)


def _inproj(x, g, sc, sh, w_tiles, w_f, tm=512):
    s, d = x.shape
    nt = w_tiles.shape[0]
    row = lambda i, j: (0, 0)
    col = pl.BlockSpec((tm, LANES), lambda i, j: (i, j))
    wide = jax.ShapeDtypeStruct((s, nt * LANES), BF16)
    return pl.pallas_call(
        functools.partial(_inproj_kernel, q_scale=HEAD_DIM ** -0.5 * LOG2E),
        grid=(s // tm, nt),
        in_specs=[pl.BlockSpec((tm, d), lambda i, j: (i, 0)),
                  pl.BlockSpec((1, d), row), pl.BlockSpec((1, d), row), pl.BlockSpec((1, d), row),
                  pl.BlockSpec((1, d, 4 * LANES), lambda i, j: (j, 0, 0)),
                  pl.BlockSpec((d, LANES), row)],
        out_specs=[col, col, pl.BlockSpec((tm, 2 * LANES), lambda i, j: (i, j)), col,
                   pl.BlockSpec((tm, LANES), lambda i, j: (i, 0))],
        out_shape=[wide, wide, jax.ShapeDtypeStruct((s, nt * 2 * LANES), BF16), wide,
                   jax.ShapeDtypeStruct((s, LANES), F32)],
        scratch_shapes=[pltpu.VMEM((tm, d), BF16)],
        compiler_params=_cparams(("arbitrary", "arbitrary"), 48),
        name="inproj",
    )(x, g, sc, sh, w_tiles, w_f)


def _cum_kernel(f_ref, b_ref, o_ref):
    z = f_ref[...] + b_ref[...]
    x = jnp.minimum(z, 0.0) - jnp.log1p(jnp.exp(-jnp.abs(z)))
    n = x.shape[1]
    lane = lax.broadcasted_iota(jnp.int32, x.shape, 1)
    shift = 1
    while shift < n:
        x = x + jnp.where(lane >= shift, pltpu.roll(x, shift, axis=1), 0.0)
        shift *= 2
    o_ref[...] = x * LOG2E


def _forget_cumsum(f_t, b_f):
    h, s = f_t.shape
    return pl.pallas_call(
        _cum_kernel,
        out_shape=jax.ShapeDtypeStruct((h, s), F32),
        name="forget_cumsum",
    )(f_t, b_f.reshape(h, 1))


def _attn_kernel(q_ref, k_ref, v_ref, cq_ref, ck_ref, o_ref, m_ref, acc_ref, kmax_ref, *,
                 tq, norm_chunk):
    i = pl.program_id(1)
    s_len = k_ref.shape[0]
    reps = tq // LANES

    @pl.when(i == 0)
    def _():
        ones = jnp.ones((HEAD_DIM, LANES), BF16)

        def norm_body(r, mx):
            kb = k_ref[pl.ds(pl.multiple_of(r * norm_chunk, norm_chunk), norm_chunk), :]
            kb = kb.astype(F32)
            row_sq = _dot((kb * kb).astype(BF16), ones)
            return jnp.maximum(mx, jnp.max(row_sq, axis=0, keepdims=True))

        mx = lax.fori_loop(0, s_len // norm_chunk, norm_body, jnp.zeros((1, LANES), F32))
        kmax_ref[...] = jnp.sqrt(mx) * NORM_SLACK

    q = q_ref[...]
    cqb = jnp.broadcast_to(cq_ref[0], (tq, LANES))

    def scores(j):
        ks = pl.multiple_of(j * tq, tq)
        t = _dot_nt(q, k_ref[pl.ds(ks, tq), :]) - ck_ref[0, :, pl.ds(ks, tq)]
        return t, v_ref[pl.ds(ks, tq), :]

    t, vb = scores(i)
    row = lax.broadcasted_iota(jnp.int32, t.shape, 0)
    col = lax.broadcasted_iota(jnp.int32, t.shape, 1)
    t = jnp.where(col <= row, t, -jnp.inf)
    m0 = jnp.max(t, axis=1, keepdims=True) + cqb
    p = jnp.exp2(t - pltpu.repeat(m0 - cqb, reps, axis=1))
    m_ref[...] = m0
    acc_ref[...] = _dot(p.astype(BF16), vb)

    qf = q.astype(F32)
    qn = jnp.sqrt(jnp.sum(qf * qf, axis=1, keepdims=True))
    bound = jnp.max(qn * kmax_ref[...] - (m0 - cqb))
    ck_all = ck_ref[0]
    pos = lax.broadcasted_iota(jnp.int32, ck_all.shape, 1)
    dead = jnp.where((pos < i * tq) & (ck_all > bound + SKIP_LOG2), 1.0, 0.0)
    j0 = jnp.sum(dead).astype(jnp.int32) // tq

    def body(j, carry):
        t, vb = scores(j)
        m_prev = m_ref[...]
        m_new = jnp.maximum(m_prev, jnp.max(t, axis=1, keepdims=True) + cqb)
        alpha = jnp.exp2(m_prev - m_new)
        p = jnp.exp2(t - pltpu.repeat(m_new - cqb, reps, axis=1))
        acc_ref[...] = pltpu.repeat(alpha, 2, axis=1) * acc_ref[...] + _dot(p.astype(BF16), vb)
        m_ref[...] = m_new
        return carry

    lax.fori_loop(j0, i, body, 0)
    acc = acc_ref[...]
    o_ref[...] = (acc[:, :HEAD_DIM] / acc[:, HEAD_DIM:]).astype(BF16)


def _attention(q, k, v_ones, cum_col, cum_row, tq=512):
    s = q.shape[0]
    return pl.pallas_call(
        functools.partial(_attn_kernel, tq=tq, norm_chunk=min(s, 1024)),
        grid=(N_HEADS, s // tq),
        in_specs=[pl.BlockSpec((tq, HEAD_DIM), lambda h, i: (i, h)),
                  pl.BlockSpec((s, HEAD_DIM), lambda h, i: (0, h)),
                  pl.BlockSpec((s, 2 * HEAD_DIM), lambda h, i: (0, h)),
                  pl.BlockSpec((1, tq, 1), lambda h, i: (h, i, 0)),
                  pl.BlockSpec((1, 1, s), lambda h, i: (h, 0, 0))],
        out_specs=pl.BlockSpec((tq, HEAD_DIM), lambda h, i: (i, h)),
        out_shape=jax.ShapeDtypeStruct((s, N_HEADS * HEAD_DIM), BF16),
        scratch_shapes=[pltpu.VMEM((tq, LANES), F32), pltpu.VMEM((tq, 2 * HEAD_DIM), F32),
                        pltpu.VMEM((1, LANES), F32)],
        compiler_params=_cparams(("arbitrary", "arbitrary"), 40),
        name="fox_attention",
    )(q, k, v_ones, cum_col, cum_row)


def _cmul(are, aim, bre, bim):
    return are * bre - aim * bim, are * bim + aim * bre


def _ssm_prep_kernel(ldt_ref, are_ref, aim_ref, btr_ref, bti_ref, cr_ref, ci_ref, dm_ref, u_ref,
                     toep_ref, pmt_ref, a1_ref, a2_ref, v_ref, vs_ref,
                     cp_ref, q_ref, qs_ref):
    t_len = SSM_T
    first = lax.broadcasted_iota(jnp.int32, (SSM_GROUP, LANES), 1) < SSM_STATE
    first1 = first[0:1, :]
    dt = jnp.exp(ldt_ref[0])
    are, aim = are_ref[0], aim_ref[0]
    mag = jnp.exp(dt * are)
    abre, abim = mag * jnp.cos(dt * aim), mag * jnp.sin(dt * aim)
    nre, nim = abre - 1.0, abim
    den = are * are + aim * aim
    zre = (nre * are + nim * aim) / den
    zim = (nim * are - nre * aim) / den
    bbre, bbim = _cmul(zre, zim, btr_ref[0], bti_ref[0])
    bbcat = jnp.where(first, bbre, bbim)

    cpre, cpim = cr_ref[0], ci_ref[0]
    qre, qim = bbre, bbim
    pwre, pwim = jnp.ones_like(abre), jnp.zeros_like(abim)
    for t in range(t_len):
        rows = slice(t * SSM_GROUP, (t + 1) * SSM_GROUP)
        cp_ref[rows, :] = jnp.where(first, cpre, -cpim)
        if t > 0:
            prev = slice((t - 1) * SSM_GROUP, t * SSM_GROUP)
            pmt_ref[0, prev, :] = jnp.where(first, cpre, -cpim).astype(BF16)
        srows = slice((t_len - 1 - t) * SSM_GROUP, (t_len - t) * SSM_GROUP)
        q_ref[srows, :] = jnp.where(first, qre, qim).astype(BF16)
        qs_ref[srows, :] = jnp.where(first, qim, qre).astype(BF16)
        cpre, cpim = _cmul(cpre, cpim, abre, abim)
        qre, qim = _cmul(qre, qim, abre, abim)
        pwre, pwim = _cmul(pwre, pwim, abre, abim)
    last = slice((t_len - 1) * SSM_GROUP, t_len * SSM_GROUP)
    pmt_ref[0, last, :] = jnp.where(first, cpre, -cpim).astype(BF16)
    a1_ref[0] = pwre
    a2_ref[0] = jnp.where(first1, -pwim, pwim)

    krow = lax.dot_general(bbcat, cp_ref[...], (((1,), (1,)), ((), ())),
                           preferred_element_type=F32, precision=lax.Precision.HIGHEST)
    krow = krow + dm_ref[0]
    lane = lax.broadcasted_iota(jnp.int32, krow.shape, 1)
    for s in range(t_len):
        rows = slice(s * SSM_GROUP, (s + 1) * SSM_GROUP)
        shifted = krow if s == 0 else pltpu.roll(krow, s * SSM_GROUP, axis=1)
        toep_ref[0, rows, :] = jnp.where(lane >= s * SSM_GROUP, shifted, 0.0).astype(BF16)

    u = u_ref[0]
    v_ref[0] = _dot(u, q_ref[...])
    vs_ref[0] = _dot(u, qs_ref[...])


def _ssm_prep(ldt, are2, aim2, bt_re2, bt_im2, c_re2, c_im2, dmat, u_chunks):
    g, c, tw = u_chunks.shape
    per_g = lambda *shape: pl.BlockSpec((1,) + shape, lambda i: (i,) + (0,) * len(shape))
    return pl.pallas_call(
        _ssm_prep_kernel,
        grid=(g,),
        in_specs=[per_g(1, LANES), per_g(1, LANES), per_g(1, LANES),
                  per_g(SSM_GROUP, LANES), per_g(SSM_GROUP, LANES),
                  per_g(SSM_GROUP, LANES), per_g(SSM_GROUP, LANES),
                  per_g(SSM_GROUP, tw), per_g(c, tw)],
        out_specs=[per_g(tw, tw), per_g(tw, LANES), per_g(1, LANES), per_g(1, LANES),
                   per_g(c, LANES), per_g(c, LANES)],
        out_shape=[jax.ShapeDtypeStruct((g, tw, tw), BF16),
                   jax.ShapeDtypeStruct((g, tw, LANES), BF16),
                   jax.ShapeDtypeStruct((g, 1, LANES), F32),
                   jax.ShapeDtypeStruct((g, 1, LANES), F32),
                   jax.ShapeDtypeStruct((g, c, LANES), F32),
                   jax.ShapeDtypeStruct((g, c, LANES), F32)],
        scratch_shapes=[pltpu.VMEM((tw, LANES), F32), pltpu.VMEM((tw, LANES), BF16),
                        pltpu.VMEM((tw, LANES), BF16)],
        compiler_params=_cparams(("arbitrary",), 32),
        name="ssm_prep",
    )(ldt, are2, aim2, bt_re2, bt_im2, c_re2, c_im2, dmat, u_chunks)


def _ssm_scan_kernel(v_ref, vs_ref, a1_ref, a2_ref, x0_ref):
    n_chunks = v_ref.shape[0]
    a1, a2 = a1_ref[...], a2_ref[...]

    def body(c, carry):
        x, xs = carry
        x0_ref[c] = x
        xn = a1 * x + a2 * xs + v_ref[c]
        xsn = a1 * xs - a2 * x + vs_ref[c]
        return xn, xsn

    zero = jnp.zeros(a1.shape, F32)
    lax.fori_loop(0, n_chunks, body, (zero, zero))


def _ssm_scan(v_cg, vs_cg, a1, a2):
    return pl.pallas_call(
        _ssm_scan_kernel,
        out_shape=jax.ShapeDtypeStruct(v_cg.shape, F32),
        compiler_params=_cparams(None, 48),
        name="ssm_scan",
    )(v_cg, vs_cg, a1, a2)


def _ssm_out_kernel(u_ref, toep_ref, pmt_ref, x0_ref, y_ref):
    for gi in range(u_ref.shape[0]):
        y = _dot(u_ref[gi], toep_ref[gi]) + _dot_nt(x0_ref[gi].astype(BF16), pmt_ref[gi])
        y_ref[gi] = y.astype(BF16)


def _ssm_out(u_chunks, toep, pmt, x0):
    g, c, tw = u_chunks.shape
    blk = lambda *shape: pl.BlockSpec((SSM_GB,) + shape, lambda i: (i,) + (0,) * len(shape))
    return pl.pallas_call(
        _ssm_out_kernel,
        grid=(g // SSM_GB,),
        in_specs=[blk(c, tw), blk(tw, tw), blk(tw, LANES), blk(c, LANES)],
        out_specs=blk(c, tw),
        out_shape=jax.ShapeDtypeStruct((g, c, tw), BF16),
        compiler_params=_cparams(("arbitrary",), 48),
        name="ssm_out",
    )(u_chunks, toep, pmt, x0)


def _s5(u, a_re, a_im, log_dt, b_re, b_im, c_re, c_im, d_skip):
    s = u.shape[0]
    g, p = a_re.shape
    c = s // SSM_T
    dup = lambda a: jnp.concatenate([a, a], axis=-1)
    ldt = jnp.broadcast_to(log_dt.reshape(g, 1, 1), (g, 1, LANES))
    are2, aim2 = dup(a_re).reshape(g, 1, LANES), dup(a_im).reshape(g, 1, LANES)
    bt_re2 = dup(jnp.swapaxes(b_re, 1, 2))
    bt_im2 = dup(jnp.swapaxes(b_im, 1, 2))
    c_re2, c_im2 = dup(c_re), dup(c_im)
    dmat = jnp.pad(d_skip[:, None, :] * jnp.eye(SSM_GROUP, dtype=F32)[None],
                   ((0, 0), (0, 0), (0, SSM_TW - SSM_GROUP)))
    u_chunks = u.reshape(c, SSM_T, g, SSM_GROUP).transpose(2, 0, 1, 3).reshape(g, c, SSM_TW)
    toep, pmt, a1, a2, v, vs = _ssm_prep(ldt, are2, aim2, bt_re2, bt_im2, c_re2, c_im2,
                                         dmat, u_chunks)
    x0 = _ssm_scan(v.transpose(1, 0, 2), vs.transpose(1, 0, 2),
                   a1.reshape(g, LANES), a2.reshape(g, LANES))
    y_chunks = _ssm_out(u_chunks, toep, pmt, x0.transpose(1, 0, 2))
    return y_chunks.reshape(g, c, SSM_T, SSM_GROUP).transpose(1, 2, 0, 3).reshape(s, g * SSM_GROUP)


def _gelu_tanh(x):
    return 0.5 * x * (1.0 + jnp.tanh(math.sqrt(2.0 / math.pi) * (x + 0.044715 * (x * x * x))))


def _mixout_kernel(x_ref, attn_ref, y_ref, wglu_ref, bglu_ref, ga_ref, gs_ref, woa_ref, wos_ref,
                   gt_ref, gf_ref, sc_ref, sh_ref, h_ref, hn_ref):
    y = _gelu_tanh(y_ref[...].astype(F32))
    gate = jax.nn.sigmoid(_dot(y.astype(BF16), wglu_ref[...]) + bglu_ref[...])
    ns = _rms(y * gate, gs_ref[...]).astype(BF16)
    na = _rms(attn_ref[...].astype(F32), ga_ref[...]).astype(BF16)
    mixed = _dot(na, woa_ref[...]) + _dot(ns, wos_ref[...])
    h = x_ref[...] + gt_ref[...] * mixed
    h_ref[...] = h
    hn_ref[...] = (_rms(h, gf_ref[...]) * (1.0 + sc_ref[...]) + sh_ref[...]).astype(BF16)


def _mixout(x, attn, y, w_glu, b_glu, g_attn, g_ssm, w_out_a, w_out_s, gt1, g_ffn, sc2, sh2,
            tm=256):
    s, d = x.shape
    w = attn.shape[1]
    const = lambda r, c: pl.BlockSpec((r, c), lambda i: (0, 0))
    rows = lambda c: pl.BlockSpec((tm, c), lambda i: (i, 0))
    return pl.pallas_call(
        _mixout_kernel,
        grid=(s // tm,),
        in_specs=[rows(d), rows(w), rows(w), const(w, w), const(1, w), const(1, w), const(1, w),
                  const(w, d), const(w, d), const(1, d), const(1, d), const(1, d), const(1, d)],
        out_specs=[rows(d), rows(d)],
        out_shape=[jax.ShapeDtypeStruct((s, d), F32), jax.ShapeDtypeStruct((s, d), BF16)],
        compiler_params=_cparams(("arbitrary",), 48),
        name="mixer_out",
    )(x, attn, y, w_glu, b_glu, g_attn, g_ssm, w_out_a, w_out_s, gt1, g_ffn, sc2, sh2)


def _ffn_kernel(hn_ref, halo_ref, h_ref, wa_ref, wb_ref, cw_ref, cb_ref, wd_ref, gt_ref, gfin_ref,
                o_ref, acc_ref):
    i, j = pl.program_id(0), pl.program_id(1)

    @pl.when(j == 0)
    def _():
        acc_ref[...] = jnp.zeros(acc_ref.shape, F32)

    hn = hn_ref[...]
    a = _dot(hn, wa_ref[...])
    b = _dot(hn, wb_ref[...])
    halo = _dot(halo_ref[...], wa_ref[...]) * (i > 0).astype(F32)
    row = lax.broadcasted_iota(jnp.int32, a.shape, 0)
    prev1 = jnp.where(row == 0, halo[7:8, :], pltpu.roll(a, 1, axis=0))
    prev2 = jnp.where(row == 0, halo[6:7, :],
                      jnp.where(row == 1, halo[7:8, :], pltpu.roll(a, 2, axis=0)))
    cw = cw_ref[...]
    conv = cb_ref[...] + cw[0:1, :] * prev2 + cw[1:2, :] * prev1 + cw[2:3, :] * a
    act = (conv * jax.nn.sigmoid(conv) * b).astype(BF16)
    acc_ref[...] += _dot(act, wd_ref[...])

    @pl.when(j == pl.num_programs(1) - 1)
    def _():
        h = h_ref[...] + gt_ref[...] * acc_ref[...]
        o_ref[...] = _rms(h, gfin_ref[...])


def _ffn(hn, h, w_a, w_b, conv_w, conv_b, w_down, gt2, g_final, tm=512, tn=512):
    s, d = h.shape
    f = w_a.shape[1]
    halo_blocks = tm // 8
    return pl.pallas_call(
        _ffn_kernel,
        grid=(s // tm, f // tn),
        in_specs=[pl.BlockSpec((tm, d), lambda i, j: (i, 0)),
                  pl.BlockSpec((8, d), lambda i, j: (jnp.maximum(i * halo_blocks - 1, 0), 0)),
                  pl.BlockSpec((tm, d), lambda i, j: (i, 0)),
                  pl.BlockSpec((d, tn), lambda i, j: (0, j)),
                  pl.BlockSpec((d, tn), lambda i, j: (0, j)),
                  pl.BlockSpec((3, tn), lambda i, j: (0, j)),
                  pl.BlockSpec((1, tn), lambda i, j: (0, j)),
                  pl.BlockSpec((tn, d), lambda i, j: (j, 0)),
                  pl.BlockSpec((1, d), lambda i, j: (0, 0)),
                  pl.BlockSpec((1, d), lambda i, j: (0, 0))],
        out_specs=pl.BlockSpec((tm, d), lambda i, j: (i, 0)),
        out_shape=jax.ShapeDtypeStruct((s, d), F32),
        scratch_shapes=[pltpu.VMEM((tm, d), F32)],
        compiler_params=_cparams(("arbitrary", "arbitrary"), 56),
        name="conv_ffn",
    )(hn, hn, h, w_a, w_b, conv_w, conv_b, w_down, gt2, g_final)


def _layer(h, mod, g_mix, w_in, b_f, a_re, a_im, log_dt, ssm_b_re, ssm_b_im, ssm_c_re, ssm_c_im,
           ssm_d, w_glu, b_glu, g_attn_out, g_ssm_out, w_out, g_ffn, w_up, conv_w, conv_b, w_down):
    s, d = h.shape
    aw = N_HEADS * HEAD_DIM
    sw = a_re.shape[0] * SSM_GROUP
    d_ff = w_down.shape[0]
    sh1, sc1, gt1, sh2, sc2, gt2 = [mod[:, i * d:(i + 1) * d] for i in range(N_MOD)]

    wq, wk, wv = (w_in[:, i * aw:(i + 1) * aw].reshape(d, N_HEADS, HEAD_DIM) for i in range(3))
    wu = w_in[:, 3 * aw + N_HEADS:].reshape(d, sw // LANES, LANES)
    w_tiles = jnp.concatenate([wq, wk, wv, wu], axis=2).transpose(1, 0, 2).astype(BF16)
    w_f = jnp.pad(w_in[:, 3 * aw:3 * aw + N_HEADS], ((0, 0), (0, LANES - N_HEADS))).astype(BF16)

    row = lambda a: a.reshape(1, -1)
    q, k, v, u, f = _inproj(h, row(g_mix), sc1, sh1, w_tiles, w_f)

    cum = _forget_cumsum(f[:, :N_HEADS].T, b_f)
    attn = _attention(q, k, v, cum.reshape(N_HEADS, s, 1), cum.reshape(N_HEADS, 1, s))

    y = _s5(u, a_re, a_im, log_dt, ssm_b_re, ssm_b_im, ssm_c_re, ssm_c_im, ssm_d)

    h1, hn2 = _mixout(h, attn, y, w_glu.astype(BF16), row(b_glu), row(g_attn_out),
                      row(g_ssm_out), w_out[:aw].astype(BF16), w_out[aw:].astype(BF16),
                      gt1, row(g_ffn), sc2, sh2)
    return hn2, h1, (w_up[:, :d_ff].astype(BF16), w_up[:, d_ff:].astype(BF16), conv_w,
                     row(conv_b), w_down.astype(BF16), gt2)


def kernel(x, c, w_ada, b_ada, g_mix, w_in, b_f, a_re, a_im, log_dt, ssm_b_re, ssm_b_im, ssm_c_re,
           ssm_c_im, ssm_d, w_glu, b_glu, g_attn_out, g_ssm_out, w_out, g_ffn, w_up, conv_w,
           conv_b, w_down, g_final):
    batch, s, d = x.shape
    assert w_ada.shape[0] == 1, "only DEPTH == 1 is supported"
    l = 0
    outs = []
    for bi in range(batch):
        mod = _adaln(c[bi:bi + 1], w_ada[l], b_ada[l])
        hn2, h1, ffn_args = _layer(
            x[bi], mod, g_mix[l], w_in[l], b_f[l], a_re[l], a_im[l], log_dt[l], ssm_b_re[l],
            ssm_b_im[l], ssm_c_re[l], ssm_c_im[l], ssm_d[l], w_glu[l], b_glu[l],
            g_attn_out[l], g_ssm_out[l], w_out[l], g_ffn[l], w_up[l], conv_w[l], conv_b[l],
            w_down[l])
        outs.append(_ffn(hn2, h1, *ffn_args, g_final.reshape(1, d)))
    return jnp.stack(outs, axis=0)
```

```python
import functools
import math

import jax
import jax.numpy as jnp
from jax import lax
from jax.experimental import pallas as pl
from jax.experimental.pallas import tpu as pltpu

F32 = jnp.float32
BF16 = jnp.bfloat16

EPS = 1e-6
HEAD_DIM = 128
N_HEADS = 8
SSM_GROUP = 16
SSM_STATE = 64
N_MOD = 6
LANES = 128
SSM_T = 16
SSM_TW = SSM_T * SSM_GROUP
SSM_PAIRS = SSM_T // 2
SSM_TILE_G = LANES // SSM_GROUP
LOG2E = 1.4426950408889634
SKIP_LOG2 = 150.0
NORM_SLACK = 1.01

_MIB = 1024 * 1024


def _cparams(semantics, vmem_mib):
    return pltpu.CompilerParams(dimension_semantics=semantics, vmem_limit_bytes=vmem_mib * _MIB)


def _resident(shape):
    return pl.BlockSpec(shape, lambda *_: (0,) * len(shape), pipeline_mode=pl.Buffered(1))


def _dot(a, b):
    return jnp.dot(a, b, preferred_element_type=F32)


def _dot_nt(a, b):
    return lax.dot_general(a, b, (((1,), (1,)), ((), ())), preferred_element_type=F32)


def _rms(x, g):
    return x * lax.rsqrt(jnp.mean(x * x, axis=-1, keepdims=True) + EPS) * g


def _lane_tile(x, reps):
    return jnp.concatenate([x] * reps, axis=1)


def _adaln_kernel(c_ref, w_ref, b_ref, o_ref):
    c = c_ref[...]
    cond = c * jax.nn.sigmoid(c)
    cond8 = jnp.broadcast_to(cond, (8, c.shape[1])).astype(BF16)
    acc = _dot(cond8, w_ref[...].astype(BF16))
    o_ref[...] = acc[0:1, :] + b_ref[...]


def _adaln(c, w, b, tn=1536):
    d, n = w.shape
    return pl.pallas_call(
        _adaln_kernel,
        grid=(n // tn,),
        in_specs=[pl.BlockSpec((1, d), lambda j: (0, 0)),
                  pl.BlockSpec((d, tn), lambda j: (0, j)),
                  pl.BlockSpec((1, tn), lambda j: (0, j))],
        out_specs=pl.BlockSpec((1, tn), lambda j: (0, j)),
        out_shape=jax.ShapeDtypeStruct((1, n), F32),
        compiler_params=_cparams(("arbitrary",), 48),
        name="adaln",
    )(c, w, b.reshape(1, n))


def _inproj_kernel(x_ref, g_ref, sc_ref, sh_ref, wq_ref, wk_ref, wv_ref, wu_ref, wf_ref,
                   q_ref, k_ref, v_ref, u_ref, f_ref, *, q_scale):
    hn = (_rms(x_ref[...], g_ref[...]) * (1.0 + sc_ref[...]) + sh_ref[...]).astype(BF16)
    f_ref[...] = _dot(hn, wf_ref[...])
    q_ref[...] = (_dot(hn, wq_ref[...]) * q_scale).astype(BF16)
    k_ref[...] = _dot(hn, wk_ref[...]).astype(BF16)
    u_ref[...] = _dot(hn, wu_ref[...])
    v = _dot(hn, wv_ref[...]).astype(BF16)
    ones = jnp.ones((v.shape[0], HEAD_DIM), BF16)
    for h in range(N_HEADS):
        v_ref[:, 2 * h * HEAD_DIM:(2 * h + 1) * HEAD_DIM] = v[:, h * HEAD_DIM:(h + 1) * HEAD_DIM]
        v_ref[:, (2 * h + 1) * HEAD_DIM:(2 * h + 2) * HEAD_DIM] = ones


def _inproj(x, g, sc, sh, wq, wk, wv, wu, wf, tm=512):
    s, d = x.shape
    aw, sw = wq.shape[1], wu.shape[1]
    rows = lambda c: pl.BlockSpec((tm, c), lambda i: (i, 0))
    return pl.pallas_call(
        functools.partial(_inproj_kernel, q_scale=HEAD_DIM ** -0.5 * LOG2E),
        grid=(s // tm,),
        in_specs=[rows(d), _resident((1, d)), _resident((1, d)), _resident((1, d)),
                  _resident((d, aw)), _resident((d, aw)), _resident((d, aw)), _resident((d, sw)),
                  _resident((d, LANES))],
        out_specs=[rows(aw), rows(aw), rows(2 * aw), rows(sw), rows(LANES)],
        out_shape=[jax.ShapeDtypeStruct((s, aw), BF16), jax.ShapeDtypeStruct((s, aw), BF16),
                   jax.ShapeDtypeStruct((s, 2 * aw), BF16), jax.ShapeDtypeStruct((s, sw), F32),
                   jax.ShapeDtypeStruct((s, LANES), F32)],
        compiler_params=_cparams(("arbitrary",), 56),
        name="inproj",
    )(x, g, sc, sh, wq, wk, wv, wu, wf)


def _cum_kernel(f_ref, b_ref, o_ref):
    z = f_ref[...] + b_ref[...]
    x = jnp.minimum(z, 0.0) - jnp.log1p(jnp.exp(-jnp.abs(z)))
    n = x.shape[1]
    lane = lax.broadcasted_iota(jnp.int32, x.shape, 1)
    shift = 1
    while shift < n:
        x = x + jnp.where(lane >= shift, pltpu.roll(x, shift, axis=1), 0.0)
        shift *= 2
    o_ref[...] = x * LOG2E


def _forget_cumsum(f_t, b_f):
    h, s = f_t.shape
    return pl.pallas_call(
        _cum_kernel,
        out_shape=jax.ShapeDtypeStruct((h, s), F32),
        name="forget_cumsum",
    )(f_t, b_f.reshape(h, 1))


def _attn_kernel(q_ref, k_ref, v_ref, cq_ref, ck_ref, o_ref, m_ref, acc_ref, kmax_ref, *,
                 tq, norm_chunk):
    i = pl.program_id(1)
    s_len = k_ref.shape[0]
    reps = tq // LANES

    @pl.when(i == 0)
    def _():
        ones = jnp.ones((HEAD_DIM, LANES), BF16)

        def norm_body(r, mx):
            kb = k_ref[pl.ds(pl.multiple_of(r * norm_chunk, norm_chunk), norm_chunk), :]
            kb = kb.astype(F32)
            row_sq = _dot((kb * kb).astype(BF16), ones)
            return jnp.maximum(mx, jnp.max(row_sq, axis=0, keepdims=True))

        mx = lax.fori_loop(0, s_len // norm_chunk, norm_body, jnp.zeros((1, LANES), F32))
        kmax_ref[...] = jnp.sqrt(mx) * NORM_SLACK

    q = q_ref[...]
    cqb = jnp.broadcast_to(cq_ref[0], (tq, LANES))

    def scores(j):
        ks = pl.multiple_of(j * tq, tq)
        t = _dot_nt(q, k_ref[pl.ds(ks, tq), :]) - ck_ref[0, :, pl.ds(ks, tq)]
        return t, v_ref[pl.ds(ks, tq), :]

    t, vb = scores(i)
    row = lax.broadcasted_iota(jnp.int32, t.shape, 0)
    col = lax.broadcasted_iota(jnp.int32, t.shape, 1)
    t = jnp.where(col <= row, t, -jnp.inf)
    m0 = jnp.max(t, axis=1, keepdims=True) + cqb
    p = jnp.exp2(t - _lane_tile(m0 - cqb, reps))
    m_ref[...] = m0
    acc_ref[...] = _dot(p.astype(BF16), vb)

    qf = q.astype(F32)
    qn = jnp.sqrt(jnp.sum(qf * qf, axis=1, keepdims=True))
    bound = jnp.max(qn * kmax_ref[...] - (m0 - cqb))
    ck_all = ck_ref[0]
    pos = lax.broadcasted_iota(jnp.int32, ck_all.shape, 1)
    dead = jnp.where((pos < i * tq) & (ck_all > bound + SKIP_LOG2), 1.0, 0.0)
    j0 = jnp.sum(dead).astype(jnp.int32) // tq

    def body(j, carry):
        t, vb = scores(j)
        m_prev = m_ref[...]
        m_new = jnp.maximum(m_prev, jnp.max(t, axis=1, keepdims=True) + cqb)
        alpha = jnp.exp2(m_prev - m_new)
        p = jnp.exp2(t - _lane_tile(m_new - cqb, reps))
        acc_ref[...] = _lane_tile(alpha, 2) * acc_ref[...] + _dot(p.astype(BF16), vb)
        m_ref[...] = m_new
        return carry

    lax.fori_loop(j0, i, body, 0)
    acc = acc_ref[...]
    o_ref[...] = (acc[:, :HEAD_DIM] / acc[:, HEAD_DIM:]).astype(BF16)


def _attention(q, k, v_ones, cum_col, cum_row, tq=512):
    s = q.shape[0]
    return pl.pallas_call(
        functools.partial(_attn_kernel, tq=tq, norm_chunk=min(s, 1024)),
        grid=(N_HEADS, s // tq),
        in_specs=[pl.BlockSpec((tq, HEAD_DIM), lambda h, i: (i, h)),
                  pl.BlockSpec((s, HEAD_DIM), lambda h, i: (0, h)),
                  pl.BlockSpec((s, 2 * HEAD_DIM), lambda h, i: (0, h)),
                  pl.BlockSpec((1, tq, 1), lambda h, i: (h, i, 0)),
                  pl.BlockSpec((1, 1, s), lambda h, i: (h, 0, 0))],
        out_specs=pl.BlockSpec((tq, HEAD_DIM), lambda h, i: (i, h)),
        out_shape=jax.ShapeDtypeStruct((s, N_HEADS * HEAD_DIM), BF16),
        scratch_shapes=[pltpu.VMEM((tq, LANES), F32), pltpu.VMEM((tq, 2 * HEAD_DIM), F32),
                        pltpu.VMEM((1, LANES), F32)],
        compiler_params=_cparams(("arbitrary", "arbitrary"), 40),
        name="fox_attention",
    )(q, k, v_ones, cum_col, cum_row)


def _cmul(are, aim, bre, bim):
    return are * bre - aim * bim, are * bim + aim * bre


def _step_pair(u_ref, a, n_chunks):
    x0 = u_ref[pl.ds(2 * a, n_chunks, stride=SSM_T), :]
    x1 = u_ref[pl.ds(2 * a + 1, n_chunks, stride=SSM_T), :]
    return jnp.concatenate([x0, x1], axis=1).astype(BF16)


def _ssm_in_kernel(u_ref, ldt_ref, are_ref, aim_ref, btr_ref, bti_ref, cr_ref, ci_ref, d_ref,
                   w2_ref, pmt2_ref, x0_ref,
                   cp_ref, wl_ref, q2_ref, v_ref, vs_ref):
    n_chunks = u_ref.shape[0] // SSM_T
    a1_parts, a2_parts = [], []
    lane = lax.broadcasted_iota(jnp.int32, (SSM_GROUP, LANES), 1)
    first = lane < SSM_STATE

    @pl.when(pl.program_id(0) == 0)
    def _():
        q2_ref[...] = jnp.zeros(q2_ref.shape, BF16)

    pmt2_ref[...] = jnp.zeros(pmt2_ref.shape, BF16)

    def block(step, gi):
        r0 = (step % 2) * LANES + gi * SSM_GROUP
        return step // 2, slice(r0, r0 + SSM_GROUP), slice(gi * LANES, (gi + 1) * LANES)

    for gi in range(SSM_TILE_G):
        dt = jnp.exp(ldt_ref[gi])
        are, aim = are_ref[gi], aim_ref[gi]
        mag = jnp.exp(dt * are)
        abre, abim = mag * jnp.cos(dt * aim), mag * jnp.sin(dt * aim)
        nre, nim = abre - 1.0, abim
        den = are * are + aim * aim
        zre = (nre * are + nim * aim) / den
        zim = (nim * are - nre * aim) / den
        bbre, bbim = _cmul(zre, zim, btr_ref[gi], bti_ref[gi])
        bbcat = jnp.where(first, bbre, bbim)

        cpre, cpim = cr_ref[gi], ci_ref[gi]
        qre, qim = bbre, bbim
        pwre, pwim = jnp.ones_like(abre), jnp.zeros_like(abim)
        for t in range(SSM_T):
            cpcat = jnp.where(first, cpre, -cpim)
            cp_ref[t * SSM_GROUP:(t + 1) * SSM_GROUP, :] = cpcat
            if t > 0:
                b, rows, cols = block(t - 1, gi)
                pmt2_ref[0, b, rows, cols] = cpcat.astype(BF16)
            a, rows, cols = block(SSM_T - 1 - t, gi)
            q2_ref[a, rows, cols] = jnp.where(first, qre, qim).astype(BF16)
            cpre, cpim = _cmul(cpre, cpim, abre, abim)
            qre, qim = _cmul(qre, qim, abre, abim)
            pwre, pwim = _cmul(pwre, pwim, abre, abim)
        b, rows, cols = block(SSM_T - 1, gi)
        pmt2_ref[0, b, rows, cols] = jnp.where(first, cpre, -cpim).astype(BF16)
        a1_parts.append(pwre)
        a2_parts.append(jnp.where(first[0:1], -pwim, pwim))

        krow = lax.dot_general(bbcat, cp_ref[...], (((1,), (1,)), ((), ())),
                               preferred_element_type=F32, precision=lax.Precision.HIGHEST)
        own = (lane >= gi * SSM_GROUP) & (lane < (gi + 1) * SSM_GROUP)
        for tau in range(SSM_T):
            half = krow[:, (tau // SSM_TILE_G) * LANES:(tau // SSM_TILE_G + 1) * LANES]
            shift = ((gi - tau % SSM_TILE_G) * SSM_GROUP) % LANES
            moved = half if shift == 0 else pltpu.roll(half, shift, axis=1)
            wl_ref[tau, gi * SSM_GROUP:(gi + 1) * SSM_GROUP, :] = jnp.where(own, moved, 0.0)

    r = lax.broadcasted_iota(jnp.int32, (LANES, LANES), 0)
    c = lax.broadcasted_iota(jnp.int32, (LANES, LANES), 1)
    wl_ref[0] = wl_ref[0] + jnp.where(r == c, d_ref[0], 0.0)
    for dl in range(SSM_PAIRS):
        diag = wl_ref[2 * dl].astype(BF16)
        w2_ref[0, dl, 0:LANES, 0:LANES] = diag
        w2_ref[0, dl, LANES:, LANES:] = diag
        w2_ref[0, dl, 0:LANES, LANES:] = wl_ref[2 * dl + 1].astype(BF16)
        below = jnp.zeros((LANES, LANES), BF16) if dl == 0 else wl_ref[2 * dl - 1].astype(BF16)
        w2_ref[0, dl, LANES:, 0:LANES] = below

    v = _dot(_step_pair(u_ref, 0, n_chunks), q2_ref[0])
    for a in range(1, SSM_PAIRS):
        v = v + _dot(_step_pair(u_ref, a, n_chunks), q2_ref[a])
    v_ref[...] = v
    for gi in range(SSM_TILE_G):
        cols = slice(gi * LANES, (gi + 1) * LANES)
        vs_ref[:, cols] = pltpu.roll(v[:, cols], SSM_STATE, axis=1)

    a1 = jnp.concatenate(a1_parts, axis=1)
    a2 = jnp.concatenate(a2_parts, axis=1)

    def scan_body(c, carry):
        x, xs = carry
        x0_ref[pl.ds(c, 1), :] = x
        xn = a1 * x + a2 * xs + v_ref[pl.ds(c, 1), :]
        xsn = a1 * xs - a2 * x + vs_ref[pl.ds(c, 1), :]
        return xn, xsn

    zero = jnp.zeros(a1.shape, F32)
    lax.fori_loop(0, n_chunks, scan_body, (zero, zero), unroll=8)


def _ssm_in(u, ldt, are2, aim2, bt_re2, bt_im2, c_re2, c_im2, d_rows):
    s, width = u.shape
    n_tiles = width // LANES
    c = s // SSM_T
    kw = SSM_TILE_G * LANES
    per_tile = lambda *shape: pl.BlockSpec((SSM_TILE_G,) + shape,
                                           lambda j: (j,) + (0,) * len(shape))
    return pl.pallas_call(
        _ssm_in_kernel,
        grid=(n_tiles,),
        in_specs=[pl.BlockSpec((s, LANES), lambda j: (0, j)),
                  per_tile(1, LANES), per_tile(1, LANES), per_tile(1, LANES),
                  per_tile(SSM_GROUP, LANES), per_tile(SSM_GROUP, LANES),
                  per_tile(SSM_GROUP, LANES), per_tile(SSM_GROUP, LANES),
                  pl.BlockSpec((1, 1, LANES), lambda j: (j, 0, 0))],
        out_specs=[pl.BlockSpec((1, SSM_PAIRS, 2 * LANES, 2 * LANES), lambda j: (j, 0, 0, 0)),
                   pl.BlockSpec((1, SSM_PAIRS, 2 * LANES, kw), lambda j: (j, 0, 0, 0)),
                   pl.BlockSpec((c, kw), lambda j: (0, j))],
        out_shape=[jax.ShapeDtypeStruct((n_tiles, SSM_PAIRS, 2 * LANES, 2 * LANES), BF16),
                   jax.ShapeDtypeStruct((n_tiles, SSM_PAIRS, 2 * LANES, kw), BF16),
                   jax.ShapeDtypeStruct((c, n_tiles * kw), F32)],
        scratch_shapes=[pltpu.VMEM((SSM_TW, LANES), F32),
                        pltpu.VMEM((SSM_T, LANES, LANES), F32),
                        pltpu.VMEM((SSM_PAIRS, 2 * LANES, kw), BF16),
                        pltpu.VMEM((c, kw), F32), pltpu.VMEM((c, kw), F32)],
        compiler_params=_cparams(("arbitrary",), 48),
        name="ssm_state_in",
    )(u, ldt, are2, aim2, bt_re2, bt_im2, c_re2, c_im2, d_rows)


def _ssm_out_kernel(u_ref, w2_ref, pmt2_ref, x0_ref, y_ref):
    n_chunks = u_ref.shape[0] // SSM_T
    x0 = x0_ref[...].astype(BF16)
    pairs = [_step_pair(u_ref, a, n_chunks) for a in range(SSM_PAIRS)]
    for b in range(SSM_PAIRS):
        acc = _dot_nt(x0, pmt2_ref[0, b])
        for a in range(b + 1):
            acc = acc + _dot(pairs[a], w2_ref[0, b - a])
        y_ref[pl.ds(2 * b, n_chunks, stride=SSM_T), :] = acc[:, :LANES]
        y_ref[pl.ds(2 * b + 1, n_chunks, stride=SSM_T), :] = acc[:, LANES:]


def _ssm_out(u, w2, pmt2, x0):
    s, width = u.shape
    n_tiles = width // LANES
    c = s // SSM_T
    kw = SSM_TILE_G * LANES
    return pl.pallas_call(
        _ssm_out_kernel,
        grid=(n_tiles,),
        in_specs=[pl.BlockSpec((s, LANES), lambda j: (0, j)),
                  pl.BlockSpec((1, SSM_PAIRS, 2 * LANES, 2 * LANES), lambda j: (j, 0, 0, 0)),
                  pl.BlockSpec((1, SSM_PAIRS, 2 * LANES, kw), lambda j: (j, 0, 0, 0)),
                  pl.BlockSpec((c, kw), lambda j: (0, j))],
        out_specs=pl.BlockSpec((s, LANES), lambda j: (0, j)),
        out_shape=jax.ShapeDtypeStruct((s, width), F32),
        compiler_params=_cparams(("arbitrary",), 48),
        name="ssm_out",
    )(u, w2, pmt2, x0)


def _s5(u, a_re, a_im, log_dt, b_re, b_im, c_re, c_im, d_skip):
    g, p = a_re.shape
    dup = lambda a: jnp.concatenate([a, a], axis=-1)
    ldt = jnp.broadcast_to(log_dt.reshape(g, 1, 1), (g, 1, LANES))
    are2, aim2 = dup(a_re).reshape(g, 1, LANES), dup(a_im).reshape(g, 1, LANES)
    bt_re2 = dup(jnp.swapaxes(b_re, 1, 2))
    bt_im2 = dup(jnp.swapaxes(b_im, 1, 2))
    w2, pmt2, x0 = _ssm_in(u, ldt, are2, aim2, bt_re2, bt_im2, dup(c_re), dup(c_im),
                           d_skip.reshape(-1, 1, LANES))
    return _ssm_out(u, w2, pmt2, x0)


def _gelu_tanh(x):
    return 0.5 * x * (1.0 + jnp.tanh(math.sqrt(2.0 / math.pi) * (x + 0.044715 * (x * x * x))))


def _mixout_kernel(x_ref, attn_ref, y_ref, wglu_ref, bglu_ref, ga_ref, gs_ref, woa_ref, wos_ref,
                   gt_ref, gf_ref, sc_ref, sh_ref, h_ref, hn_ref):
    y = _gelu_tanh(y_ref[...])
    gate = jax.nn.sigmoid(_dot(y.astype(BF16), wglu_ref[...]) + bglu_ref[...])
    ns = _rms(y * gate, gs_ref[...]).astype(BF16)
    na = _rms(attn_ref[...].astype(F32), ga_ref[...]).astype(BF16)
    mixed = _dot(na, woa_ref[...]) + _dot(ns, wos_ref[...])
    h = x_ref[...] + gt_ref[...] * mixed
    h_ref[...] = h
    hn_ref[...] = (_rms(h, gf_ref[...]) * (1.0 + sc_ref[...]) + sh_ref[...]).astype(BF16)


def _mixout(x, attn, y, w_glu, b_glu, g_attn, g_ssm, w_out_a, w_out_s, gt1, g_ffn, sc2, sh2,
            tm=512):
    s, d = x.shape
    w = attn.shape[1]
    rows = lambda c: pl.BlockSpec((tm, c), lambda i: (i, 0))
    return pl.pallas_call(
        _mixout_kernel,
        grid=(s // tm,),
        in_specs=[rows(d), rows(w), rows(w), _resident((w, w)), _resident((1, w)),
                  _resident((1, w)), _resident((1, w)), _resident((w, d)), _resident((w, d)),
                  _resident((1, d)), _resident((1, d)), _resident((1, d)), _resident((1, d))],
        out_specs=[rows(d), rows(d)],
        out_shape=[jax.ShapeDtypeStruct((s, d), F32), jax.ShapeDtypeStruct((s, d), BF16)],
        compiler_params=_cparams(("arbitrary",), 56),
        name="mixer_out",
    )(x, attn, y, w_glu, b_glu, g_attn, g_ssm, w_out_a, w_out_s, gt1, g_ffn, sc2, sh2)


def _ffn_kernel(hn_ref, halo_ref, h_ref, wa_ref, wb_ref, cw_ref, cb_ref, wd_ref, gt_ref, gfin_ref,
                o_ref):
    i, j = pl.program_id(0), pl.program_id(1)
    hn = hn_ref[...]
    a = _dot(hn, wa_ref[0])
    b = _dot(hn, wb_ref[0])
    halo = _dot(halo_ref[...], wa_ref[0]) * (i > 0).astype(F32)
    row = lax.broadcasted_iota(jnp.int32, a.shape, 0)
    prev1 = jnp.where(row == 0, halo[7:8, :], pltpu.roll(a, 1, axis=0))
    prev2 = jnp.where(row == 0, halo[6:7, :],
                      jnp.where(row == 1, halo[7:8, :], pltpu.roll(a, 2, axis=0)))
    cw = cw_ref[...]
    conv = cb_ref[...] + cw[0:1, :] * prev2 + cw[1:2, :] * prev1 + cw[2:3, :] * a
    act = (conv * jax.nn.sigmoid(conv) * b).astype(BF16)
    part = _dot(act, wd_ref[...])

    @pl.when(j == 0)
    def _():
        o_ref[...] = part

    @pl.when(j > 0)
    def _():
        o_ref[...] += part

    @pl.when(j == pl.num_programs(1) - 1)
    def _():
        h = h_ref[...] + gt_ref[...] * o_ref[...]
        o_ref[...] = _rms(h, gfin_ref[...])


def _ffn(hn, h, w_a, w_b, conv_w, conv_b, w_down, gt2, g_final, tm=1024):
    s, d = h.shape
    nf, _, tn = w_a.shape
    halo_blocks = tm // 8
    return pl.pallas_call(
        _ffn_kernel,
        grid=(s // tm, nf),
        in_specs=[pl.BlockSpec((tm, d), lambda i, j: (i, 0)),
                  pl.BlockSpec((8, d), lambda i, j: (jnp.maximum(i * halo_blocks - 1, 0), 0)),
                  pl.BlockSpec((tm, d), lambda i, j: (i, 0), pipeline_mode=pl.Buffered(1)),
                  pl.BlockSpec((1, d, tn), lambda i, j: (j, 0, 0)),
                  pl.BlockSpec((1, d, tn), lambda i, j: (j, 0, 0)),
                  pl.BlockSpec((3, tn), lambda i, j: (0, j)),
                  pl.BlockSpec((1, tn), lambda i, j: (0, j)),
                  pl.BlockSpec((tn, d), lambda i, j: (j, 0)),
                  _resident((1, d)), _resident((1, d))],
        out_specs=pl.BlockSpec((tm, d), lambda i, j: (i, 0)),
        out_shape=jax.ShapeDtypeStruct((s, d), F32),
        compiler_params=_cparams(("arbitrary", "arbitrary"), 56),
        name="conv_ffn",
    )(hn, hn, h, w_a, w_b, conv_w, conv_b, w_down, gt2, g_final)


FFN_TN = 256


def _layer(h, mod, g_mix, w_in, b_f, a_re, a_im, log_dt, ssm_b_re, ssm_b_im, ssm_c_re, ssm_c_im,
           ssm_d, w_glu, b_glu, g_attn_out, g_ssm_out, w_out, g_ffn, w_up, conv_w, conv_b, w_down):
    s, d = h.shape
    aw = N_HEADS * HEAD_DIM
    d_ff = w_down.shape[0]
    sh1, sc1, gt1, sh2, sc2, gt2 = [mod[:, i * d:(i + 1) * d] for i in range(N_MOD)]
    row = lambda a: a.reshape(1, -1)

    wq, wk, wv = (w_in[:, i * aw:(i + 1) * aw].astype(BF16) for i in range(3))
    wf = jnp.pad(w_in[:, 3 * aw:3 * aw + N_HEADS], ((0, 0), (0, LANES - N_HEADS))).astype(BF16)
    wu = w_in[:, 3 * aw + N_HEADS:].astype(BF16)
    q, k, v, u, f = _inproj(h, row(g_mix), sc1, sh1, wq, wk, wv, wu, wf)

    cum = _forget_cumsum(f[:, :N_HEADS].T, b_f)
    attn = _attention(q, k, v, cum.reshape(N_HEADS, s, 1), cum.reshape(N_HEADS, 1, s))

    y = _s5(u, a_re, a_im, log_dt, ssm_b_re, ssm_b_im, ssm_c_re, ssm_c_im, ssm_d)

    h1, hn2 = _mixout(h, attn, y, w_glu.astype(BF16), row(b_glu), row(g_attn_out),
                      row(g_ssm_out), w_out[:aw].astype(BF16), w_out[aw:].astype(BF16),
                      gt1, row(g_ffn), sc2, sh2)

    tiles = lambda w: w.reshape(d, d_ff // FFN_TN, FFN_TN).transpose(1, 0, 2).astype(BF16)
    return hn2, h1, (tiles(w_up[:, :d_ff]), tiles(w_up[:, d_ff:]), conv_w, row(conv_b),
                     w_down.astype(BF16), gt2)


def kernel(x, c, w_ada, b_ada, g_mix, w_in, b_f, a_re, a_im, log_dt, ssm_b_re, ssm_b_im, ssm_c_re,
           ssm_c_im, ssm_d, w_glu, b_glu, g_attn_out, g_ssm_out, w_out, g_ffn, w_up, conv_w,
           conv_b, w_down, g_final):
    batch, s, d = x.shape
    assert w_ada.shape[0] == 1, "only DEPTH == 1 is supported"
    l = 0
    outs = []
    for bi in range(batch):
        mod = _adaln(c[bi:bi + 1], w_ada[l], b_ada[l])
        hn2, h1, ffn_args = _layer(
            x[bi], mod, g_mix[l], w_in[l], b_f[l], a_re[l], a_im[l], log_dt[l], ssm_b_re[l],
            ssm_b_im[l], ssm_c_re[l], ssm_c_im[l], ssm_d[l], w_glu[l], b_glu[l],
            g_attn_out[l], g_ssm_out[l], w_out[l], g_ffn[l], w_up[l], conv_w[l], conv_b[l],
            w_down[l])
        outs.append(_ffn(hn2, h1, *ffn_args, g_final.reshape(1, d)))
    return jnp.stack(outs, axis=0)
```

```python
import functools
import math

import jax
import jax.numpy as jnp
from jax import lax
from jax.experimental import pallas as pl
from jax.experimental.pallas import tpu as pltpu

F32 = jnp.float32
BF16 = jnp.bfloat16

EPS = 1e-6
HEAD_DIM = 128
N_HEADS = 8
SSM_GROUP = 16
SSM_STATE = 64
N_MOD = 6
LANES = 128
SSM_T = 16
SSM_TW = SSM_T * SSM_GROUP
SSM_PAIRS = SSM_T // 2
SSM_TILE_G = LANES // SSM_GROUP
LOG2E = 1.4426950408889634
SKIP_LOG2 = 150.0
NORM_SLACK = 1.01

_MIB = 1024 * 1024


def _cparams(semantics, vmem_mib):
    return pltpu.CompilerParams(dimension_semantics=semantics, vmem_limit_bytes=vmem_mib * _MIB)


def _resident(shape):
    return pl.BlockSpec(shape, lambda *_: (0,) * len(shape), pipeline_mode=pl.Buffered(1))


def _dot(a, b):
    return jnp.dot(a, b, preferred_element_type=F32)


def _dot_nt(a, b):
    return lax.dot_general(a, b, (((1,), (1,)), ((), ())), preferred_element_type=F32)


def _rms(x, g):
    return x * lax.rsqrt(jnp.mean(x * x, axis=-1, keepdims=True) + EPS) * g


def _lane_tile(x, reps):
    return jnp.concatenate([x] * reps, axis=1)


def _adaln_kernel(c_ref, w_ref, b_ref, o_ref):
    c = c_ref[...]
    cond = c * jax.nn.sigmoid(c)
    cond8 = jnp.broadcast_to(cond, (8, c.shape[1])).astype(BF16)
    acc = _dot(cond8, w_ref[...].astype(BF16))
    o_ref[...] = acc[0:1, :] + b_ref[...]


def _adaln(c, w, b, tn=1536):
    d, n = w.shape
    return pl.pallas_call(
        _adaln_kernel,
        grid=(n // tn,),
        in_specs=[pl.BlockSpec((1, d), lambda j: (0, 0)),
                  pl.BlockSpec((d, tn), lambda j: (0, j)),
                  pl.BlockSpec((1, tn), lambda j: (0, j))],
        out_specs=pl.BlockSpec((1, tn), lambda j: (0, j)),
        out_shape=jax.ShapeDtypeStruct((1, n), F32),
        compiler_params=_cparams(("arbitrary",), 48),
        name="adaln",
    )(c, w, b.reshape(1, n))


def _inproj_kernel(x_ref, g_ref, sc_ref, sh_ref, wq_ref, wk_ref, wv_ref, wu_ref, wf_ref,
                   q_ref, k_ref, v_ref, u_ref, f_ref, *, q_scale):
    hn = (_rms(x_ref[...], g_ref[...]) * (1.0 + sc_ref[...]) + sh_ref[...]).astype(BF16)
    f_ref[...] = _dot(hn, wf_ref[...])
    q_ref[...] = (_dot(hn, wq_ref[...]) * q_scale).astype(BF16)
    k_ref[...] = _dot(hn, wk_ref[...]).astype(BF16)
    u_ref[...] = _dot(hn, wu_ref[...])
    v = _dot(hn, wv_ref[...]).astype(BF16)
    ones = jnp.ones((v.shape[0], HEAD_DIM), BF16)
    for h in range(N_HEADS):
        v_ref[:, 2 * h * HEAD_DIM:(2 * h + 1) * HEAD_DIM] = v[:, h * HEAD_DIM:(h + 1) * HEAD_DIM]
        v_ref[:, (2 * h + 1) * HEAD_DIM:(2 * h + 2) * HEAD_DIM] = ones


def _inproj(x, g, sc, sh, wq, wk, wv, wu, wf, tm=512):
    s, d = x.shape
    aw, sw = wq.shape[1], wu.shape[1]
    rows = lambda c: pl.BlockSpec((tm, c), lambda i: (i, 0))
    return pl.pallas_call(
        functools.partial(_inproj_kernel, q_scale=HEAD_DIM ** -0.5 * LOG2E),
        grid=(s // tm,),
        in_specs=[rows(d), _resident((1, d)), _resident((1, d)), _resident((1, d)),
                  _resident((d, aw)), _resident((d, aw)), _resident((d, aw)), _resident((d, sw)),
                  _resident((d, LANES))],
        out_specs=[rows(aw), rows(aw), rows(2 * aw), rows(sw), rows(LANES)],
        out_shape=[jax.ShapeDtypeStruct((s, aw), BF16), jax.ShapeDtypeStruct((s, aw), BF16),
                   jax.ShapeDtypeStruct((s, 2 * aw), BF16), jax.ShapeDtypeStruct((s, sw), F32),
                   jax.ShapeDtypeStruct((s, LANES), F32)],
        compiler_params=_cparams(("arbitrary",), 56),
        name="inproj",
    )(x, g, sc, sh, wq, wk, wv, wu, wf)


def _log2_forget_cumsum(z, axis):
    x = jnp.minimum(z, 0.0) - jnp.log1p(jnp.exp(-jnp.abs(z)))
    pos = lax.broadcasted_iota(jnp.int32, x.shape, axis)
    shift = 1
    while shift < x.shape[axis]:
        x = x + jnp.where(pos >= shift, pltpu.roll(x, shift, axis=axis), 0.0)
        shift *= 2
    return x * LOG2E


def _cum_kernel(ft_ref, bcol_ref, f_ref, brow_ref, row_ref, col_ref):
    row_ref[...] = _log2_forget_cumsum(ft_ref[...] + bcol_ref[...], 1)
    col_ref[...] = _log2_forget_cumsum(f_ref[...] + brow_ref[...], 0)


def _forget_cumsum(f, b_f):
    s, w = f.shape
    h = b_f.shape[0]
    return pl.pallas_call(
        _cum_kernel,
        out_shape=[jax.ShapeDtypeStruct((h, s), F32), jax.ShapeDtypeStruct((s, w), F32)],
        compiler_params=_cparams(None, 48),
        name="forget_cumsum",
    )(f[:, :h].T, b_f.reshape(h, 1), f, jnp.pad(b_f, (0, w - h)).reshape(1, w))


def _attn_kernel(q_ref, k_ref, v_ref, cq_ref, ck_ref, o_ref, m_ref, acc_ref, kmax_ref, *,
                 tq, norm_chunk):
    i = pl.program_id(1)
    s_len = k_ref.shape[0]
    reps = tq // LANES

    @pl.when(i == 0)
    def _():
        ones = jnp.ones((HEAD_DIM, LANES), BF16)

        def norm_body(r, mx):
            kb = k_ref[pl.ds(pl.multiple_of(r * norm_chunk, norm_chunk), norm_chunk), :]
            kb = kb.astype(F32)
            row_sq = _dot((kb * kb).astype(BF16), ones)
            return jnp.maximum(mx, jnp.max(row_sq, axis=0, keepdims=True))

        mx = lax.fori_loop(0, s_len // norm_chunk, norm_body, jnp.zeros((1, LANES), F32))
        kmax_ref[...] = jnp.sqrt(mx) * NORM_SLACK

    head_lane = lax.broadcasted_iota(jnp.int32, (tq, LANES), 1) == pl.program_id(0)
    cqb = jnp.broadcast_to(
        jnp.sum(jnp.where(head_lane, cq_ref[...], 0.0), axis=1, keepdims=True), (tq, LANES))

    half = tq // 2
    halves = [slice(0, half), slice(half, tq)]

    base = pl.multiple_of(i * tq, tq)
    bound = None
    for hh, rs in enumerate(halves):
        nk = (hh + 1) * half
        qh = q_ref[rs, :]
        t = _dot_nt(qh, k_ref[pl.ds(base, nk), :]) - ck_ref[0, :, pl.ds(base, nk)]
        row = lax.broadcasted_iota(jnp.int32, t.shape, 0)
        col = lax.broadcasted_iota(jnp.int32, t.shape, 1)
        t = jnp.where(col <= row + hh * half, t, -jnp.inf)
        m0 = jnp.max(t, axis=1, keepdims=True) + cqb[rs]
        p = jnp.exp2(t - _lane_tile(m0 - cqb[rs], nk // LANES))
        m_ref[rs, :] = m0
        acc_ref[rs, :] = _dot(p.astype(BF16), v_ref[pl.ds(base, nk), :])
        qf = qh.astype(F32)
        qn = jnp.sqrt(jnp.sum(qf * qf, axis=1, keepdims=True))
        b_half = jnp.max(qn * kmax_ref[...] - (m0 - cqb[rs]))
        bound = b_half if bound is None else jnp.maximum(bound, b_half)

    ck_all = ck_ref[0]
    pos = lax.broadcasted_iota(jnp.int32, ck_all.shape, 1)
    dead = jnp.where((pos < i * tq) & (ck_all > bound + SKIP_LOG2), 1.0, 0.0)
    j0 = jnp.sum(dead).astype(jnp.int32) // tq

    def body(j, carry):
        ks = pl.multiple_of(j * tq, tq)
        kb = k_ref[pl.ds(ks, tq), :]
        vb = v_ref[pl.ds(ks, tq), :]
        ck = ck_ref[0, :, pl.ds(ks, tq)]
        for rs in halves:
            t = _dot_nt(q_ref[rs, :], kb) - ck
            m_prev = m_ref[rs, :]
            m_new = jnp.maximum(m_prev, jnp.max(t, axis=1, keepdims=True) + cqb[rs])
            alpha = jnp.exp2(m_prev - m_new)
            p = jnp.exp2(t - _lane_tile(m_new - cqb[rs], reps))
            acc_ref[rs, :] = _lane_tile(alpha, 2) * acc_ref[rs, :] + _dot(p.astype(BF16), vb)
            m_ref[rs, :] = m_new
        return carry

    lax.fori_loop(j0, i, body, 0)
    acc = acc_ref[...]
    o_ref[...] = (acc[:, :HEAD_DIM] / acc[:, HEAD_DIM:]).astype(BF16)


def _attention(q, k, v_ones, cum_col, cum_row, tq=512):
    s = q.shape[0]
    return pl.pallas_call(
        functools.partial(_attn_kernel, tq=tq, norm_chunk=min(s, 1024)),
        grid=(N_HEADS, s // tq),
        in_specs=[pl.BlockSpec((tq, HEAD_DIM), lambda h, i: (i, h)),
                  pl.BlockSpec((s, HEAD_DIM), lambda h, i: (0, h)),
                  pl.BlockSpec((s, 2 * HEAD_DIM), lambda h, i: (0, h)),
                  pl.BlockSpec((tq, LANES), lambda h, i: (i, 0)),
                  pl.BlockSpec((1, 1, s), lambda h, i: (h, 0, 0))],
        out_specs=pl.BlockSpec((tq, HEAD_DIM), lambda h, i: (i, h)),
        out_shape=jax.ShapeDtypeStruct((s, N_HEADS * HEAD_DIM), BF16),
        scratch_shapes=[pltpu.VMEM((tq, LANES), F32), pltpu.VMEM((tq, 2 * HEAD_DIM), F32),
                        pltpu.VMEM((1, LANES), F32)],
        compiler_params=_cparams(("arbitrary", "arbitrary"), 40),
        name="fox_attention",
    )(q, k, v_ones, cum_col, cum_row)


def _cmul(are, aim, bre, bim):
    return are * bre - aim * bim, are * bim + aim * bre


def _step_pair(u_ref, a, n_chunks):
    x0 = u_ref[pl.ds(2 * a, n_chunks, stride=SSM_T), :]
    x1 = u_ref[pl.ds(2 * a + 1, n_chunks, stride=SSM_T), :]
    return jnp.concatenate([x0, x1], axis=1).astype(BF16)


def _ssm_in_kernel(u_ref, ldt_ref, are_ref, aim_ref, btr_ref, bti_ref, cr_ref, ci_ref, d_ref,
                   w2_ref, pmt2_ref, x0_ref,
                   cp_ref, wl_ref, q2_ref, v_ref):
    n_chunks = u_ref.shape[0] // SSM_T
    half_w = SSM_TILE_G * SSM_STATE
    pw_parts = []
    lane = lax.broadcasted_iota(jnp.int32, (SSM_GROUP, LANES), 1)
    first = lane < SSM_STATE

    @pl.when(pl.program_id(0) == 0)
    def _():
        q2_ref[...] = jnp.zeros(q2_ref.shape, BF16)

    pmt2_ref[...] = jnp.zeros(pmt2_ref.shape, BF16)

    def place(ref, lead, step, gi, re_part, im_part):
        r0 = (step % 2) * LANES + gi * SSM_GROUP
        rows = slice(r0, r0 + SSM_GROUP)
        mine = first if gi % 2 == 0 else jnp.logical_not(first)
        c_re = (gi // 2) * LANES
        c_im = half_w + c_re
        ref[lead + (step // 2, rows, slice(c_re, c_re + LANES))] = (
            jnp.where(mine, re_part, 0.0).astype(BF16))
        ref[lead + (step // 2, rows, slice(c_im, c_im + LANES))] = (
            jnp.where(mine, im_part, 0.0).astype(BF16))

    for gi in range(SSM_TILE_G):
        dt = jnp.exp(ldt_ref[gi])
        are, aim = are_ref[gi], aim_ref[gi]
        mag = jnp.exp(dt * are)
        abre, abim = mag * jnp.cos(dt * aim), mag * jnp.sin(dt * aim)
        nre, nim = abre - 1.0, abim
        den = are * are + aim * aim
        zre = (nre * are + nim * aim) / den
        zim = (nim * are - nre * aim) / den
        bbre, bbim = _cmul(zre, zim, btr_ref[gi], bti_ref[gi])
        bbcat = jnp.where(first, bbre, bbim)

        cpre, cpim = cr_ref[gi], ci_ref[gi]
        qre, qim = bbre, bbim
        pwre, pwim = jnp.ones_like(abre), jnp.zeros_like(abim)
        for t in range(SSM_T):
            cp_ref[t * SSM_GROUP:(t + 1) * SSM_GROUP, :] = jnp.where(first, cpre, -cpim)
            if t > 0:
                place(pmt2_ref, (0,), t - 1, gi, cpre, -cpim)
            place(q2_ref, (), SSM_T - 1 - t, gi, qre, qim)
            cpre, cpim = _cmul(cpre, cpim, abre, abim)
            qre, qim = _cmul(qre, qim, abre, abim)
            pwre, pwim = _cmul(pwre, pwim, abre, abim)
        place(pmt2_ref, (0,), SSM_T - 1, gi, cpre, -cpim)
        pw_parts.append((pwre, pwim))

        krow = lax.dot_general(bbcat, cp_ref[...], (((1,), (1,)), ((), ())),
                               preferred_element_type=F32, precision=lax.Precision.HIGHEST)
        own = (lane >= gi * SSM_GROUP) & (lane < (gi + 1) * SSM_GROUP)
        for tau in range(SSM_T):
            half = krow[:, (tau // SSM_TILE_G) * LANES:(tau // SSM_TILE_G + 1) * LANES]
            shift = ((gi - tau % SSM_TILE_G) * SSM_GROUP) % LANES
            moved = half if shift == 0 else pltpu.roll(half, shift, axis=1)
            wl_ref[tau, gi * SSM_GROUP:(gi + 1) * SSM_GROUP, :] = jnp.where(own, moved, 0.0)

    r = lax.broadcasted_iota(jnp.int32, (LANES, LANES), 0)
    c = lax.broadcasted_iota(jnp.int32, (LANES, LANES), 1)
    wl_ref[0] = wl_ref[0] + jnp.where(r == c, d_ref[0], 0.0)
    for dl in range(SSM_PAIRS):
        diag = wl_ref[2 * dl].astype(BF16)
        w2_ref[0, dl, 0:LANES, 0:LANES] = diag
        w2_ref[0, dl, LANES:, LANES:] = diag
        w2_ref[0, dl, 0:LANES, LANES:] = wl_ref[2 * dl + 1].astype(BF16)
        below = jnp.zeros((LANES, LANES), BF16) if dl == 0 else wl_ref[2 * dl - 1].astype(BF16)
        w2_ref[0, dl, LANES:, 0:LANES] = below

    v = _dot(_step_pair(u_ref, 0, n_chunks), q2_ref[0])
    for a in range(1, SSM_PAIRS):
        v = v + _dot(_step_pair(u_ref, a, n_chunks), q2_ref[a])
    v_ref[...] = v

    pair = lambda k, part: jnp.where(first[0:1], pw_parts[2 * k][part], pw_parts[2 * k + 1][part])
    ar = jnp.concatenate([pair(k, 0) for k in range(SSM_TILE_G // 2)], axis=1)
    ai = jnp.concatenate([pair(k, 1) for k in range(SSM_TILE_G // 2)], axis=1)

    def scan_body(c, carry):
        xre, xim = carry
        x0_ref[pl.ds(c, 1), 0:half_w] = xre
        x0_ref[pl.ds(c, 1), half_w:] = xim
        vre = v_ref[pl.ds(c, 1), 0:half_w]
        vim = v_ref[pl.ds(c, 1), half_w:]
        return ar * xre - ai * xim + vre, ar * xim + ai * xre + vim

    zero = jnp.zeros(ar.shape, F32)
    lax.fori_loop(0, n_chunks, scan_body, (zero, zero), unroll=8)


def _ssm_in(u, ldt, are2, aim2, bt_re2, bt_im2, c_re2, c_im2, d_rows):
    s, width = u.shape
    n_tiles = width // LANES
    c = s // SSM_T
    kw = SSM_TILE_G * LANES
    per_tile = lambda *shape: pl.BlockSpec((SSM_TILE_G,) + shape,
                                           lambda j: (j,) + (0,) * len(shape))
    return pl.pallas_call(
        _ssm_in_kernel,
        grid=(n_tiles,),
        in_specs=[pl.BlockSpec((s, LANES), lambda j: (0, j)),
                  per_tile(1, LANES), per_tile(1, LANES), per_tile(1, LANES),
                  per_tile(SSM_GROUP, LANES), per_tile(SSM_GROUP, LANES),
                  per_tile(SSM_GROUP, LANES), per_tile(SSM_GROUP, LANES),
                  pl.BlockSpec((1, 1, LANES), lambda j: (j, 0, 0))],
        out_specs=[pl.BlockSpec((1, SSM_PAIRS, 2 * LANES, 2 * LANES), lambda j: (j, 0, 0, 0)),
                   pl.BlockSpec((1, SSM_PAIRS, 2 * LANES, kw), lambda j: (j, 0, 0, 0)),
                   pl.BlockSpec((c, kw), lambda j: (0, j))],
        out_shape=[jax.ShapeDtypeStruct((n_tiles, SSM_PAIRS, 2 * LANES, 2 * LANES), BF16),
                   jax.ShapeDtypeStruct((n_tiles, SSM_PAIRS, 2 * LANES, kw), BF16),
                   jax.ShapeDtypeStruct((c, n_tiles * kw), F32)],
        scratch_shapes=[pltpu.VMEM((SSM_TW, LANES), F32),
                        pltpu.VMEM((SSM_T, LANES, LANES), F32),
                        pltpu.VMEM((SSM_PAIRS, 2 * LANES, kw), BF16),
                        pltpu.VMEM((c, kw), F32)],
        compiler_params=_cparams(("arbitrary",), 48),
        name="ssm_state_in",
    )(u, ldt, are2, aim2, bt_re2, bt_im2, c_re2, c_im2, d_rows)


def _ssm_out_kernel(u_ref, w2_ref, pmt2_ref, x0_ref, y_ref):
    n_chunks = u_ref.shape[0] // SSM_T
    x0 = x0_ref[...].astype(BF16)
    pairs = [_step_pair(u_ref, a, n_chunks) for a in range(SSM_PAIRS)]
    for b in range(SSM_PAIRS):
        acc = _dot_nt(x0, pmt2_ref[0, b])
        for a in range(b + 1):
            acc = acc + _dot(pairs[a], w2_ref[0, b - a])
        y_ref[pl.ds(2 * b, n_chunks, stride=SSM_T), :] = acc[:, :LANES]
        y_ref[pl.ds(2 * b + 1, n_chunks, stride=SSM_T), :] = acc[:, LANES:]


def _ssm_out(u, w2, pmt2, x0):
    s, width = u.shape
    n_tiles = width // LANES
    c = s // SSM_T
    kw = SSM_TILE_G * LANES
    return pl.pallas_call(
        _ssm_out_kernel,
        grid=(n_tiles,),
        in_specs=[pl.BlockSpec((s, LANES), lambda j: (0, j)),
                  pl.BlockSpec((1, SSM_PAIRS, 2 * LANES, 2 * LANES), lambda j: (j, 0, 0, 0)),
                  pl.BlockSpec((1, SSM_PAIRS, 2 * LANES, kw), lambda j: (j, 0, 0, 0)),
                  pl.BlockSpec((c, kw), lambda j: (0, j))],
        out_specs=pl.BlockSpec((s, LANES), lambda j: (0, j)),
        out_shape=jax.ShapeDtypeStruct((s, width), F32),
        compiler_params=_cparams(("arbitrary",), 48),
        name="ssm_out",
    )(u, w2, pmt2, x0)


def _s5(u, a_re, a_im, log_dt, b_re, b_im, c_re, c_im, d_skip):
    g, p = a_re.shape
    dup = lambda a: jnp.concatenate([a, a], axis=-1)
    ldt = jnp.broadcast_to(log_dt.reshape(g, 1, 1), (g, 1, LANES))
    are2, aim2 = dup(a_re).reshape(g, 1, LANES), dup(a_im).reshape(g, 1, LANES)
    bt_re2 = dup(jnp.swapaxes(b_re, 1, 2))
    bt_im2 = dup(jnp.swapaxes(b_im, 1, 2))
    w2, pmt2, x0 = _ssm_in(u, ldt, are2, aim2, bt_re2, bt_im2, dup(c_re), dup(c_im),
                           d_skip.reshape(-1, 1, LANES))
    return _ssm_out(u, w2, pmt2, x0)


def _gelu_tanh(x):
    return 0.5 * x * (1.0 + jnp.tanh(math.sqrt(2.0 / math.pi) * (x + 0.044715 * (x * x * x))))


def _mixout_kernel(x_ref, attn_ref, y_ref, wglu_ref, bglu_ref, ga_ref, gs_ref, wo_ref,
                   gt_ref, gf_ref, sc_ref, sh_ref, h_ref, hn_ref):
    aw = attn_ref.shape[1]
    y = _gelu_tanh(y_ref[...])
    gate = jax.nn.sigmoid(_dot(y.astype(BF16), wglu_ref[...]) + bglu_ref[...])
    ns = _rms(y * gate, gs_ref[...]).astype(BF16)
    na = _rms(attn_ref[...].astype(F32), ga_ref[...]).astype(BF16)
    mixed = _dot(na, wo_ref[0:aw, :]) + _dot(ns, wo_ref[aw:, :])
    h = x_ref[...] + gt_ref[...] * mixed
    h_ref[...] = h
    hn_ref[...] = (_rms(h, gf_ref[...]) * (1.0 + sc_ref[...]) + sh_ref[...]).astype(BF16)


def _mixout(x, attn, y, w_glu, b_glu, g_attn, g_ssm, w_out, gt1, g_ffn, sc2, sh2, tm=512):
    s, d = x.shape
    w = attn.shape[1]
    rows = lambda c: pl.BlockSpec((tm, c), lambda i: (i, 0))
    return pl.pallas_call(
        _mixout_kernel,
        grid=(s // tm,),
        in_specs=[rows(d), rows(w), rows(w), _resident((w, w)), _resident((1, w)),
                  _resident((1, w)), _resident((1, w)), _resident(w_out.shape),
                  _resident((1, d)), _resident((1, d)), _resident((1, d)), _resident((1, d))],
        out_specs=[rows(d), rows(d)],
        out_shape=[jax.ShapeDtypeStruct((s, d), F32), jax.ShapeDtypeStruct((s, d), BF16)],
        compiler_params=_cparams(("arbitrary",), 56),
        name="mixer_out",
    )(x, attn, y, w_glu, b_glu, g_attn, g_ssm, w_out, gt1, g_ffn, sc2, sh2)


def _ffn_kernel(hn_ref, halo_ref, h_ref, wa_ref, wb_ref, cw_ref, cb_ref, wd_ref, gt_ref, gfin_ref,
                o_ref):
    i, j = pl.program_id(0), pl.program_id(1)

    @pl.when(j == 0)
    def _():
        o_ref[...] = jnp.zeros(o_ref.shape, F32)

    hn = hn_ref[...]
    a = _dot(hn, wa_ref[...])
    b = _dot(hn, wb_ref[...])
    halo = _dot(halo_ref[...], wa_ref[...]) * (i > 0).astype(F32)
    row = lax.broadcasted_iota(jnp.int32, a.shape, 0)
    prev1 = jnp.where(row == 0, halo[7:8, :], pltpu.roll(a, 1, axis=0))
    prev2 = jnp.where(row == 0, halo[6:7, :],
                      jnp.where(row == 1, halo[7:8, :], pltpu.roll(a, 2, axis=0)))
    cw = cw_ref[...]
    conv = cb_ref[...] + cw[0:1, :] * prev2 + cw[1:2, :] * prev1 + cw[2:3, :] * a
    act = (conv * jax.nn.sigmoid(conv) * b).astype(BF16)
    o_ref[...] += _dot(act, wd_ref[...])

    @pl.when(j == pl.num_programs(1) - 1)
    def _():
        h = h_ref[...] + gt_ref[...] * o_ref[...]
        o_ref[...] = _rms(h, gfin_ref[...])


def _ffn(hn, h, w_up, conv_w, conv_b, w_down, gt2, g_final, tm=1024, tn=512):
    s, d = h.shape
    d_ff = w_down.shape[0]
    nf = d_ff // tn
    halo_blocks = tm // 8
    return pl.pallas_call(
        _ffn_kernel,
        grid=(s // tm, nf),
        in_specs=[pl.BlockSpec((tm, d), lambda i, j: (i, 0)),
                  pl.BlockSpec((8, d), lambda i, j: (jnp.maximum(i * halo_blocks - 1, 0), 0)),
                  pl.BlockSpec((tm, d), lambda i, j: (i, 0), pipeline_mode=pl.Buffered(1)),
                  pl.BlockSpec((d, tn), lambda i, j: (0, j)),
                  pl.BlockSpec((d, tn), lambda i, j: (0, nf + j)),
                  pl.BlockSpec((3, tn), lambda i, j: (0, j)),
                  pl.BlockSpec((1, tn), lambda i, j: (0, j)),
                  pl.BlockSpec((tn, d), lambda i, j: (j, 0)),
                  _resident((1, d)), _resident((1, d))],
        out_specs=pl.BlockSpec((tm, d), lambda i, j: (i, 0)),
        out_shape=jax.ShapeDtypeStruct((s, d), F32),
        compiler_params=_cparams(("arbitrary", "arbitrary"), 62),
        name="conv_ffn",
    )(hn, hn, h, w_up, w_up, conv_w, conv_b, w_down, gt2, g_final)


def _layer(h, mod, g_mix, w_in, b_f, a_re, a_im, log_dt, ssm_b_re, ssm_b_im, ssm_c_re, ssm_c_im,
           ssm_d, w_glu, b_glu, g_attn_out, g_ssm_out, w_out, g_ffn, w_up, conv_w, conv_b, w_down):
    s, d = h.shape
    aw = N_HEADS * HEAD_DIM
    sh1, sc1, gt1, sh2, sc2, gt2 = [mod[:, i * d:(i + 1) * d] for i in range(N_MOD)]
    row = lambda a: a.reshape(1, -1)

    wq, wk, wv = (w_in[:, i * aw:(i + 1) * aw].astype(BF16) for i in range(3))
    wf = jnp.pad(w_in[:, 3 * aw:3 * aw + N_HEADS], ((0, 0), (0, LANES - N_HEADS))).astype(BF16)
    wu = w_in[:, 3 * aw + N_HEADS:].astype(BF16)
    q, k, v, u, f = _inproj(h, row(g_mix), sc1, sh1, wq, wk, wv, wu, wf)

    cum_row, cum_col = _forget_cumsum(f, b_f)
    attn = _attention(q, k, v, cum_col, cum_row.reshape(N_HEADS, 1, s))

    y = _s5(u, a_re, a_im, log_dt, ssm_b_re, ssm_b_im, ssm_c_re, ssm_c_im, ssm_d)

    h1, hn2 = _mixout(h, attn, y, w_glu.astype(BF16), row(b_glu), row(g_attn_out),
                      row(g_ssm_out), w_out.astype(BF16), gt1, row(g_ffn), sc2, sh2)
    return hn2, h1, (w_up.astype(BF16), conv_w, row(conv_b), w_down.astype(BF16), gt2)


def kernel(x, c, w_ada, b_ada, g_mix, w_in, b_f, a_re, a_im, log_dt, ssm_b_re, ssm_b_im, ssm_c_re,
           ssm_c_im, ssm_d, w_glu, b_glu, g_attn_out, g_ssm_out, w_out, g_ffn, w_up, conv_w,
           conv_b, w_down, g_final):
    batch, s, d = x.shape
    assert w_ada.shape[0] == 1, "only DEPTH == 1 is supported"
    l = 0
    outs = []
    for bi in range(batch):
        mod = _adaln(c[bi:bi + 1], w_ada[l], b_ada[l])
        hn2, h1, ffn_args = _layer(
            x[bi], mod, g_mix[l], w_in[l], b_f[l], a_re[l], a_im[l], log_dt[l], ssm_b_re[l],
            ssm_b_im[l], ssm_c_re[l], ssm_c_im[l], ssm_d[l], w_glu[l], b_glu[l],
            g_attn_out[l], g_ssm_out[l], w_out[l], g_ffn[l], w_up[l], conv_w[l], conv_b[l],
            w_down[l])
        outs.append(_ffn(hn2, h1, *ffn_args, g_final.reshape(1, d)))
    return jnp.stack(outs, axis=0)
```

```python
import functools
import math

import jax
import jax.numpy as jnp
from jax import lax
from jax.experimental import pallas as pl
from jax.experimental.pallas import tpu as pltpu

F32 = jnp.float32
BF16 = jnp.bfloat16

EPS = 1e-6
HEAD_DIM = 128
N_HEADS = 8
SSM_GROUP = 16
SSM_STATE = 64
N_MOD = 6
LANES = 128
SSM_T = 16
SSM_TW = SSM_T * SSM_GROUP
SSM_PAIRS = SSM_T // 2
SSM_TILE_G = LANES // SSM_GROUP
LOG2E = 1.4426950408889634
SKIP_LOG2 = 150.0
NORM_SLACK = 1.01

_MIB = 1024 * 1024


def _cparams(semantics, vmem_mib):
    return pltpu.CompilerParams(dimension_semantics=semantics, vmem_limit_bytes=vmem_mib * _MIB)


def _resident(shape):
    return pl.BlockSpec(shape, lambda *_: (0,) * len(shape), pipeline_mode=pl.Buffered(1))


def _dot(a, b):
    return jnp.dot(a, b, preferred_element_type=F32)


def _dot_nt(a, b):
    return lax.dot_general(a, b, (((1,), (1,)), ((), ())), preferred_element_type=F32)


def _rms(x, g):
    return x * lax.rsqrt(jnp.mean(x * x, axis=-1, keepdims=True) + EPS) * g


def _lane_tile(x, reps):
    return jnp.concatenate([x] * reps, axis=1)


def _adaln_kernel(c_ref, w_ref, b_ref, o_ref):
    c = c_ref[...]
    cond = c * jax.nn.sigmoid(c)
    cond8 = jnp.broadcast_to(cond, (8, c.shape[1])).astype(BF16)
    acc = _dot(cond8, w_ref[...].astype(BF16))
    o_ref[...] = acc[0:1, :] + b_ref[...]


def _adaln(c, w, b, tn=1536):
    d, n = w.shape
    return pl.pallas_call(
        _adaln_kernel,
        grid=(n // tn,),
        in_specs=[pl.BlockSpec((1, d), lambda j: (0, 0)),
                  pl.BlockSpec((d, tn), lambda j: (0, j)),
                  pl.BlockSpec((1, tn), lambda j: (0, j))],
        out_specs=pl.BlockSpec((1, tn), lambda j: (0, j)),
        out_shape=jax.ShapeDtypeStruct((1, n), F32),
        compiler_params=_cparams(("arbitrary",), 48),
        name="adaln",
    )(c, w, b.reshape(1, n))


def _inproj_kernel(x_ref, g_ref, sc_ref, sh_ref, wq_ref, wk_ref, wv_ref, wu_ref, wf_ref,
                   q_ref, k_ref, v_ref, u_ref, f_ref, *, q_scale):
    hn = (_rms(x_ref[...], g_ref[...]) * (1.0 + sc_ref[...]) + sh_ref[...]).astype(BF16)
    f_ref[...] = _dot(hn, wf_ref[...])
    q_ref[...] = (_dot(hn, wq_ref[...]) * q_scale).astype(BF16)
    k_ref[...] = _dot(hn, wk_ref[...]).astype(BF16)
    u_ref[...] = _dot(hn, wu_ref[...])
    v = _dot(hn, wv_ref[...]).astype(BF16)
    ones = jnp.ones((v.shape[0], HEAD_DIM), BF16)
    for h in range(N_HEADS):
        v_ref[:, 2 * h * HEAD_DIM:(2 * h + 1) * HEAD_DIM] = v[:, h * HEAD_DIM:(h + 1) * HEAD_DIM]
        v_ref[:, (2 * h + 1) * HEAD_DIM:(2 * h + 2) * HEAD_DIM] = ones


def _inproj(x, g, sc, sh, w_in, wu, tm=512):
    s, d = x.shape
    aw, sw = N_HEADS * HEAD_DIM, wu.shape[1]
    rows = lambda c: pl.BlockSpec((tm, c), lambda i: (i, 0))
    cols = lambda width, blk: pl.BlockSpec((d, width), lambda i: (0, blk),
                                           pipeline_mode=pl.Buffered(1))
    return pl.pallas_call(
        functools.partial(_inproj_kernel, q_scale=HEAD_DIM ** -0.5 * LOG2E),
        grid=(s // tm,),
        in_specs=[rows(d), _resident((1, d)), _resident((1, d)), _resident((1, d)),
                  cols(aw, 0), cols(aw, 1), cols(aw, 2), _resident((d, sw)),
                  cols(LANES, 3 * aw // LANES)],
        out_specs=[rows(aw), rows(aw), rows(2 * aw), rows(sw), rows(LANES)],
        out_shape=[jax.ShapeDtypeStruct((s, aw), BF16), jax.ShapeDtypeStruct((s, aw), BF16),
                   jax.ShapeDtypeStruct((s, 2 * aw), BF16), jax.ShapeDtypeStruct((s, sw), F32),
                   jax.ShapeDtypeStruct((s, LANES), F32)],
        compiler_params=_cparams(("arbitrary",), 56),
        name="inproj",
    )(x, g, sc, sh, w_in, w_in, w_in, wu, w_in)


def _log2_forget_cumsum(z, axis):
    x = jnp.minimum(z, 0.0) - jnp.log1p(jnp.exp(-jnp.abs(z)))
    pos = lax.broadcasted_iota(jnp.int32, x.shape, axis)
    shift = 1
    while shift < x.shape[axis]:
        x = x + jnp.where(pos >= shift, pltpu.roll(x, shift, axis=axis), 0.0)
        shift *= 2
    return x * LOG2E


def _cum_kernel(ft_ref, bcol_ref, f_ref, brow_ref, row_ref, col_ref):
    row_ref[...] = _log2_forget_cumsum(ft_ref[...] + bcol_ref[...], 1)
    col_ref[...] = _log2_forget_cumsum(f_ref[...] + brow_ref[...], 0)


def _forget_cumsum(f, b_f):
    s, w = f.shape
    h = b_f.shape[0]
    return pl.pallas_call(
        _cum_kernel,
        out_shape=[jax.ShapeDtypeStruct((h, s), F32), jax.ShapeDtypeStruct((s, w), F32)],
        compiler_params=_cparams(None, 48),
        name="forget_cumsum",
    )(f[:, :h].T, b_f.reshape(h, 1), f, jnp.pad(b_f, (0, w - h)).reshape(1, w))


def _attn_kernel(q_ref, k_ref, v_ref, cq_ref, ck_ref, o_ref, m_ref, acc_ref, kmax_ref, *,
                 tq, near, norm_chunk):
    i = pl.program_id(1)
    s_len = k_ref.shape[0]

    @pl.when(i == 0)
    def _():
        ones = jnp.ones((HEAD_DIM, LANES), BF16)

        def norm_body(r, mx):
            kb = k_ref[pl.ds(pl.multiple_of(r * norm_chunk, norm_chunk), norm_chunk), :]
            kb = kb.astype(F32)
            row_sq = _dot((kb * kb).astype(BF16), ones)
            return jnp.maximum(mx, jnp.max(row_sq, axis=0, keepdims=True))

        mx = lax.fori_loop(0, s_len // norm_chunk, norm_body, jnp.zeros((1, LANES), F32))
        kmax_ref[...] = jnp.sqrt(mx) * NORM_SLACK

    head_lane = lax.broadcasted_iota(jnp.int32, (tq, LANES), 1) == pl.program_id(0)
    cqb = jnp.broadcast_to(
        jnp.sum(jnp.where(head_lane, cq_ref[...], 0.0), axis=1, keepdims=True), (tq, LANES))

    half = tq // 2
    halves = [slice(0, half), slice(half, tq)]

    base = pl.multiple_of(i * tq, tq)
    bound = None
    for hh, rs in enumerate(halves):
        nk = (hh + 1) * half
        qh = q_ref[rs, :]
        t = _dot_nt(qh, k_ref[pl.ds(base, nk), :]) - ck_ref[0, :, pl.ds(base, nk)]
        row = lax.broadcasted_iota(jnp.int32, t.shape, 0)
        col = lax.broadcasted_iota(jnp.int32, t.shape, 1)
        t = jnp.where(col <= row + hh * half, t, -jnp.inf)
        m0 = jnp.max(t, axis=1, keepdims=True) + cqb[rs]
        p = jnp.exp2(t - _lane_tile(m0 - cqb[rs], nk // LANES))
        m_ref[rs, :] = m0
        acc_ref[rs, :] = _dot(p.astype(BF16), v_ref[pl.ds(base, nk), :])
        qf = qh.astype(F32)
        qn = jnp.sqrt(jnp.sum(qf * qf, axis=1, keepdims=True))
        b_half = jnp.max(qn * kmax_ref[...] - (m0 - cqb[rs]))
        bound = b_half if bound is None else jnp.maximum(bound, b_half)

    ck_all = ck_ref[0]
    pos = lax.broadcasted_iota(jnp.int32, ck_all.shape, 1)
    dead = jnp.where((pos < i * tq) & (ck_all > bound + SKIP_LOG2), 1.0, 0.0)
    j0 = jnp.sum(dead).astype(jnp.int32) // tq

    def visit(ks, width, limit=None):
        kb = k_ref[pl.ds(ks, width), :]
        vb = v_ref[pl.ds(ks, width), :]
        ck = ck_ref[0, :, pl.ds(ks, width)]
        if limit is not None:
            key = ks + lax.broadcasted_iota(jnp.int32, ck.shape, 1)
            ck = jnp.where(key < limit, ck, jnp.inf)
        for rs in halves:
            t = _dot_nt(q_ref[rs, :], kb) - ck
            m_prev = m_ref[rs, :]
            m_new = jnp.maximum(m_prev, jnp.max(t, axis=1, keepdims=True) + cqb[rs])
            alpha = jnp.exp2(m_prev - m_new)
            p = jnp.exp2(t - _lane_tile(m_new - cqb[rs], width // LANES))
            acc_ref[rs, :] = _lane_tile(alpha, 2) * acc_ref[rs, :] + _dot(p.astype(BF16), vb)
            m_ref[rs, :] = m_new

    first_near = jnp.maximum(i - near, 0)
    visit(pl.multiple_of(first_near * tq, tq), near * tq, limit=base)

    def body(j, carry):
        visit(pl.multiple_of(j * tq, tq), tq)
        return carry

    lax.fori_loop(j0, first_near, body, 0)
    acc = acc_ref[...]
    o_ref[...] = (acc[:, :HEAD_DIM] / acc[:, HEAD_DIM:]).astype(BF16)


def _attention(q, k, v_ones, cum_col, cum_row, tq=512, near=2):
    s = q.shape[0]
    return pl.pallas_call(
        functools.partial(_attn_kernel, tq=tq, near=near, norm_chunk=min(s, 1024)),
        grid=(N_HEADS, s // tq),
        in_specs=[pl.BlockSpec((tq, HEAD_DIM), lambda h, i: (i, h)),
                  pl.BlockSpec((s, HEAD_DIM), lambda h, i: (0, h)),
                  pl.BlockSpec((s, 2 * HEAD_DIM), lambda h, i: (0, h)),
                  pl.BlockSpec((tq, LANES), lambda h, i: (i, 0)),
                  pl.BlockSpec((1, 1, s), lambda h, i: (h, 0, 0))],
        out_specs=pl.BlockSpec((tq, HEAD_DIM), lambda h, i: (i, h)),
        out_shape=jax.ShapeDtypeStruct((s, N_HEADS * HEAD_DIM), BF16),
        scratch_shapes=[pltpu.VMEM((tq, LANES), F32), pltpu.VMEM((tq, 2 * HEAD_DIM), F32),
                        pltpu.VMEM((1, LANES), F32)],
        compiler_params=_cparams(("arbitrary", "arbitrary"), 40),
        name="fox_attention",
    )(q, k, v_ones, cum_col, cum_row)


def _cmul(are, aim, bre, bim):
    return are * bre - aim * bim, are * bim + aim * bre


def _step_pair(u_ref, a, n_chunks):
    x0 = u_ref[pl.ds(2 * a, n_chunks, stride=SSM_T), :]
    x1 = u_ref[pl.ds(2 * a + 1, n_chunks, stride=SSM_T), :]
    return jnp.concatenate([x0, x1], axis=1).astype(BF16)


def _ssm_in_kernel(u_ref, ldt_ref, are_ref, aim_ref, btr_ref, bti_ref, cr_ref, ci_ref, d_ref,
                   w2_ref, pmt2_ref, x0_ref,
                   cp_ref, wl_ref, q2_ref, v_ref):
    n_chunks = u_ref.shape[0] // SSM_T
    half_w = SSM_TILE_G * SSM_STATE
    pw_parts = []
    lane = lax.broadcasted_iota(jnp.int32, (SSM_GROUP, LANES), 1)
    first = lane < SSM_STATE

    @pl.when(pl.program_id(0) == 0)
    def _():
        q2_ref[...] = jnp.zeros(q2_ref.shape, BF16)

    pmt2_ref[...] = jnp.zeros(pmt2_ref.shape, BF16)

    def place(ref, lead, step, gi, re_part, im_part):
        r0 = (step % 2) * LANES + gi * SSM_GROUP
        rows = slice(r0, r0 + SSM_GROUP)
        mine = first if gi % 2 == 0 else jnp.logical_not(first)
        c_re = (gi // 2) * LANES
        c_im = half_w + c_re
        ref[lead + (step // 2, rows, slice(c_re, c_re + LANES))] = (
            jnp.where(mine, re_part, 0.0).astype(BF16))
        ref[lead + (step // 2, rows, slice(c_im, c_im + LANES))] = (
            jnp.where(mine, im_part, 0.0).astype(BF16))

    for gi in range(SSM_TILE_G):
        dt = jnp.exp(ldt_ref[gi])
        are, aim = are_ref[gi], aim_ref[gi]
        mag = jnp.exp(dt * are)
        abre, abim = mag * jnp.cos(dt * aim), mag * jnp.sin(dt * aim)
        nre, nim = abre - 1.0, abim
        den = are * are + aim * aim
        zre = (nre * are + nim * aim) / den
        zim = (nim * are - nre * aim) / den
        bbre, bbim = _cmul(zre, zim, btr_ref[gi], bti_ref[gi])
        bbcat = jnp.where(first, bbre, bbim)

        cpre, cpim = cr_ref[gi], ci_ref[gi]
        qre, qim = bbre, bbim
        pwre, pwim = jnp.ones_like(abre), jnp.zeros_like(abim)
        for t in range(SSM_T):
            cp_ref[t * SSM_GROUP:(t + 1) * SSM_GROUP, :] = jnp.where(first, cpre, -cpim)
            if t > 0:
                place(pmt2_ref, (0,), t - 1, gi, cpre, -cpim)
            place(q2_ref, (), SSM_T - 1 - t, gi, qre, qim)
            cpre, cpim = _cmul(cpre, cpim, abre, abim)
            qre, qim = _cmul(qre, qim, abre, abim)
            pwre, pwim = _cmul(pwre, pwim, abre, abim)
        place(pmt2_ref, (0,), SSM_T - 1, gi, cpre, -cpim)
        pw_parts.append((pwre, pwim))

        krow = lax.dot_general(bbcat, cp_ref[...], (((1,), (1,)), ((), ())),
                               preferred_element_type=F32, precision=lax.Precision.HIGHEST)
        own = (lane >= gi * SSM_GROUP) & (lane < (gi + 1) * SSM_GROUP)
        for tau in range(SSM_T):
            half = krow[:, (tau // SSM_TILE_G) * LANES:(tau // SSM_TILE_G + 1) * LANES]
            shift = ((gi - tau % SSM_TILE_G) * SSM_GROUP) % LANES
            moved = half if shift == 0 else pltpu.roll(half, shift, axis=1)
            wl_ref[tau, gi * SSM_GROUP:(gi + 1) * SSM_GROUP, :] = jnp.where(own, moved, 0.0)

    r = lax.broadcasted_iota(jnp.int32, (LANES, LANES), 0)
    c = lax.broadcasted_iota(jnp.int32, (LANES, LANES), 1)
    wl_ref[0] = wl_ref[0] + jnp.where(r == c, d_ref[0], 0.0)
    for dl in range(SSM_PAIRS):
        diag = wl_ref[2 * dl].astype(BF16)
        w2_ref[0, dl, 0:LANES, 0:LANES] = diag
        w2_ref[0, dl, LANES:, LANES:] = diag
        w2_ref[0, dl, 0:LANES, LANES:] = wl_ref[2 * dl + 1].astype(BF16)
        below = jnp.zeros((LANES, LANES), BF16) if dl == 0 else wl_ref[2 * dl - 1].astype(BF16)
        w2_ref[0, dl, LANES:, 0:LANES] = below

    v = _dot(_step_pair(u_ref, 0, n_chunks), q2_ref[0])
    for a in range(1, SSM_PAIRS):
        v = v + _dot(_step_pair(u_ref, a, n_chunks), q2_ref[a])
    v_ref[...] = v

    pair = lambda k, part: jnp.where(first[0:1], pw_parts[2 * k][part], pw_parts[2 * k + 1][part])
    ar = jnp.concatenate([pair(k, 0) for k in range(SSM_TILE_G // 2)], axis=1)
    ai = jnp.concatenate([pair(k, 1) for k in range(SSM_TILE_G // 2)], axis=1)

    def scan_body(c, carry):
        xre, xim = carry
        x0_ref[pl.ds(c, 1), 0:half_w] = xre
        x0_ref[pl.ds(c, 1), half_w:] = xim
        vre = v_ref[pl.ds(c, 1), 0:half_w]
        vim = v_ref[pl.ds(c, 1), half_w:]
        return ar * xre - ai * xim + vre, ar * xim + ai * xre + vim

    zero = jnp.zeros(ar.shape, F32)
    lax.fori_loop(0, n_chunks, scan_body, (zero, zero), unroll=8)


def _ssm_in(u, ldt, are2, aim2, bt_re2, bt_im2, c_re2, c_im2, d_rows):
    s, width = u.shape
    n_tiles = width // LANES
    c = s // SSM_T
    kw = SSM_TILE_G * LANES
    per_tile = lambda *shape: pl.BlockSpec((SSM_TILE_G,) + shape,
                                           lambda j: (j,) + (0,) * len(shape))
    return pl.pallas_call(
        _ssm_in_kernel,
        grid=(n_tiles,),
        in_specs=[pl.BlockSpec((s, LANES), lambda j: (0, j)),
                  per_tile(1, LANES), per_tile(1, LANES), per_tile(1, LANES),
                  per_tile(SSM_GROUP, LANES), per_tile(SSM_GROUP, LANES),
                  per_tile(SSM_GROUP, LANES), per_tile(SSM_GROUP, LANES),
                  pl.BlockSpec((1, 1, LANES), lambda j: (j, 0, 0))],
        out_specs=[pl.BlockSpec((1, SSM_PAIRS, 2 * LANES, 2 * LANES), lambda j: (j, 0, 0, 0)),
                   pl.BlockSpec((1, SSM_PAIRS, 2 * LANES, kw), lambda j: (j, 0, 0, 0)),
                   pl.BlockSpec((c, kw), lambda j: (0, j))],
        out_shape=[jax.ShapeDtypeStruct((n_tiles, SSM_PAIRS, 2 * LANES, 2 * LANES), BF16),
                   jax.ShapeDtypeStruct((n_tiles, SSM_PAIRS, 2 * LANES, kw), BF16),
                   jax.ShapeDtypeStruct((c, n_tiles * kw), F32)],
        scratch_shapes=[pltpu.VMEM((SSM_TW, LANES), F32),
                        pltpu.VMEM((SSM_T, LANES, LANES), F32),
                        pltpu.VMEM((SSM_PAIRS, 2 * LANES, kw), BF16),
                        pltpu.VMEM((c, kw), F32)],
        compiler_params=_cparams(("arbitrary",), 48),
        name="ssm_state_in",
    )(u, ldt, are2, aim2, bt_re2, bt_im2, c_re2, c_im2, d_rows)


def _ssm_out_kernel(u_ref, w2_ref, pmt2_ref, x0_ref, y_ref):
    n_chunks = u_ref.shape[0] // SSM_T
    x0 = x0_ref[...].astype(BF16)
    pairs = [_step_pair(u_ref, a, n_chunks) for a in range(SSM_PAIRS)]
    for b in range(SSM_PAIRS):
        acc = _dot_nt(x0, pmt2_ref[0, b])
        for a in range(b + 1):
            acc = acc + _dot(pairs[a], w2_ref[0, b - a])
        y_ref[pl.ds(2 * b, n_chunks, stride=SSM_T), :] = acc[:, :LANES]
        y_ref[pl.ds(2 * b + 1, n_chunks, stride=SSM_T), :] = acc[:, LANES:]


def _ssm_out(u, w2, pmt2, x0):
    s, width = u.shape
    n_tiles = width // LANES
    c = s // SSM_T
    kw = SSM_TILE_G * LANES
    return pl.pallas_call(
        _ssm_out_kernel,
        grid=(n_tiles,),
        in_specs=[pl.BlockSpec((s, LANES), lambda j: (0, j)),
                  pl.BlockSpec((1, SSM_PAIRS, 2 * LANES, 2 * LANES), lambda j: (j, 0, 0, 0)),
                  pl.BlockSpec((1, SSM_PAIRS, 2 * LANES, kw), lambda j: (j, 0, 0, 0)),
                  pl.BlockSpec((c, kw), lambda j: (0, j))],
        out_specs=pl.BlockSpec((s, LANES), lambda j: (0, j)),
        out_shape=jax.ShapeDtypeStruct((s, width), F32),
        compiler_params=_cparams(("arbitrary",), 48),
        name="ssm_out",
    )(u, w2, pmt2, x0)


def _s5(u, a_re, a_im, log_dt, b_re, b_im, c_re, c_im, d_skip):
    g, p = a_re.shape
    dup = lambda a: jnp.concatenate([a, a], axis=-1)
    ldt = jnp.broadcast_to(log_dt.reshape(g, 1, 1), (g, 1, LANES))
    are2, aim2 = dup(a_re).reshape(g, 1, LANES), dup(a_im).reshape(g, 1, LANES)
    bt_re2 = dup(jnp.swapaxes(b_re, 1, 2))
    bt_im2 = dup(jnp.swapaxes(b_im, 1, 2))
    w2, pmt2, x0 = _ssm_in(u, ldt, are2, aim2, bt_re2, bt_im2, dup(c_re), dup(c_im),
                           d_skip.reshape(-1, 1, LANES))
    return _ssm_out(u, w2, pmt2, x0)


def _gelu_tanh(x):
    return 0.5 * x * (1.0 + jnp.tanh(math.sqrt(2.0 / math.pi) * (x + 0.044715 * (x * x * x))))


def _mixout_kernel(x_ref, attn_ref, y_ref, wglu_ref, bglu_ref, ga_ref, gs_ref, wo_ref,
                   gt_ref, gf_ref, sc_ref, sh_ref, h_ref, hn_ref):
    aw = attn_ref.shape[1]
    y = _gelu_tanh(y_ref[...])
    gate = jax.nn.sigmoid(_dot(y.astype(BF16), wglu_ref[...]) + bglu_ref[...])
    ns = _rms(y * gate, gs_ref[...]).astype(BF16)
    na = _rms(attn_ref[...].astype(F32), ga_ref[...]).astype(BF16)
    mixed = _dot(na, wo_ref[0:aw, :]) + _dot(ns, wo_ref[aw:, :])
    h = x_ref[...] + gt_ref[...] * mixed
    h_ref[...] = h
    hn_ref[...] = (_rms(h, gf_ref[...]) * (1.0 + sc_ref[...]) + sh_ref[...]).astype(BF16)


def _mixout(x, attn, y, w_glu, b_glu, g_attn, g_ssm, w_out, gt1, g_ffn, sc2, sh2, tm=512):
    s, d = x.shape
    w = attn.shape[1]
    rows = lambda c: pl.BlockSpec((tm, c), lambda i: (i, 0))
    return pl.pallas_call(
        _mixout_kernel,
        grid=(s // tm,),
        in_specs=[rows(d), rows(w), rows(w), _resident((w, w)), _resident((1, w)),
                  _resident((1, w)), _resident((1, w)), _resident(w_out.shape),
                  _resident((1, d)), _resident((1, d)), _resident((1, d)), _resident((1, d))],
        out_specs=[rows(d), rows(d)],
        out_shape=[jax.ShapeDtypeStruct((s, d), F32), jax.ShapeDtypeStruct((s, d), BF16)],
        compiler_params=_cparams(("arbitrary",), 56),
        name="mixer_out",
    )(x, attn, y, w_glu, b_glu, g_attn, g_ssm, w_out, gt1, g_ffn, sc2, sh2)


def _ffn_kernel(hn_ref, halo_ref, h_ref, wa_ref, wb_ref, cw_ref, cb_ref, wd_ref, gt_ref, gfin_ref,
                o_ref):
    i, j = pl.program_id(0), pl.program_id(1)

    @pl.when(j == 0)
    def _():
        o_ref[...] = jnp.zeros(o_ref.shape, F32)

    hn = hn_ref[...]
    a = _dot(hn, wa_ref[...])
    b = _dot(hn, wb_ref[...])
    halo = _dot(halo_ref[...], wa_ref[...]) * (i > 0).astype(F32)
    row = lax.broadcasted_iota(jnp.int32, a.shape, 0)
    prev1 = jnp.where(row == 0, halo[7:8, :], pltpu.roll(a, 1, axis=0))
    prev2 = jnp.where(row == 0, halo[6:7, :],
                      jnp.where(row == 1, halo[7:8, :], pltpu.roll(a, 2, axis=0)))
    cw = cw_ref[...]
    conv = cb_ref[...] + cw[0:1, :] * prev2 + cw[1:2, :] * prev1 + cw[2:3, :] * a
    act = (conv * jax.nn.sigmoid(conv) * b).astype(BF16)
    o_ref[...] += _dot(act, wd_ref[...])

    @pl.when(j == pl.num_programs(1) - 1)
    def _():
        h = h_ref[...] + gt_ref[...] * o_ref[...]
        o_ref[...] = _rms(h, gfin_ref[...])


def _ffn(hn, h, w_up, conv_w, conv_b, w_down, gt2, g_final, tm=1024, tn=512):
    s, d = h.shape
    d_ff = w_down.shape[0]
    nf = d_ff // tn
    halo_blocks = tm // 8
    return pl.pallas_call(
        _ffn_kernel,
        grid=(s // tm, nf),
        in_specs=[pl.BlockSpec((tm, d), lambda i, j: (i, 0)),
                  pl.BlockSpec((8, d), lambda i, j: (jnp.maximum(i * halo_blocks - 1, 0), 0)),
                  pl.BlockSpec((tm, d), lambda i, j: (i, 0), pipeline_mode=pl.Buffered(1)),
                  pl.BlockSpec((d, tn), lambda i, j: (0, j)),
                  pl.BlockSpec((d, tn), lambda i, j: (0, nf + j)),
                  pl.BlockSpec((3, tn), lambda i, j: (0, j)),
                  pl.BlockSpec((1, tn), lambda i, j: (0, j)),
                  pl.BlockSpec((tn, d), lambda i, j: (j, 0)),
                  _resident((1, d)), _resident((1, d))],
        out_specs=pl.BlockSpec((tm, d), lambda i, j: (i, 0)),
        out_shape=jax.ShapeDtypeStruct((s, d), F32),
        compiler_params=_cparams(("arbitrary", "arbitrary"), 62),
        name="conv_ffn",
    )(hn, hn, h, w_up, w_up, conv_w, conv_b, w_down, gt2, g_final)


def _layer(h, mod, g_mix, w_in, b_f, a_re, a_im, log_dt, ssm_b_re, ssm_b_im, ssm_c_re, ssm_c_im,
           ssm_d, w_glu, b_glu, g_attn_out, g_ssm_out, w_out, g_ffn, w_up, conv_w, conv_b, w_down):
    s, d = h.shape
    aw = N_HEADS * HEAD_DIM
    sh1, sc1, gt1, sh2, sc2, gt2 = [mod[:, i * d:(i + 1) * d] for i in range(N_MOD)]
    row = lambda a: a.reshape(1, -1)

    q, k, v, u, f = _inproj(h, row(g_mix), sc1, sh1, w_in.astype(BF16),
                            w_in[:, 3 * aw + N_HEADS:].astype(BF16))

    cum_row, cum_col = _forget_cumsum(f, b_f)
    attn = _attention(q, k, v, cum_col, cum_row.reshape(N_HEADS, 1, s))

    y = _s5(u, a_re, a_im, log_dt, ssm_b_re, ssm_b_im, ssm_c_re, ssm_c_im, ssm_d)

    h1, hn2 = _mixout(h, attn, y, w_glu.astype(BF16), row(b_glu), row(g_attn_out),
                      row(g_ssm_out), w_out.astype(BF16), gt1, row(g_ffn), sc2, sh2)
    return hn2, h1, (w_up.astype(BF16), conv_w, row(conv_b), w_down.astype(BF16), gt2)


def kernel(x, c, w_ada, b_ada, g_mix, w_in, b_f, a_re, a_im, log_dt, ssm_b_re, ssm_b_im, ssm_c_re,
           ssm_c_im, ssm_d, w_glu, b_glu, g_attn_out, g_ssm_out, w_out, g_ffn, w_up, conv_w,
           conv_b, w_down, g_final):
    batch, s, d = x.shape
    assert w_ada.shape[0] == 1, "only DEPTH == 1 is supported"
    l = 0
    outs = []
    for bi in range(batch):
        mod = _adaln(c[bi:bi + 1], w_ada[l], b_ada[l])
        hn2, h1, ffn_args = _layer(
            x[bi], mod, g_mix[l], w_in[l], b_f[l], a_re[l], a_im[l], log_dt[l], ssm_b_re[l],
            ssm_b_im[l], ssm_c_re[l], ssm_c_im[l], ssm_d[l], w_glu[l], b_glu[l],
            g_attn_out[l], g_ssm_out[l], w_out[l], g_ffn[l], w_up[l], conv_w[l], conv_b[l],
            w_down[l])
        outs.append(_ffn(hn2, h1, *ffn_args, g_final.reshape(1, d)))
    return jnp.stack(outs, axis=0)
```

```python
import functools
import math

import jax
import jax.numpy as jnp
from jax import lax
from jax.experimental import pallas as pl
from jax.experimental.pallas import tpu as pltpu

F32 = jnp.float32
BF16 = jnp.bfloat16

EPS = 1e-6
HEAD_DIM = 128
N_HEADS = 8
SSM_GROUP = 16
SSM_STATE = 64
N_MOD = 6
LANES = 128
SSM_T = 16
SSM_TW = SSM_T * SSM_GROUP
SSM_PAIRS = SSM_T // 2
SSM_TILE_G = LANES // SSM_GROUP
LOG2E = 1.4426950408889634
SKIP_LOG2 = 150.0
NORM_SLACK = 1.01

_MIB = 1024 * 1024


def _cparams(semantics, vmem_mib):
    return pltpu.CompilerParams(dimension_semantics=semantics, vmem_limit_bytes=vmem_mib * _MIB)


def _resident(shape):
    return pl.BlockSpec(shape, lambda *_: (0,) * len(shape), pipeline_mode=pl.Buffered(1))


def _dot(a, b):
    return jnp.dot(a, b, preferred_element_type=F32)


def _dot_nt(a, b):
    return lax.dot_general(a, b, (((1,), (1,)), ((), ())), preferred_element_type=F32)


def _rms(x, g):
    return x * lax.rsqrt(jnp.mean(x * x, axis=-1, keepdims=True) + EPS) * g


def _lane_tile(x, reps):
    return jnp.concatenate([x] * reps, axis=1)


def _adaln_kernel(c_ref, w_ref, b_ref, o_ref):
    c = c_ref[...]
    cond = c * jax.nn.sigmoid(c)
    cond8 = jnp.broadcast_to(cond, (8, c.shape[1])).astype(BF16)
    acc = _dot(cond8, w_ref[...].astype(BF16))
    o_ref[...] = acc[0:1, :] + b_ref[...]


def _adaln(c, w, b, tn=1536):
    d, n = w.shape
    return pl.pallas_call(
        _adaln_kernel,
        grid=(n // tn,),
        in_specs=[pl.BlockSpec((1, d), lambda j: (0, 0)),
                  pl.BlockSpec((d, tn), lambda j: (0, j)),
                  pl.BlockSpec((1, tn), lambda j: (0, j))],
        out_specs=pl.BlockSpec((1, tn), lambda j: (0, j)),
        out_shape=jax.ShapeDtypeStruct((1, n), F32),
        compiler_params=_cparams(("arbitrary",), 48),
        name="adaln",
    )(c, w, b.reshape(1, n))


def _inproj_kernel(x_ref, g_ref, sc_ref, sh_ref, wq_ref, wk_ref, wv_ref, wu_ref, wf_ref,
                   q_ref, k_ref, v_ref, u_ref, f_ref, kn_ref, *, q_scale):
    hn = (_rms(x_ref[...], g_ref[...]) * (1.0 + sc_ref[...]) + sh_ref[...]).astype(BF16)
    f_ref[...] = _dot(hn, wf_ref[...])
    q_ref[...] = (_dot(hn, wq_ref[...]) * q_scale).astype(BF16)
    k = _dot(hn, wk_ref[...]).astype(BF16)
    k_ref[...] = k
    u_ref[...] = _dot(hn, wu_ref[...])

    k32 = k.astype(F32)
    ksq = k32 * k32
    tile_max = jnp.concatenate(
        [jnp.broadcast_to(
            jnp.max(jnp.sum(ksq[:, h * HEAD_DIM:(h + 1) * HEAD_DIM], axis=1, keepdims=True),
                    axis=0, keepdims=True), (1, LANES)) for h in range(N_HEADS)], axis=0)

    @pl.when(pl.program_id(0) == 0)
    def _():
        kn_ref[...] = tile_max

    @pl.when(pl.program_id(0) > 0)
    def _():
        kn_ref[...] = jnp.maximum(kn_ref[...], tile_max)

    v = _dot(hn, wv_ref[...]).astype(BF16)
    ones = jnp.ones((v.shape[0], HEAD_DIM), BF16)
    for h in range(N_HEADS):
        v_ref[:, 2 * h * HEAD_DIM:(2 * h + 1) * HEAD_DIM] = v[:, h * HEAD_DIM:(h + 1) * HEAD_DIM]
        v_ref[:, (2 * h + 1) * HEAD_DIM:(2 * h + 2) * HEAD_DIM] = ones


def _inproj(x, g, sc, sh, w_in, wu, tm=512):
    s, d = x.shape
    aw, sw = N_HEADS * HEAD_DIM, wu.shape[1]
    rows = lambda c: pl.BlockSpec((tm, c), lambda i: (i, 0))
    cols = lambda width, blk: pl.BlockSpec((d, width), lambda i: (0, blk),
                                           pipeline_mode=pl.Buffered(1))
    return pl.pallas_call(
        functools.partial(_inproj_kernel, q_scale=HEAD_DIM ** -0.5 * LOG2E),
        grid=(s // tm,),
        in_specs=[rows(d), _resident((1, d)), _resident((1, d)), _resident((1, d)),
                  cols(aw, 0), cols(aw, 1), cols(aw, 2), _resident((d, sw)),
                  cols(LANES, 3 * aw // LANES)],
        out_specs=[rows(aw), rows(aw), rows(2 * aw), rows(sw), rows(LANES),
                   pl.BlockSpec((N_HEADS, LANES), lambda i: (0, 0))],
        out_shape=[jax.ShapeDtypeStruct((s, aw), BF16), jax.ShapeDtypeStruct((s, aw), BF16),
                   jax.ShapeDtypeStruct((s, 2 * aw), BF16), jax.ShapeDtypeStruct((s, sw), F32),
                   jax.ShapeDtypeStruct((s, LANES), F32),
                   jax.ShapeDtypeStruct((N_HEADS, LANES), F32)],
        compiler_params=_cparams(("arbitrary",), 56),
        name="inproj",
    )(x, g, sc, sh, w_in, w_in, w_in, wu, w_in)


def _log2_forget_cumsum(z, axis):
    x = jnp.minimum(z, 0.0) - jnp.log1p(jnp.exp(-jnp.abs(z)))
    pos = lax.broadcasted_iota(jnp.int32, x.shape, axis)
    shift = 1
    while shift < x.shape[axis]:
        x = x + jnp.where(pos >= shift, pltpu.roll(x, shift, axis=axis), 0.0)
        shift *= 2
    return x * LOG2E


def _cum_kernel(ft_ref, bcol_ref, f_ref, brow_ref, row_ref, col_ref):
    row_ref[...] = _log2_forget_cumsum(ft_ref[...] + bcol_ref[...], 1)
    col_ref[...] = _log2_forget_cumsum(f_ref[...] + brow_ref[...], 0)


def _forget_cumsum(f, b_f):
    s, w = f.shape
    h = b_f.shape[0]
    return pl.pallas_call(
        _cum_kernel,
        out_shape=[jax.ShapeDtypeStruct((h, s), F32), jax.ShapeDtypeStruct((s, w), F32)],
        compiler_params=_cparams(None, 48),
        name="forget_cumsum",
    )(f[:, :h].T, b_f.reshape(h, 1), f, jnp.pad(b_f, (0, w - h)).reshape(1, w))


def _attn_kernel(q_ref, k_ref, v_ref, cq_ref, ck_ref, kn_ref, o_ref, m_ref, acc_ref, *,
                 tq, near, splits):
    i = pl.program_id(1)
    kmax = jnp.sqrt(kn_ref[pl.ds(pl.program_id(0), 1), :]) * NORM_SLACK

    head_lane = lax.broadcasted_iota(jnp.int32, (tq, LANES), 1) == pl.program_id(0)
    cqb = jnp.broadcast_to(
        jnp.sum(jnp.where(head_lane, cq_ref[...], 0.0), axis=1, keepdims=True), (tq, LANES))

    half = tq // splits
    halves = [slice(hh * half, (hh + 1) * half) for hh in range(splits)]

    base = pl.multiple_of(i * tq, tq)
    first_near = jnp.maximum(i - near, 0)
    near_w = near * tq
    near_ks = pl.multiple_of(first_near * tq, tq)
    near_k = k_ref[pl.ds(near_ks, near_w), :]
    near_v = v_ref[pl.ds(near_ks, near_w), :]
    near_key = near_ks + lax.broadcasted_iota(jnp.int32, (1, near_w), 1)
    near_ck = jnp.where(near_key < base, ck_ref[0, :, pl.ds(near_ks, near_w)], jnp.inf)
    bound = None
    for hh, rs in enumerate(halves):
        nk = (hh + 1) * half
        qh = q_ref[rs, :]
        t_diag = _dot_nt(qh, k_ref[pl.ds(base, nk), :]) - ck_ref[0, :, pl.ds(base, nk)]
        row = lax.broadcasted_iota(jnp.int32, t_diag.shape, 0)
        col = lax.broadcasted_iota(jnp.int32, t_diag.shape, 1)
        t_diag = jnp.where(col <= row + hh * half, t_diag, -jnp.inf)
        t = jnp.concatenate([_dot_nt(qh, near_k) - near_ck, t_diag], axis=1)
        m0 = jnp.max(t, axis=1, keepdims=True) + cqb[rs]
        p = jnp.exp2(t - _lane_tile(m0 - cqb[rs], (near_w + nk) // LANES)).astype(BF16)
        m_ref[rs, :] = m0
        acc_ref[rs, :] = (_dot(p[:, :near_w], near_v)
                          + _dot(p[:, near_w:], v_ref[pl.ds(base, nk), :]))
        qf = qh.astype(F32)
        qn = jnp.sqrt(jnp.sum(qf * qf, axis=1, keepdims=True))
        b_half = jnp.max(qn * kmax - (m0 - cqb[rs]))
        bound = b_half if bound is None else jnp.maximum(bound, b_half)

    ck_all = ck_ref[0]
    pos = lax.broadcasted_iota(jnp.int32, ck_all.shape, 1)
    dead = jnp.where((pos < i * tq) & (ck_all > bound + SKIP_LOG2), 1.0, 0.0)
    j0 = jnp.sum(dead).astype(jnp.int32) // tq

    def body(j, carry):
        ks = pl.multiple_of(j * tq, tq)
        kb = k_ref[pl.ds(ks, tq), :]
        vb = v_ref[pl.ds(ks, tq), :]
        ck = ck_ref[0, :, pl.ds(ks, tq)]
        for rs in halves:
            t = _dot_nt(q_ref[rs, :], kb) - ck
            m_prev = m_ref[rs, :]
            m_new = jnp.maximum(m_prev, jnp.max(t, axis=1, keepdims=True) + cqb[rs])
            alpha = jnp.exp2(m_prev - m_new)
            p = jnp.exp2(t - _lane_tile(m_new - cqb[rs], tq // LANES))
            acc_ref[rs, :] = _lane_tile(alpha, 2) * acc_ref[rs, :] + _dot(p.astype(BF16), vb)
            m_ref[rs, :] = m_new
        return carry

    lax.fori_loop(j0, first_near, body, 0)
    acc = acc_ref[...]
    o_ref[...] = (acc[:, :HEAD_DIM] / acc[:, HEAD_DIM:]).astype(BF16)


def _attention(q, k, v_ones, cum_col, cum_row, k_norm_sq, tq=512, near=2, splits=2):
    s = q.shape[0]
    return pl.pallas_call(
        functools.partial(_attn_kernel, tq=tq, near=near, splits=splits),
        grid=(N_HEADS, s // tq),
        in_specs=[pl.BlockSpec((tq, HEAD_DIM), lambda h, i: (i, h)),
                  pl.BlockSpec((s, HEAD_DIM), lambda h, i: (0, h)),
                  pl.BlockSpec((s, 2 * HEAD_DIM), lambda h, i: (0, h)),
                  pl.BlockSpec((tq, LANES), lambda h, i: (i, 0)),
                  pl.BlockSpec((1, 1, s), lambda h, i: (h, 0, 0)),
                  pl.BlockSpec((N_HEADS, LANES), lambda h, i: (0, 0))],
        out_specs=pl.BlockSpec((tq, HEAD_DIM), lambda h, i: (i, h)),
        out_shape=jax.ShapeDtypeStruct((s, N_HEADS * HEAD_DIM), BF16),
        scratch_shapes=[pltpu.VMEM((tq, LANES), F32), pltpu.VMEM((tq, 2 * HEAD_DIM), F32)],
        compiler_params=_cparams(("arbitrary", "arbitrary"), 40),
        name="fox_attention",
    )(q, k, v_ones, cum_col, cum_row, k_norm_sq)


def _cmul(are, aim, bre, bim):
    return are * bre - aim * bim, are * bim + aim * bre


def _step_pair(u_ref, a, n_chunks):
    x0 = u_ref[pl.ds(2 * a, n_chunks, stride=SSM_T), :]
    x1 = u_ref[pl.ds(2 * a + 1, n_chunks, stride=SSM_T), :]
    return jnp.concatenate([x0, x1], axis=1).astype(BF16)


def _ssm_in_kernel(u_ref, ldt_ref, are_ref, aim_ref, btr_ref, bti_ref, cr_ref, ci_ref, d_ref,
                   w2_ref, pmt2_ref, x0_ref,
                   cp_ref, wl_ref, q2_ref, v_ref):
    n_chunks = u_ref.shape[0] // SSM_T
    half_w = SSM_TILE_G * SSM_STATE
    pw_parts = []
    lane = lax.broadcasted_iota(jnp.int32, (SSM_GROUP, LANES), 1)
    first = lane < SSM_STATE

    @pl.when(pl.program_id(0) == 0)
    def _():
        q2_ref[...] = jnp.zeros(q2_ref.shape, BF16)

    pmt2_ref[...] = jnp.zeros(pmt2_ref.shape, BF16)

    def place(ref, lead, step, gi, re_part, im_part):
        r0 = (step % 2) * LANES + gi * SSM_GROUP
        rows = slice(r0, r0 + SSM_GROUP)
        mine = first if gi % 2 == 0 else jnp.logical_not(first)
        c_re = (gi // 2) * LANES
        c_im = half_w + c_re
        ref[lead + (step // 2, rows, slice(c_re, c_re + LANES))] = (
            jnp.where(mine, re_part, 0.0).astype(BF16))
        ref[lead + (step // 2, rows, slice(c_im, c_im + LANES))] = (
            jnp.where(mine, im_part, 0.0).astype(BF16))

    for gi in range(SSM_TILE_G):
        dt = jnp.exp(ldt_ref[gi])
        are, aim = are_ref[gi], aim_ref[gi]
        mag = jnp.exp(dt * are)
        abre, abim = mag * jnp.cos(dt * aim), mag * jnp.sin(dt * aim)
        nre, nim = abre - 1.0, abim
        den = are * are + aim * aim
        zre = (nre * are + nim * aim) / den
        zim = (nim * are - nre * aim) / den
        bbre, bbim = _cmul(zre, zim, btr_ref[gi], bti_ref[gi])
        bbcat = jnp.where(first, bbre, bbim)

        cpre, cpim = cr_ref[gi], ci_ref[gi]
        qre, qim = bbre, bbim
        pwre, pwim = jnp.ones_like(abre), jnp.zeros_like(abim)
        for t in range(SSM_T):
            cp_ref[t * SSM_GROUP:(t + 1) * SSM_GROUP, :] = jnp.where(first, cpre, -cpim)
            if t > 0:
                place(pmt2_ref, (0,), t - 1, gi, cpre, -cpim)
            place(q2_ref, (), SSM_T - 1 - t, gi, qre, qim)
            cpre, cpim = _cmul(cpre, cpim, abre, abim)
            qre, qim = _cmul(qre, qim, abre, abim)
            pwre, pwim = _cmul(pwre, pwim, abre, abim)
        place(pmt2_ref, (0,), SSM_T - 1, gi, cpre, -cpim)
        pw_parts.append((pwre, pwim))

        krow = lax.dot_general(bbcat, cp_ref[...], (((1,), (1,)), ((), ())),
                               preferred_element_type=F32, precision=lax.Precision.HIGHEST)
        own = (lane >= gi * SSM_GROUP) & (lane < (gi + 1) * SSM_GROUP)
        for tau in range(SSM_T):
            half = krow[:, (tau // SSM_TILE_G) * LANES:(tau // SSM_TILE_G + 1) * LANES]
            shift = ((gi - tau % SSM_TILE_G) * SSM_GROUP) % LANES
            moved = half if shift == 0 else pltpu.roll(half, shift, axis=1)
            wl_ref[tau, gi * SSM_GROUP:(gi + 1) * SSM_GROUP, :] = jnp.where(own, moved, 0.0)

    r = lax.broadcasted_iota(jnp.int32, (LANES, LANES), 0)
    c = lax.broadcasted_iota(jnp.int32, (LANES, LANES), 1)
    wl_ref[0] = wl_ref[0] + jnp.where(r == c, d_ref[0], 0.0)
    for dl in range(SSM_PAIRS):
        diag = wl_ref[2 * dl].astype(BF16)
        w2_ref[0, dl, 0:LANES, 0:LANES] = diag
        w2_ref[0, dl, LANES:, LANES:] = diag
        w2_ref[0, dl, 0:LANES, LANES:] = wl_ref[2 * dl + 1].astype(BF16)
        below = jnp.zeros((LANES, LANES), BF16) if dl == 0 else wl_ref[2 * dl - 1].astype(BF16)
        w2_ref[0, dl, LANES:, 0:LANES] = below

    v = _dot(_step_pair(u_ref, 0, n_chunks), q2_ref[0])
    for a in range(1, SSM_PAIRS):
        v = v + _dot(_step_pair(u_ref, a, n_chunks), q2_ref[a])
    v_ref[...] = v

    pair = lambda k, part: jnp.where(first[0:1], pw_parts[2 * k][part], pw_parts[2 * k + 1][part])
    ar = jnp.concatenate([pair(k, 0) for k in range(SSM_TILE_G // 2)], axis=1)
    ai = jnp.concatenate([pair(k, 1) for k in range(SSM_TILE_G // 2)], axis=1)

    def scan_body(c, carry):
        xre, xim = carry
        x0_ref[pl.ds(c, 1), 0:half_w] = xre
        x0_ref[pl.ds(c, 1), half_w:] = xim
        vre = v_ref[pl.ds(c, 1), 0:half_w]
        vim = v_ref[pl.ds(c, 1), half_w:]
        return ar * xre - ai * xim + vre, ar * xim + ai * xre + vim

    zero = jnp.zeros(ar.shape, F32)
    lax.fori_loop(0, n_chunks, scan_body, (zero, zero), unroll=8)


def _ssm_in(u, ldt, are2, aim2, bt_re2, bt_im2, c_re2, c_im2, d_rows):
    s, width = u.shape
    n_tiles = width // LANES
    c = s // SSM_T
    kw = SSM_TILE_G * LANES
    per_tile = lambda *shape: pl.BlockSpec((SSM_TILE_G,) + shape,
                                           lambda j: (j,) + (0,) * len(shape))
    return pl.pallas_call(
        _ssm_in_kernel,
        grid=(n_tiles,),
        in_specs=[pl.BlockSpec((s, LANES), lambda j: (0, j)),
                  per_tile(1, LANES), per_tile(1, LANES), per_tile(1, LANES),
                  per_tile(SSM_GROUP, LANES), per_tile(SSM_GROUP, LANES),
                  per_tile(SSM_GROUP, LANES), per_tile(SSM_GROUP, LANES),
                  pl.BlockSpec((1, 1, LANES), lambda j: (j, 0, 0))],
        out_specs=[pl.BlockSpec((1, SSM_PAIRS, 2 * LANES, 2 * LANES), lambda j: (j, 0, 0, 0)),
                   pl.BlockSpec((1, SSM_PAIRS, 2 * LANES, kw), lambda j: (j, 0, 0, 0)),
                   pl.BlockSpec((c, kw), lambda j: (0, j))],
        out_shape=[jax.ShapeDtypeStruct((n_tiles, SSM_PAIRS, 2 * LANES, 2 * LANES), BF16),
                   jax.ShapeDtypeStruct((n_tiles, SSM_PAIRS, 2 * LANES, kw), BF16),
                   jax.ShapeDtypeStruct((c, n_tiles * kw), F32)],
        scratch_shapes=[pltpu.VMEM((SSM_TW, LANES), F32),
                        pltpu.VMEM((SSM_T, LANES, LANES), F32),
                        pltpu.VMEM((SSM_PAIRS, 2 * LANES, kw), BF16),
                        pltpu.VMEM((c, kw), F32)],
        compiler_params=_cparams(("arbitrary",), 48),
        name="ssm_state_in",
    )(u, ldt, are2, aim2, bt_re2, bt_im2, c_re2, c_im2, d_rows)


def _ssm_out_kernel(u_ref, w2_ref, pmt2_ref, x0_ref, y_ref):
    n_chunks = u_ref.shape[0] // SSM_T
    x0 = x0_ref[...].astype(BF16)
    pairs = [_step_pair(u_ref, a, n_chunks) for a in range(SSM_PAIRS)]
    for b in range(SSM_PAIRS):
        acc = _dot_nt(x0, pmt2_ref[0, b])
        for a in range(b + 1):
            acc = acc + _dot(pairs[a], w2_ref[0, b - a])
        y_ref[pl.ds(2 * b, n_chunks, stride=SSM_T), :] = acc[:, :LANES]
        y_ref[pl.ds(2 * b + 1, n_chunks, stride=SSM_T), :] = acc[:, LANES:]


def _ssm_out(u, w2, pmt2, x0):
    s, width = u.shape
    n_tiles = width // LANES
    c = s // SSM_T
    kw = SSM_TILE_G * LANES
    return pl.pallas_call(
        _ssm_out_kernel,
        grid=(n_tiles,),
        in_specs=[pl.BlockSpec((s, LANES), lambda j: (0, j)),
                  pl.BlockSpec((1, SSM_PAIRS, 2 * LANES, 2 * LANES), lambda j: (j, 0, 0, 0)),
                  pl.BlockSpec((1, SSM_PAIRS, 2 * LANES, kw), lambda j: (j, 0, 0, 0)),
                  pl.BlockSpec((c, kw), lambda j: (0, j))],
        out_specs=pl.BlockSpec((s, LANES), lambda j: (0, j)),
        out_shape=jax.ShapeDtypeStruct((s, width), F32),
        compiler_params=_cparams(("arbitrary",), 48),
        name="ssm_out",
    )(u, w2, pmt2, x0)


def _s5(u, a_re, a_im, log_dt, b_re, b_im, c_re, c_im, d_skip):
    g, p = a_re.shape
    dup = lambda a: jnp.concatenate([a, a], axis=-1)
    ldt = jnp.broadcast_to(log_dt.reshape(g, 1, 1), (g, 1, LANES))
    are2, aim2 = dup(a_re).reshape(g, 1, LANES), dup(a_im).reshape(g, 1, LANES)
    bt_re2 = dup(jnp.swapaxes(b_re, 1, 2))
    bt_im2 = dup(jnp.swapaxes(b_im, 1, 2))
    w2, pmt2, x0 = _ssm_in(u, ldt, are2, aim2, bt_re2, bt_im2, dup(c_re), dup(c_im),
                           d_skip.reshape(-1, 1, LANES))
    return _ssm_out(u, w2, pmt2, x0)


def _gelu_tanh(x):
    return 0.5 * x * (1.0 + jnp.tanh(math.sqrt(2.0 / math.pi) * (x + 0.044715 * (x * x * x))))


def _mixout_kernel(x_ref, attn_ref, y_ref, wglu_ref, bglu_ref, ga_ref, gs_ref, wo_ref,
                   gt_ref, gf_ref, sc_ref, sh_ref, h_ref, hn_ref):
    aw = attn_ref.shape[1]
    y = _gelu_tanh(y_ref[...])
    gate = jax.nn.sigmoid(_dot(y.astype(BF16), wglu_ref[...]) + bglu_ref[...])
    ns = _rms(y * gate, gs_ref[...]).astype(BF16)
    na = _rms(attn_ref[...].astype(F32), ga_ref[...]).astype(BF16)
    mixed = _dot(na, wo_ref[0:aw, :]) + _dot(ns, wo_ref[aw:, :])
    h = x_ref[...] + gt_ref[...] * mixed
    h_ref[...] = h
    hn_ref[...] = (_rms(h, gf_ref[...]) * (1.0 + sc_ref[...]) + sh_ref[...]).astype(BF16)


def _mixout(x, attn, y, w_glu, b_glu, g_attn, g_ssm, w_out, gt1, g_ffn, sc2, sh2, tm=512):
    s, d = x.shape
    w = attn.shape[1]
    rows = lambda c: pl.BlockSpec((tm, c), lambda i: (i, 0))
    return pl.pallas_call(
        _mixout_kernel,
        grid=(s // tm,),
        in_specs=[rows(d), rows(w), rows(w), _resident((w, w)), _resident((1, w)),
                  _resident((1, w)), _resident((1, w)), _resident(w_out.shape),
                  _resident((1, d)), _resident((1, d)), _resident((1, d)), _resident((1, d))],
        out_specs=[rows(d), rows(d)],
        out_shape=[jax.ShapeDtypeStruct((s, d), F32), jax.ShapeDtypeStruct((s, d), BF16)],
        compiler_params=_cparams(("arbitrary",), 56),
        name="mixer_out",
    )(x, attn, y, w_glu, b_glu, g_attn, g_ssm, w_out, gt1, g_ffn, sc2, sh2)


def _ffn_kernel(hn_ref, halo_ref, h_ref, wa_ref, wb_ref, cw_ref, cb_ref, wd_ref, gt_ref, gfin_ref,
                o_ref):
    i, j = pl.program_id(0), pl.program_id(1)

    @pl.when(j == 0)
    def _():
        o_ref[...] = jnp.zeros(o_ref.shape, F32)

    hn = hn_ref[...]
    a = _dot(hn, wa_ref[...])
    b = _dot(hn, wb_ref[...])
    halo = _dot(halo_ref[...], wa_ref[...]) * (i > 0).astype(F32)
    row = lax.broadcasted_iota(jnp.int32, a.shape, 0)
    prev1 = jnp.where(row == 0, halo[7:8, :], pltpu.roll(a, 1, axis=0))
    prev2 = jnp.where(row == 0, halo[6:7, :],
                      jnp.where(row == 1, halo[7:8, :], pltpu.roll(a, 2, axis=0)))
    cw = cw_ref[...]
    conv = cb_ref[...] + cw[0:1, :] * prev2 + cw[1:2, :] * prev1 + cw[2:3, :] * a
    act = (conv * jax.nn.sigmoid(conv) * b).astype(BF16)
    o_ref[...] += _dot(act, wd_ref[...])

    @pl.when(j == pl.num_programs(1) - 1)
    def _():
        h = h_ref[...] + gt_ref[...] * o_ref[...]
        o_ref[...] = _rms(h, gfin_ref[...])


def _ffn(hn, h, w_up, conv_w, conv_b, w_down, gt2, g_final, tm=1024, tn=512):
    s, d = h.shape
    d_ff = w_down.shape[0]
    nf = d_ff // tn
    halo_blocks = tm // 8
    return pl.pallas_call(
        _ffn_kernel,
        grid=(s // tm, nf),
        in_specs=[pl.BlockSpec((tm, d), lambda i, j: (i, 0)),
                  pl.BlockSpec((8, d), lambda i, j: (jnp.maximum(i * halo_blocks - 1, 0), 0)),
                  pl.BlockSpec((tm, d), lambda i, j: (i, 0), pipeline_mode=pl.Buffered(1)),
                  pl.BlockSpec((d, tn), lambda i, j: (0, j)),
                  pl.BlockSpec((d, tn), lambda i, j: (0, nf + j)),
                  pl.BlockSpec((3, tn), lambda i, j: (0, j)),
                  pl.BlockSpec((1, tn), lambda i, j: (0, j)),
                  pl.BlockSpec((tn, d), lambda i, j: (j, 0)),
                  _resident((1, d)), _resident((1, d))],
        out_specs=pl.BlockSpec((tm, d), lambda i, j: (i, 0)),
        out_shape=jax.ShapeDtypeStruct((s, d), F32),
        compiler_params=_cparams(("arbitrary", "arbitrary"), 62),
        name="conv_ffn",
    )(hn, hn, h, w_up, w_up, conv_w, conv_b, w_down, gt2, g_final)


def _layer(h, mod, g_mix, w_in, b_f, a_re, a_im, log_dt, ssm_b_re, ssm_b_im, ssm_c_re, ssm_c_im,
           ssm_d, w_glu, b_glu, g_attn_out, g_ssm_out, w_out, g_ffn, w_up, conv_w, conv_b, w_down):
    s, d = h.shape
    aw = N_HEADS * HEAD_DIM
    sh1, sc1, gt1, sh2, sc2, gt2 = [mod[:, i * d:(i + 1) * d] for i in range(N_MOD)]
    row = lambda a: a.reshape(1, -1)

    q, k, v, u, f, k_norm_sq = _inproj(h, row(g_mix), sc1, sh1, w_in.astype(BF16),
                            w_in[:, 3 * aw + N_HEADS:].astype(BF16))

    cum_row, cum_col = _forget_cumsum(f, b_f)
    attn = _attention(q, k, v, cum_col, cum_row.reshape(N_HEADS, 1, s), k_norm_sq)

    y = _s5(u, a_re, a_im, log_dt, ssm_b_re, ssm_b_im, ssm_c_re, ssm_c_im, ssm_d)

    h1, hn2 = _mixout(h, attn, y, w_glu.astype(BF16), row(b_glu), row(g_attn_out),
                      row(g_ssm_out), w_out.astype(BF16), gt1, row(g_ffn), sc2, sh2)
    return hn2, h1, (w_up.astype(BF16), conv_w, row(conv_b), w_down.astype(BF16), gt2)


def kernel(x, c, w_ada, b_ada, g_mix, w_in, b_f, a_re, a_im, log_dt, ssm_b_re, ssm_b_im, ssm_c_re,
           ssm_c_im, ssm_d, w_glu, b_glu, g_attn_out, g_ssm_out, w_out, g_ffn, w_up, conv_w,
           conv_b, w_down, g_final):
    batch, s, d = x.shape
    assert w_ada.shape[0] == 1, "only DEPTH == 1 is supported"
    l = 0
    outs = []
    for bi in range(batch):
        mod = _adaln(c[bi:bi + 1], w_ada[l], b_ada[l])
        hn2, h1, ffn_args = _layer(
            x[bi], mod, g_mix[l], w_in[l], b_f[l], a_re[l], a_im[l], log_dt[l], ssm_b_re[l],
            ssm_b_im[l], ssm_c_re[l], ssm_c_im[l], ssm_d[l], w_glu[l], b_glu[l],
            g_attn_out[l], g_ssm_out[l], w_out[l], g_ffn[l], w_up[l], conv_w[l], conv_b[l],
            w_down[l])
        outs.append(_ffn(hn2, h1, *ffn_args, g_final.reshape(1, d)))
    return jnp.stack(outs, axis=0)
```

```python
import functools
import math

import jax
import jax.numpy as jnp
from jax import lax
from jax.experimental import pallas as pl
from jax.experimental.pallas import tpu as pltpu

F32 = jnp.float32
BF16 = jnp.bfloat16

EPS = 1e-6
HEAD_DIM = 128
N_HEADS = 8
SSM_GROUP = 16
SSM_STATE = 64
N_MOD = 6
LANES = 128
SSM_T = 16
SSM_TW = SSM_T * SSM_GROUP
SSM_PAIRS = SSM_T // 2
SSM_TILE_G = LANES // SSM_GROUP
LOG2E = 1.4426950408889634
SKIP_LOG2 = 151.0
NORM_SLACK = 1.01

_MIB = 1024 * 1024


def _cparams(semantics, vmem_mib):
    return pltpu.CompilerParams(dimension_semantics=semantics, vmem_limit_bytes=vmem_mib * _MIB)


def _resident(shape):
    return pl.BlockSpec(shape, lambda *_: (0,) * len(shape), pipeline_mode=pl.Buffered(1))


def _dot(a, b):
    return jnp.dot(a, b, preferred_element_type=F32)


def _dot_nt(a, b):
    return lax.dot_general(a, b, (((1,), (1,)), ((), ())), preferred_element_type=F32)


def _rms(x, g):
    return x * lax.rsqrt(jnp.mean(x * x, axis=-1, keepdims=True) + EPS) * g


def _lane_tile(x, reps):
    return jnp.concatenate([x] * reps, axis=1)


def _adaln_kernel(c_ref, w_ref, b_ref, o_ref):
    c = c_ref[...]
    cond = c * jax.nn.sigmoid(c)
    cond8 = jnp.broadcast_to(cond, (8, c.shape[1])).astype(BF16)
    acc = _dot(cond8, w_ref[...].astype(BF16))
    o_ref[...] = acc[0:1, :] + b_ref[...]


def _adaln(c, w, b, tn=1536):
    d, n = w.shape
    return pl.pallas_call(
        _adaln_kernel,
        grid=(n // tn,),
        in_specs=[pl.BlockSpec((1, d), lambda j: (0, 0)),
                  pl.BlockSpec((d, tn), lambda j: (0, j)),
                  pl.BlockSpec((1, tn), lambda j: (0, j))],
        out_specs=pl.BlockSpec((1, tn), lambda j: (0, j)),
        out_shape=jax.ShapeDtypeStruct((1, n), F32),
        compiler_params=_cparams(("arbitrary",), 48),
        name="adaln",
    )(c, w, b.reshape(1, n))


def _inproj_kernel(x_ref, g_ref, sc_ref, sh_ref, wq_ref, wk_ref, wv_ref, wu_ref, wf_ref,
                   q_ref, k_ref, v_ref, u_ref, f_ref, kn_ref, *, q_scale):
    hn = (_rms(x_ref[...], g_ref[...]) * (1.0 + sc_ref[...]) + sh_ref[...]).astype(BF16)
    f_ref[...] = _dot(hn, wf_ref[...])
    q_ref[...] = (_dot(hn, wq_ref[...]) * q_scale).astype(BF16)
    k = _dot(hn, wk_ref[...]).astype(BF16)
    k_ref[...] = k
    u_ref[...] = _dot(hn, wu_ref[...])

    k32 = k.astype(F32)
    ksq = k32 * k32
    tile_max = jnp.concatenate(
        [jnp.broadcast_to(
            jnp.max(jnp.sum(ksq[:, h * HEAD_DIM:(h + 1) * HEAD_DIM], axis=1, keepdims=True),
                    axis=0, keepdims=True), (1, LANES)) for h in range(N_HEADS)], axis=0)

    @pl.when(pl.program_id(0) == 0)
    def _():
        kn_ref[...] = tile_max

    @pl.when(pl.program_id(0) > 0)
    def _():
        kn_ref[...] = jnp.maximum(kn_ref[...], tile_max)

    v = _dot(hn, wv_ref[...]).astype(BF16)
    ones = jnp.ones((v.shape[0], HEAD_DIM), BF16)
    for h in range(N_HEADS):
        v_ref[:, 2 * h * HEAD_DIM:(2 * h + 1) * HEAD_DIM] = v[:, h * HEAD_DIM:(h + 1) * HEAD_DIM]
        v_ref[:, (2 * h + 1) * HEAD_DIM:(2 * h + 2) * HEAD_DIM] = ones


def _inproj(x, g, sc, sh, w_in, wu, tm=512):
    s, d = x.shape
    aw, sw = N_HEADS * HEAD_DIM, wu.shape[1]
    rows = lambda c: pl.BlockSpec((tm, c), lambda i: (i, 0))
    cols = lambda width, blk: pl.BlockSpec((d, width), lambda i: (0, blk),
                                           pipeline_mode=pl.Buffered(1))
    return pl.pallas_call(
        functools.partial(_inproj_kernel, q_scale=HEAD_DIM ** -0.5 * LOG2E),
        grid=(s // tm,),
        in_specs=[rows(d), _resident((1, d)), _resident((1, d)), _resident((1, d)),
                  cols(aw, 0), cols(aw, 1), cols(aw, 2), _resident((d, sw)),
                  cols(LANES, 3 * aw // LANES)],
        out_specs=[rows(aw), rows(aw), rows(2 * aw), rows(sw), rows(LANES),
                   pl.BlockSpec((N_HEADS, LANES), lambda i: (0, 0))],
        out_shape=[jax.ShapeDtypeStruct((s, aw), BF16), jax.ShapeDtypeStruct((s, aw), BF16),
                   jax.ShapeDtypeStruct((s, 2 * aw), BF16), jax.ShapeDtypeStruct((s, sw), F32),
                   jax.ShapeDtypeStruct((s, LANES), F32),
                   jax.ShapeDtypeStruct((N_HEADS, LANES), F32)],
        compiler_params=_cparams(("arbitrary",), 56),
        name="inproj",
    )(x, g, sc, sh, w_in, w_in, w_in, wu, w_in)


def _log2_forget_cumsum(z, axis):
    x = jnp.minimum(z, 0.0) - jnp.log1p(jnp.exp(-jnp.abs(z)))
    pos = lax.broadcasted_iota(jnp.int32, x.shape, axis)
    shift = 1
    while shift < x.shape[axis]:
        x = x + jnp.where(pos >= shift, pltpu.roll(x, shift, axis=axis), 0.0)
        shift *= 2
    return x * LOG2E


def _cum_kernel(ft_ref, bcol_ref, f_ref, brow_ref, row_ref, col_ref):
    row_ref[...] = _log2_forget_cumsum(ft_ref[...] + bcol_ref[...], 1)
    col_ref[...] = _log2_forget_cumsum(f_ref[...] + brow_ref[...], 0)


def _forget_cumsum(f, b_f):
    s, w = f.shape
    h = b_f.shape[0]
    return pl.pallas_call(
        _cum_kernel,
        out_shape=[jax.ShapeDtypeStruct((h, s), F32), jax.ShapeDtypeStruct((s, w), F32)],
        compiler_params=_cparams(None, 48),
        name="forget_cumsum",
    )(f[:, :h].T, b_f.reshape(h, 1), f, jnp.pad(b_f, (0, w - h)).reshape(1, w))


def _attn_kernel(q_ref, k_ref, v_ref, cq_ref, ck_ref, kn_ref, o_ref, m_ref, acc_ref, *,
                 tq, near, splits):
    i = pl.program_id(1)
    kmax = jnp.sqrt(kn_ref[pl.ds(pl.program_id(0), 1), :]) * NORM_SLACK

    head_lane = lax.broadcasted_iota(jnp.int32, (tq, LANES), 1) == pl.program_id(0)
    cqb = jnp.broadcast_to(
        jnp.sum(jnp.where(head_lane, cq_ref[...], 0.0), axis=1, keepdims=True), (tq, LANES))

    half = tq // splits
    halves = [slice(hh * half, (hh + 1) * half) for hh in range(splits)]

    base = pl.multiple_of(i * tq, tq)
    qf = q_ref[...].astype(F32)
    own = jnp.sum(qf * k_ref[pl.ds(base, tq), :].astype(F32), axis=1, keepdims=True)
    qn = jnp.sqrt(jnp.sum(qf * qf, axis=1, keepdims=True))
    bound = jnp.max(qn * kmax - (own - cqb))
    ck_all = ck_ref[0]
    pos = lax.broadcasted_iota(jnp.int32, ck_all.shape, 1)
    dead = jnp.where((pos < i * tq) & (ck_all > bound + SKIP_LOG2), 1.0, 0.0)
    j0 = jnp.sum(dead).astype(jnp.int32) // tq

    first_near = jnp.maximum(i - near, 0)
    near_w = near * tq
    near_ks = pl.multiple_of(first_near * tq, tq)
    near_k = k_ref[pl.ds(near_ks, near_w), :]
    near_v = v_ref[pl.ds(near_ks, near_w), :]
    near_key = near_ks + lax.broadcasted_iota(jnp.int32, (1, near_w), 1)
    near_ck = jnp.where(near_key < base, ck_ref[0, :, pl.ds(near_ks, near_w)], jnp.inf)
    for hh, rs in enumerate(halves):
        nk = (hh + 1) * half
        qh = q_ref[rs, :]
        t_diag = _dot_nt(qh, k_ref[pl.ds(base, nk), :]) - ck_ref[0, :, pl.ds(base, nk)]
        row = lax.broadcasted_iota(jnp.int32, t_diag.shape, 0)
        col = lax.broadcasted_iota(jnp.int32, t_diag.shape, 1)
        t_diag = jnp.where(col <= row + hh * half, t_diag, -jnp.inf)
        t = jnp.concatenate([_dot_nt(qh, near_k) - near_ck, t_diag], axis=1)
        m0 = jnp.max(t, axis=1, keepdims=True) + cqb[rs]
        p = jnp.exp2(t - _lane_tile(m0 - cqb[rs], (near_w + nk) // LANES)).astype(BF16)
        m_ref[rs, :] = m0
        acc = _dot(p[:, :near_w], near_v) + _dot(p[:, near_w:], v_ref[pl.ds(base, nk), :])
        acc_ref[rs, :] = acc
        o_ref[rs, :] = (acc[:, :HEAD_DIM] / acc[:, HEAD_DIM:]).astype(BF16)

    def body(j, carry):
        ks = pl.multiple_of(j * tq, tq)
        kb = k_ref[pl.ds(ks, tq), :]
        vb = v_ref[pl.ds(ks, tq), :]
        ck = ck_ref[0, :, pl.ds(ks, tq)]
        for rs in halves:
            t = _dot_nt(q_ref[rs, :], kb) - ck
            m_prev = m_ref[rs, :]
            m_new = jnp.maximum(m_prev, jnp.max(t, axis=1, keepdims=True) + cqb[rs])
            alpha = jnp.exp2(m_prev - m_new)
            p = jnp.exp2(t - _lane_tile(m_new - cqb[rs], tq // LANES))
            acc_ref[rs, :] = _lane_tile(alpha, 2) * acc_ref[rs, :] + _dot(p.astype(BF16), vb)
            m_ref[rs, :] = m_new
        return carry

    @pl.when(j0 < first_near)
    def _():
        lax.fori_loop(j0, first_near, body, 0)
        acc = acc_ref[...]
        o_ref[...] = (acc[:, :HEAD_DIM] / acc[:, HEAD_DIM:]).astype(BF16)


def _attention(q, k, v_ones, cum_col, cum_row, k_norm_sq, tq=512, near=2, splits=2):
    s = q.shape[0]
    return pl.pallas_call(
        functools.partial(_attn_kernel, tq=tq, near=near, splits=splits),
        grid=(N_HEADS, s // tq),
        in_specs=[pl.BlockSpec((tq, HEAD_DIM), lambda h, i: (i, h)),
                  pl.BlockSpec((s, HEAD_DIM), lambda h, i: (0, h)),
                  pl.BlockSpec((s, 2 * HEAD_DIM), lambda h, i: (0, h)),
                  pl.BlockSpec((tq, LANES), lambda h, i: (i, 0)),
                  pl.BlockSpec((1, 1, s), lambda h, i: (h, 0, 0)),
                  pl.BlockSpec((N_HEADS, LANES), lambda h, i: (0, 0))],
        out_specs=pl.BlockSpec((tq, HEAD_DIM), lambda h, i: (i, h)),
        out_shape=jax.ShapeDtypeStruct((s, N_HEADS * HEAD_DIM), BF16),
        scratch_shapes=[pltpu.VMEM((tq, LANES), F32), pltpu.VMEM((tq, 2 * HEAD_DIM), F32)],
        compiler_params=_cparams(("arbitrary", "arbitrary"), 40),
        name="fox_attention",
    )(q, k, v_ones, cum_col, cum_row, k_norm_sq)


def _cmul(are, aim, bre, bim):
    return are * bre - aim * bim, are * bim + aim * bre


def _step_pair(u_ref, a, n_chunks):
    x0 = u_ref[pl.ds(2 * a, n_chunks, stride=SSM_T), :]
    x1 = u_ref[pl.ds(2 * a + 1, n_chunks, stride=SSM_T), :]
    return jnp.concatenate([x0, x1], axis=1).astype(BF16)


def _ssm_in_kernel(u_ref, ldt_ref, are_ref, aim_ref, btr_ref, bti_ref, cr_ref, ci_ref, d_ref,
                   w2_ref, pmt2_ref, x0_ref,
                   cp_ref, wl_ref, q2_ref, v_ref):
    n_chunks = u_ref.shape[0] // SSM_T
    half_w = SSM_TILE_G * SSM_STATE
    pw_parts = []
    lane = lax.broadcasted_iota(jnp.int32, (SSM_GROUP, LANES), 1)
    first = lane < SSM_STATE

    @pl.when(pl.program_id(0) == 0)
    def _():
        q2_ref[...] = jnp.zeros(q2_ref.shape, BF16)

    pmt2_ref[...] = jnp.zeros(pmt2_ref.shape, BF16)

    def place(ref, lead, step, gi, re_part, im_part):
        r0 = (step % 2) * LANES + gi * SSM_GROUP
        rows = slice(r0, r0 + SSM_GROUP)
        mine = first if gi % 2 == 0 else jnp.logical_not(first)
        c_re = (gi // 2) * LANES
        c_im = half_w + c_re
        ref[lead + (step // 2, rows, slice(c_re, c_re + LANES))] = (
            jnp.where(mine, re_part, 0.0).astype(BF16))
        ref[lead + (step // 2, rows, slice(c_im, c_im + LANES))] = (
            jnp.where(mine, im_part, 0.0).astype(BF16))

    for gi in range(SSM_TILE_G):
        dt = jnp.exp(ldt_ref[gi])
        are, aim = are_ref[gi], aim_ref[gi]
        mag = jnp.exp(dt * are)
        abre, abim = mag * jnp.cos(dt * aim), mag * jnp.sin(dt * aim)
        nre, nim = abre - 1.0, abim
        den = are * are + aim * aim
        zre = (nre * are + nim * aim) / den
        zim = (nim * are - nre * aim) / den
        bbre, bbim = _cmul(zre, zim, btr_ref[gi], bti_ref[gi])
        bbcat = jnp.where(first, bbre, bbim)

        cpre, cpim = cr_ref[gi], ci_ref[gi]
        qre, qim = bbre, bbim
        pwre, pwim = jnp.ones_like(abre), jnp.zeros_like(abim)
        for t in range(SSM_T):
            cp_ref[t * SSM_GROUP:(t + 1) * SSM_GROUP, :] = jnp.where(first, cpre, -cpim)
            if t > 0:
                place(pmt2_ref, (0,), t - 1, gi, cpre, -cpim)
            place(q2_ref, (), SSM_T - 1 - t, gi, qre, qim)
            cpre, cpim = _cmul(cpre, cpim, abre, abim)
            qre, qim = _cmul(qre, qim, abre, abim)
            pwre, pwim = _cmul(pwre, pwim, abre, abim)
        place(pmt2_ref, (0,), SSM_T - 1, gi, cpre, -cpim)
        pw_parts.append((pwre, pwim))

        krow = lax.dot_general(bbcat, cp_ref[...], (((1,), (1,)), ((), ())),
                               preferred_element_type=F32, precision=lax.Precision.HIGHEST)
        own = (lane >= gi * SSM_GROUP) & (lane < (gi + 1) * SSM_GROUP)
        for tau in range(SSM_T):
            half = krow[:, (tau // SSM_TILE_G) * LANES:(tau // SSM_TILE_G + 1) * LANES]
            shift = ((gi - tau % SSM_TILE_G) * SSM_GROUP) % LANES
            moved = half if shift == 0 else pltpu.roll(half, shift, axis=1)
            wl_ref[tau, gi * SSM_GROUP:(gi + 1) * SSM_GROUP, :] = jnp.where(own, moved, 0.0)

    r = lax.broadcasted_iota(jnp.int32, (LANES, LANES), 0)
    c = lax.broadcasted_iota(jnp.int32, (LANES, LANES), 1)
    wl_ref[0] = wl_ref[0] + jnp.where(r == c, d_ref[0], 0.0)
    for dl in range(SSM_PAIRS):
        diag = wl_ref[2 * dl].astype(BF16)
        w2_ref[0, dl, 0:LANES, 0:LANES] = diag
        w2_ref[0, dl, LANES:, LANES:] = diag
        w2_ref[0, dl, 0:LANES, LANES:] = wl_ref[2 * dl + 1].astype(BF16)
        below = jnp.zeros((LANES, LANES), BF16) if dl == 0 else wl_ref[2 * dl - 1].astype(BF16)
        w2_ref[0, dl, LANES:, 0:LANES] = below

    v = _dot(_step_pair(u_ref, 0, n_chunks), q2_ref[0])
    for a in range(1, SSM_PAIRS):
        v = v + _dot(_step_pair(u_ref, a, n_chunks), q2_ref[a])
    v_ref[...] = v

    pair = lambda k, part: jnp.where(first[0:1], pw_parts[2 * k][part], pw_parts[2 * k + 1][part])
    ar = jnp.concatenate([pair(k, 0) for k in range(SSM_TILE_G // 2)], axis=1)
    ai = jnp.concatenate([pair(k, 1) for k in range(SSM_TILE_G // 2)], axis=1)

    def scan_body(c, carry):
        xre, xim = carry
        x0_ref[pl.ds(c, 1), 0:half_w] = xre
        x0_ref[pl.ds(c, 1), half_w:] = xim
        vre = v_ref[pl.ds(c, 1), 0:half_w]
        vim = v_ref[pl.ds(c, 1), half_w:]
        return ar * xre - ai * xim + vre, ar * xim + ai * xre + vim

    zero = jnp.zeros(ar.shape, F32)
    lax.fori_loop(0, n_chunks, scan_body, (zero, zero), unroll=8)


def _ssm_in(u, ldt, are2, aim2, bt_re2, bt_im2, c_re2, c_im2, d_rows):
    s, width = u.shape
    n_tiles = width // LANES
    c = s // SSM_T
    kw = SSM_TILE_G * LANES
    per_tile = lambda *shape: pl.BlockSpec((SSM_TILE_G,) + shape,
                                           lambda j: (j,) + (0,) * len(shape))
    return pl.pallas_call(
        _ssm_in_kernel,
        grid=(n_tiles,),
        in_specs=[pl.BlockSpec((s, LANES), lambda j: (0, j)),
                  per_tile(1, LANES), per_tile(1, LANES), per_tile(1, LANES),
                  per_tile(SSM_GROUP, LANES), per_tile(SSM_GROUP, LANES),
                  per_tile(SSM_GROUP, LANES), per_tile(SSM_GROUP, LANES),
                  pl.BlockSpec((1, 1, LANES), lambda j: (j, 0, 0))],
        out_specs=[pl.BlockSpec((1, SSM_PAIRS, 2 * LANES, 2 * LANES), lambda j: (j, 0, 0, 0)),
                   pl.BlockSpec((1, SSM_PAIRS, 2 * LANES, kw), lambda j: (j, 0, 0, 0)),
                   pl.BlockSpec((c, kw), lambda j: (0, j))],
        out_shape=[jax.ShapeDtypeStruct((n_tiles, SSM_PAIRS, 2 * LANES, 2 * LANES), BF16),
                   jax.ShapeDtypeStruct((n_tiles, SSM_PAIRS, 2 * LANES, kw), BF16),
                   jax.ShapeDtypeStruct((c, n_tiles * kw), F32)],
        scratch_shapes=[pltpu.VMEM((SSM_TW, LANES), F32),
                        pltpu.VMEM((SSM_T, LANES, LANES), F32),
                        pltpu.VMEM((SSM_PAIRS, 2 * LANES, kw), BF16),
                        pltpu.VMEM((c, kw), F32)],
        compiler_params=_cparams(("arbitrary",), 48),
        name="ssm_state_in",
    )(u, ldt, are2, aim2, bt_re2, bt_im2, c_re2, c_im2, d_rows)


def _ssm_out_kernel(u_ref, w2_ref, pmt2_ref, x0_ref, y_ref):
    n_chunks = u_ref.shape[0] // SSM_T
    x0 = x0_ref[...].astype(BF16)
    pairs = [_step_pair(u_ref, a, n_chunks) for a in range(SSM_PAIRS)]
    for b in range(SSM_PAIRS):
        acc = _dot_nt(x0, pmt2_ref[0, b])
        for a in range(b + 1):
            acc = acc + _dot(pairs[a], w2_ref[0, b - a])
        y_ref[pl.ds(2 * b, n_chunks, stride=SSM_T), :] = acc[:, :LANES]
        y_ref[pl.ds(2 * b + 1, n_chunks, stride=SSM_T), :] = acc[:, LANES:]


def _ssm_out(u, w2, pmt2, x0):
    s, width = u.shape
    n_tiles = width // LANES
    c = s // SSM_T
    kw = SSM_TILE_G * LANES
    return pl.pallas_call(
        _ssm_out_kernel,
        grid=(n_tiles,),
        in_specs=[pl.BlockSpec((s, LANES), lambda j: (0, j)),
                  pl.BlockSpec((1, SSM_PAIRS, 2 * LANES, 2 * LANES), lambda j: (j, 0, 0, 0)),
                  pl.BlockSpec((1, SSM_PAIRS, 2 * LANES, kw), lambda j: (j, 0, 0, 0)),
                  pl.BlockSpec((c, kw), lambda j: (0, j))],
        out_specs=pl.BlockSpec((s, LANES), lambda j: (0, j)),
        out_shape=jax.ShapeDtypeStruct((s, width), F32),
        compiler_params=_cparams(("arbitrary",), 48),
        name="ssm_out",
    )(u, w2, pmt2, x0)


def _s5(u, a_re, a_im, log_dt, b_re, b_im, c_re, c_im, d_skip):
    g, p = a_re.shape
    dup = lambda a: jnp.concatenate([a, a], axis=-1)
    ldt = jnp.broadcast_to(log_dt.reshape(g, 1, 1), (g, 1, LANES))
    are2, aim2 = dup(a_re).reshape(g, 1, LANES), dup(a_im).reshape(g, 1, LANES)
    bt_re2 = dup(jnp.swapaxes(b_re, 1, 2))
    bt_im2 = dup(jnp.swapaxes(b_im, 1, 2))
    w2, pmt2, x0 = _ssm_in(u, ldt, are2, aim2, bt_re2, bt_im2, dup(c_re), dup(c_im),
                           d_skip.reshape(-1, 1, LANES))
    return _ssm_out(u, w2, pmt2, x0)


def _gelu_tanh(x):
    return 0.5 * x * (1.0 + jnp.tanh(math.sqrt(2.0 / math.pi) * (x + 0.044715 * (x * x * x))))


def _mixout_kernel(x_ref, attn_ref, y_ref, wglu_ref, bglu_ref, ga_ref, gs_ref, wo_ref,
                   gt_ref, gf_ref, sc_ref, sh_ref, h_ref, hn_ref):
    aw = attn_ref.shape[1]
    y = _gelu_tanh(y_ref[...])
    gate = jax.nn.sigmoid(_dot(y.astype(BF16), wglu_ref[...]) + bglu_ref[...])
    ns = _rms(y * gate, gs_ref[...]).astype(BF16)
    na = _rms(attn_ref[...].astype(F32), ga_ref[...]).astype(BF16)
    mixed = _dot(na, wo_ref[0:aw, :]) + _dot(ns, wo_ref[aw:, :])
    h = x_ref[...] + gt_ref[...] * mixed
    h_ref[...] = h
    hn_ref[...] = (_rms(h, gf_ref[...]) * (1.0 + sc_ref[...]) + sh_ref[...]).astype(BF16)


def _mixout(x, attn, y, w_glu, b_glu, g_attn, g_ssm, w_out, gt1, g_ffn, sc2, sh2, tm=512):
    s, d = x.shape
    w = attn.shape[1]
    rows = lambda c: pl.BlockSpec((tm, c), lambda i: (i, 0))
    return pl.pallas_call(
        _mixout_kernel,
        grid=(s // tm,),
        in_specs=[rows(d), rows(w), rows(w), _resident((w, w)), _resident((1, w)),
                  _resident((1, w)), _resident((1, w)), _resident(w_out.shape),
                  _resident((1, d)), _resident((1, d)), _resident((1, d)), _resident((1, d))],
        out_specs=[rows(d), rows(d)],
        out_shape=[jax.ShapeDtypeStruct((s, d), F32), jax.ShapeDtypeStruct((s, d), BF16)],
        compiler_params=_cparams(("arbitrary",), 56),
        name="mixer_out",
    )(x, attn, y, w_glu, b_glu, g_attn, g_ssm, w_out, gt1, g_ffn, sc2, sh2)


def _ffn_kernel(hn_ref, halo_ref, h_ref, wa_ref, wb_ref, cw_ref, cb_ref, wd_ref, gt_ref, gfin_ref,
                o_ref):
    i, j = pl.program_id(0), pl.program_id(1)

    @pl.when(j == 0)
    def _():
        o_ref[...] = jnp.zeros(o_ref.shape, F32)

    hn = hn_ref[...]
    a = _dot(hn, wa_ref[...])
    b = _dot(hn, wb_ref[...])
    halo = _dot(halo_ref[...], wa_ref[...]) * (i > 0).astype(F32)
    row = lax.broadcasted_iota(jnp.int32, a.shape, 0)
    prev1 = jnp.where(row == 0, halo[7:8, :], pltpu.roll(a, 1, axis=0))
    prev2 = jnp.where(row == 0, halo[6:7, :],
                      jnp.where(row == 1, halo[7:8, :], pltpu.roll(a, 2, axis=0)))
    cw = cw_ref[...]
    conv = cb_ref[...] + cw[0:1, :] * prev2 + cw[1:2, :] * prev1 + cw[2:3, :] * a
    act = (conv * jax.nn.sigmoid(conv) * b).astype(BF16)
    o_ref[...] += _dot(act, wd_ref[...])

    @pl.when(j == pl.num_programs(1) - 1)
    def _():
        h = h_ref[...] + gt_ref[...] * o_ref[...]
        o_ref[...] = _rms(h, gfin_ref[...])


def _ffn(hn, h, w_up, conv_w, conv_b, w_down, gt2, g_final, tm=1024, tn=512):
    s, d = h.shape
    d_ff = w_down.shape[0]
    nf = d_ff // tn
    halo_blocks = tm // 8
    return pl.pallas_call(
        _ffn_kernel,
        grid=(s // tm, nf),
        in_specs=[pl.BlockSpec((tm, d), lambda i, j: (i, 0)),
                  pl.BlockSpec((8, d), lambda i, j: (jnp.maximum(i * halo_blocks - 1, 0), 0)),
                  pl.BlockSpec((tm, d), lambda i, j: (i, 0), pipeline_mode=pl.Buffered(1)),
                  pl.BlockSpec((d, tn), lambda i, j: (0, j)),
                  pl.BlockSpec((d, tn), lambda i, j: (0, nf + j)),
                  pl.BlockSpec((3, tn), lambda i, j: (0, j)),
                  pl.BlockSpec((1, tn), lambda i, j: (0, j)),
                  pl.BlockSpec((tn, d), lambda i, j: (j, 0)),
                  _resident((1, d)), _resident((1, d))],
        out_specs=pl.BlockSpec((tm, d), lambda i, j: (i, 0)),
        out_shape=jax.ShapeDtypeStruct((s, d), F32),
        compiler_params=_cparams(("arbitrary", "arbitrary"), 62),
        name="conv_ffn",
    )(hn, hn, h, w_up, w_up, conv_w, conv_b, w_down, gt2, g_final)


def _layer(h, mod, g_mix, w_in, b_f, a_re, a_im, log_dt, ssm_b_re, ssm_b_im, ssm_c_re, ssm_c_im,
           ssm_d, w_glu, b_glu, g_attn_out, g_ssm_out, w_out, g_ffn, w_up, conv_w, conv_b, w_down):
    s, d = h.shape
    aw = N_HEADS * HEAD_DIM
    sh1, sc1, gt1, sh2, sc2, gt2 = [mod[:, i * d:(i + 1) * d] for i in range(N_MOD)]
    row = lambda a: a.reshape(1, -1)

    q, k, v, u, f, k_norm_sq = _inproj(h, row(g_mix), sc1, sh1, w_in.astype(BF16),
                            w_in[:, 3 * aw + N_HEADS:].astype(BF16))

    cum_row, cum_col = _forget_cumsum(f, b_f)
    attn = _attention(q, k, v, cum_col, cum_row.reshape(N_HEADS, 1, s), k_norm_sq)

    y = _s5(u, a_re, a_im, log_dt, ssm_b_re, ssm_b_im, ssm_c_re, ssm_c_im, ssm_d)

    h1, hn2 = _mixout(h, attn, y, w_glu.astype(BF16), row(b_glu), row(g_attn_out),
                      row(g_ssm_out), w_out.astype(BF16), gt1, row(g_ffn), sc2, sh2)
    return hn2, h1, (w_up.astype(BF16), conv_w, row(conv_b), w_down.astype(BF16), gt2)


def kernel(x, c, w_ada, b_ada, g_mix, w_in, b_f, a_re, a_im, log_dt, ssm_b_re, ssm_b_im, ssm_c_re,
           ssm_c_im, ssm_d, w_glu, b_glu, g_attn_out, g_ssm_out, w_out, g_ffn, w_up, conv_w,
           conv_b, w_down, g_final):
    batch, s, d = x.shape
    assert w_ada.shape[0] == 1, "only DEPTH == 1 is supported"
    l = 0
    outs = []
    for bi in range(batch):
        mod = _adaln(c[bi:bi + 1], w_ada[l], b_ada[l])
        hn2, h1, ffn_args = _layer(
            x[bi], mod, g_mix[l], w_in[l], b_f[l], a_re[l], a_im[l], log_dt[l], ssm_b_re[l],
            ssm_b_im[l], ssm_c_re[l], ssm_c_im[l], ssm_d[l], w_glu[l], b_glu[l],
            g_attn_out[l], g_ssm_out[l], w_out[l], g_ffn[l], w_up[l], conv_w[l], conv_b[l],
            w_down[l])
        outs.append(_ffn(hn2, h1, *ffn_args, g_final.reshape(1, d)))
    return jnp.stack(outs, axis=0)
```

```python
import functools
import math

import jax
import jax.numpy as jnp
from jax import lax
from jax.experimental import pallas as pl
from jax.experimental.pallas import tpu as pltpu

F32 = jnp.float32
BF16 = jnp.bfloat16

EPS = 1e-6
HEAD_DIM = 128
N_HEADS = 8
SSM_GROUP = 16
SSM_STATE = 64
N_MOD = 6
LANES = 128
SSM_T = 16
SSM_TW = SSM_T * SSM_GROUP
SSM_PAIRS = SSM_T // 2
SSM_TILE_G = LANES // SSM_GROUP
LOG2E = 1.4426950408889634
SKIP_LOG2 = 151.0
NORM_SLACK = 1.01

_MIB = 1024 * 1024


def _cparams(semantics, vmem_mib):
    return pltpu.CompilerParams(dimension_semantics=semantics, vmem_limit_bytes=vmem_mib * _MIB)


def _resident(shape):
    return pl.BlockSpec(shape, lambda *_: (0,) * len(shape), pipeline_mode=pl.Buffered(1))


def _dot(a, b):
    return jnp.dot(a, b, preferred_element_type=F32)


def _dot_nt(a, b):
    return lax.dot_general(a, b, (((1,), (1,)), ((), ())), preferred_element_type=F32)


def _rms(x, g):
    return x * lax.rsqrt(jnp.mean(x * x, axis=-1, keepdims=True) + EPS) * g


def _lane_tile(x, reps):
    return jnp.concatenate([x] * reps, axis=1)


def _cast_slab(w, steps):
    rows = w.shape[0] // steps
    assert rows * steps == w.shape[0] and rows % 16 == 0, (w.shape, steps)
    return rows, jax.ShapeDtypeStruct(w.shape, BF16)


def _adaln_kernel(c_ref, w_ref, b_ref, wsrc_ref, o_ref, wdst_ref):
    wdst_ref[...] = wsrc_ref[...].astype(BF16)
    c = c_ref[...]
    cond = c * jax.nn.sigmoid(c)
    cond8 = jnp.broadcast_to(cond, (8, c.shape[1])).astype(BF16)
    acc = _dot(cond8, w_ref[...].astype(BF16))
    o_ref[...] = acc[0:1, :] + b_ref[...]


def _adaln(c, w, b, w_cast, tn=1536):
    d, n = w.shape
    slab, w_bf16 = _cast_slab(w_cast, n // tn)
    slab_spec = pl.BlockSpec((slab, w_cast.shape[1]), lambda j: (j, 0))
    return pl.pallas_call(
        _adaln_kernel,
        grid=(n // tn,),
        in_specs=[pl.BlockSpec((1, d), lambda j: (0, 0)),
                  pl.BlockSpec((d, tn), lambda j: (0, j)),
                  pl.BlockSpec((1, tn), lambda j: (0, j)), slab_spec],
        out_specs=[pl.BlockSpec((1, tn), lambda j: (0, j)), slab_spec],
        out_shape=[jax.ShapeDtypeStruct((1, n), F32), w_bf16],
        compiler_params=_cparams(("arbitrary",), 56),
        name="adaln",
    )(c, w, b.reshape(1, n), w_cast)


def _inproj_kernel(x_ref, g_ref, sc_ref, sh_ref, wq_ref, wk_ref, wv_ref, wu_ref, wf_ref,
                   q_ref, k_ref, v_ref, u_ref, f_ref, kn_ref, *, q_scale):
    hn = (_rms(x_ref[...], g_ref[...]) * (1.0 + sc_ref[...]) + sh_ref[...]).astype(BF16)
    f_ref[...] = _dot(hn, wf_ref[...])
    q_ref[...] = (_dot(hn, wq_ref[...]) * q_scale).astype(BF16)
    k = _dot(hn, wk_ref[...]).astype(BF16)
    k_ref[...] = k
    u_ref[...] = _dot(hn, wu_ref[...])

    k32 = k.astype(F32)
    ksq = k32 * k32
    tile_max = jnp.concatenate(
        [jnp.broadcast_to(
            jnp.max(jnp.sum(ksq[:, h * HEAD_DIM:(h + 1) * HEAD_DIM], axis=1, keepdims=True),
                    axis=0, keepdims=True), (1, LANES)) for h in range(N_HEADS)], axis=0)

    @pl.when(pl.program_id(0) == 0)
    def _():
        kn_ref[...] = tile_max

    @pl.when(pl.program_id(0) > 0)
    def _():
        kn_ref[...] = jnp.maximum(kn_ref[...], tile_max)

    v = _dot(hn, wv_ref[...]).astype(BF16)
    ones = jnp.ones((v.shape[0], HEAD_DIM), BF16)
    for h in range(N_HEADS):
        v_ref[:, 2 * h * HEAD_DIM:(2 * h + 1) * HEAD_DIM] = v[:, h * HEAD_DIM:(h + 1) * HEAD_DIM]
        v_ref[:, (2 * h + 1) * HEAD_DIM:(2 * h + 2) * HEAD_DIM] = ones


def _inproj(x, g, sc, sh, w_in, wu, tm=512):
    s, d = x.shape
    aw, sw = N_HEADS * HEAD_DIM, wu.shape[1]
    rows = lambda c: pl.BlockSpec((tm, c), lambda i: (i, 0))
    cols = lambda width, blk: pl.BlockSpec((d, width), lambda i: (0, blk),
                                           pipeline_mode=pl.Buffered(1))
    return pl.pallas_call(
        functools.partial(_inproj_kernel, q_scale=HEAD_DIM ** -0.5 * LOG2E),
        grid=(s // tm,),
        in_specs=[rows(d), _resident((1, d)), _resident((1, d)), _resident((1, d)),
                  cols(aw, 0), cols(aw, 1), cols(aw, 2), _resident((d, sw)),
                  cols(LANES, 3 * aw // LANES)],
        out_specs=[rows(aw), rows(aw), rows(2 * aw), rows(sw), rows(LANES),
                   pl.BlockSpec((N_HEADS, LANES), lambda i: (0, 0))],
        out_shape=[jax.ShapeDtypeStruct((s, aw), BF16), jax.ShapeDtypeStruct((s, aw), BF16),
                   jax.ShapeDtypeStruct((s, 2 * aw), BF16), jax.ShapeDtypeStruct((s, sw), F32),
                   jax.ShapeDtypeStruct((s, LANES), F32),
                   jax.ShapeDtypeStruct((N_HEADS, LANES), F32)],
        compiler_params=_cparams(("arbitrary",), 56),
        name="inproj",
    )(x, g, sc, sh, w_in, w_in, w_in, wu, w_in)


def _log2_forget_cumsum(z, axis):
    x = jnp.minimum(z, 0.0) - jnp.log1p(jnp.exp(-jnp.abs(z)))
    pos = lax.broadcasted_iota(jnp.int32, x.shape, axis)
    shift = 1
    while shift < x.shape[axis]:
        x = x + jnp.where(pos >= shift, pltpu.roll(x, shift, axis=axis), 0.0)
        shift *= 2
    return x * LOG2E


def _cum_kernel(ft_ref, bcol_ref, f_ref, brow_ref, row_ref, col_ref):
    row_ref[...] = _log2_forget_cumsum(ft_ref[...] + bcol_ref[...], 1)
    col_ref[...] = _log2_forget_cumsum(f_ref[...] + brow_ref[...], 0)


def _forget_cumsum(f, b_f):
    s, w = f.shape
    h = b_f.shape[0]
    return pl.pallas_call(
        _cum_kernel,
        out_shape=[jax.ShapeDtypeStruct((h, s), F32), jax.ShapeDtypeStruct((s, w), F32)],
        compiler_params=_cparams(None, 48),
        name="forget_cumsum",
    )(f[:, :h].T, b_f.reshape(h, 1), f, jnp.pad(b_f, (0, w - h)).reshape(1, w))


def _attn_kernel(q_ref, k_ref, v_ref, cq_ref, ck_ref, kn_ref, wsrc_ref, o_ref, wdst_ref,
                 m_ref, acc_ref, *, tq, near, splits):
    i = pl.program_id(1)
    wdst_ref[...] = wsrc_ref[...].astype(BF16)
    kmax = jnp.sqrt(kn_ref[pl.ds(pl.program_id(0), 1), :]) * NORM_SLACK

    head_lane = lax.broadcasted_iota(jnp.int32, (tq, LANES), 1) == pl.program_id(0)
    cqb = jnp.broadcast_to(
        jnp.sum(jnp.where(head_lane, cq_ref[...], 0.0), axis=1, keepdims=True), (tq, LANES))

    half = tq // splits
    halves = [slice(hh * half, (hh + 1) * half) for hh in range(splits)]

    base = pl.multiple_of(i * tq, tq)
    qf = q_ref[...].astype(F32)
    own = jnp.sum(qf * k_ref[pl.ds(base, tq), :].astype(F32), axis=1, keepdims=True)
    qn = jnp.sqrt(jnp.sum(qf * qf, axis=1, keepdims=True))
    bound = jnp.max(qn * kmax - (own - cqb))
    ck_all = ck_ref[0]
    pos = lax.broadcasted_iota(jnp.int32, ck_all.shape, 1)
    dead = jnp.where((pos < i * tq) & (ck_all > bound + SKIP_LOG2), 1.0, 0.0)
    j0 = jnp.sum(dead).astype(jnp.int32) // tq

    first_near = jnp.maximum(i - near, 0)
    near_w = near * tq
    near_ks = pl.multiple_of(first_near * tq, tq)
    near_k = k_ref[pl.ds(near_ks, near_w), :]
    near_v = v_ref[pl.ds(near_ks, near_w), :]
    near_key = near_ks + lax.broadcasted_iota(jnp.int32, (1, near_w), 1)
    near_ck = jnp.where(near_key < base, ck_ref[0, :, pl.ds(near_ks, near_w)], jnp.inf)
    for hh, rs in enumerate(halves):
        nk = (hh + 1) * half
        qh = q_ref[rs, :]
        t_diag = _dot_nt(qh, k_ref[pl.ds(base, nk), :]) - ck_ref[0, :, pl.ds(base, nk)]
        row = lax.broadcasted_iota(jnp.int32, t_diag.shape, 0)
        col = lax.broadcasted_iota(jnp.int32, t_diag.shape, 1)
        t_diag = jnp.where(col <= row + hh * half, t_diag, -jnp.inf)
        t = jnp.concatenate([_dot_nt(qh, near_k) - near_ck, t_diag], axis=1)
        m0 = jnp.max(t, axis=1, keepdims=True) + cqb[rs]
        p = jnp.exp2(t - _lane_tile(m0 - cqb[rs], (near_w + nk) // LANES)).astype(BF16)
        m_ref[rs, :] = m0
        acc = _dot(p[:, :near_w], near_v) + _dot(p[:, near_w:], v_ref[pl.ds(base, nk), :])
        acc_ref[rs, :] = acc
        o_ref[rs, :] = (acc[:, :HEAD_DIM] / acc[:, HEAD_DIM:]).astype(BF16)

    def body(j, carry):
        ks = pl.multiple_of(j * tq, tq)
        kb = k_ref[pl.ds(ks, tq), :]
        vb = v_ref[pl.ds(ks, tq), :]
        ck = ck_ref[0, :, pl.ds(ks, tq)]
        for rs in halves:
            t = _dot_nt(q_ref[rs, :], kb) - ck
            m_prev = m_ref[rs, :]
            m_new = jnp.maximum(m_prev, jnp.max(t, axis=1, keepdims=True) + cqb[rs])
            alpha = jnp.exp2(m_prev - m_new)
            p = jnp.exp2(t - _lane_tile(m_new - cqb[rs], tq // LANES))
            acc_ref[rs, :] = _lane_tile(alpha, 2) * acc_ref[rs, :] + _dot(p.astype(BF16), vb)
            m_ref[rs, :] = m_new
        return carry

    @pl.when(j0 < first_near)
    def _():
        lax.fori_loop(j0, first_near, body, 0)
        acc = acc_ref[...]
        o_ref[...] = (acc[:, :HEAD_DIM] / acc[:, HEAD_DIM:]).astype(BF16)


def _attention(q, k, v_ones, cum_col, cum_row, k_norm_sq, w_cast, tq=512, near=2, splits=2):
    s = q.shape[0]
    nq = s // tq
    slab, w_bf16 = _cast_slab(w_cast, N_HEADS * nq)
    slab_spec = pl.BlockSpec((slab, w_cast.shape[1]), lambda h, i: (h * nq + i, 0))
    return pl.pallas_call(
        functools.partial(_attn_kernel, tq=tq, near=near, splits=splits),
        grid=(N_HEADS, nq),
        in_specs=[pl.BlockSpec((tq, HEAD_DIM), lambda h, i: (i, h)),
                  pl.BlockSpec((s, HEAD_DIM), lambda h, i: (0, h)),
                  pl.BlockSpec((s, 2 * HEAD_DIM), lambda h, i: (0, h)),
                  pl.BlockSpec((tq, LANES), lambda h, i: (i, 0)),
                  pl.BlockSpec((1, 1, s), lambda h, i: (h, 0, 0)),
                  pl.BlockSpec((N_HEADS, LANES), lambda h, i: (0, 0)),
                  slab_spec],
        out_specs=[pl.BlockSpec((tq, HEAD_DIM), lambda h, i: (i, h)), slab_spec],
        out_shape=[jax.ShapeDtypeStruct((s, N_HEADS * HEAD_DIM), BF16), w_bf16],
        scratch_shapes=[pltpu.VMEM((tq, LANES), F32), pltpu.VMEM((tq, 2 * HEAD_DIM), F32)],
        compiler_params=_cparams(("arbitrary", "arbitrary"), 40),
        name="fox_attention",
    )(q, k, v_ones, cum_col, cum_row, k_norm_sq, w_cast)


def _cmul(are, aim, bre, bim):
    return are * bre - aim * bim, are * bim + aim * bre


def _step_pair(u_ref, a, n_chunks):
    x0 = u_ref[pl.ds(2 * a, n_chunks, stride=SSM_T), :]
    x1 = u_ref[pl.ds(2 * a + 1, n_chunks, stride=SSM_T), :]
    return jnp.concatenate([x0, x1], axis=1).astype(BF16)


def _ssm_in_kernel(u_ref, ldt_ref, are_ref, aim_ref, btr_ref, bti_ref, cr_ref, ci_ref, d_ref,
                   wsrc_ref, w2_ref, pmt2_ref, x0_ref, wdst_ref,
                   cp_ref, wl_ref, q2_ref, v_ref):
    wdst_ref[...] = wsrc_ref[...].astype(BF16)
    n_chunks = u_ref.shape[0] // SSM_T
    half_w = SSM_TILE_G * SSM_STATE
    pw_parts = []
    lane = lax.broadcasted_iota(jnp.int32, (SSM_GROUP, LANES), 1)
    first = lane < SSM_STATE

    @pl.when(pl.program_id(0) == 0)
    def _():
        q2_ref[...] = jnp.zeros(q2_ref.shape, BF16)

    pmt2_ref[...] = jnp.zeros(pmt2_ref.shape, BF16)

    def place(ref, lead, step, gi, re_part, im_part):
        r0 = (step % 2) * LANES + gi * SSM_GROUP
        rows = slice(r0, r0 + SSM_GROUP)
        mine = first if gi % 2 == 0 else jnp.logical_not(first)
        c_re = (gi // 2) * LANES
        c_im = half_w + c_re
        ref[lead + (step // 2, rows, slice(c_re, c_re + LANES))] = (
            jnp.where(mine, re_part, 0.0).astype(BF16))
        ref[lead + (step // 2, rows, slice(c_im, c_im + LANES))] = (
            jnp.where(mine, im_part, 0.0).astype(BF16))

    for gi in range(SSM_TILE_G):
        dt = jnp.exp(ldt_ref[gi])
        are, aim = are_ref[gi], aim_ref[gi]
        mag = jnp.exp(dt * are)
        abre, abim = mag * jnp.cos(dt * aim), mag * jnp.sin(dt * aim)
        nre, nim = abre - 1.0, abim
        den = are * are + aim * aim
        zre = (nre * are + nim * aim) / den
        zim = (nim * are - nre * aim) / den
        bbre, bbim = _cmul(zre, zim, btr_ref[gi], bti_ref[gi])
        bbcat = jnp.where(first, bbre, bbim)

        cpre, cpim = cr_ref[gi], ci_ref[gi]
        qre, qim = bbre, bbim
        pwre, pwim = jnp.ones_like(abre), jnp.zeros_like(abim)
        for t in range(SSM_T):
            cp_ref[t * SSM_GROUP:(t + 1) * SSM_GROUP, :] = jnp.where(first, cpre, -cpim)
            if t > 0:
                place(pmt2_ref, (0,), t - 1, gi, cpre, -cpim)
            place(q2_ref, (), SSM_T - 1 - t, gi, qre, qim)
            cpre, cpim = _cmul(cpre, cpim, abre, abim)
            qre, qim = _cmul(qre, qim, abre, abim)
            pwre, pwim = _cmul(pwre, pwim, abre, abim)
        place(pmt2_ref, (0,), SSM_T - 1, gi, cpre, -cpim)
        pw_parts.append((pwre, pwim))

        krow = lax.dot_general(bbcat, cp_ref[...], (((1,), (1,)), ((), ())),
                               preferred_element_type=F32, precision=lax.Precision.HIGHEST)
        own = (lane >= gi * SSM_GROUP) & (lane < (gi + 1) * SSM_GROUP)
        for tau in range(SSM_T):
            half = krow[:, (tau // SSM_TILE_G) * LANES:(tau // SSM_TILE_G + 1) * LANES]
            shift = ((gi - tau % SSM_TILE_G) * SSM_GROUP) % LANES
            moved = half if shift == 0 else pltpu.roll(half, shift, axis=1)
            wl_ref[tau, gi * SSM_GROUP:(gi + 1) * SSM_GROUP, :] = jnp.where(own, moved, 0.0)

    r = lax.broadcasted_iota(jnp.int32, (LANES, LANES), 0)
    c = lax.broadcasted_iota(jnp.int32, (LANES, LANES), 1)
    wl_ref[0] = wl_ref[0] + jnp.where(r == c, d_ref[0], 0.0)
    for dl in range(SSM_PAIRS):
        diag = wl_ref[2 * dl].astype(BF16)
        w2_ref[0, dl, 0:LANES, 0:LANES] = diag
        w2_ref[0, dl, LANES:, LANES:] = diag
        w2_ref[0, dl, 0:LANES, LANES:] = wl_ref[2 * dl + 1].astype(BF16)
        below = jnp.zeros((LANES, LANES), BF16) if dl == 0 else wl_ref[2 * dl - 1].astype(BF16)
        w2_ref[0, dl, LANES:, 0:LANES] = below

    v = _dot(_step_pair(u_ref, 0, n_chunks), q2_ref[0])
    for a in range(1, SSM_PAIRS):
        v = v + _dot(_step_pair(u_ref, a, n_chunks), q2_ref[a])
    v_ref[...] = v

    pair = lambda k, part: jnp.where(first[0:1], pw_parts[2 * k][part], pw_parts[2 * k + 1][part])
    ar = jnp.concatenate([pair(k, 0) for k in range(SSM_TILE_G // 2)], axis=1)
    ai = jnp.concatenate([pair(k, 1) for k in range(SSM_TILE_G // 2)], axis=1)

    def scan_body(c, carry):
        xre, xim = carry
        x0_ref[pl.ds(c, 1), 0:half_w] = xre
        x0_ref[pl.ds(c, 1), half_w:] = xim
        vre = v_ref[pl.ds(c, 1), 0:half_w]
        vim = v_ref[pl.ds(c, 1), half_w:]
        return ar * xre - ai * xim + vre, ar * xim + ai * xre + vim

    zero = jnp.zeros(ar.shape, F32)
    lax.fori_loop(0, n_chunks, scan_body, (zero, zero), unroll=8)


def _ssm_in(u, ldt, are2, aim2, bt_re2, bt_im2, c_re2, c_im2, d_rows, w_cast):
    s, width = u.shape
    n_tiles = width // LANES
    c = s // SSM_T
    kw = SSM_TILE_G * LANES
    per_tile = lambda *shape: pl.BlockSpec((SSM_TILE_G,) + shape,
                                           lambda j: (j,) + (0,) * len(shape))
    slab, w_bf16 = _cast_slab(w_cast, n_tiles)
    slab_spec = pl.BlockSpec((slab, w_cast.shape[1]), lambda j: (j, 0))
    return pl.pallas_call(
        _ssm_in_kernel,
        grid=(n_tiles,),
        in_specs=[pl.BlockSpec((s, LANES), lambda j: (0, j)),
                  per_tile(1, LANES), per_tile(1, LANES), per_tile(1, LANES),
                  per_tile(SSM_GROUP, LANES), per_tile(SSM_GROUP, LANES),
                  per_tile(SSM_GROUP, LANES), per_tile(SSM_GROUP, LANES),
                  pl.BlockSpec((1, 1, LANES), lambda j: (j, 0, 0)), slab_spec],
        out_specs=[pl.BlockSpec((1, SSM_PAIRS, 2 * LANES, 2 * LANES), lambda j: (j, 0, 0, 0)),
                   pl.BlockSpec((1, SSM_PAIRS, 2 * LANES, kw), lambda j: (j, 0, 0, 0)),
                   pl.BlockSpec((c, kw), lambda j: (0, j)), slab_spec],
        out_shape=[jax.ShapeDtypeStruct((n_tiles, SSM_PAIRS, 2 * LANES, 2 * LANES), BF16),
                   jax.ShapeDtypeStruct((n_tiles, SSM_PAIRS, 2 * LANES, kw), BF16),
                   jax.ShapeDtypeStruct((c, n_tiles * kw), F32), w_bf16],
        scratch_shapes=[pltpu.VMEM((SSM_TW, LANES), F32),
                        pltpu.VMEM((SSM_T, LANES, LANES), F32),
                        pltpu.VMEM((SSM_PAIRS, 2 * LANES, kw), BF16),
                        pltpu.VMEM((c, kw), F32)],
        compiler_params=_cparams(("arbitrary",), 60),
        name="ssm_state_in",
    )(u, ldt, are2, aim2, bt_re2, bt_im2, c_re2, c_im2, d_rows, w_cast)


def _ssm_out_kernel(u_ref, w2_ref, pmt2_ref, x0_ref, wsrc_a_ref, wsrc_b_ref,
                    y_ref, wdst_a_ref, wdst_b_ref):
    wdst_a_ref[...] = wsrc_a_ref[...].astype(BF16)
    wdst_b_ref[...] = wsrc_b_ref[...].astype(BF16)
    n_chunks = u_ref.shape[0] // SSM_T
    x0 = x0_ref[...].astype(BF16)
    pairs = [_step_pair(u_ref, a, n_chunks) for a in range(SSM_PAIRS)]
    for b in range(SSM_PAIRS):
        acc = _dot_nt(x0, pmt2_ref[0, b])
        for a in range(b + 1):
            acc = acc + _dot(pairs[a], w2_ref[0, b - a])
        y_ref[pl.ds(2 * b, n_chunks, stride=SSM_T), :] = acc[:, :LANES]
        y_ref[pl.ds(2 * b + 1, n_chunks, stride=SSM_T), :] = acc[:, LANES:]


def _ssm_out(u, w2, pmt2, x0, w_cast_a, w_cast_b):
    s, width = u.shape
    n_tiles = width // LANES
    c = s // SSM_T
    kw = SSM_TILE_G * LANES
    slab_a, a_bf16 = _cast_slab(w_cast_a, n_tiles)
    slab_b, b_bf16 = _cast_slab(w_cast_b, n_tiles)
    spec_a = pl.BlockSpec((slab_a, w_cast_a.shape[1]), lambda j: (j, 0))
    spec_b = pl.BlockSpec((slab_b, w_cast_b.shape[1]), lambda j: (j, 0))
    return pl.pallas_call(
        _ssm_out_kernel,
        grid=(n_tiles,),
        in_specs=[pl.BlockSpec((s, LANES), lambda j: (0, j)),
                  pl.BlockSpec((1, SSM_PAIRS, 2 * LANES, 2 * LANES), lambda j: (j, 0, 0, 0)),
                  pl.BlockSpec((1, SSM_PAIRS, 2 * LANES, kw), lambda j: (j, 0, 0, 0)),
                  pl.BlockSpec((c, kw), lambda j: (0, j)), spec_a, spec_b],
        out_specs=[pl.BlockSpec((s, LANES), lambda j: (0, j)), spec_a, spec_b],
        out_shape=[jax.ShapeDtypeStruct((s, width), F32), a_bf16, b_bf16],
        compiler_params=_cparams(("arbitrary",), 48),
        name="ssm_out",
    )(u, w2, pmt2, x0, w_cast_a, w_cast_b)


def _s5(u, a_re, a_im, log_dt, b_re, b_im, c_re, c_im, d_skip, w_cast_in, w_cast_a, w_cast_b):
    g, p = a_re.shape
    dup = lambda a: jnp.concatenate([a, a], axis=-1)
    ldt = jnp.broadcast_to(log_dt.reshape(g, 1, 1), (g, 1, LANES))
    are2, aim2 = dup(a_re).reshape(g, 1, LANES), dup(a_im).reshape(g, 1, LANES)
    bt_re2 = dup(jnp.swapaxes(b_re, 1, 2))
    bt_im2 = dup(jnp.swapaxes(b_im, 1, 2))
    w2, pmt2, x0, w_in_bf16 = _ssm_in(u, ldt, are2, aim2, bt_re2, bt_im2, dup(c_re), dup(c_im),
                                      d_skip.reshape(-1, 1, LANES), w_cast_in)
    y, w_a_bf16, w_b_bf16 = _ssm_out(u, w2, pmt2, x0, w_cast_a, w_cast_b)
    return y, w_in_bf16, w_a_bf16, w_b_bf16


def _gelu_tanh(x):
    return 0.5 * x * (1.0 + jnp.tanh(math.sqrt(2.0 / math.pi) * (x + 0.044715 * (x * x * x))))


def _mixout_kernel(x_ref, attn_ref, y_ref, wglu_ref, bglu_ref, ga_ref, gs_ref, wo_ref,
                   gt_ref, gf_ref, sc_ref, sh_ref, h_ref, hn_ref):
    aw = attn_ref.shape[1]
    y = _gelu_tanh(y_ref[...])
    gate = jax.nn.sigmoid(_dot(y.astype(BF16), wglu_ref[...]) + bglu_ref[...])
    ns = _rms(y * gate, gs_ref[...]).astype(BF16)
    na = _rms(attn_ref[...].astype(F32), ga_ref[...]).astype(BF16)
    mixed = _dot(na, wo_ref[0:aw, :]) + _dot(ns, wo_ref[aw:, :])
    h = x_ref[...] + gt_ref[...] * mixed
    h_ref[...] = h
    hn_ref[...] = (_rms(h, gf_ref[...]) * (1.0 + sc_ref[...]) + sh_ref[...]).astype(BF16)


def _mixout(x, attn, y, w_glu, b_glu, g_attn, g_ssm, w_out, gt1, g_ffn, sc2, sh2, tm=512):
    s, d = x.shape
    w = attn.shape[1]
    rows = lambda c: pl.BlockSpec((tm, c), lambda i: (i, 0))
    return pl.pallas_call(
        _mixout_kernel,
        grid=(s // tm,),
        in_specs=[rows(d), rows(w), rows(w), _resident((w, w)), _resident((1, w)),
                  _resident((1, w)), _resident((1, w)), _resident(w_out.shape),
                  _resident((1, d)), _resident((1, d)), _resident((1, d)), _resident((1, d))],
        out_specs=[rows(d), rows(d)],
        out_shape=[jax.ShapeDtypeStruct((s, d), F32), jax.ShapeDtypeStruct((s, d), BF16)],
        compiler_params=_cparams(("arbitrary",), 56),
        name="mixer_out",
    )(x, attn, y, w_glu, b_glu, g_attn, g_ssm, w_out, gt1, g_ffn, sc2, sh2)


def _ffn_kernel(hn_ref, halo_ref, h_ref, wa_ref, wb_ref, cw_ref, cb_ref, wd_ref, gt_ref, gfin_ref,
                o_ref):
    i, j = pl.program_id(0), pl.program_id(1)

    @pl.when(j == 0)
    def _():
        o_ref[...] = jnp.zeros(o_ref.shape, F32)

    hn = hn_ref[...]
    a = _dot(hn, wa_ref[...])
    b = _dot(hn, wb_ref[...])
    halo = _dot(halo_ref[...], wa_ref[...]) * (i > 0).astype(F32)
    row = lax.broadcasted_iota(jnp.int32, a.shape, 0)
    prev1 = jnp.where(row == 0, halo[7:8, :], pltpu.roll(a, 1, axis=0))
    prev2 = jnp.where(row == 0, halo[6:7, :],
                      jnp.where(row == 1, halo[7:8, :], pltpu.roll(a, 2, axis=0)))
    cw = cw_ref[...]
    conv = cb_ref[...] + cw[0:1, :] * prev2 + cw[1:2, :] * prev1 + cw[2:3, :] * a
    act = (conv * jax.nn.sigmoid(conv) * b).astype(BF16)
    o_ref[...] += _dot(act, wd_ref[...])

    @pl.when(j == pl.num_programs(1) - 1)
    def _():
        h = h_ref[...] + gt_ref[...] * o_ref[...]
        o_ref[...] = _rms(h, gfin_ref[...])


def _ffn(hn, h, w_up, conv_w, conv_b, w_down, gt2, g_final, tm=1024, tn=512):
    s, d = h.shape
    d_ff = w_down.shape[0]
    nf = d_ff // tn
    halo_blocks = tm // 8
    return pl.pallas_call(
        _ffn_kernel,
        grid=(s // tm, nf),
        in_specs=[pl.BlockSpec((tm, d), lambda i, j: (i, 0)),
                  pl.BlockSpec((8, d), lambda i, j: (jnp.maximum(i * halo_blocks - 1, 0), 0)),
                  pl.BlockSpec((tm, d), lambda i, j: (i, 0), pipeline_mode=pl.Buffered(1)),
                  pl.BlockSpec((d, tn), lambda i, j: (0, j)),
                  pl.BlockSpec((d, tn), lambda i, j: (0, nf + j)),
                  pl.BlockSpec((3, tn), lambda i, j: (0, j)),
                  pl.BlockSpec((1, tn), lambda i, j: (0, j)),
                  pl.BlockSpec((tn, d), lambda i, j: (j, 0)),
                  _resident((1, d)), _resident((1, d))],
        out_specs=pl.BlockSpec((tm, d), lambda i, j: (i, 0)),
        out_shape=jax.ShapeDtypeStruct((s, d), F32),
        compiler_params=_cparams(("arbitrary", "arbitrary"), 62),
        name="conv_ffn",
    )(hn, hn, h, w_up, w_up, conv_w, conv_b, w_down, gt2, g_final)


def _layer(h, mod, g_mix, w_in, b_f, a_re, a_im, log_dt, ssm_b_re, ssm_b_im, ssm_c_re, ssm_c_im,
           ssm_d, w_glu, b_glu, g_attn_out, g_ssm_out, w_out, g_ffn, w_up, conv_w, conv_b, w_down):
    s, d = h.shape
    aw = N_HEADS * HEAD_DIM
    sh1, sc1, gt1, sh2, sc2, gt2 = [mod[:, i * d:(i + 1) * d] for i in range(N_MOD)]
    row = lambda a: a.reshape(1, -1)

    q, k, v, u, f, k_norm_sq = _inproj(h, row(g_mix), sc1, sh1, w_in,
                                       w_in[:, 3 * aw + N_HEADS:])

    cum_row, cum_col = _forget_cumsum(f, b_f)
    attn, w_up_bf16 = _attention(q, k, v, cum_col, cum_row.reshape(N_HEADS, 1, s), k_norm_sq,
                                 w_up)

    y, w_down_bf16, w_out_bf16, w_glu_bf16 = _s5(
        u, a_re, a_im, log_dt, ssm_b_re, ssm_b_im, ssm_c_re, ssm_c_im, ssm_d,
        w_down, w_out, w_glu)

    h1, hn2 = _mixout(h, attn, y, w_glu_bf16, row(b_glu), row(g_attn_out),
                      row(g_ssm_out), w_out_bf16, gt1, row(g_ffn), sc2, sh2)
    return hn2, h1, (w_up_bf16, conv_w, row(conv_b), w_down_bf16, gt2)


def kernel(x, c, w_ada, b_ada, g_mix, w_in, b_f, a_re, a_im, log_dt, ssm_b_re, ssm_b_im, ssm_c_re,
           ssm_c_im, ssm_d, w_glu, b_glu, g_attn_out, g_ssm_out, w_out, g_ffn, w_up, conv_w,
           conv_b, w_down, g_final):
    batch, s, d = x.shape
    assert w_ada.shape[0] == 1, "only DEPTH == 1 is supported"
    l = 0
    outs = []
    for bi in range(batch):
        mod, w_in_bf16 = _adaln(c[bi:bi + 1], w_ada[l], b_ada[l], w_in[l])
        hn2, h1, ffn_args = _layer(
            x[bi], mod, g_mix[l], w_in_bf16, b_f[l], a_re[l], a_im[l], log_dt[l], ssm_b_re[l],
            ssm_b_im[l], ssm_c_re[l], ssm_c_im[l], ssm_d[l], w_glu[l], b_glu[l],
            g_attn_out[l], g_ssm_out[l], w_out[l], g_ffn[l], w_up[l], conv_w[l], conv_b[l],
            w_down[l])
        outs.append(_ffn(hn2, h1, *ffn_args, g_final.reshape(1, d)))
    return jnp.stack(outs, axis=0)
```

```python
import functools
import math

import jax
import jax.numpy as jnp
from jax import lax
from jax.experimental import pallas as pl
from jax.experimental.pallas import tpu as pltpu

F32 = jnp.float32
BF16 = jnp.bfloat16

EPS = 1e-6
HEAD_DIM = 128
N_HEADS = 8
SSM_GROUP = 16
SSM_STATE = 64
N_MOD = 6
LANES = 128
SSM_T = 16
SSM_TW = SSM_T * SSM_GROUP
SSM_PAIRS = SSM_T // 2
SSM_TILE_G = LANES // SSM_GROUP
LOG2E = 1.4426950408889634
SKIP_LOG2 = 151.0
NORM_SLACK = 1.01

_MIB = 1024 * 1024


def _cparams(semantics, vmem_mib):
    return pltpu.CompilerParams(dimension_semantics=semantics, vmem_limit_bytes=vmem_mib * _MIB)


def _resident(shape):
    return pl.BlockSpec(shape, lambda *_: (0,) * len(shape), pipeline_mode=pl.Buffered(1))


def _dot(a, b):
    return jnp.dot(a, b, preferred_element_type=F32)


def _dot_nt(a, b):
    return lax.dot_general(a, b, (((1,), (1,)), ((), ())), preferred_element_type=F32)


def _rms(x, g):
    return x * lax.rsqrt(jnp.mean(x * x, axis=-1, keepdims=True) + EPS) * g


def _lane_tile(x, reps):
    return jnp.concatenate([x] * reps, axis=1)


def _cast_slab(w, steps):
    rows = w.shape[0] // steps
    assert rows * steps == w.shape[0] and rows % 16 == 0, (w.shape, steps)
    return rows, jax.ShapeDtypeStruct(w.shape, BF16)


def _adaln_kernel(c_ref, w_ref, b_ref, o_ref):
    c = c_ref[...]
    cond = c * jax.nn.sigmoid(c)
    cond8 = jnp.broadcast_to(cond, (8, c.shape[1])).astype(BF16)
    acc = _dot(cond8, w_ref[...].astype(BF16))
    o_ref[...] = acc[0:1, :] + b_ref[...]


def _adaln(c, w, b, tn=1536):
    d, n = w.shape
    return pl.pallas_call(
        _adaln_kernel,
        grid=(n // tn,),
        in_specs=[pl.BlockSpec((1, d), lambda j: (0, 0)),
                  pl.BlockSpec((d, tn), lambda j: (0, j)),
                  pl.BlockSpec((1, tn), lambda j: (0, j))],
        out_specs=pl.BlockSpec((1, tn), lambda j: (0, j)),
        out_shape=jax.ShapeDtypeStruct((1, n), F32),
        compiler_params=_cparams(("arbitrary",), 48),
        name="adaln",
    )(c, w, b.reshape(1, n))


def _inproj_kernel(x_ref, g_ref, sc_ref, sh_ref, wq_ref, wk_ref, wv_ref, wu_ref, wf_ref,
                   q_ref, k_ref, v_ref, u_ref, f_ref, kn_ref, *, q_scale):
    hn = (_rms(x_ref[...], g_ref[...]) * (1.0 + sc_ref[...]) + sh_ref[...]).astype(BF16)
    f_ref[...] = _dot(hn, wf_ref[...])
    q_ref[...] = (_dot(hn, wq_ref[...]) * q_scale).astype(BF16)
    k = _dot(hn, wk_ref[...]).astype(BF16)
    k_ref[...] = k
    u_ref[...] = _dot(hn, wu_ref[...])

    k32 = k.astype(F32)
    ksq = k32 * k32
    tile_max = jnp.concatenate(
        [jnp.broadcast_to(
            jnp.max(jnp.sum(ksq[:, h * HEAD_DIM:(h + 1) * HEAD_DIM], axis=1, keepdims=True),
                    axis=0, keepdims=True), (1, LANES)) for h in range(N_HEADS)], axis=0)

    @pl.when(pl.program_id(0) == 0)
    def _():
        kn_ref[...] = tile_max

    @pl.when(pl.program_id(0) > 0)
    def _():
        kn_ref[...] = jnp.maximum(kn_ref[...], tile_max)

    v = _dot(hn, wv_ref[...]).astype(BF16)
    ones = jnp.ones((v.shape[0], HEAD_DIM), BF16)
    for h in range(N_HEADS):
        v_ref[:, 2 * h * HEAD_DIM:(2 * h + 1) * HEAD_DIM] = v[:, h * HEAD_DIM:(h + 1) * HEAD_DIM]
        v_ref[:, (2 * h + 1) * HEAD_DIM:(2 * h + 2) * HEAD_DIM] = ones


def _inproj(x, g, sc, sh, w_in, wu, tm=512):
    s, d = x.shape
    aw, sw = N_HEADS * HEAD_DIM, wu.shape[1]
    rows = lambda c: pl.BlockSpec((tm, c), lambda i: (i, 0))
    cols = lambda width, blk: pl.BlockSpec((d, width), lambda i: (0, blk),
                                           pipeline_mode=pl.Buffered(1))
    return pl.pallas_call(
        functools.partial(_inproj_kernel, q_scale=HEAD_DIM ** -0.5 * LOG2E),
        grid=(s // tm,),
        in_specs=[rows(d), _resident((1, d)), _resident((1, d)), _resident((1, d)),
                  cols(aw, 0), cols(aw, 1), cols(aw, 2), _resident((d, sw)),
                  cols(LANES, 3 * aw // LANES)],
        out_specs=[rows(aw), rows(aw), rows(2 * aw), rows(sw), rows(LANES),
                   pl.BlockSpec((N_HEADS, LANES), lambda i: (0, 0))],
        out_shape=[jax.ShapeDtypeStruct((s, aw), BF16), jax.ShapeDtypeStruct((s, aw), BF16),
                   jax.ShapeDtypeStruct((s, 2 * aw), BF16), jax.ShapeDtypeStruct((s, sw), F32),
                   jax.ShapeDtypeStruct((s, LANES), F32),
                   jax.ShapeDtypeStruct((N_HEADS, LANES), F32)],
        compiler_params=_cparams(("arbitrary",), 56),
        name="inproj",
    )(x, g, sc, sh, w_in, w_in, w_in, wu, w_in)


def _log2_forget_cumsum(z, axis):
    x = jnp.minimum(z, 0.0) - jnp.log1p(jnp.exp(-jnp.abs(z)))
    pos = lax.broadcasted_iota(jnp.int32, x.shape, axis)
    shift = 1
    while shift < x.shape[axis]:
        x = x + jnp.where(pos >= shift, pltpu.roll(x, shift, axis=axis), 0.0)
        shift *= 2
    return x * LOG2E


def _cum_kernel(ft_ref, bcol_ref, f_ref, brow_ref, row_ref, col_ref):
    row_ref[...] = _log2_forget_cumsum(ft_ref[...] + bcol_ref[...], 1)
    col_ref[...] = _log2_forget_cumsum(f_ref[...] + brow_ref[...], 0)


def _forget_cumsum(f, b_f):
    s, w = f.shape
    h = b_f.shape[0]
    return pl.pallas_call(
        _cum_kernel,
        out_shape=[jax.ShapeDtypeStruct((h, s), F32), jax.ShapeDtypeStruct((s, w), F32)],
        compiler_params=_cparams(None, 48),
        name="forget_cumsum",
    )(f[:, :h].T, b_f.reshape(h, 1), f, jnp.pad(b_f, (0, w - h)).reshape(1, w))


def _attn_kernel(q_ref, k_ref, v_ref, cq_ref, ck_ref, kn_ref, wsrc_ref, o_ref, wdst_ref,
                 m_ref, acc_ref, *, tq, near, splits):
    i = pl.program_id(1)
    wdst_ref[...] = wsrc_ref[...].astype(BF16)
    kmax = jnp.sqrt(kn_ref[pl.ds(pl.program_id(0), 1), :]) * NORM_SLACK

    head_lane = lax.broadcasted_iota(jnp.int32, (tq, LANES), 1) == pl.program_id(0)
    cqb = jnp.broadcast_to(
        jnp.sum(jnp.where(head_lane, cq_ref[...], 0.0), axis=1, keepdims=True), (tq, LANES))

    half = tq // splits
    halves = [slice(hh * half, (hh + 1) * half) for hh in range(splits)]

    base = pl.multiple_of(i * tq, tq)
    qf = q_ref[...].astype(F32)
    own = jnp.sum(qf * k_ref[pl.ds(base, tq), :].astype(F32), axis=1, keepdims=True)
    qn = jnp.sqrt(jnp.sum(qf * qf, axis=1, keepdims=True))
    bound = jnp.max(qn * kmax - (own - cqb))
    ck_all = ck_ref[0]
    pos = lax.broadcasted_iota(jnp.int32, ck_all.shape, 1)
    dead = jnp.where((pos < i * tq) & (ck_all > bound + SKIP_LOG2), 1.0, 0.0)
    j0 = jnp.sum(dead).astype(jnp.int32) // tq

    first_near = jnp.maximum(i - near, 0)
    near_w = near * tq
    near_ks = pl.multiple_of(first_near * tq, tq)
    near_k = k_ref[pl.ds(near_ks, near_w), :]
    near_v = v_ref[pl.ds(near_ks, near_w), :]
    near_key = near_ks + lax.broadcasted_iota(jnp.int32, (1, near_w), 1)
    near_ck = jnp.where(near_key < base, ck_ref[0, :, pl.ds(near_ks, near_w)], jnp.inf)
    for hh, rs in enumerate(halves):
        nk = (hh + 1) * half
        qh = q_ref[rs, :]
        t_diag = _dot_nt(qh, k_ref[pl.ds(base, nk), :]) - ck_ref[0, :, pl.ds(base, nk)]
        row = lax.broadcasted_iota(jnp.int32, t_diag.shape, 0)
        col = lax.broadcasted_iota(jnp.int32, t_diag.shape, 1)
        t_diag = jnp.where(col <= row + hh * half, t_diag, -jnp.inf)
        t = jnp.concatenate([_dot_nt(qh, near_k) - near_ck, t_diag], axis=1)
        m0 = jnp.max(t, axis=1, keepdims=True) + cqb[rs]
        p = jnp.exp2(t - _lane_tile(m0 - cqb[rs], (near_w + nk) // LANES)).astype(BF16)
        m_ref[rs, :] = m0
        acc = _dot(p[:, :near_w], near_v) + _dot(p[:, near_w:], v_ref[pl.ds(base, nk), :])
        acc_ref[rs, :] = acc
        o_ref[rs, :] = (acc[:, :HEAD_DIM] / acc[:, HEAD_DIM:]).astype(BF16)

    def body(j, carry):
        ks = pl.multiple_of(j * tq, tq)
        kb = k_ref[pl.ds(ks, tq), :]
        vb = v_ref[pl.ds(ks, tq), :]
        ck = ck_ref[0, :, pl.ds(ks, tq)]
        for rs in halves:
            t = _dot_nt(q_ref[rs, :], kb) - ck
            m_prev = m_ref[rs, :]
            m_new = jnp.maximum(m_prev, jnp.max(t, axis=1, keepdims=True) + cqb[rs])
            alpha = jnp.exp2(m_prev - m_new)
            p = jnp.exp2(t - _lane_tile(m_new - cqb[rs], tq // LANES))
            acc_ref[rs, :] = _lane_tile(alpha, 2) * acc_ref[rs, :] + _dot(p.astype(BF16), vb)
            m_ref[rs, :] = m_new
        return carry

    @pl.when(j0 < first_near)
    def _():
        lax.fori_loop(j0, first_near, body, 0)
        acc = acc_ref[...]
        o_ref[...] = (acc[:, :HEAD_DIM] / acc[:, HEAD_DIM:]).astype(BF16)


def _attention(q, k, v_ones, cum_col, cum_row, k_norm_sq, w_cast, tq=512, near=2, splits=2):
    s = q.shape[0]
    nq = s // tq
    slab, w_bf16 = _cast_slab(w_cast, N_HEADS * nq)
    slab_spec = pl.BlockSpec((slab, w_cast.shape[1]), lambda h, i: (h * nq + i, 0))
    return pl.pallas_call(
        functools.partial(_attn_kernel, tq=tq, near=near, splits=splits),
        grid=(N_HEADS, nq),
        in_specs=[pl.BlockSpec((tq, HEAD_DIM), lambda h, i: (i, h)),
                  pl.BlockSpec((s, HEAD_DIM), lambda h, i: (0, h)),
                  pl.BlockSpec((s, 2 * HEAD_DIM), lambda h, i: (0, h)),
                  pl.BlockSpec((tq, LANES), lambda h, i: (i, 0)),
                  pl.BlockSpec((1, 1, s), lambda h, i: (h, 0, 0)),
                  pl.BlockSpec((N_HEADS, LANES), lambda h, i: (0, 0)),
                  slab_spec],
        out_specs=[pl.BlockSpec((tq, HEAD_DIM), lambda h, i: (i, h)), slab_spec],
        out_shape=[jax.ShapeDtypeStruct((s, N_HEADS * HEAD_DIM), BF16), w_bf16],
        scratch_shapes=[pltpu.VMEM((tq, LANES), F32), pltpu.VMEM((tq, 2 * HEAD_DIM), F32)],
        compiler_params=_cparams(("arbitrary", "arbitrary"), 40),
        name="fox_attention",
    )(q, k, v_ones, cum_col, cum_row, k_norm_sq, w_cast)


def _cmul(are, aim, bre, bim):
    return are * bre - aim * bim, are * bim + aim * bre


def _step_pair(u_ref, a, n_chunks):
    x0 = u_ref[pl.ds(2 * a, n_chunks, stride=SSM_T), :]
    x1 = u_ref[pl.ds(2 * a + 1, n_chunks, stride=SSM_T), :]
    return jnp.concatenate([x0, x1], axis=1).astype(BF16)


def _ssm_in_kernel(u_ref, ldt_ref, are_ref, aim_ref, btr_ref, bti_ref, cr_ref, ci_ref, d_ref,
                   wsrc_ref, w2_ref, pmt2_ref, x0_ref, wdst_ref,
                   cp_ref, wl_ref, q2_ref, v_ref):
    wdst_ref[...] = wsrc_ref[...].astype(BF16)
    n_chunks = u_ref.shape[0] // SSM_T
    half_w = SSM_TILE_G * SSM_STATE
    pw_parts = []
    lane = lax.broadcasted_iota(jnp.int32, (SSM_GROUP, LANES), 1)
    first = lane < SSM_STATE

    @pl.when(pl.program_id(0) == 0)
    def _():
        q2_ref[...] = jnp.zeros(q2_ref.shape, BF16)

    pmt2_ref[...] = jnp.zeros(pmt2_ref.shape, BF16)

    def place(ref, lead, step, gi, re_part, im_part):
        r0 = (step % 2) * LANES + gi * SSM_GROUP
        rows = slice(r0, r0 + SSM_GROUP)
        mine = first if gi % 2 == 0 else jnp.logical_not(first)
        c_re = (gi // 2) * LANES
        c_im = half_w + c_re
        ref[lead + (step // 2, rows, slice(c_re, c_re + LANES))] = (
            jnp.where(mine, re_part, 0.0).astype(BF16))
        ref[lead + (step // 2, rows, slice(c_im, c_im + LANES))] = (
            jnp.where(mine, im_part, 0.0).astype(BF16))

    for gi in range(SSM_TILE_G):
        dt = jnp.exp(ldt_ref[gi])
        are, aim = are_ref[gi], aim_ref[gi]
        mag = jnp.exp(dt * are)
        abre, abim = mag * jnp.cos(dt * aim), mag * jnp.sin(dt * aim)
        nre, nim = abre - 1.0, abim
        den = are * are + aim * aim
        zre = (nre * are + nim * aim) / den
        zim = (nim * are - nre * aim) / den
        bbre, bbim = _cmul(zre, zim, btr_ref[gi], bti_ref[gi])
        bbcat = jnp.where(first, bbre, bbim)

        cpre, cpim = cr_ref[gi], ci_ref[gi]
        qre, qim = bbre, bbim
        pwre, pwim = jnp.ones_like(abre), jnp.zeros_like(abim)
        for t in range(SSM_T):
            cp_ref[t * SSM_GROUP:(t + 1) * SSM_GROUP, :] = jnp.where(first, cpre, -cpim)
            if t > 0:
                place(pmt2_ref, (0,), t - 1, gi, cpre, -cpim)
            place(q2_ref, (), SSM_T - 1 - t, gi, qre, qim)
            cpre, cpim = _cmul(cpre, cpim, abre, abim)
            qre, qim = _cmul(qre, qim, abre, abim)
            pwre, pwim = _cmul(pwre, pwim, abre, abim)
        place(pmt2_ref, (0,), SSM_T - 1, gi, cpre, -cpim)
        pw_parts.append((pwre, pwim))

        krow = lax.dot_general(bbcat, cp_ref[...], (((1,), (1,)), ((), ())),
                               preferred_element_type=F32, precision=lax.Precision.HIGHEST)
        own = (lane >= gi * SSM_GROUP) & (lane < (gi + 1) * SSM_GROUP)
        for tau in range(SSM_T):
            half = krow[:, (tau // SSM_TILE_G) * LANES:(tau // SSM_TILE_G + 1) * LANES]
            shift = ((gi - tau % SSM_TILE_G) * SSM_GROUP) % LANES
            moved = half if shift == 0 else pltpu.roll(half, shift, axis=1)
            wl_ref[tau, gi * SSM_GROUP:(gi + 1) * SSM_GROUP, :] = jnp.where(own, moved, 0.0)

    r = lax.broadcasted_iota(jnp.int32, (LANES, LANES), 0)
    c = lax.broadcasted_iota(jnp.int32, (LANES, LANES), 1)
    wl_ref[0] = wl_ref[0] + jnp.where(r == c, d_ref[0], 0.0)
    for dl in range(SSM_PAIRS):
        diag = wl_ref[2 * dl].astype(BF16)
        w2_ref[0, dl, 0:LANES, 0:LANES] = diag
        w2_ref[0, dl, LANES:, LANES:] = diag
        w2_ref[0, dl, 0:LANES, LANES:] = wl_ref[2 * dl + 1].astype(BF16)
        below = jnp.zeros((LANES, LANES), BF16) if dl == 0 else wl_ref[2 * dl - 1].astype(BF16)
        w2_ref[0, dl, LANES:, 0:LANES] = below

    v = _dot(_step_pair(u_ref, 0, n_chunks), q2_ref[0])
    for a in range(1, SSM_PAIRS):
        v = v + _dot(_step_pair(u_ref, a, n_chunks), q2_ref[a])
    v_ref[...] = v

    pair = lambda k, part: jnp.where(first[0:1], pw_parts[2 * k][part], pw_parts[2 * k + 1][part])
    ar = jnp.concatenate([pair(k, 0) for k in range(SSM_TILE_G // 2)], axis=1)
    ai = jnp.concatenate([pair(k, 1) for k in range(SSM_TILE_G // 2)], axis=1)

    def scan_body(c, carry):
        xre, xim = carry
        x0_ref[pl.ds(c, 1), 0:half_w] = xre
        x0_ref[pl.ds(c, 1), half_w:] = xim
        vre = v_ref[pl.ds(c, 1), 0:half_w]
        vim = v_ref[pl.ds(c, 1), half_w:]
        return ar * xre - ai * xim + vre, ar * xim + ai * xre + vim

    zero = jnp.zeros(ar.shape, F32)
    lax.fori_loop(0, n_chunks, scan_body, (zero, zero), unroll=8)


def _ssm_in(u, ldt, are2, aim2, bt_re2, bt_im2, c_re2, c_im2, d_rows, w_cast):
    s, width = u.shape
    n_tiles = width // LANES
    c = s // SSM_T
    kw = SSM_TILE_G * LANES
    per_tile = lambda *shape: pl.BlockSpec((SSM_TILE_G,) + shape,
                                           lambda j: (j,) + (0,) * len(shape))
    slab, w_bf16 = _cast_slab(w_cast, n_tiles)
    slab_spec = pl.BlockSpec((slab, w_cast.shape[1]), lambda j: (j, 0))
    return pl.pallas_call(
        _ssm_in_kernel,
        grid=(n_tiles,),
        in_specs=[pl.BlockSpec((s, LANES), lambda j: (0, j)),
                  per_tile(1, LANES), per_tile(1, LANES), per_tile(1, LANES),
                  per_tile(SSM_GROUP, LANES), per_tile(SSM_GROUP, LANES),
                  per_tile(SSM_GROUP, LANES), per_tile(SSM_GROUP, LANES),
                  pl.BlockSpec((1, 1, LANES), lambda j: (j, 0, 0)), slab_spec],
        out_specs=[pl.BlockSpec((1, SSM_PAIRS, 2 * LANES, 2 * LANES), lambda j: (j, 0, 0, 0)),
                   pl.BlockSpec((1, SSM_PAIRS, 2 * LANES, kw), lambda j: (j, 0, 0, 0)),
                   pl.BlockSpec((c, kw), lambda j: (0, j)), slab_spec],
        out_shape=[jax.ShapeDtypeStruct((n_tiles, SSM_PAIRS, 2 * LANES, 2 * LANES), BF16),
                   jax.ShapeDtypeStruct((n_tiles, SSM_PAIRS, 2 * LANES, kw), BF16),
                   jax.ShapeDtypeStruct((c, n_tiles * kw), F32), w_bf16],
        scratch_shapes=[pltpu.VMEM((SSM_TW, LANES), F32),
                        pltpu.VMEM((SSM_T, LANES, LANES), F32),
                        pltpu.VMEM((SSM_PAIRS, 2 * LANES, kw), BF16),
                        pltpu.VMEM((c, kw), F32)],
        compiler_params=_cparams(("arbitrary",), 60),
        name="ssm_state_in",
    )(u, ldt, are2, aim2, bt_re2, bt_im2, c_re2, c_im2, d_rows, w_cast)


def _ssm_out_kernel(u_ref, w2_ref, pmt2_ref, x0_ref, wsrc_a_ref, wsrc_b_ref,
                    y_ref, wdst_a_ref, wdst_b_ref):
    wdst_a_ref[...] = wsrc_a_ref[...].astype(BF16)
    wdst_b_ref[...] = wsrc_b_ref[...].astype(BF16)
    n_chunks = u_ref.shape[0] // SSM_T
    x0 = x0_ref[...].astype(BF16)
    pairs = [_step_pair(u_ref, a, n_chunks) for a in range(SSM_PAIRS)]
    for b in range(SSM_PAIRS):
        acc = _dot_nt(x0, pmt2_ref[0, b])
        for a in range(b + 1):
            acc = acc + _dot(pairs[a], w2_ref[0, b - a])
        y_ref[pl.ds(2 * b, n_chunks, stride=SSM_T), :] = acc[:, :LANES]
        y_ref[pl.ds(2 * b + 1, n_chunks, stride=SSM_T), :] = acc[:, LANES:]


def _ssm_out(u, w2, pmt2, x0, w_cast_a, w_cast_b):
    s, width = u.shape
    n_tiles = width // LANES
    c = s // SSM_T
    kw = SSM_TILE_G * LANES
    slab_a, a_bf16 = _cast_slab(w_cast_a, n_tiles)
    slab_b, b_bf16 = _cast_slab(w_cast_b, n_tiles)
    spec_a = pl.BlockSpec((slab_a, w_cast_a.shape[1]), lambda j: (j, 0))
    spec_b = pl.BlockSpec((slab_b, w_cast_b.shape[1]), lambda j: (j, 0))
    return pl.pallas_call(
        _ssm_out_kernel,
        grid=(n_tiles,),
        in_specs=[pl.BlockSpec((s, LANES), lambda j: (0, j)),
                  pl.BlockSpec((1, SSM_PAIRS, 2 * LANES, 2 * LANES), lambda j: (j, 0, 0, 0)),
                  pl.BlockSpec((1, SSM_PAIRS, 2 * LANES, kw), lambda j: (j, 0, 0, 0)),
                  pl.BlockSpec((c, kw), lambda j: (0, j)), spec_a, spec_b],
        out_specs=[pl.BlockSpec((s, LANES), lambda j: (0, j)), spec_a, spec_b],
        out_shape=[jax.ShapeDtypeStruct((s, width), F32), a_bf16, b_bf16],
        compiler_params=_cparams(("arbitrary",), 48),
        name="ssm_out",
    )(u, w2, pmt2, x0, w_cast_a, w_cast_b)


def _s5(u, a_re, a_im, log_dt, b_re, b_im, c_re, c_im, d_skip, w_cast_in, w_cast_a, w_cast_b):
    g, p = a_re.shape
    dup = lambda a: jnp.concatenate([a, a], axis=-1)
    ldt = jnp.broadcast_to(log_dt.reshape(g, 1, 1), (g, 1, LANES))
    are2, aim2 = dup(a_re).reshape(g, 1, LANES), dup(a_im).reshape(g, 1, LANES)
    bt_re2 = dup(jnp.swapaxes(b_re, 1, 2))
    bt_im2 = dup(jnp.swapaxes(b_im, 1, 2))
    w2, pmt2, x0, w_in_bf16 = _ssm_in(u, ldt, are2, aim2, bt_re2, bt_im2, dup(c_re), dup(c_im),
                                      d_skip.reshape(-1, 1, LANES), w_cast_in)
    y, w_a_bf16, w_b_bf16 = _ssm_out(u, w2, pmt2, x0, w_cast_a, w_cast_b)
    return y, w_in_bf16, w_a_bf16, w_b_bf16


def _gelu_tanh(x):
    return 0.5 * x * (1.0 + jnp.tanh(math.sqrt(2.0 / math.pi) * (x + 0.044715 * (x * x * x))))


def _mixout_kernel(x_ref, attn_ref, y_ref, wglu_ref, bglu_ref, ga_ref, gs_ref, wo_ref,
                   gt_ref, gf_ref, sc_ref, sh_ref, h_ref, hn_ref):
    aw = attn_ref.shape[1]
    y = _gelu_tanh(y_ref[...])
    gate = jax.nn.sigmoid(_dot(y.astype(BF16), wglu_ref[...]) + bglu_ref[...])
    ns = _rms(y * gate, gs_ref[...]).astype(BF16)
    na = _rms(attn_ref[...].astype(F32), ga_ref[...]).astype(BF16)
    mixed = _dot(na, wo_ref[0:aw, :]) + _dot(ns, wo_ref[aw:, :])
    h = x_ref[...] + gt_ref[...] * mixed
    h_ref[...] = h
    hn_ref[...] = (_rms(h, gf_ref[...]) * (1.0 + sc_ref[...]) + sh_ref[...]).astype(BF16)


def _mixout(x, attn, y, w_glu, b_glu, g_attn, g_ssm, w_out, gt1, g_ffn, sc2, sh2, tm=512):
    s, d = x.shape
    w = attn.shape[1]
    rows = lambda c: pl.BlockSpec((tm, c), lambda i: (i, 0))
    return pl.pallas_call(
        _mixout_kernel,
        grid=(s // tm,),
        in_specs=[rows(d), rows(w), rows(w), _resident((w, w)), _resident((1, w)),
                  _resident((1, w)), _resident((1, w)), _resident(w_out.shape),
                  _resident((1, d)), _resident((1, d)), _resident((1, d)), _resident((1, d))],
        out_specs=[rows(d), rows(d)],
        out_shape=[jax.ShapeDtypeStruct((s, d), F32), jax.ShapeDtypeStruct((s, d), BF16)],
        compiler_params=_cparams(("arbitrary",), 56),
        name="mixer_out",
    )(x, attn, y, w_glu, b_glu, g_attn, g_ssm, w_out, gt1, g_ffn, sc2, sh2)


def _ffn_kernel(hn_ref, halo_ref, h_ref, wa_ref, wb_ref, cw_ref, cb_ref, wd_ref, gt_ref, gfin_ref,
                o_ref):
    i, j = pl.program_id(0), pl.program_id(1)

    @pl.when(j == 0)
    def _():
        o_ref[...] = jnp.zeros(o_ref.shape, F32)

    hn = hn_ref[...]
    a = _dot(hn, wa_ref[...])
    b = _dot(hn, wb_ref[...])
    halo = _dot(halo_ref[...], wa_ref[...]) * (i > 0).astype(F32)
    row = lax.broadcasted_iota(jnp.int32, a.shape, 0)
    prev1 = jnp.where(row == 0, halo[7:8, :], pltpu.roll(a, 1, axis=0))
    prev2 = jnp.where(row == 0, halo[6:7, :],
                      jnp.where(row == 1, halo[7:8, :], pltpu.roll(a, 2, axis=0)))
    cw = cw_ref[...]
    conv = cb_ref[...] + cw[0:1, :] * prev2 + cw[1:2, :] * prev1 + cw[2:3, :] * a
    act = (conv * jax.nn.sigmoid(conv) * b).astype(BF16)
    o_ref[...] += _dot(act, wd_ref[...])

    @pl.when(j == pl.num_programs(1) - 1)
    def _():
        h = h_ref[...] + gt_ref[...] * o_ref[...]
        o_ref[...] = _rms(h, gfin_ref[...])


def _ffn(hn, h, w_up, conv_w, conv_b, w_down, gt2, g_final, tm=1024, tn=512):
    s, d = h.shape
    d_ff = w_down.shape[0]
    nf = d_ff // tn
    halo_blocks = tm // 8
    return pl.pallas_call(
        _ffn_kernel,
        grid=(s // tm, nf),
        in_specs=[pl.BlockSpec((tm, d), lambda i, j: (i, 0)),
                  pl.BlockSpec((8, d), lambda i, j: (jnp.maximum(i * halo_blocks - 1, 0), 0)),
                  pl.BlockSpec((tm, d), lambda i, j: (i, 0), pipeline_mode=pl.Buffered(1)),
                  pl.BlockSpec((d, tn), lambda i, j: (0, j)),
                  pl.BlockSpec((d, tn), lambda i, j: (0, nf + j)),
                  pl.BlockSpec((3, tn), lambda i, j: (0, j)),
                  pl.BlockSpec((1, tn), lambda i, j: (0, j)),
                  pl.BlockSpec((tn, d), lambda i, j: (j, 0)),
                  _resident((1, d)), _resident((1, d))],
        out_specs=pl.BlockSpec((tm, d), lambda i, j: (i, 0)),
        out_shape=jax.ShapeDtypeStruct((s, d), F32),
        compiler_params=_cparams(("arbitrary", "arbitrary"), 62),
        name="conv_ffn",
    )(hn, hn, h, w_up, w_up, conv_w, conv_b, w_down, gt2, g_final)


def _layer(h, mod, g_mix, w_in, b_f, a_re, a_im, log_dt, ssm_b_re, ssm_b_im, ssm_c_re, ssm_c_im,
           ssm_d, w_glu, b_glu, g_attn_out, g_ssm_out, w_out, g_ffn, w_up, conv_w, conv_b, w_down):
    s, d = h.shape
    aw = N_HEADS * HEAD_DIM
    sh1, sc1, gt1, sh2, sc2, gt2 = [mod[:, i * d:(i + 1) * d] for i in range(N_MOD)]
    row = lambda a: a.reshape(1, -1)

    q, k, v, u, f, k_norm_sq = _inproj(h, row(g_mix), sc1, sh1, w_in,
                                       w_in[:, 3 * aw + N_HEADS:])

    cum_row, cum_col = _forget_cumsum(f, b_f)
    attn, w_up_bf16 = _attention(q, k, v, cum_col, cum_row.reshape(N_HEADS, 1, s), k_norm_sq,
                                 w_up)

    y, w_down_bf16, w_out_bf16, w_glu_bf16 = _s5(
        u, a_re, a_im, log_dt, ssm_b_re, ssm_b_im, ssm_c_re, ssm_c_im, ssm_d,
        w_down, w_out, w_glu)

    h1, hn2 = _mixout(h, attn, y, w_glu_bf16, row(b_glu), row(g_attn_out),
                      row(g_ssm_out), w_out_bf16, gt1, row(g_ffn), sc2, sh2)
    return hn2, h1, (w_up_bf16, conv_w, row(conv_b), w_down_bf16, gt2)


def kernel(x, c, w_ada, b_ada, g_mix, w_in, b_f, a_re, a_im, log_dt, ssm_b_re, ssm_b_im, ssm_c_re,
           ssm_c_im, ssm_d, w_glu, b_glu, g_attn_out, g_ssm_out, w_out, g_ffn, w_up, conv_w,
           conv_b, w_down, g_final):
    batch, s, d = x.shape
    assert w_ada.shape[0] == 1, "only DEPTH == 1 is supported"
    l = 0
    outs = []
    for bi in range(batch):
        mod = _adaln(c[bi:bi + 1], w_ada[l], b_ada[l])
        hn2, h1, ffn_args = _layer(
            x[bi], mod, g_mix[l], w_in[l].astype(BF16), b_f[l], a_re[l], a_im[l], log_dt[l], ssm_b_re[l],
            ssm_b_im[l], ssm_c_re[l], ssm_c_im[l], ssm_d[l], w_glu[l], b_glu[l],
            g_attn_out[l], g_ssm_out[l], w_out[l], g_ffn[l], w_up[l], conv_w[l], conv_b[l],
            w_down[l])
        outs.append(_ffn(hn2, h1, *ffn_args, g_final.reshape(1, d)))
    return jnp.stack(outs, axis=0)
```

```python
import functools
import math

import jax
import jax.numpy as jnp
from jax import lax
from jax.experimental import pallas as pl
from jax.experimental.pallas import tpu as pltpu

F32 = jnp.float32
BF16 = jnp.bfloat16

EPS = 1e-6
HEAD_DIM = 128
N_HEADS = 8
SSM_GROUP = 16
SSM_STATE = 64
N_MOD = 6
LANES = 128
SSM_T = 16
SSM_TW = SSM_T * SSM_GROUP
SSM_PAIRS = SSM_T // 2
SSM_TILE_G = LANES // SSM_GROUP
LOG2E = 1.4426950408889634
SKIP_LOG2 = 151.0
NORM_SLACK = 1.01

_MIB = 1024 * 1024


def _cparams(semantics, vmem_mib):
    return pltpu.CompilerParams(dimension_semantics=semantics, vmem_limit_bytes=vmem_mib * _MIB)


def _resident(shape):
    return pl.BlockSpec(shape, lambda *_: (0,) * len(shape), pipeline_mode=pl.Buffered(1))


def _dot(a, b):
    return jnp.dot(a, b, preferred_element_type=F32)


def _dot_nt(a, b):
    return lax.dot_general(a, b, (((1,), (1,)), ((), ())), preferred_element_type=F32)


def _rms(x, g):
    return x * lax.rsqrt(jnp.mean(x * x, axis=-1, keepdims=True) + EPS) * g


def _lane_tile(x, reps):
    return jnp.concatenate([x] * reps, axis=1)


def _cast_slab(w, steps):
    rows = w.shape[0] // steps
    assert rows * steps == w.shape[0] and rows % 16 == 0, (w.shape, steps)
    return rows, jax.ShapeDtypeStruct(w.shape, BF16)


def _adaln_kernel(c_ref, w_ref, b_ref, o_ref):
    c = c_ref[...]
    cond = c * jax.nn.sigmoid(c)
    cond8 = jnp.broadcast_to(cond, (8, c.shape[1])).astype(BF16)
    acc = _dot(cond8, w_ref[...].astype(BF16))
    o_ref[...] = acc[0:1, :] + b_ref[...]


def _adaln(c, w, b, tn=1536):
    d, n = w.shape
    return pl.pallas_call(
        _adaln_kernel,
        grid=(n // tn,),
        in_specs=[pl.BlockSpec((1, d), lambda j: (0, 0)),
                  pl.BlockSpec((d, tn), lambda j: (0, j)),
                  pl.BlockSpec((1, tn), lambda j: (0, j))],
        out_specs=pl.BlockSpec((1, tn), lambda j: (0, j)),
        out_shape=jax.ShapeDtypeStruct((1, n), F32),
        compiler_params=_cparams(("arbitrary",), 48),
        name="adaln",
    )(c, w, b.reshape(1, n))


def _inproj_kernel(x_ref, g_ref, sc_ref, sh_ref, wq_ref, wk_ref, wv_ref, wu_ref, wf_ref,
                   q_ref, k_ref, v_ref, u_ref, f_ref, kn_ref, *, q_scale):
    hn = (_rms(x_ref[...], g_ref[...]) * (1.0 + sc_ref[...]) + sh_ref[...]).astype(BF16)
    f_ref[...] = _dot(hn, wf_ref[...])
    q_ref[...] = (_dot(hn, wq_ref[...]) * q_scale).astype(BF16)
    k = _dot(hn, wk_ref[...]).astype(BF16)
    k_ref[...] = k
    u_ref[...] = _dot(hn, wu_ref[...])

    k32 = k.astype(F32)
    ksq = k32 * k32
    tile_max = jnp.concatenate(
        [jnp.broadcast_to(
            jnp.max(jnp.sum(ksq[:, h * HEAD_DIM:(h + 1) * HEAD_DIM], axis=1, keepdims=True),
                    axis=0, keepdims=True), (1, LANES)) for h in range(N_HEADS)], axis=0)

    @pl.when(pl.program_id(0) == 0)
    def _():
        kn_ref[...] = tile_max

    @pl.when(pl.program_id(0) > 0)
    def _():
        kn_ref[...] = jnp.maximum(kn_ref[...], tile_max)

    v = _dot(hn, wv_ref[...]).astype(BF16)
    ones = jnp.ones((v.shape[0], HEAD_DIM), BF16)
    for h in range(N_HEADS):
        v_ref[:, 2 * h * HEAD_DIM:(2 * h + 1) * HEAD_DIM] = v[:, h * HEAD_DIM:(h + 1) * HEAD_DIM]
        v_ref[:, (2 * h + 1) * HEAD_DIM:(2 * h + 2) * HEAD_DIM] = ones


def _inproj(x, g, sc, sh, w_in, wu, tm=512):
    s, d = x.shape
    aw, sw = N_HEADS * HEAD_DIM, wu.shape[1]
    rows = lambda c: pl.BlockSpec((tm, c), lambda i: (i, 0))
    cols = lambda width, blk: pl.BlockSpec((d, width), lambda i: (0, blk),
                                           pipeline_mode=pl.Buffered(1))
    return pl.pallas_call(
        functools.partial(_inproj_kernel, q_scale=HEAD_DIM ** -0.5 * LOG2E),
        grid=(s // tm,),
        in_specs=[rows(d), _resident((1, d)), _resident((1, d)), _resident((1, d)),
                  cols(aw, 0), cols(aw, 1), cols(aw, 2), _resident((d, sw)),
                  cols(LANES, 3 * aw // LANES)],
        out_specs=[rows(aw), rows(aw), rows(2 * aw), rows(sw), rows(LANES),
                   pl.BlockSpec((N_HEADS, LANES), lambda i: (0, 0))],
        out_shape=[jax.ShapeDtypeStruct((s, aw), BF16), jax.ShapeDtypeStruct((s, aw), BF16),
                   jax.ShapeDtypeStruct((s, 2 * aw), BF16), jax.ShapeDtypeStruct((s, sw), F32),
                   jax.ShapeDtypeStruct((s, LANES), F32),
                   jax.ShapeDtypeStruct((N_HEADS, LANES), F32)],
        compiler_params=_cparams(("arbitrary",), 56),
        name="inproj",
    )(x, g, sc, sh, w_in, w_in, w_in, wu, w_in)


def _log2_forget_cumsum(z, axis):
    x = jnp.minimum(z, 0.0) - jnp.log1p(jnp.exp(-jnp.abs(z)))
    pos = lax.broadcasted_iota(jnp.int32, x.shape, axis)
    shift = 1
    while shift < x.shape[axis]:
        x = x + jnp.where(pos >= shift, pltpu.roll(x, shift, axis=axis), 0.0)
        shift *= 2
    return x * LOG2E


def _cum_kernel(ft_ref, bcol_ref, f_ref, brow_ref, row_ref, col_ref):
    row_ref[...] = _log2_forget_cumsum(ft_ref[...] + bcol_ref[...], 1)
    col_ref[...] = _log2_forget_cumsum(f_ref[...] + brow_ref[...], 0)


def _forget_cumsum(f, b_f):
    s, w = f.shape
    h = b_f.shape[0]
    return pl.pallas_call(
        _cum_kernel,
        out_shape=[jax.ShapeDtypeStruct((h, s), F32), jax.ShapeDtypeStruct((s, w), F32)],
        compiler_params=_cparams(None, 48),
        name="forget_cumsum",
    )(f[:, :h].T, b_f.reshape(h, 1), f, jnp.pad(b_f, (0, w - h)).reshape(1, w))


def _attn_kernel(q_ref, k_ref, v_ref, cq_ref, ck_ref, kn_ref, wsrc_ref, o_ref, wdst_ref,
                 m_ref, acc_ref, *, tq, near, splits):
    i = pl.program_id(1)
    wdst_ref[...] = wsrc_ref[...].astype(BF16)
    kmax = jnp.sqrt(kn_ref[pl.ds(pl.program_id(0), 1), :]) * NORM_SLACK

    head_lane = lax.broadcasted_iota(jnp.int32, (tq, LANES), 1) == pl.program_id(0)
    cqb = jnp.broadcast_to(
        jnp.sum(jnp.where(head_lane, cq_ref[...], 0.0), axis=1, keepdims=True), (tq, LANES))

    half = tq // splits
    halves = [slice(hh * half, (hh + 1) * half) for hh in range(splits)]

    base = pl.multiple_of(i * tq, tq)
    qf = q_ref[...].astype(F32)
    own = jnp.sum(qf * k_ref[pl.ds(base, tq), :].astype(F32), axis=1, keepdims=True)
    qn = jnp.sqrt(jnp.sum(qf * qf, axis=1, keepdims=True))
    bound = jnp.max(qn * kmax - (own - cqb))
    ck_all = ck_ref[0]
    pos = lax.broadcasted_iota(jnp.int32, ck_all.shape, 1)
    dead = jnp.where((pos < i * tq) & (ck_all > bound + SKIP_LOG2), 1.0, 0.0)
    j0 = jnp.sum(dead).astype(jnp.int32) // tq

    first_near = jnp.maximum(i - near, 0)
    near_w = near * tq
    near_ks = pl.multiple_of(first_near * tq, tq)
    near_k = k_ref[pl.ds(near_ks, near_w), :]
    near_v = v_ref[pl.ds(near_ks, near_w), :]
    near_key = near_ks + lax.broadcasted_iota(jnp.int32, (1, near_w), 1)
    near_ck = jnp.where(near_key < base, ck_ref[0, :, pl.ds(near_ks, near_w)], jnp.inf)
    for hh, rs in enumerate(halves):
        nk = (hh + 1) * half
        qh = q_ref[rs, :]
        t_diag = _dot_nt(qh, k_ref[pl.ds(base, nk), :]) - ck_ref[0, :, pl.ds(base, nk)]
        row = lax.broadcasted_iota(jnp.int32, t_diag.shape, 0)
        col = lax.broadcasted_iota(jnp.int32, t_diag.shape, 1)
        t_diag = jnp.where(col <= row + hh * half, t_diag, -jnp.inf)
        t = jnp.concatenate([_dot_nt(qh, near_k) - near_ck, t_diag], axis=1)
        m0 = jnp.max(t, axis=1, keepdims=True) + cqb[rs]
        p = jnp.exp2(t - _lane_tile(m0 - cqb[rs], (near_w + nk) // LANES)).astype(BF16)
        m_ref[rs, :] = m0
        acc = _dot(p[:, :near_w], near_v) + _dot(p[:, near_w:], v_ref[pl.ds(base, nk), :])
        acc_ref[rs, :] = acc
        o_ref[rs, :] = (acc[:, :HEAD_DIM] / acc[:, HEAD_DIM:]).astype(BF16)

    def body(j, carry):
        ks = pl.multiple_of(j * tq, tq)
        kb = k_ref[pl.ds(ks, tq), :]
        vb = v_ref[pl.ds(ks, tq), :]
        ck = ck_ref[0, :, pl.ds(ks, tq)]
        for rs in halves:
            t = _dot_nt(q_ref[rs, :], kb) - ck
            m_prev = m_ref[rs, :]
            m_new = jnp.maximum(m_prev, jnp.max(t, axis=1, keepdims=True) + cqb[rs])
            alpha = jnp.exp2(m_prev - m_new)
            p = jnp.exp2(t - _lane_tile(m_new - cqb[rs], tq // LANES))
            acc_ref[rs, :] = _lane_tile(alpha, 2) * acc_ref[rs, :] + _dot(p.astype(BF16), vb)
            m_ref[rs, :] = m_new
        return carry

    @pl.when(j0 < first_near)
    def _():
        lax.fori_loop(j0, first_near, body, 0)
        acc = acc_ref[...]
        o_ref[...] = (acc[:, :HEAD_DIM] / acc[:, HEAD_DIM:]).astype(BF16)


def _attention(q, k, v_ones, cum_col, cum_row, k_norm_sq, w_cast, tq=512, near=2, splits=2):
    s = q.shape[0]
    nq = s // tq
    slab, w_bf16 = _cast_slab(w_cast, N_HEADS * nq)
    slab_spec = pl.BlockSpec((slab, w_cast.shape[1]), lambda h, i: (h * nq + i, 0))
    return pl.pallas_call(
        functools.partial(_attn_kernel, tq=tq, near=near, splits=splits),
        grid=(N_HEADS, nq),
        in_specs=[pl.BlockSpec((tq, HEAD_DIM), lambda h, i: (i, h)),
                  pl.BlockSpec((s, HEAD_DIM), lambda h, i: (0, h)),
                  pl.BlockSpec((s, 2 * HEAD_DIM), lambda h, i: (0, h)),
                  pl.BlockSpec((tq, LANES), lambda h, i: (i, 0)),
                  pl.BlockSpec((1, 1, s), lambda h, i: (h, 0, 0)),
                  pl.BlockSpec((N_HEADS, LANES), lambda h, i: (0, 0)),
                  slab_spec],
        out_specs=[pl.BlockSpec((tq, HEAD_DIM), lambda h, i: (i, h)), slab_spec],
        out_shape=[jax.ShapeDtypeStruct((s, N_HEADS * HEAD_DIM), BF16), w_bf16],
        scratch_shapes=[pltpu.VMEM((tq, LANES), F32), pltpu.VMEM((tq, 2 * HEAD_DIM), F32)],
        compiler_params=_cparams(("arbitrary", "arbitrary"), 40),
        name="fox_attention",
    )(q, k, v_ones, cum_col, cum_row, k_norm_sq, w_cast)


def _cmul(are, aim, bre, bim):
    return are * bre - aim * bim, are * bim + aim * bre


def _step_pair(u_ref, a, n_chunks):
    x0 = u_ref[pl.ds(2 * a, n_chunks, stride=SSM_T), :]
    x1 = u_ref[pl.ds(2 * a + 1, n_chunks, stride=SSM_T), :]
    return jnp.concatenate([x0, x1], axis=1).astype(BF16)


def _ssm_in_kernel(u_ref, ldt_ref, are_ref, aim_ref, btr_ref, bti_ref, cr_ref, ci_ref, d_ref,
                   wsrc_ref, w2_ref, pmt2_ref, x0_ref, wdst_ref,
                   cp_ref, wl_ref, q2_ref, v_ref):
    wdst_ref[...] = wsrc_ref[...].astype(BF16)
    n_chunks = u_ref.shape[0] // SSM_T
    half_w = SSM_TILE_G * SSM_STATE
    pw_parts = []
    lane = lax.broadcasted_iota(jnp.int32, (SSM_GROUP, LANES), 1)
    first = lane < SSM_STATE

    @pl.when(pl.program_id(0) == 0)
    def _():
        q2_ref[...] = jnp.zeros(q2_ref.shape, BF16)

    pmt2_ref[...] = jnp.zeros(pmt2_ref.shape, BF16)

    def place(ref, lead, step, gi, re_part, im_part):
        r0 = (step % 2) * LANES + gi * SSM_GROUP
        rows = slice(r0, r0 + SSM_GROUP)
        mine = first if gi % 2 == 0 else jnp.logical_not(first)
        c_re = (gi // 2) * LANES
        c_im = half_w + c_re
        ref[lead + (step // 2, rows, slice(c_re, c_re + LANES))] = (
            jnp.where(mine, re_part, 0.0).astype(BF16))
        ref[lead + (step // 2, rows, slice(c_im, c_im + LANES))] = (
            jnp.where(mine, im_part, 0.0).astype(BF16))

    for gi in range(SSM_TILE_G):
        dt = jnp.exp(ldt_ref[gi])
        are, aim = are_ref[gi], aim_ref[gi]
        mag = jnp.exp(dt * are)
        abre, abim = mag * jnp.cos(dt * aim), mag * jnp.sin(dt * aim)
        nre, nim = abre - 1.0, abim
        den = are * are + aim * aim
        zre = (nre * are + nim * aim) / den
        zim = (nim * are - nre * aim) / den
        bbre, bbim = _cmul(zre, zim, btr_ref[gi], bti_ref[gi])
        bbcat = jnp.where(first, bbre, bbim)

        cpre, cpim = cr_ref[gi], ci_ref[gi]
        qre, qim = bbre, bbim
        pwre, pwim = jnp.ones_like(abre), jnp.zeros_like(abim)
        for t in range(SSM_T):
            cp_ref[t * SSM_GROUP:(t + 1) * SSM_GROUP, :] = jnp.where(first, cpre, -cpim)
            if t > 0:
                place(pmt2_ref, (0,), t - 1, gi, cpre, -cpim)
            place(q2_ref, (), SSM_T - 1 - t, gi, qre, qim)
            cpre, cpim = _cmul(cpre, cpim, abre, abim)
            qre, qim = _cmul(qre, qim, abre, abim)
            pwre, pwim = _cmul(pwre, pwim, abre, abim)
        place(pmt2_ref, (0,), SSM_T - 1, gi, cpre, -cpim)
        pw_parts.append((pwre, pwim))

        krow = lax.dot_general(bbcat, cp_ref[...], (((1,), (1,)), ((), ())),
                               preferred_element_type=F32, precision=lax.Precision.HIGHEST)
        own = (lane >= gi * SSM_GROUP) & (lane < (gi + 1) * SSM_GROUP)
        for tau in range(SSM_T):
            half = krow[:, (tau // SSM_TILE_G) * LANES:(tau // SSM_TILE_G + 1) * LANES]
            shift = ((gi - tau % SSM_TILE_G) * SSM_GROUP) % LANES
            moved = half if shift == 0 else pltpu.roll(half, shift, axis=1)
            wl_ref[tau, gi * SSM_GROUP:(gi + 1) * SSM_GROUP, :] = jnp.where(own, moved, 0.0)

    r = lax.broadcasted_iota(jnp.int32, (LANES, LANES), 0)
    c = lax.broadcasted_iota(jnp.int32, (LANES, LANES), 1)
    wl_ref[0] = wl_ref[0] + jnp.where(r == c, d_ref[0], 0.0)
    for dl in range(SSM_PAIRS):
        diag = wl_ref[2 * dl].astype(BF16)
        w2_ref[0, dl, 0:LANES, 0:LANES] = diag
        w2_ref[0, dl, LANES:, LANES:] = diag
        w2_ref[0, dl, 0:LANES, LANES:] = wl_ref[2 * dl + 1].astype(BF16)
        below = jnp.zeros((LANES, LANES), BF16) if dl == 0 else wl_ref[2 * dl - 1].astype(BF16)
        w2_ref[0, dl, LANES:, 0:LANES] = below

    v = _dot(_step_pair(u_ref, 0, n_chunks), q2_ref[0])
    for a in range(1, SSM_PAIRS):
        v = v + _dot(_step_pair(u_ref, a, n_chunks), q2_ref[a])
    v_ref[...] = v

    pair = lambda k, part: jnp.where(first[0:1], pw_parts[2 * k][part], pw_parts[2 * k + 1][part])
    ar = jnp.concatenate([pair(k, 0) for k in range(SSM_TILE_G // 2)], axis=1)
    ai = jnp.concatenate([pair(k, 1) for k in range(SSM_TILE_G // 2)], axis=1)

    def scan_body(c, carry):
        xre, xim = carry
        x0_ref[pl.ds(c, 1), 0:half_w] = xre
        x0_ref[pl.ds(c, 1), half_w:] = xim
        vre = v_ref[pl.ds(c, 1), 0:half_w]
        vim = v_ref[pl.ds(c, 1), half_w:]
        return ar * xre - ai * xim + vre, ar * xim + ai * xre + vim

    zero = jnp.zeros(ar.shape, F32)
    lax.fori_loop(0, n_chunks, scan_body, (zero, zero), unroll=8)


def _ssm_in(u, ldt, are2, aim2, bt_re2, bt_im2, c_re2, c_im2, d_rows, w_cast):
    s, width = u.shape
    n_tiles = width // LANES
    c = s // SSM_T
    kw = SSM_TILE_G * LANES
    per_tile = lambda *shape: pl.BlockSpec((SSM_TILE_G,) + shape,
                                           lambda j: (j,) + (0,) * len(shape))
    slab, w_bf16 = _cast_slab(w_cast, n_tiles)
    slab_spec = pl.BlockSpec((slab, w_cast.shape[1]), lambda j: (j, 0))
    return pl.pallas_call(
        _ssm_in_kernel,
        grid=(n_tiles,),
        in_specs=[pl.BlockSpec((s, LANES), lambda j: (0, j)),
                  per_tile(1, LANES), per_tile(1, LANES), per_tile(1, LANES),
                  per_tile(SSM_GROUP, LANES), per_tile(SSM_GROUP, LANES),
                  per_tile(SSM_GROUP, LANES), per_tile(SSM_GROUP, LANES),
                  pl.BlockSpec((1, 1, LANES), lambda j: (j, 0, 0)), slab_spec],
        out_specs=[pl.BlockSpec((1, SSM_PAIRS, 2 * LANES, 2 * LANES), lambda j: (j, 0, 0, 0)),
                   pl.BlockSpec((1, SSM_PAIRS, 2 * LANES, kw), lambda j: (j, 0, 0, 0)),
                   pl.BlockSpec((c, kw), lambda j: (0, j)), slab_spec],
        out_shape=[jax.ShapeDtypeStruct((n_tiles, SSM_PAIRS, 2 * LANES, 2 * LANES), BF16),
                   jax.ShapeDtypeStruct((n_tiles, SSM_PAIRS, 2 * LANES, kw), BF16),
                   jax.ShapeDtypeStruct((c, n_tiles * kw), F32), w_bf16],
        scratch_shapes=[pltpu.VMEM((SSM_TW, LANES), F32),
                        pltpu.VMEM((SSM_T, LANES, LANES), F32),
                        pltpu.VMEM((SSM_PAIRS, 2 * LANES, kw), BF16),
                        pltpu.VMEM((c, kw), F32)],
        compiler_params=_cparams(("arbitrary",), 60),
        name="ssm_state_in",
    )(u, ldt, are2, aim2, bt_re2, bt_im2, c_re2, c_im2, d_rows, w_cast)


def _ssm_out_kernel(u_ref, w2_ref, pmt2_ref, x0_ref, wsrc_a_ref, wsrc_b_ref,
                    y_ref, wdst_a_ref, wdst_b_ref):
    wdst_a_ref[...] = wsrc_a_ref[...].astype(BF16)
    wdst_b_ref[...] = wsrc_b_ref[...].astype(BF16)
    n_chunks = u_ref.shape[0] // SSM_T
    x0 = x0_ref[...].astype(BF16)
    pairs = [_step_pair(u_ref, a, n_chunks) for a in range(SSM_PAIRS)]
    for b in range(SSM_PAIRS):
        acc = _dot_nt(x0, pmt2_ref[0, b])
        for a in range(b + 1):
            acc = acc + _dot(pairs[a], w2_ref[0, b - a])
        y_ref[pl.ds(2 * b, n_chunks, stride=SSM_T), :] = acc[:, :LANES]
        y_ref[pl.ds(2 * b + 1, n_chunks, stride=SSM_T), :] = acc[:, LANES:]


def _ssm_out(u, w2, pmt2, x0, w_cast_a, w_cast_b):
    s, width = u.shape
    n_tiles = width // LANES
    c = s // SSM_T
    kw = SSM_TILE_G * LANES
    slab_a, a_bf16 = _cast_slab(w_cast_a, n_tiles)
    slab_b, b_bf16 = _cast_slab(w_cast_b, n_tiles)
    spec_a = pl.BlockSpec((slab_a, w_cast_a.shape[1]), lambda j: (j, 0))
    spec_b = pl.BlockSpec((slab_b, w_cast_b.shape[1]), lambda j: (j, 0))
    return pl.pallas_call(
        _ssm_out_kernel,
        grid=(n_tiles,),
        in_specs=[pl.BlockSpec((s, LANES), lambda j: (0, j)),
                  pl.BlockSpec((1, SSM_PAIRS, 2 * LANES, 2 * LANES), lambda j: (j, 0, 0, 0)),
                  pl.BlockSpec((1, SSM_PAIRS, 2 * LANES, kw), lambda j: (j, 0, 0, 0)),
                  pl.BlockSpec((c, kw), lambda j: (0, j)), spec_a, spec_b],
        out_specs=[pl.BlockSpec((s, LANES), lambda j: (0, j)), spec_a, spec_b],
        out_shape=[jax.ShapeDtypeStruct((s, width), F32), a_bf16, b_bf16],
        compiler_params=_cparams(("arbitrary",), 48),
        name="ssm_out",
    )(u, w2, pmt2, x0, w_cast_a, w_cast_b)


def _s5(u, a_re, a_im, log_dt, b_re, b_im, c_re, c_im, d_skip, w_cast_in, w_cast_a, w_cast_b):
    g, p = a_re.shape
    dup = lambda a: jnp.concatenate([a, a], axis=-1)
    ldt = jnp.broadcast_to(log_dt.reshape(g, 1, 1), (g, 1, LANES))
    are2, aim2 = dup(a_re).reshape(g, 1, LANES), dup(a_im).reshape(g, 1, LANES)
    bt_re2 = dup(jnp.swapaxes(b_re, 1, 2))
    bt_im2 = dup(jnp.swapaxes(b_im, 1, 2))
    w2, pmt2, x0, w_in_bf16 = _ssm_in(u, ldt, are2, aim2, bt_re2, bt_im2, dup(c_re), dup(c_im),
                                      d_skip.reshape(-1, 1, LANES), w_cast_in)
    y, w_a_bf16, w_b_bf16 = _ssm_out(u, w2, pmt2, x0, w_cast_a, w_cast_b)
    return y, w_in_bf16, w_a_bf16, w_b_bf16


def _gelu_tanh(x):
    return 0.5 * x * (1.0 + jnp.tanh(math.sqrt(2.0 / math.pi) * (x + 0.044715 * (x * x * x))))


def _mixout_kernel(x_ref, attn_ref, y_ref, wglu_ref, bglu_ref, ga_ref, gs_ref, wo_ref,
                   gt_ref, gf_ref, sc_ref, sh_ref, h_ref, hn_ref):
    aw = attn_ref.shape[1]
    y = _gelu_tanh(y_ref[...])
    gate = jax.nn.sigmoid(_dot(y.astype(BF16), wglu_ref[...]) + bglu_ref[...])
    ns = _rms(y * gate, gs_ref[...]).astype(BF16)
    na = _rms(attn_ref[...].astype(F32), ga_ref[...]).astype(BF16)
    mixed = _dot(na, wo_ref[0:aw, :]) + _dot(ns, wo_ref[aw:, :])
    h = x_ref[...] + gt_ref[...] * mixed
    h_ref[...] = h
    hn_ref[...] = (_rms(h, gf_ref[...]) * (1.0 + sc_ref[...]) + sh_ref[...]).astype(BF16)


def _mixout(x, attn, y, w_glu, b_glu, g_attn, g_ssm, w_out, gt1, g_ffn, sc2, sh2, tm=512):
    s, d = x.shape
    w = attn.shape[1]
    rows = lambda c: pl.BlockSpec((tm, c), lambda i: (i, 0))
    return pl.pallas_call(
        _mixout_kernel,
        grid=(s // tm,),
        in_specs=[rows(d), rows(w), rows(w), _resident((w, w)), _resident((1, w)),
                  _resident((1, w)), _resident((1, w)), _resident(w_out.shape),
                  _resident((1, d)), _resident((1, d)), _resident((1, d)), _resident((1, d))],
        out_specs=[rows(d), rows(d)],
        out_shape=[jax.ShapeDtypeStruct((s, d), F32), jax.ShapeDtypeStruct((s, d), BF16)],
        compiler_params=_cparams(("arbitrary",), 56),
        name="mixer_out",
    )(x, attn, y, w_glu, b_glu, g_attn, g_ssm, w_out, gt1, g_ffn, sc2, sh2)


FFN_HALO = 16


def _ffn_kernel(hn_ref, halo_ref, h_ref, wa_ref, wb_ref, cw_ref, cb_ref, wd_ref, gt_ref, gfin_ref,
                o_ref):
    i, j = pl.program_id(0), pl.program_id(1)

    @pl.when(j == 0)
    def _():
        o_ref[...] = jnp.zeros(o_ref.shape, F32)

    hn = hn_ref[...]
    a_ext = _dot(jnp.concatenate([halo_ref[...], hn], axis=0), wa_ref[...])
    a = a_ext[FFN_HALO:]
    b = _dot(hn, wb_ref[...])
    halo = a_ext[FFN_HALO - 8:FFN_HALO] * (i > 0).astype(F32)
    row = lax.broadcasted_iota(jnp.int32, a.shape, 0)
    prev1 = jnp.where(row == 0, halo[7:8, :], pltpu.roll(a, 1, axis=0))
    prev2 = jnp.where(row == 0, halo[6:7, :],
                      jnp.where(row == 1, halo[7:8, :], pltpu.roll(a, 2, axis=0)))
    cw = cw_ref[...]
    conv = cb_ref[...] + cw[0:1, :] * prev2 + cw[1:2, :] * prev1 + cw[2:3, :] * a
    act = (conv * jax.nn.sigmoid(conv) * b).astype(BF16)
    o_ref[...] += _dot(act, wd_ref[...])

    @pl.when(j == pl.num_programs(1) - 1)
    def _():
        h = h_ref[...] + gt_ref[...] * o_ref[...]
        o_ref[...] = _rms(h, gfin_ref[...])


def _ffn(hn, h, w_up, conv_w, conv_b, w_down, gt2, g_final, tm=1024, tn=512):
    s, d = h.shape
    d_ff = w_down.shape[0]
    nf = d_ff // tn
    halo_blocks = tm // FFN_HALO
    return pl.pallas_call(
        _ffn_kernel,
        grid=(s // tm, nf),
        in_specs=[pl.BlockSpec((tm, d), lambda i, j: (i, 0)),
                  pl.BlockSpec((FFN_HALO, d),
                               lambda i, j: (jnp.maximum(i * halo_blocks - 1, 0), 0)),
                  pl.BlockSpec((tm, d), lambda i, j: (i, 0), pipeline_mode=pl.Buffered(1)),
                  pl.BlockSpec((d, tn), lambda i, j: (0, j)),
                  pl.BlockSpec((d, tn), lambda i, j: (0, nf + j)),
                  pl.BlockSpec((3, tn), lambda i, j: (0, j)),
                  pl.BlockSpec((1, tn), lambda i, j: (0, j)),
                  pl.BlockSpec((tn, d), lambda i, j: (j, 0)),
                  _resident((1, d)), _resident((1, d))],
        out_specs=pl.BlockSpec((tm, d), lambda i, j: (i, 0)),
        out_shape=jax.ShapeDtypeStruct((s, d), F32),
        compiler_params=_cparams(("arbitrary", "arbitrary"), 62),
        name="conv_ffn",
    )(hn, hn, h, w_up, w_up, conv_w, conv_b, w_down, gt2, g_final)


def _layer(h, mod, g_mix, w_in, b_f, a_re, a_im, log_dt, ssm_b_re, ssm_b_im, ssm_c_re, ssm_c_im,
           ssm_d, w_glu, b_glu, g_attn_out, g_ssm_out, w_out, g_ffn, w_up, conv_w, conv_b, w_down):
    s, d = h.shape
    aw = N_HEADS * HEAD_DIM
    sh1, sc1, gt1, sh2, sc2, gt2 = [mod[:, i * d:(i + 1) * d] for i in range(N_MOD)]
    row = lambda a: a.reshape(1, -1)

    q, k, v, u, f, k_norm_sq = _inproj(h, row(g_mix), sc1, sh1, w_in,
                                       w_in[:, 3 * aw + N_HEADS:])

    cum_row, cum_col = _forget_cumsum(f, b_f)
    attn, w_up_bf16 = _attention(q, k, v, cum_col, cum_row.reshape(N_HEADS, 1, s), k_norm_sq,
                                 w_up)

    y, w_down_bf16, w_out_bf16, w_glu_bf16 = _s5(
        u, a_re, a_im, log_dt, ssm_b_re, ssm_b_im, ssm_c_re, ssm_c_im, ssm_d,
        w_down, w_out, w_glu)

    h1, hn2 = _mixout(h, attn, y, w_glu_bf16, row(b_glu), row(g_attn_out),
                      row(g_ssm_out), w_out_bf16, gt1, row(g_ffn), sc2, sh2)
    return hn2, h1, (w_up_bf16, conv_w, row(conv_b), w_down_bf16, gt2)


def kernel(x, c, w_ada, b_ada, g_mix, w_in, b_f, a_re, a_im, log_dt, ssm_b_re, ssm_b_im, ssm_c_re,
           ssm_c_im, ssm_d, w_glu, b_glu, g_attn_out, g_ssm_out, w_out, g_ffn, w_up, conv_w,
           conv_b, w_down, g_final):
    batch, s, d = x.shape
    assert w_ada.shape[0] == 1, "only DEPTH == 1 is supported"
    l = 0
    outs = []
    for bi in range(batch):
        mod = _adaln(c[bi:bi + 1], w_ada[l], b_ada[l])
        hn2, h1, ffn_args = _layer(
            x[bi], mod, g_mix[l], w_in[l].astype(BF16), b_f[l], a_re[l], a_im[l], log_dt[l], ssm_b_re[l],
            ssm_b_im[l], ssm_c_re[l], ssm_c_im[l], ssm_d[l], w_glu[l], b_glu[l],
            g_attn_out[l], g_ssm_out[l], w_out[l], g_ffn[l], w_up[l], conv_w[l], conv_b[l],
            w_down[l])
        outs.append(_ffn(hn2, h1, *ffn_args, g_final.reshape(1, d)))
    return jnp.stack(outs, axis=0)
```

```python
import functools
import math

import jax
import jax.numpy as jnp
from jax import lax
from jax.experimental import pallas as pl
from jax.experimental.pallas import tpu as pltpu

F32 = jnp.float32
BF16 = jnp.bfloat16

EPS = 1e-6
HEAD_DIM = 128
N_HEADS = 8
SSM_GROUP = 16
SSM_STATE = 64
N_MOD = 6
LANES = 128
SSM_T = 16
SSM_TW = SSM_T * SSM_GROUP
SSM_PAIRS = SSM_T // 2
SSM_TILE_G = LANES // SSM_GROUP
LOG2E = 1.4426950408889634
SKIP_LOG2 = 151.0
NORM_SLACK = 1.01

_MIB = 1024 * 1024


def _cparams(semantics, vmem_mib):
    return pltpu.CompilerParams(dimension_semantics=semantics, vmem_limit_bytes=vmem_mib * _MIB)


def _resident(shape):
    return pl.BlockSpec(shape, lambda *_: (0,) * len(shape), pipeline_mode=pl.Buffered(1))


def _dot(a, b):
    return jnp.dot(a, b, preferred_element_type=F32)


def _dot_nt(a, b):
    return lax.dot_general(a, b, (((1,), (1,)), ((), ())), preferred_element_type=F32)


def _rms(x, g):
    return x * lax.rsqrt(jnp.mean(x * x, axis=-1, keepdims=True) + EPS) * g


def _lane_tile(x, reps):
    return jnp.concatenate([x] * reps, axis=1)


def _cast_slab(w, steps):
    rows = w.shape[0] // steps
    assert rows * steps == w.shape[0] and rows % 16 == 0, (w.shape, steps)
    return rows, jax.ShapeDtypeStruct(w.shape, BF16)


def _adaln_kernel(c_ref, w_ref, b_ref, o_ref):
    c = c_ref[...]
    cond = c * jax.nn.sigmoid(c)
    cond8 = jnp.broadcast_to(cond, (8, c.shape[1])).astype(BF16)
    acc = _dot(cond8, w_ref[...].astype(BF16))
    o_ref[...] = acc[0:1, :] + b_ref[...]


def _adaln(c, w, b, tn=1536):
    d, n = w.shape
    return pl.pallas_call(
        _adaln_kernel,
        grid=(n // tn,),
        in_specs=[pl.BlockSpec((1, d), lambda j: (0, 0)),
                  pl.BlockSpec((d, tn), lambda j: (0, j)),
                  pl.BlockSpec((1, tn), lambda j: (0, j))],
        out_specs=pl.BlockSpec((1, tn), lambda j: (0, j)),
        out_shape=jax.ShapeDtypeStruct((1, n), F32),
        compiler_params=_cparams(("arbitrary",), 48),
        name="adaln",
    )(c, w, b.reshape(1, n))


def _inproj_kernel(x_ref, g_ref, sc_ref, sh_ref, wq_ref, wk_ref, wv_ref, wu_ref, wf_ref,
                   q_ref, k_ref, v_ref, u_ref, f_ref, kn_ref, *, q_scale):
    hn = (_rms(x_ref[...], g_ref[...]) * (1.0 + sc_ref[...]) + sh_ref[...]).astype(BF16)
    f_ref[...] = _dot(hn, wf_ref[...])
    q_ref[...] = (_dot(hn, wq_ref[...]) * q_scale).astype(BF16)
    k = _dot(hn, wk_ref[...]).astype(BF16)
    k_ref[...] = k
    u_ref[...] = _dot(hn, wu_ref[...])

    k32 = k.astype(F32)
    ksq = k32 * k32
    tile_max = jnp.concatenate(
        [jnp.broadcast_to(
            jnp.max(jnp.sum(ksq[:, h * HEAD_DIM:(h + 1) * HEAD_DIM], axis=1, keepdims=True),
                    axis=0, keepdims=True), (1, LANES)) for h in range(N_HEADS)], axis=0)

    @pl.when(pl.program_id(0) == 0)
    def _():
        kn_ref[...] = tile_max

    @pl.when(pl.program_id(0) > 0)
    def _():
        kn_ref[...] = jnp.maximum(kn_ref[...], tile_max)

    v = _dot(hn, wv_ref[...]).astype(BF16)
    ones = jnp.ones((v.shape[0], HEAD_DIM), BF16)
    for h in range(N_HEADS):
        v_ref[:, 2 * h * HEAD_DIM:(2 * h + 1) * HEAD_DIM] = v[:, h * HEAD_DIM:(h + 1) * HEAD_DIM]
        v_ref[:, (2 * h + 1) * HEAD_DIM:(2 * h + 2) * HEAD_DIM] = ones


def _inproj(x, g, sc, sh, w_in, wu, tm=512):
    s, d = x.shape
    aw, sw = N_HEADS * HEAD_DIM, wu.shape[1]
    rows = lambda c: pl.BlockSpec((tm, c), lambda i: (i, 0))
    cols = lambda width, blk: pl.BlockSpec((d, width), lambda i: (0, blk),
                                           pipeline_mode=pl.Buffered(1))
    return pl.pallas_call(
        functools.partial(_inproj_kernel, q_scale=HEAD_DIM ** -0.5 * LOG2E),
        grid=(s // tm,),
        in_specs=[rows(d), _resident((1, d)), _resident((1, d)), _resident((1, d)),
                  cols(aw, 0), cols(aw, 1), cols(aw, 2), _resident((d, sw)),
                  cols(LANES, 3 * aw // LANES)],
        out_specs=[rows(aw), rows(aw), rows(2 * aw), rows(sw), rows(LANES),
                   pl.BlockSpec((N_HEADS, LANES), lambda i: (0, 0))],
        out_shape=[jax.ShapeDtypeStruct((s, aw), BF16), jax.ShapeDtypeStruct((s, aw), BF16),
                   jax.ShapeDtypeStruct((s, 2 * aw), BF16), jax.ShapeDtypeStruct((s, sw), F32),
                   jax.ShapeDtypeStruct((s, LANES), F32),
                   jax.ShapeDtypeStruct((N_HEADS, LANES), F32)],
        compiler_params=_cparams(("arbitrary",), 56),
        name="inproj",
    )(x, g, sc, sh, w_in, w_in, w_in, wu, w_in)


def _log2_forget_cumsum(z, axis):
    x = jnp.minimum(z, 0.0) - jnp.log1p(jnp.exp(-jnp.abs(z)))
    pos = lax.broadcasted_iota(jnp.int32, x.shape, axis)
    shift = 1
    while shift < x.shape[axis]:
        x = x + jnp.where(pos >= shift, pltpu.roll(x, shift, axis=axis), 0.0)
        shift *= 2
    return x * LOG2E


def _cum_kernel(ft_ref, bcol_ref, f_ref, brow_ref, row_ref, col_ref):
    row_ref[...] = _log2_forget_cumsum(ft_ref[...] + bcol_ref[...], 1)
    col_ref[...] = _log2_forget_cumsum(f_ref[...] + brow_ref[...], 0)


def _forget_cumsum(f, b_f):
    s, w = f.shape
    h = b_f.shape[0]
    return pl.pallas_call(
        _cum_kernel,
        out_shape=[jax.ShapeDtypeStruct((h, s), F32), jax.ShapeDtypeStruct((s, w), F32)],
        compiler_params=_cparams(None, 48),
        name="forget_cumsum",
    )(f[:, :h].T, b_f.reshape(h, 1), f, jnp.pad(b_f, (0, w - h)).reshape(1, w))


def _attn_kernel(q_ref, k_ref, v_ref, cq_ref, ck_ref, kn_ref, wsrc_ref, o_ref, wdst_ref,
                 m_ref, acc_ref, *, tq, near, splits):
    i = pl.program_id(1)
    wdst_ref[...] = wsrc_ref[...].astype(BF16)
    kmax = jnp.sqrt(kn_ref[pl.ds(pl.program_id(0), 1), :]) * NORM_SLACK

    head_lane = lax.broadcasted_iota(jnp.int32, (tq, LANES), 1) == pl.program_id(0)
    cqb = jnp.broadcast_to(
        jnp.sum(jnp.where(head_lane, cq_ref[...], 0.0), axis=1, keepdims=True), (tq, LANES))

    half = tq // splits
    halves = [slice(hh * half, (hh + 1) * half) for hh in range(splits)]

    base = pl.multiple_of(i * tq, tq)
    qf = q_ref[...].astype(F32)
    own = jnp.sum(qf * k_ref[pl.ds(base, tq), :].astype(F32), axis=1, keepdims=True)
    qn = jnp.sqrt(jnp.sum(qf * qf, axis=1, keepdims=True))
    bound = jnp.max(qn * kmax - (own - cqb))
    ck_all = ck_ref[0]
    pos = lax.broadcasted_iota(jnp.int32, ck_all.shape, 1)
    dead = jnp.where((pos < i * tq) & (ck_all > bound + SKIP_LOG2), 1.0, 0.0)
    j0 = jnp.sum(dead).astype(jnp.int32) // tq

    first_near = jnp.maximum(i - near, 0)
    near_w = near * tq
    near_ks = pl.multiple_of(first_near * tq, tq)
    near_k = k_ref[pl.ds(near_ks, near_w), :]
    near_v = v_ref[pl.ds(near_ks, near_w), :]
    near_key = near_ks + lax.broadcasted_iota(jnp.int32, (1, near_w), 1)
    near_ck = jnp.where(near_key < base, ck_ref[0, :, pl.ds(near_ks, near_w)], jnp.inf)
    t_near = _dot_nt(q_ref[...], near_k) - near_ck
    for hh, rs in enumerate(halves):
        nk = (hh + 1) * half
        qh = q_ref[rs, :]
        t_diag = _dot_nt(qh, k_ref[pl.ds(base, nk), :]) - ck_ref[0, :, pl.ds(base, nk)]
        row = lax.broadcasted_iota(jnp.int32, t_diag.shape, 0)
        col = lax.broadcasted_iota(jnp.int32, t_diag.shape, 1)
        t_diag = jnp.where(col <= row + hh * half, t_diag, -jnp.inf)
        t = jnp.concatenate([t_near[rs], t_diag], axis=1)
        m0 = jnp.max(t, axis=1, keepdims=True) + cqb[rs]
        p = jnp.exp2(t - _lane_tile(m0 - cqb[rs], (near_w + nk) // LANES)).astype(BF16)
        m_ref[rs, :] = m0
        acc = _dot(p[:, :near_w], near_v) + _dot(p[:, near_w:], v_ref[pl.ds(base, nk), :])
        acc_ref[rs, :] = acc
        o_ref[rs, :] = (acc[:, :HEAD_DIM] / acc[:, HEAD_DIM:]).astype(BF16)

    def body(j, carry):
        ks = pl.multiple_of(j * tq, tq)
        kb = k_ref[pl.ds(ks, tq), :]
        vb = v_ref[pl.ds(ks, tq), :]
        t_all = _dot_nt(q_ref[...], kb) - ck_ref[0, :, pl.ds(ks, tq)]
        for rs in halves:
            t = t_all[rs]
            m_prev = m_ref[rs, :]
            m_new = jnp.maximum(m_prev, jnp.max(t, axis=1, keepdims=True) + cqb[rs])
            alpha = jnp.exp2(m_prev - m_new)
            p = jnp.exp2(t - _lane_tile(m_new - cqb[rs], tq // LANES))
            acc_ref[rs, :] = _lane_tile(alpha, 2) * acc_ref[rs, :] + _dot(p.astype(BF16), vb)
            m_ref[rs, :] = m_new
        return carry

    @pl.when(j0 < first_near)
    def _():
        lax.fori_loop(j0, first_near, body, 0)
        acc = acc_ref[...]
        o_ref[...] = (acc[:, :HEAD_DIM] / acc[:, HEAD_DIM:]).astype(BF16)


def _attention(q, k, v_ones, cum_col, cum_row, k_norm_sq, w_cast, tq=512, near=2, splits=2):
    s = q.shape[0]
    nq = s // tq
    slab, w_bf16 = _cast_slab(w_cast, N_HEADS * nq)
    slab_spec = pl.BlockSpec((slab, w_cast.shape[1]), lambda h, i: (h * nq + i, 0))
    return pl.pallas_call(
        functools.partial(_attn_kernel, tq=tq, near=near, splits=splits),
        grid=(N_HEADS, nq),
        in_specs=[pl.BlockSpec((tq, HEAD_DIM), lambda h, i: (i, h)),
                  pl.BlockSpec((s, HEAD_DIM), lambda h, i: (0, h)),
                  pl.BlockSpec((s, 2 * HEAD_DIM), lambda h, i: (0, h)),
                  pl.BlockSpec((tq, LANES), lambda h, i: (i, 0)),
                  pl.BlockSpec((1, 1, s), lambda h, i: (h, 0, 0)),
                  pl.BlockSpec((N_HEADS, LANES), lambda h, i: (0, 0)),
                  slab_spec],
        out_specs=[pl.BlockSpec((tq, HEAD_DIM), lambda h, i: (i, h)), slab_spec],
        out_shape=[jax.ShapeDtypeStruct((s, N_HEADS * HEAD_DIM), BF16), w_bf16],
        scratch_shapes=[pltpu.VMEM((tq, LANES), F32), pltpu.VMEM((tq, 2 * HEAD_DIM), F32)],
        compiler_params=_cparams(("arbitrary", "arbitrary"), 40),
        name="fox_attention",
    )(q, k, v_ones, cum_col, cum_row, k_norm_sq, w_cast)


def _cmul(are, aim, bre, bim):
    return are * bre - aim * bim, are * bim + aim * bre


def _step_pair(u_ref, a, n_chunks):
    x0 = u_ref[pl.ds(2 * a, n_chunks, stride=SSM_T), :]
    x1 = u_ref[pl.ds(2 * a + 1, n_chunks, stride=SSM_T), :]
    return jnp.concatenate([x0, x1], axis=1).astype(BF16)


def _ssm_in_kernel(u_ref, ldt_ref, are_ref, aim_ref, btr_ref, bti_ref, cr_ref, ci_ref, d_ref,
                   wsrc_ref, w2_ref, pmt2_ref, x0_ref, wdst_ref,
                   cp_ref, wl_ref, q2_ref, v_ref):
    wdst_ref[...] = wsrc_ref[...].astype(BF16)
    n_chunks = u_ref.shape[0] // SSM_T
    half_w = SSM_TILE_G * SSM_STATE
    pw_parts = []
    lane = lax.broadcasted_iota(jnp.int32, (SSM_GROUP, LANES), 1)
    first = lane < SSM_STATE

    @pl.when(pl.program_id(0) == 0)
    def _():
        q2_ref[...] = jnp.zeros(q2_ref.shape, BF16)

    pmt2_ref[...] = jnp.zeros(pmt2_ref.shape, BF16)

    def place(ref, lead, step, gi, re_part, im_part):
        r0 = (step % 2) * LANES + gi * SSM_GROUP
        rows = slice(r0, r0 + SSM_GROUP)
        mine = first if gi % 2 == 0 else jnp.logical_not(first)
        c_re = (gi // 2) * LANES
        c_im = half_w + c_re
        ref[lead + (step // 2, rows, slice(c_re, c_re + LANES))] = (
            jnp.where(mine, re_part, 0.0).astype(BF16))
        ref[lead + (step // 2, rows, slice(c_im, c_im + LANES))] = (
            jnp.where(mine, im_part, 0.0).astype(BF16))

    for gi in range(SSM_TILE_G):
        dt = jnp.exp(ldt_ref[gi])
        are, aim = are_ref[gi], aim_ref[gi]
        mag = jnp.exp(dt * are)
        abre, abim = mag * jnp.cos(dt * aim), mag * jnp.sin(dt * aim)
        nre, nim = abre - 1.0, abim
        den = are * are + aim * aim
        zre = (nre * are + nim * aim) / den
        zim = (nim * are - nre * aim) / den
        bbre, bbim = _cmul(zre, zim, btr_ref[gi], bti_ref[gi])
        bbcat = jnp.where(first, bbre, bbim)

        cpre, cpim = cr_ref[gi], ci_ref[gi]
        qre, qim = bbre, bbim
        pwre, pwim = jnp.ones_like(abre), jnp.zeros_like(abim)
        for t in range(SSM_T):
            cp_ref[t * SSM_GROUP:(t + 1) * SSM_GROUP, :] = jnp.where(first, cpre, -cpim)
            if t > 0:
                place(pmt2_ref, (0,), t - 1, gi, cpre, -cpim)
            place(q2_ref, (), SSM_T - 1 - t, gi, qre, qim)
            cpre, cpim = _cmul(cpre, cpim, abre, abim)
            qre, qim = _cmul(qre, qim, abre, abim)
            pwre, pwim = _cmul(pwre, pwim, abre, abim)
        place(pmt2_ref, (0,), SSM_T - 1, gi, cpre, -cpim)
        pw_parts.append((pwre, pwim))

        krow = lax.dot_general(bbcat, cp_ref[...], (((1,), (1,)), ((), ())),
                               preferred_element_type=F32, precision=lax.Precision.HIGHEST)
        own = (lane >= gi * SSM_GROUP) & (lane < (gi + 1) * SSM_GROUP)
        for tau in range(SSM_T):
            half = krow[:, (tau // SSM_TILE_G) * LANES:(tau // SSM_TILE_G + 1) * LANES]
            shift = ((gi - tau % SSM_TILE_G) * SSM_GROUP) % LANES
            moved = half if shift == 0 else pltpu.roll(half, shift, axis=1)
            wl_ref[tau, gi * SSM_GROUP:(gi + 1) * SSM_GROUP, :] = jnp.where(own, moved, 0.0)

    r = lax.broadcasted_iota(jnp.int32, (LANES, LANES), 0)
    c = lax.broadcasted_iota(jnp.int32, (LANES, LANES), 1)
    wl_ref[0] = wl_ref[0] + jnp.where(r == c, d_ref[0], 0.0)
    for dl in range(SSM_PAIRS):
        diag = wl_ref[2 * dl].astype(BF16)
        w2_ref[0, dl, 0:LANES, 0:LANES] = diag
        w2_ref[0, dl, LANES:, LANES:] = diag
        w2_ref[0, dl, 0:LANES, LANES:] = wl_ref[2 * dl + 1].astype(BF16)
        below = jnp.zeros((LANES, LANES), BF16) if dl == 0 else wl_ref[2 * dl - 1].astype(BF16)
        w2_ref[0, dl, LANES:, 0:LANES] = below

    v = _dot(_step_pair(u_ref, 0, n_chunks), q2_ref[0])
    for a in range(1, SSM_PAIRS):
        v = v + _dot(_step_pair(u_ref, a, n_chunks), q2_ref[a])
    v_ref[...] = v

    pair = lambda k, part: jnp.where(first[0:1], pw_parts[2 * k][part], pw_parts[2 * k + 1][part])
    ar = jnp.concatenate([pair(k, 0) for k in range(SSM_TILE_G // 2)], axis=1)
    ai = jnp.concatenate([pair(k, 1) for k in range(SSM_TILE_G // 2)], axis=1)

    def scan_body(c, carry):
        xre, xim = carry
        x0_ref[pl.ds(c, 1), 0:half_w] = xre
        x0_ref[pl.ds(c, 1), half_w:] = xim
        vre = v_ref[pl.ds(c, 1), 0:half_w]
        vim = v_ref[pl.ds(c, 1), half_w:]
        return ar * xre - ai * xim + vre, ar * xim + ai * xre + vim

    zero = jnp.zeros(ar.shape, F32)
    lax.fori_loop(0, n_chunks, scan_body, (zero, zero), unroll=8)


def _ssm_in(u, ldt, are2, aim2, bt_re2, bt_im2, c_re2, c_im2, d_rows, w_cast):
    s, width = u.shape
    n_tiles = width // LANES
    c = s // SSM_T
    kw = SSM_TILE_G * LANES
    per_tile = lambda *shape: pl.BlockSpec((SSM_TILE_G,) + shape,
                                           lambda j: (j,) + (0,) * len(shape))
    slab, w_bf16 = _cast_slab(w_cast, n_tiles)
    slab_spec = pl.BlockSpec((slab, w_cast.shape[1]), lambda j: (j, 0))
    return pl.pallas_call(
        _ssm_in_kernel,
        grid=(n_tiles,),
        in_specs=[pl.BlockSpec((s, LANES), lambda j: (0, j)),
                  per_tile(1, LANES), per_tile(1, LANES), per_tile(1, LANES),
                  per_tile(SSM_GROUP, LANES), per_tile(SSM_GROUP, LANES),
                  per_tile(SSM_GROUP, LANES), per_tile(SSM_GROUP, LANES),
                  pl.BlockSpec((1, 1, LANES), lambda j: (j, 0, 0)), slab_spec],
        out_specs=[pl.BlockSpec((1, SSM_PAIRS, 2 * LANES, 2 * LANES), lambda j: (j, 0, 0, 0)),
                   pl.BlockSpec((1, SSM_PAIRS, 2 * LANES, kw), lambda j: (j, 0, 0, 0)),
                   pl.BlockSpec((c, kw), lambda j: (0, j)), slab_spec],
        out_shape=[jax.ShapeDtypeStruct((n_tiles, SSM_PAIRS, 2 * LANES, 2 * LANES), BF16),
                   jax.ShapeDtypeStruct((n_tiles, SSM_PAIRS, 2 * LANES, kw), BF16),
                   jax.ShapeDtypeStruct((c, n_tiles * kw), F32), w_bf16],
        scratch_shapes=[pltpu.VMEM((SSM_TW, LANES), F32),
                        pltpu.VMEM((SSM_T, LANES, LANES), F32),
                        pltpu.VMEM((SSM_PAIRS, 2 * LANES, kw), BF16),
                        pltpu.VMEM((c, kw), F32)],
        compiler_params=_cparams(("arbitrary",), 60),
        name="ssm_state_in",
    )(u, ldt, are2, aim2, bt_re2, bt_im2, c_re2, c_im2, d_rows, w_cast)


def _ssm_out_kernel(u_ref, w2_ref, pmt2_ref, x0_ref, wsrc_a_ref, wsrc_b_ref,
                    y_ref, wdst_a_ref, wdst_b_ref):
    wdst_a_ref[...] = wsrc_a_ref[...].astype(BF16)
    wdst_b_ref[...] = wsrc_b_ref[...].astype(BF16)
    n_chunks = u_ref.shape[0] // SSM_T
    x0 = x0_ref[...].astype(BF16)
    pairs = [_step_pair(u_ref, a, n_chunks) for a in range(SSM_PAIRS)]
    for b in range(SSM_PAIRS):
        acc = _dot_nt(x0, pmt2_ref[0, b])
        for a in range(b + 1):
            acc = acc + _dot(pairs[a], w2_ref[0, b - a])
        y_ref[pl.ds(2 * b, n_chunks, stride=SSM_T), :] = acc[:, :LANES]
        y_ref[pl.ds(2 * b + 1, n_chunks, stride=SSM_T), :] = acc[:, LANES:]


def _ssm_out(u, w2, pmt2, x0, w_cast_a, w_cast_b):
    s, width = u.shape
    n_tiles = width // LANES
    c = s // SSM_T
    kw = SSM_TILE_G * LANES
    slab_a, a_bf16 = _cast_slab(w_cast_a, n_tiles)
    slab_b, b_bf16 = _cast_slab(w_cast_b, n_tiles)
    spec_a = pl.BlockSpec((slab_a, w_cast_a.shape[1]), lambda j: (j, 0))
    spec_b = pl.BlockSpec((slab_b, w_cast_b.shape[1]), lambda j: (j, 0))
    return pl.pallas_call(
        _ssm_out_kernel,
        grid=(n_tiles,),
        in_specs=[pl.BlockSpec((s, LANES), lambda j: (0, j)),
                  pl.BlockSpec((1, SSM_PAIRS, 2 * LANES, 2 * LANES), lambda j: (j, 0, 0, 0)),
                  pl.BlockSpec((1, SSM_PAIRS, 2 * LANES, kw), lambda j: (j, 0, 0, 0)),
                  pl.BlockSpec((c, kw), lambda j: (0, j)), spec_a, spec_b],
        out_specs=[pl.BlockSpec((s, LANES), lambda j: (0, j)), spec_a, spec_b],
        out_shape=[jax.ShapeDtypeStruct((s, width), F32), a_bf16, b_bf16],
        compiler_params=_cparams(("arbitrary",), 48),
        name="ssm_out",
    )(u, w2, pmt2, x0, w_cast_a, w_cast_b)


def _s5(u, a_re, a_im, log_dt, b_re, b_im, c_re, c_im, d_skip, w_cast_in, w_cast_a, w_cast_b):
    g, p = a_re.shape
    dup = lambda a: jnp.concatenate([a, a], axis=-1)
    ldt = jnp.broadcast_to(log_dt.reshape(g, 1, 1), (g, 1, LANES))
    are2, aim2 = dup(a_re).reshape(g, 1, LANES), dup(a_im).reshape(g, 1, LANES)
    bt_re2 = dup(jnp.swapaxes(b_re, 1, 2))
    bt_im2 = dup(jnp.swapaxes(b_im, 1, 2))
    w2, pmt2, x0, w_in_bf16 = _ssm_in(u, ldt, are2, aim2, bt_re2, bt_im2, dup(c_re), dup(c_im),
                                      d_skip.reshape(-1, 1, LANES), w_cast_in)
    y, w_a_bf16, w_b_bf16 = _ssm_out(u, w2, pmt2, x0, w_cast_a, w_cast_b)
    return y, w_in_bf16, w_a_bf16, w_b_bf16


def _gelu_tanh(x):
    return 0.5 * x * (1.0 + jnp.tanh(math.sqrt(2.0 / math.pi) * (x + 0.044715 * (x * x * x))))


def _mixout_kernel(x_ref, attn_ref, y_ref, wglu_ref, bglu_ref, ga_ref, gs_ref, wo_ref,
                   gt_ref, gf_ref, sc_ref, sh_ref, h_ref, hn_ref):
    aw = attn_ref.shape[1]
    y = _gelu_tanh(y_ref[...])
    gate = jax.nn.sigmoid(_dot(y.astype(BF16), wglu_ref[...]) + bglu_ref[...])
    ns = _rms(y * gate, gs_ref[...]).astype(BF16)
    na = _rms(attn_ref[...].astype(F32), ga_ref[...]).astype(BF16)
    mixed = _dot(na, wo_ref[0:aw, :]) + _dot(ns, wo_ref[aw:, :])
    h = x_ref[...] + gt_ref[...] * mixed
    h_ref[...] = h
    hn_ref[...] = (_rms(h, gf_ref[...]) * (1.0 + sc_ref[...]) + sh_ref[...]).astype(BF16)


def _mixout(x, attn, y, w_glu, b_glu, g_attn, g_ssm, w_out, gt1, g_ffn, sc2, sh2, tm=512):
    s, d = x.shape
    w = attn.shape[1]
    rows = lambda c: pl.BlockSpec((tm, c), lambda i: (i, 0))
    return pl.pallas_call(
        _mixout_kernel,
        grid=(s // tm,),
        in_specs=[rows(d), rows(w), rows(w), _resident((w, w)), _resident((1, w)),
                  _resident((1, w)), _resident((1, w)), _resident(w_out.shape),
                  _resident((1, d)), _resident((1, d)), _resident((1, d)), _resident((1, d))],
        out_specs=[rows(d), rows(d)],
        out_shape=[jax.ShapeDtypeStruct((s, d), F32), jax.ShapeDtypeStruct((s, d), BF16)],
        compiler_params=_cparams(("arbitrary",), 56),
        name="mixer_out",
    )(x, attn, y, w_glu, b_glu, g_attn, g_ssm, w_out, gt1, g_ffn, sc2, sh2)


FFN_HALO = 16


def _ffn_kernel(hn_ref, halo_ref, h_ref, wa_ref, wb_ref, cw_ref, cb_ref, wd_ref, gt_ref, gfin_ref,
                o_ref):
    i, j = pl.program_id(0), pl.program_id(1)

    @pl.when(j == 0)
    def _():
        o_ref[...] = jnp.zeros(o_ref.shape, F32)

    hn = hn_ref[...]
    a_ext = _dot(jnp.concatenate([halo_ref[...], hn], axis=0), wa_ref[...])
    a = a_ext[FFN_HALO:]
    b = _dot(hn, wb_ref[...])
    halo = a_ext[FFN_HALO - 8:FFN_HALO] * (i > 0).astype(F32)
    row = lax.broadcasted_iota(jnp.int32, a.shape, 0)
    prev1 = jnp.where(row == 0, halo[7:8, :], pltpu.roll(a, 1, axis=0))
    prev2 = jnp.where(row == 0, halo[6:7, :],
                      jnp.where(row == 1, halo[7:8, :], pltpu.roll(a, 2, axis=0)))
    cw = cw_ref[...]
    conv = cb_ref[...] + cw[0:1, :] * prev2 + cw[1:2, :] * prev1 + cw[2:3, :] * a
    act = (conv * jax.nn.sigmoid(conv) * b).astype(BF16)
    o_ref[...] += _dot(act, wd_ref[...])

    @pl.when(j == pl.num_programs(1) - 1)
    def _():
        h = h_ref[...] + gt_ref[...] * o_ref[...]
        o_ref[...] = _rms(h, gfin_ref[...])


def _ffn(hn, h, w_up, conv_w, conv_b, w_down, gt2, g_final, tm=1024, tn=512):
    s, d = h.shape
    d_ff = w_down.shape[0]
    nf = d_ff // tn
    halo_blocks = tm // FFN_HALO
    return pl.pallas_call(
        _ffn_kernel,
        grid=(s // tm, nf),
        in_specs=[pl.BlockSpec((tm, d), lambda i, j: (i, 0)),
                  pl.BlockSpec((FFN_HALO, d),
                               lambda i, j: (jnp.maximum(i * halo_blocks - 1, 0), 0)),
                  pl.BlockSpec((tm, d), lambda i, j: (i, 0), pipeline_mode=pl.Buffered(1)),
                  pl.BlockSpec((d, tn), lambda i, j: (0, j)),
                  pl.BlockSpec((d, tn), lambda i, j: (0, nf + j)),
                  pl.BlockSpec((3, tn), lambda i, j: (0, j)),
                  pl.BlockSpec((1, tn), lambda i, j: (0, j)),
                  pl.BlockSpec((tn, d), lambda i, j: (j, 0)),
                  _resident((1, d)), _resident((1, d))],
        out_specs=pl.BlockSpec((tm, d), lambda i, j: (i, 0)),
        out_shape=jax.ShapeDtypeStruct((s, d), F32),
        compiler_params=_cparams(("arbitrary", "arbitrary"), 62),
        name="conv_ffn",
    )(hn, hn, h, w_up, w_up, conv_w, conv_b, w_down, gt2, g_final)


def _layer(h, mod, g_mix, w_in, b_f, a_re, a_im, log_dt, ssm_b_re, ssm_b_im, ssm_c_re, ssm_c_im,
           ssm_d, w_glu, b_glu, g_attn_out, g_ssm_out, w_out, g_ffn, w_up, conv_w, conv_b, w_down):
    s, d = h.shape
    aw = N_HEADS * HEAD_DIM
    sh1, sc1, gt1, sh2, sc2, gt2 = [mod[:, i * d:(i + 1) * d] for i in range(N_MOD)]
    row = lambda a: a.reshape(1, -1)

    q, k, v, u, f, k_norm_sq = _inproj(h, row(g_mix), sc1, sh1, w_in,
                                       w_in[:, 3 * aw + N_HEADS:])

    cum_row, cum_col = _forget_cumsum(f, b_f)
    attn, w_up_bf16 = _attention(q, k, v, cum_col, cum_row.reshape(N_HEADS, 1, s), k_norm_sq,
                                 w_up)

    y, w_down_bf16, w_out_bf16, w_glu_bf16 = _s5(
        u, a_re, a_im, log_dt, ssm_b_re, ssm_b_im, ssm_c_re, ssm_c_im, ssm_d,
        w_down, w_out, w_glu)

    h1, hn2 = _mixout(h, attn, y, w_glu_bf16, row(b_glu), row(g_attn_out),
                      row(g_ssm_out), w_out_bf16, gt1, row(g_ffn), sc2, sh2)
    return hn2, h1, (w_up_bf16, conv_w, row(conv_b), w_down_bf16, gt2)


def kernel(x, c, w_ada, b_ada, g_mix, w_in, b_f, a_re, a_im, log_dt, ssm_b_re, ssm_b_im, ssm_c_re,
           ssm_c_im, ssm_d, w_glu, b_glu, g_attn_out, g_ssm_out, w_out, g_ffn, w_up, conv_w,
           conv_b, w_down, g_final):
    batch, s, d = x.shape
    assert w_ada.shape[0] == 1, "only DEPTH == 1 is supported"
    l = 0
    outs = []
    for bi in range(batch):
        mod = _adaln(c[bi:bi + 1], w_ada[l], b_ada[l])
        hn2, h1, ffn_args = _layer(
            x[bi], mod, g_mix[l], w_in[l].astype(BF16), b_f[l], a_re[l], a_im[l], log_dt[l], ssm_b_re[l],
            ssm_b_im[l], ssm_c_re[l], ssm_c_im[l], ssm_d[l], w_glu[l], b_glu[l],
            g_attn_out[l], g_ssm_out[l], w_out[l], g_ffn[l], w_up[l], conv_w[l], conv_b[l],
            w_down[l])
        outs.append(_ffn(hn2, h1, *ffn_args, g_final.reshape(1, d)))
    return jnp.stack(outs, axis=0)
```

```python
import functools
import math

import jax
import jax.numpy as jnp
from jax import lax
from jax.experimental import pallas as pl
from jax.experimental.pallas import tpu as pltpu

F32 = jnp.float32
BF16 = jnp.bfloat16

EPS = 1e-6
HEAD_DIM = 128
N_HEADS = 8
SSM_GROUP = 16
SSM_STATE = 64
N_MOD = 6
LANES = 128
SSM_T = 16
SSM_TW = SSM_T * SSM_GROUP
SSM_PAIRS = SSM_T // 2
SSM_TILE_G = LANES // SSM_GROUP
LOG2E = 1.4426950408889634
SKIP_LOG2 = 151.0
NORM_SLACK = 1.01

_MIB = 1024 * 1024


def _cparams(semantics, vmem_mib):
    return pltpu.CompilerParams(dimension_semantics=semantics, vmem_limit_bytes=vmem_mib * _MIB)


def _resident(shape):
    return pl.BlockSpec(shape, lambda *_: (0,) * len(shape), pipeline_mode=pl.Buffered(1))


def _dot(a, b):
    return jnp.dot(a, b, preferred_element_type=F32)


def _dot_nt(a, b):
    return lax.dot_general(a, b, (((1,), (1,)), ((), ())), preferred_element_type=F32)


def _rms(x, g):
    return x * lax.rsqrt(jnp.mean(x * x, axis=-1, keepdims=True) + EPS) * g


def _lane_tile(x, reps):
    return jnp.concatenate([x] * reps, axis=1)


def _cast_slab(w, steps):
    rows = w.shape[0] // steps
    assert rows * steps == w.shape[0] and rows % 16 == 0, (w.shape, steps)
    return rows, jax.ShapeDtypeStruct(w.shape, BF16)


def _adaln_kernel(c_ref, w_ref, b_ref, o_ref):
    c = c_ref[...]
    cond = c * jax.nn.sigmoid(c)
    cond8 = jnp.broadcast_to(cond, (8, c.shape[1])).astype(BF16)
    acc = _dot(cond8, w_ref[...].astype(BF16))
    o_ref[...] = acc[0:1, :] + b_ref[...]


def _adaln(c, w, b, tn=1536):
    d, n = w.shape
    return pl.pallas_call(
        _adaln_kernel,
        grid=(n // tn,),
        in_specs=[pl.BlockSpec((1, d), lambda j: (0, 0)),
                  pl.BlockSpec((d, tn), lambda j: (0, j)),
                  pl.BlockSpec((1, tn), lambda j: (0, j))],
        out_specs=pl.BlockSpec((1, tn), lambda j: (0, j)),
        out_shape=jax.ShapeDtypeStruct((1, n), F32),
        compiler_params=_cparams(("arbitrary",), 48),
        name="adaln",
    )(c, w, b.reshape(1, n))


def _inproj_kernel(x_ref, g_ref, sc_ref, sh_ref, wq_ref, wk_ref, wv_ref, wu_ref, wf_ref,
                   q_ref, k_ref, v_ref, u_ref, f_ref, kn_ref, *, q_scale):
    hn = (_rms(x_ref[...], g_ref[...]) * (1.0 + sc_ref[...]) + sh_ref[...]).astype(BF16)
    f_ref[...] = _dot(hn, wf_ref[...])
    q_ref[...] = (_dot(hn, wq_ref[...]) * q_scale).astype(BF16)
    k = _dot(hn, wk_ref[...]).astype(BF16)
    k_ref[...] = k
    u_ref[...] = _dot(hn, wu_ref[...])

    k32 = k.astype(F32)
    ksq = k32 * k32
    tile_max = jnp.concatenate(
        [jnp.broadcast_to(
            jnp.max(jnp.sum(ksq[:, h * HEAD_DIM:(h + 1) * HEAD_DIM], axis=1, keepdims=True),
                    axis=0, keepdims=True), (1, LANES)) for h in range(N_HEADS)], axis=0)

    @pl.when(pl.program_id(0) == 0)
    def _():
        kn_ref[...] = tile_max

    @pl.when(pl.program_id(0) > 0)
    def _():
        kn_ref[...] = jnp.maximum(kn_ref[...], tile_max)

    v = _dot(hn, wv_ref[...]).astype(BF16)
    ones = jnp.ones((v.shape[0], HEAD_DIM), BF16)
    for h in range(N_HEADS):
        v_ref[:, 2 * h * HEAD_DIM:(2 * h + 1) * HEAD_DIM] = v[:, h * HEAD_DIM:(h + 1) * HEAD_DIM]
        v_ref[:, (2 * h + 1) * HEAD_DIM:(2 * h + 2) * HEAD_DIM] = ones


def _inproj(x, g, sc, sh, w_in, wu, tm=512):
    s, d = x.shape
    aw, sw = N_HEADS * HEAD_DIM, wu.shape[1]
    rows = lambda c: pl.BlockSpec((tm, c), lambda i: (i, 0))
    cols = lambda width, blk: pl.BlockSpec((d, width), lambda i: (0, blk),
                                           pipeline_mode=pl.Buffered(1))
    return pl.pallas_call(
        functools.partial(_inproj_kernel, q_scale=HEAD_DIM ** -0.5 * LOG2E),
        grid=(s // tm,),
        in_specs=[rows(d), _resident((1, d)), _resident((1, d)), _resident((1, d)),
                  cols(aw, 0), cols(aw, 1), cols(aw, 2), _resident((d, sw)),
                  cols(LANES, 3 * aw // LANES)],
        out_specs=[rows(aw), rows(aw), rows(2 * aw), rows(sw), rows(LANES),
                   pl.BlockSpec((N_HEADS, LANES), lambda i: (0, 0))],
        out_shape=[jax.ShapeDtypeStruct((s, aw), BF16), jax.ShapeDtypeStruct((s, aw), BF16),
                   jax.ShapeDtypeStruct((s, 2 * aw), BF16), jax.ShapeDtypeStruct((s, sw), F32),
                   jax.ShapeDtypeStruct((s, LANES), F32),
                   jax.ShapeDtypeStruct((N_HEADS, LANES), F32)],
        compiler_params=_cparams(("arbitrary",), 56),
        name="inproj",
    )(x, g, sc, sh, w_in, w_in, w_in, wu, w_in)


def _log2_forget_cumsum(z, axis):
    x = jnp.minimum(z, 0.0) - jnp.log1p(jnp.exp(-jnp.abs(z)))
    pos = lax.broadcasted_iota(jnp.int32, x.shape, axis)
    shift = 1
    while shift < x.shape[axis]:
        x = x + jnp.where(pos >= shift, pltpu.roll(x, shift, axis=axis), 0.0)
        shift *= 2
    return x * LOG2E


def _cum_kernel(ft_ref, bcol_ref, f_ref, brow_ref, row_ref, col_ref):
    row_ref[...] = _log2_forget_cumsum(ft_ref[...] + bcol_ref[...], 1)
    col_ref[...] = _log2_forget_cumsum(f_ref[...] + brow_ref[...], 0)


def _forget_cumsum(f, b_f):
    s, w = f.shape
    h = b_f.shape[0]
    return pl.pallas_call(
        _cum_kernel,
        out_shape=[jax.ShapeDtypeStruct((h, s), F32), jax.ShapeDtypeStruct((s, w), F32)],
        compiler_params=_cparams(None, 48),
        name="forget_cumsum",
    )(f[:, :h].T, b_f.reshape(h, 1), f, jnp.pad(b_f, (0, w - h)).reshape(1, w))


def _attn_kernel(q_ref, k_ref, v_ref, cq_ref, ck_ref, kn_ref, wsrc_ref, o_ref, wdst_ref,
                 m_ref, acc_ref, *, tq, near, splits):
    i = pl.program_id(1)
    wdst_ref[...] = wsrc_ref[...].astype(BF16)
    kmax = jnp.sqrt(kn_ref[pl.ds(pl.program_id(0), 1), :]) * NORM_SLACK

    head_lane = lax.broadcasted_iota(jnp.int32, (tq, LANES), 1) == pl.program_id(0)
    cqb = jnp.broadcast_to(
        jnp.sum(jnp.where(head_lane, cq_ref[...], 0.0), axis=1, keepdims=True), (tq, LANES))

    half = tq // splits
    halves = [slice(hh * half, (hh + 1) * half) for hh in range(splits)]

    base = pl.multiple_of(i * tq, tq)
    qf = q_ref[...].astype(F32)
    own = jnp.sum(qf * k_ref[pl.ds(base, tq), :].astype(F32), axis=1, keepdims=True)
    qn = jnp.sqrt(jnp.sum(qf * qf, axis=1, keepdims=True))
    bound = jnp.max(qn * kmax - (own - cqb))
    ck_all = ck_ref[0]
    pos = lax.broadcasted_iota(jnp.int32, ck_all.shape, 1)
    dead = jnp.where((pos < i * tq) & (ck_all > bound + SKIP_LOG2), 1.0, 0.0)
    j0 = jnp.sum(dead).astype(jnp.int32) // tq

    first_near = jnp.maximum(i - near, 0)
    near_w = near * tq
    near_ks = pl.multiple_of(first_near * tq, tq)
    near_k = k_ref[pl.ds(near_ks, near_w), :]
    near_v = v_ref[pl.ds(near_ks, near_w), :]
    near_key = near_ks + lax.broadcasted_iota(jnp.int32, (1, near_w), 1)
    near_ck = jnp.where(near_key < base, ck_ref[0, :, pl.ds(near_ks, near_w)], jnp.inf)
    t_near = _dot_nt(q_ref[...], near_k) - near_ck
    for hh, rs in enumerate(halves):
        nk = (hh + 1) * half
        qh = q_ref[rs, :]
        t_diag = _dot_nt(qh, k_ref[pl.ds(base, nk), :]) - ck_ref[0, :, pl.ds(base, nk)]
        row = lax.broadcasted_iota(jnp.int32, t_diag.shape, 0)
        col = lax.broadcasted_iota(jnp.int32, t_diag.shape, 1)
        t_diag = jnp.where(col <= row + hh * half, t_diag, -jnp.inf)
        t = jnp.concatenate([t_near[rs], t_diag], axis=1)
        m0 = jnp.max(t, axis=1, keepdims=True) + cqb[rs]
        p = jnp.exp2(t - _lane_tile(m0 - cqb[rs], (near_w + nk) // LANES)).astype(BF16)
        m_ref[rs, :] = m0
        acc = _dot(p[:, :near_w], near_v) + _dot(p[:, near_w:], v_ref[pl.ds(base, nk), :])
        acc_ref[rs, :] = acc
        o_ref[rs, :] = (acc[:, :HEAD_DIM] / acc[:, HEAD_DIM:]).astype(BF16)

    def body(j, carry):
        ks = pl.multiple_of(j * tq, tq)
        kb = k_ref[pl.ds(ks, tq), :]
        vb = v_ref[pl.ds(ks, tq), :]
        t_all = _dot_nt(q_ref[...], kb) - ck_ref[0, :, pl.ds(ks, tq)]
        for rs in halves:
            t = t_all[rs]
            m_prev = m_ref[rs, :]
            m_new = jnp.maximum(m_prev, jnp.max(t, axis=1, keepdims=True) + cqb[rs])
            alpha = jnp.exp2(m_prev - m_new)
            p = jnp.exp2(t - _lane_tile(m_new - cqb[rs], tq // LANES))
            acc_ref[rs, :] = _lane_tile(alpha, 2) * acc_ref[rs, :] + _dot(p.astype(BF16), vb)
            m_ref[rs, :] = m_new
        return carry

    @pl.when(j0 < first_near)
    def _():
        lax.fori_loop(j0, first_near, body, 0)
        acc = acc_ref[...]
        o_ref[...] = (acc[:, :HEAD_DIM] / acc[:, HEAD_DIM:]).astype(BF16)


def _attention(q, k, v_ones, cum_col, cum_row, k_norm_sq, w_cast, tq=512, near=2, splits=2):
    s = q.shape[0]
    nq = s // tq
    slab, w_bf16 = _cast_slab(w_cast, N_HEADS * nq)
    slab_spec = pl.BlockSpec((slab, w_cast.shape[1]), lambda h, i: (h * nq + i, 0))
    return pl.pallas_call(
        functools.partial(_attn_kernel, tq=tq, near=near, splits=splits),
        grid=(N_HEADS, nq),
        in_specs=[pl.BlockSpec((tq, HEAD_DIM), lambda h, i: (i, h)),
                  pl.BlockSpec((s, HEAD_DIM), lambda h, i: (0, h)),
                  pl.BlockSpec((s, 2 * HEAD_DIM), lambda h, i: (0, h)),
                  pl.BlockSpec((tq, LANES), lambda h, i: (i, 0)),
                  pl.BlockSpec((1, 1, s), lambda h, i: (h, 0, 0)),
                  pl.BlockSpec((N_HEADS, LANES), lambda h, i: (0, 0)),
                  slab_spec],
        out_specs=[pl.BlockSpec((tq, HEAD_DIM), lambda h, i: (i, h)), slab_spec],
        out_shape=[jax.ShapeDtypeStruct((s, N_HEADS * HEAD_DIM), BF16), w_bf16],
        scratch_shapes=[pltpu.VMEM((tq, LANES), F32), pltpu.VMEM((tq, 2 * HEAD_DIM), F32)],
        compiler_params=_cparams(("arbitrary", "arbitrary"), 40),
        name="fox_attention",
    )(q, k, v_ones, cum_col, cum_row, k_norm_sq, w_cast)


def _cmul(are, aim, bre, bim):
    return are * bre - aim * bim, are * bim + aim * bre


def _step_pair(u_ref, a, n_chunks):
    x0 = u_ref[pl.ds(2 * a, n_chunks, stride=SSM_T), :]
    x1 = u_ref[pl.ds(2 * a + 1, n_chunks, stride=SSM_T), :]
    return jnp.concatenate([x0, x1], axis=1).astype(BF16)


def _ssm_in_kernel(u_ref, ldt_ref, are_ref, aim_ref, btr_ref, bti_ref, cr_ref, ci_ref, d_ref,
                   wsrc_ref, w2_ref, pmt2_ref, x0_ref, wdst_ref,
                   cp_ref, wl_ref, q2_ref, v_ref):
    wdst_ref[...] = wsrc_ref[...].astype(BF16)
    n_chunks = u_ref.shape[0] // SSM_T
    half_w = SSM_TILE_G * SSM_STATE
    pw_parts = []
    lane = lax.broadcasted_iota(jnp.int32, (SSM_GROUP, LANES), 1)
    first = lane < SSM_STATE

    @pl.when(pl.program_id(0) == 0)
    def _():
        q2_ref[...] = jnp.zeros(q2_ref.shape, BF16)

    pmt2_ref[...] = jnp.zeros(pmt2_ref.shape, BF16)

    def place(ref, lead, step, gi, re_part, im_part):
        r0 = (step % 2) * LANES + gi * SSM_GROUP
        rows = slice(r0, r0 + SSM_GROUP)
        mine = first if gi % 2 == 0 else jnp.logical_not(first)
        c_re = (gi // 2) * LANES
        c_im = half_w + c_re
        ref[lead + (step // 2, rows, slice(c_re, c_re + LANES))] = (
            jnp.where(mine, re_part, 0.0).astype(BF16))
        ref[lead + (step // 2, rows, slice(c_im, c_im + LANES))] = (
            jnp.where(mine, im_part, 0.0).astype(BF16))

    for gi in range(SSM_TILE_G):
        dt = jnp.exp(ldt_ref[gi])
        are, aim = are_ref[gi], aim_ref[gi]
        mag = jnp.exp(dt * are)
        abre, abim = mag * jnp.cos(dt * aim), mag * jnp.sin(dt * aim)
        nre, nim = abre - 1.0, abim
        den = are * are + aim * aim
        zre = (nre * are + nim * aim) / den
        zim = (nim * are - nre * aim) / den
        bbre, bbim = _cmul(zre, zim, btr_ref[gi], bti_ref[gi])
        bbcat = jnp.where(first, bbre, bbim)

        cpre, cpim = cr_ref[gi], ci_ref[gi]
        qre, qim = bbre, bbim
        pwre, pwim = jnp.ones_like(abre), jnp.zeros_like(abim)
        for t in range(SSM_T):
            cp_ref[t * SSM_GROUP:(t + 1) * SSM_GROUP, :] = jnp.where(first, cpre, -cpim)
            if t > 0:
                place(pmt2_ref, (0,), t - 1, gi, cpre, -cpim)
            place(q2_ref, (), SSM_T - 1 - t, gi, qre, qim)
            cpre, cpim = _cmul(cpre, cpim, abre, abim)
            qre, qim = _cmul(qre, qim, abre, abim)
            pwre, pwim = _cmul(pwre, pwim, abre, abim)
        place(pmt2_ref, (0,), SSM_T - 1, gi, cpre, -cpim)
        pw_parts.append((pwre, pwim))

        krow = lax.dot_general(bbcat, cp_ref[...], (((1,), (1,)), ((), ())),
                               preferred_element_type=F32, precision=lax.Precision.HIGHEST)
        own = (lane >= gi * SSM_GROUP) & (lane < (gi + 1) * SSM_GROUP)
        for tau in range(SSM_T):
            half = krow[:, (tau // SSM_TILE_G) * LANES:(tau // SSM_TILE_G + 1) * LANES]
            shift = ((gi - tau % SSM_TILE_G) * SSM_GROUP) % LANES
            moved = half if shift == 0 else pltpu.roll(half, shift, axis=1)
            wl_ref[tau, gi * SSM_GROUP:(gi + 1) * SSM_GROUP, :] = jnp.where(own, moved, 0.0)

    r = lax.broadcasted_iota(jnp.int32, (LANES, LANES), 0)
    c = lax.broadcasted_iota(jnp.int32, (LANES, LANES), 1)
    wl_ref[0] = wl_ref[0] + jnp.where(r == c, d_ref[0], 0.0)
    for dl in range(SSM_PAIRS):
        diag = wl_ref[2 * dl].astype(BF16)
        w2_ref[0, dl, 0:LANES, 0:LANES] = diag
        w2_ref[0, dl, LANES:, LANES:] = diag
        w2_ref[0, dl, 0:LANES, LANES:] = wl_ref[2 * dl + 1].astype(BF16)
        below = jnp.zeros((LANES, LANES), BF16) if dl == 0 else wl_ref[2 * dl - 1].astype(BF16)
        w2_ref[0, dl, LANES:, 0:LANES] = below

    v = _dot(_step_pair(u_ref, 0, n_chunks), q2_ref[0])
    for a in range(1, SSM_PAIRS):
        v = v + _dot(_step_pair(u_ref, a, n_chunks), q2_ref[a])
    v_ref[...] = v

    pair = lambda k, part: jnp.where(first[0:1], pw_parts[2 * k][part], pw_parts[2 * k + 1][part])
    ar = jnp.concatenate([pair(k, 0) for k in range(SSM_TILE_G // 2)], axis=1)
    ai = jnp.concatenate([pair(k, 1) for k in range(SSM_TILE_G // 2)], axis=1)

    def scan_body(c, carry):
        xre, xim = carry
        x0_ref[pl.ds(c, 1), 0:half_w] = xre
        x0_ref[pl.ds(c, 1), half_w:] = xim
        vre = v_ref[pl.ds(c, 1), 0:half_w]
        vim = v_ref[pl.ds(c, 1), half_w:]
        return ar * xre - ai * xim + vre, ar * xim + ai * xre + vim

    zero = jnp.zeros(ar.shape, F32)
    lax.fori_loop(0, n_chunks, scan_body, (zero, zero), unroll=8)


def _ssm_in(u, ldt, are2, aim2, bt_re2, bt_im2, c_re2, c_im2, d_rows, w_cast):
    s, width = u.shape
    n_tiles = width // LANES
    c = s // SSM_T
    kw = SSM_TILE_G * LANES
    per_tile = lambda *shape: pl.BlockSpec((SSM_TILE_G,) + shape,
                                           lambda j: (j,) + (0,) * len(shape))
    slab, w_bf16 = _cast_slab(w_cast, n_tiles)
    slab_spec = pl.BlockSpec((slab, w_cast.shape[1]), lambda j: (j, 0))
    return pl.pallas_call(
        _ssm_in_kernel,
        grid=(n_tiles,),
        in_specs=[pl.BlockSpec((s, LANES), lambda j: (0, j)),
                  per_tile(1, LANES), per_tile(1, LANES), per_tile(1, LANES),
                  per_tile(SSM_GROUP, LANES), per_tile(SSM_GROUP, LANES),
                  per_tile(SSM_GROUP, LANES), per_tile(SSM_GROUP, LANES),
                  pl.BlockSpec((1, 1, LANES), lambda j: (j, 0, 0)), slab_spec],
        out_specs=[pl.BlockSpec((1, SSM_PAIRS, 2 * LANES, 2 * LANES), lambda j: (j, 0, 0, 0)),
                   pl.BlockSpec((1, SSM_PAIRS, 2 * LANES, kw), lambda j: (j, 0, 0, 0)),
                   pl.BlockSpec((c, kw), lambda j: (0, j)), slab_spec],
        out_shape=[jax.ShapeDtypeStruct((n_tiles, SSM_PAIRS, 2 * LANES, 2 * LANES), BF16),
                   jax.ShapeDtypeStruct((n_tiles, SSM_PAIRS, 2 * LANES, kw), BF16),
                   jax.ShapeDtypeStruct((c, n_tiles * kw), F32), w_bf16],
        scratch_shapes=[pltpu.VMEM((SSM_TW, LANES), F32),
                        pltpu.VMEM((SSM_T, LANES, LANES), F32),
                        pltpu.VMEM((SSM_PAIRS, 2 * LANES, kw), BF16),
                        pltpu.VMEM((c, kw), F32)],
        compiler_params=_cparams(("arbitrary",), 60),
        name="ssm_state_in",
    )(u, ldt, are2, aim2, bt_re2, bt_im2, c_re2, c_im2, d_rows, w_cast)


def _ssm_out_kernel(u_ref, w2_ref, pmt2_ref, x0_ref, wsrc_a_ref, wsrc_b_ref,
                    y_ref, wdst_a_ref, wdst_b_ref):
    wdst_a_ref[...] = wsrc_a_ref[...].astype(BF16)
    wdst_b_ref[...] = wsrc_b_ref[...].astype(BF16)
    n_chunks = u_ref.shape[0] // SSM_T
    x0 = x0_ref[...].astype(BF16)
    pairs = [_step_pair(u_ref, a, n_chunks) for a in range(SSM_PAIRS)]
    for b in range(SSM_PAIRS):
        acc = _dot_nt(x0, pmt2_ref[0, b])
        for a in range(b + 1):
            acc = acc + _dot(pairs[a], w2_ref[0, b - a])
        y_ref[pl.ds(2 * b, n_chunks, stride=SSM_T), :] = acc[:, :LANES]
        y_ref[pl.ds(2 * b + 1, n_chunks, stride=SSM_T), :] = acc[:, LANES:]


def _ssm_out(u, w2, pmt2, x0, w_cast_a, w_cast_b):
    s, width = u.shape
    n_tiles = width // LANES
    c = s // SSM_T
    kw = SSM_TILE_G * LANES
    slab_a, a_bf16 = _cast_slab(w_cast_a, n_tiles)
    slab_b, b_bf16 = _cast_slab(w_cast_b, n_tiles)
    spec_a = pl.BlockSpec((slab_a, w_cast_a.shape[1]), lambda j: (j, 0))
    spec_b = pl.BlockSpec((slab_b, w_cast_b.shape[1]), lambda j: (j, 0))
    return pl.pallas_call(
        _ssm_out_kernel,
        grid=(n_tiles,),
        in_specs=[pl.BlockSpec((s, LANES), lambda j: (0, j)),
                  pl.BlockSpec((1, SSM_PAIRS, 2 * LANES, 2 * LANES), lambda j: (j, 0, 0, 0)),
                  pl.BlockSpec((1, SSM_PAIRS, 2 * LANES, kw), lambda j: (j, 0, 0, 0)),
                  pl.BlockSpec((c, kw), lambda j: (0, j)), spec_a, spec_b],
        out_specs=[pl.BlockSpec((s, LANES), lambda j: (0, j)), spec_a, spec_b],
        out_shape=[jax.ShapeDtypeStruct((s, width), F32), a_bf16, b_bf16],
        compiler_params=_cparams(("arbitrary",), 48),
        name="ssm_out",
    )(u, w2, pmt2, x0, w_cast_a, w_cast_b)


def _s5(u, a_re, a_im, log_dt, b_re, b_im, c_re, c_im, d_skip, w_cast_in, w_cast_a, w_cast_b):
    g, p = a_re.shape
    dup = lambda a: jnp.concatenate([a, a], axis=-1)
    ldt = jnp.broadcast_to(log_dt.reshape(g, 1, 1), (g, 1, LANES))
    are2, aim2 = dup(a_re).reshape(g, 1, LANES), dup(a_im).reshape(g, 1, LANES)
    bt_re2 = dup(jnp.swapaxes(b_re, 1, 2))
    bt_im2 = dup(jnp.swapaxes(b_im, 1, 2))
    w2, pmt2, x0, w_in_bf16 = _ssm_in(u, ldt, are2, aim2, bt_re2, bt_im2, dup(c_re), dup(c_im),
                                      d_skip.reshape(-1, 1, LANES), w_cast_in)
    y, w_a_bf16, w_b_bf16 = _ssm_out(u, w2, pmt2, x0, w_cast_a, w_cast_b)
    return y, w_in_bf16, w_a_bf16, w_b_bf16


def _gelu_tanh(x):
    return 0.5 * x * (1.0 + jnp.tanh(math.sqrt(2.0 / math.pi) * (x + 0.044715 * (x * x * x))))


def _mixout_kernel(x_ref, attn_ref, y_ref, wglu_ref, bglu_ref, ga_ref, gs_ref, wo_ref,
                   gt_ref, gf_ref, sc_ref, sh_ref, h_ref, hn_ref):
    aw = attn_ref.shape[1]
    y = _gelu_tanh(y_ref[...])
    gate = jax.nn.sigmoid(_dot(y.astype(BF16), wglu_ref[...]) + bglu_ref[...])
    ns = _rms(y * gate, gs_ref[...]).astype(BF16)
    na = _rms(attn_ref[...].astype(F32), ga_ref[...]).astype(BF16)
    mixed = _dot(na, wo_ref[0:aw, :]) + _dot(ns, wo_ref[aw:, :])
    h = x_ref[...] + gt_ref[...] * mixed
    h_ref[...] = h
    hn_ref[...] = (_rms(h, gf_ref[...]) * (1.0 + sc_ref[...]) + sh_ref[...]).astype(BF16)


def _mixout(x, attn, y, w_glu, b_glu, g_attn, g_ssm, w_out, gt1, g_ffn, sc2, sh2, tm=512):
    s, d = x.shape
    w = attn.shape[1]
    rows = lambda c: pl.BlockSpec((tm, c), lambda i: (i, 0))
    return pl.pallas_call(
        _mixout_kernel,
        grid=(s // tm,),
        in_specs=[rows(d), rows(w), rows(w), _resident((w, w)), _resident((1, w)),
                  _resident((1, w)), _resident((1, w)), _resident(w_out.shape),
                  _resident((1, d)), _resident((1, d)), _resident((1, d)), _resident((1, d))],
        out_specs=[rows(d), rows(d)],
        out_shape=[jax.ShapeDtypeStruct((s, d), F32), jax.ShapeDtypeStruct((s, d), BF16)],
        compiler_params=_cparams(("arbitrary",), 56),
        name="mixer_out",
    )(x, attn, y, w_glu, b_glu, g_attn, g_ssm, w_out, gt1, g_ffn, sc2, sh2)


FFN_HALO = 16


def _ffn_kernel(hn_ref, halo_ref, h_hbm, wa_ref, wb_ref, cw_ref, cb_ref, wd_ref, gt_ref, gfin_ref,
                o_ref, h_buf, h_sem):
    i, j = pl.program_id(0), pl.program_id(1)
    tm = o_ref.shape[0]

    def residual_copy():
        return pltpu.make_async_copy(h_hbm.at[pl.ds(i * tm, tm), :], h_buf, h_sem)

    @pl.when(j == 0)
    def _():
        residual_copy().start()
        o_ref[...] = jnp.zeros(o_ref.shape, F32)

    hn = hn_ref[...]
    a_ext = _dot(jnp.concatenate([halo_ref[...], hn], axis=0), wa_ref[...])
    a = a_ext[FFN_HALO:]
    b = _dot(hn, wb_ref[...])
    halo = a_ext[FFN_HALO - 8:FFN_HALO] * (i > 0).astype(F32)
    row = lax.broadcasted_iota(jnp.int32, a.shape, 0)
    prev1 = jnp.where(row == 0, halo[7:8, :], pltpu.roll(a, 1, axis=0))
    prev2 = jnp.where(row == 0, halo[6:7, :],
                      jnp.where(row == 1, halo[7:8, :], pltpu.roll(a, 2, axis=0)))
    cw = cw_ref[...]
    conv = cb_ref[...] + cw[0:1, :] * prev2 + cw[1:2, :] * prev1 + cw[2:3, :] * a
    act = (conv * jax.nn.sigmoid(conv) * b).astype(BF16)
    o_ref[...] += _dot(act, wd_ref[...])

    @pl.when(j == pl.num_programs(1) - 1)
    def _():
        residual_copy().wait()
        h = h_buf[...] + gt_ref[...] * o_ref[...]
        o_ref[...] = _rms(h, gfin_ref[...])


def _ffn(hn, h, w_up, conv_w, conv_b, w_down, gt2, g_final, tm=1024, tn=512):
    s, d = h.shape
    d_ff = w_down.shape[0]
    nf = d_ff // tn
    halo_blocks = tm // FFN_HALO
    return pl.pallas_call(
        _ffn_kernel,
        grid=(s // tm, nf),
        in_specs=[pl.BlockSpec((tm, d), lambda i, j: (i, 0)),
                  pl.BlockSpec((FFN_HALO, d),
                               lambda i, j: (jnp.maximum(i * halo_blocks - 1, 0), 0)),
                  pl.BlockSpec(memory_space=pl.ANY),
                  pl.BlockSpec((d, tn), lambda i, j: (0, j)),
                  pl.BlockSpec((d, tn), lambda i, j: (0, nf + j)),
                  pl.BlockSpec((3, tn), lambda i, j: (0, j)),
                  pl.BlockSpec((1, tn), lambda i, j: (0, j)),
                  pl.BlockSpec((tn, d), lambda i, j: (j, 0)),
                  _resident((1, d)), _resident((1, d))],
        out_specs=pl.BlockSpec((tm, d), lambda i, j: (i, 0)),
        out_shape=jax.ShapeDtypeStruct((s, d), F32),
        scratch_shapes=[pltpu.VMEM((tm, d), F32), pltpu.SemaphoreType.DMA(())],
        compiler_params=_cparams(("arbitrary", "arbitrary"), 62),
        name="conv_ffn",
    )(hn, hn, h, w_up, w_up, conv_w, conv_b, w_down, gt2, g_final)


def _layer(h, mod, g_mix, w_in, b_f, a_re, a_im, log_dt, ssm_b_re, ssm_b_im, ssm_c_re, ssm_c_im,
           ssm_d, w_glu, b_glu, g_attn_out, g_ssm_out, w_out, g_ffn, w_up, conv_w, conv_b, w_down):
    s, d = h.shape
    aw = N_HEADS * HEAD_DIM
    sh1, sc1, gt1, sh2, sc2, gt2 = [mod[:, i * d:(i + 1) * d] for i in range(N_MOD)]
    row = lambda a: a.reshape(1, -1)

    q, k, v, u, f, k_norm_sq = _inproj(h, row(g_mix), sc1, sh1, w_in,
                                       w_in[:, 3 * aw + N_HEADS:])

    cum_row, cum_col = _forget_cumsum(f, b_f)
    attn, w_up_bf16 = _attention(q, k, v, cum_col, cum_row.reshape(N_HEADS, 1, s), k_norm_sq,
                                 w_up)

    y, w_down_bf16, w_out_bf16, w_glu_bf16 = _s5(
        u, a_re, a_im, log_dt, ssm_b_re, ssm_b_im, ssm_c_re, ssm_c_im, ssm_d,
        w_down, w_out, w_glu)

    h1, hn2 = _mixout(h, attn, y, w_glu_bf16, row(b_glu), row(g_attn_out),
                      row(g_ssm_out), w_out_bf16, gt1, row(g_ffn), sc2, sh2)
    return hn2, h1, (w_up_bf16, conv_w, row(conv_b), w_down_bf16, gt2)


def kernel(x, c, w_ada, b_ada, g_mix, w_in, b_f, a_re, a_im, log_dt, ssm_b_re, ssm_b_im, ssm_c_re,
           ssm_c_im, ssm_d, w_glu, b_glu, g_attn_out, g_ssm_out, w_out, g_ffn, w_up, conv_w,
           conv_b, w_down, g_final):
    batch, s, d = x.shape
    assert w_ada.shape[0] == 1, "only DEPTH == 1 is supported"
    l = 0
    outs = []
    for bi in range(batch):
        mod = _adaln(c[bi:bi + 1], w_ada[l], b_ada[l])
        hn2, h1, ffn_args = _layer(
            x[bi], mod, g_mix[l], w_in[l].astype(BF16), b_f[l], a_re[l], a_im[l], log_dt[l], ssm_b_re[l],
            ssm_b_im[l], ssm_c_re[l], ssm_c_im[l], ssm_d[l], w_glu[l], b_glu[l],
            g_attn_out[l], g_ssm_out[l], w_out[l], g_ffn[l], w_up[l], conv_w[l], conv_b[l],
            w_down[l])
        outs.append(_ffn(hn2, h1, *ffn_args, g_final.reshape(1, d)))
    return jnp.stack(outs, axis=0)
```

```python
import functools
import math

import jax
import jax.numpy as jnp
from jax import lax
from jax.experimental import pallas as pl
from jax.experimental.pallas import tpu as pltpu

F32 = jnp.float32
BF16 = jnp.bfloat16

EPS = 1e-6
HEAD_DIM = 128
N_HEADS = 8
SSM_GROUP = 16
SSM_STATE = 64
N_MOD = 6
LANES = 128
SSM_T = 16
SSM_TW = SSM_T * SSM_GROUP
SSM_PAIRS = SSM_T // 2
SSM_TILE_G = LANES // SSM_GROUP
LOG2E = 1.4426950408889634
SKIP_LOG2 = 151.0
NORM_SLACK = 1.01

_MIB = 1024 * 1024


def _cparams(semantics, vmem_mib):
    return pltpu.CompilerParams(dimension_semantics=semantics, vmem_limit_bytes=vmem_mib * _MIB)


def _resident(shape):
    return pl.BlockSpec(shape, lambda *_: (0,) * len(shape), pipeline_mode=pl.Buffered(1))


def _dot(a, b):
    return jnp.dot(a, b, preferred_element_type=F32)


def _dot_nt(a, b):
    return lax.dot_general(a, b, (((1,), (1,)), ((), ())), preferred_element_type=F32)


def _rms(x, g):
    return x * lax.rsqrt(jnp.mean(x * x, axis=-1, keepdims=True) + EPS) * g


def _lane_tile(x, reps):
    return jnp.concatenate([x] * reps, axis=1)


def _cast_slab(w, steps):
    rows = w.shape[0] // steps
    assert rows * steps == w.shape[0] and rows % 16 == 0, (w.shape, steps)
    return rows, jax.ShapeDtypeStruct(w.shape, BF16)


def _adaln_kernel(c_ref, w_ref, b_ref, o_ref):
    c = c_ref[...]
    cond = c * jax.nn.sigmoid(c)
    cond8 = jnp.broadcast_to(cond, (8, c.shape[1])).astype(BF16)
    acc = _dot(cond8, w_ref[...].astype(BF16))
    o_ref[...] = acc[0:1, :] + b_ref[...]


def _adaln(c, w, b, tn=1536):
    d, n = w.shape
    return pl.pallas_call(
        _adaln_kernel,
        grid=(n // tn,),
        in_specs=[pl.BlockSpec((1, d), lambda j: (0, 0)),
                  pl.BlockSpec((d, tn), lambda j: (0, j)),
                  pl.BlockSpec((1, tn), lambda j: (0, j))],
        out_specs=pl.BlockSpec((1, tn), lambda j: (0, j)),
        out_shape=jax.ShapeDtypeStruct((1, n), F32),
        compiler_params=_cparams(("arbitrary",), 48),
        name="adaln",
    )(c, w, b.reshape(1, n))


def _inproj_kernel(x_ref, g_ref, sc_ref, sh_ref, wq_ref, wk_ref, wv_ref, wu_ref, wf_ref,
                   q_ref, k_ref, v_ref, u_ref, f_ref, kn_ref, *, q_scale):
    hn = (_rms(x_ref[...], g_ref[...]) * (1.0 + sc_ref[...]) + sh_ref[...]).astype(BF16)
    f_ref[...] = _dot(hn, wf_ref[...])
    q_ref[...] = (_dot(hn, wq_ref[...]) * q_scale).astype(BF16)
    k = _dot(hn, wk_ref[...]).astype(BF16)
    k_ref[...] = k
    u_ref[...] = _dot(hn, wu_ref[...])

    k32 = k.astype(F32)
    ksq = k32 * k32
    tile_max = jnp.concatenate(
        [jnp.broadcast_to(
            jnp.max(jnp.sum(ksq[:, h * HEAD_DIM:(h + 1) * HEAD_DIM], axis=1, keepdims=True),
                    axis=0, keepdims=True), (1, LANES)) for h in range(N_HEADS)], axis=0)

    @pl.when(pl.program_id(0) == 0)
    def _():
        kn_ref[...] = tile_max

    @pl.when(pl.program_id(0) > 0)
    def _():
        kn_ref[...] = jnp.maximum(kn_ref[...], tile_max)

    v = _dot(hn, wv_ref[...]).astype(BF16)
    ones = jnp.ones((v.shape[0], HEAD_DIM), BF16)
    for h in range(N_HEADS):
        v_ref[:, 2 * h * HEAD_DIM:(2 * h + 1) * HEAD_DIM] = v[:, h * HEAD_DIM:(h + 1) * HEAD_DIM]
        v_ref[:, (2 * h + 1) * HEAD_DIM:(2 * h + 2) * HEAD_DIM] = ones


def _inproj(x, g, sc, sh, w_in, wu, tm=512):
    s, d = x.shape
    aw, sw = N_HEADS * HEAD_DIM, wu.shape[1]
    rows = lambda c: pl.BlockSpec((tm, c), lambda i: (i, 0))
    cols = lambda width, blk: pl.BlockSpec((d, width), lambda i: (0, blk),
                                           pipeline_mode=pl.Buffered(1))
    return pl.pallas_call(
        functools.partial(_inproj_kernel, q_scale=HEAD_DIM ** -0.5 * LOG2E),
        grid=(s // tm,),
        in_specs=[rows(d), _resident((1, d)), _resident((1, d)), _resident((1, d)),
                  cols(aw, 0), cols(aw, 1), cols(aw, 2), _resident((d, sw)),
                  cols(LANES, 3 * aw // LANES)],
        out_specs=[rows(aw), rows(aw), rows(2 * aw), rows(sw), rows(LANES),
                   pl.BlockSpec((N_HEADS, LANES), lambda i: (0, 0))],
        out_shape=[jax.ShapeDtypeStruct((s, aw), BF16), jax.ShapeDtypeStruct((s, aw), BF16),
                   jax.ShapeDtypeStruct((s, 2 * aw), BF16), jax.ShapeDtypeStruct((s, sw), F32),
                   jax.ShapeDtypeStruct((s, LANES), F32),
                   jax.ShapeDtypeStruct((N_HEADS, LANES), F32)],
        compiler_params=_cparams(("arbitrary",), 56),
        name="inproj",
    )(x, g, sc, sh, w_in, w_in, w_in, wu, w_in)


def _log2_forget_cumsum(z, axis):
    x = jnp.minimum(z, 0.0) - jnp.log1p(jnp.exp(-jnp.abs(z)))
    pos = lax.broadcasted_iota(jnp.int32, x.shape, axis)
    shift = 1
    while shift < x.shape[axis]:
        x = x + jnp.where(pos >= shift, pltpu.roll(x, shift, axis=axis), 0.0)
        shift *= 2
    return x * LOG2E


def _cum_kernel(ft_ref, bcol_ref, f_ref, brow_ref, row_ref, col_ref):
    row_ref[...] = _log2_forget_cumsum(ft_ref[...] + bcol_ref[...], 1)
    col_ref[...] = _log2_forget_cumsum(f_ref[...] + brow_ref[...], 0)


def _forget_cumsum(f, b_f):
    s, w = f.shape
    h = b_f.shape[0]
    return pl.pallas_call(
        _cum_kernel,
        out_shape=[jax.ShapeDtypeStruct((h, s), F32), jax.ShapeDtypeStruct((s, w), F32)],
        compiler_params=_cparams(None, 48),
        name="forget_cumsum",
    )(f[:, :h].T, b_f.reshape(h, 1), f, jnp.pad(b_f, (0, w - h)).reshape(1, w))


def _attn_kernel(q_ref, k_ref, v_ref, cq_ref, ck_ref, kn_ref, wsrc_ref, o_ref, wdst_ref,
                 m_ref, acc_ref, *, tq, near, splits):
    i = pl.program_id(1)
    wdst_ref[...] = wsrc_ref[...].astype(BF16)
    kmax = jnp.sqrt(kn_ref[pl.ds(pl.program_id(0), 1), :]) * NORM_SLACK

    head_lane = lax.broadcasted_iota(jnp.int32, (tq, LANES), 1) == pl.program_id(0)
    cqb = jnp.broadcast_to(
        jnp.sum(jnp.where(head_lane, cq_ref[...], 0.0), axis=1, keepdims=True), (tq, LANES))

    half = tq // splits
    halves = [slice(hh * half, (hh + 1) * half) for hh in range(splits)]

    base = pl.multiple_of(i * tq, tq)
    qf = q_ref[...].astype(F32)
    own = jnp.sum(qf * k_ref[pl.ds(base, tq), :].astype(F32), axis=1, keepdims=True)
    qn = jnp.sqrt(jnp.sum(qf * qf, axis=1, keepdims=True))
    bound = jnp.max(qn * kmax - (own - cqb))
    ck_all = ck_ref[0]
    pos = lax.broadcasted_iota(jnp.int32, ck_all.shape, 1)
    dead = jnp.where((pos < i * tq) & (ck_all > bound + SKIP_LOG2), 1.0, 0.0)
    j0 = jnp.sum(dead).astype(jnp.int32) // tq

    first_near = jnp.maximum(i - near, 0)
    near_w = near * tq
    near_ks = pl.multiple_of(first_near * tq, tq)
    near_k = k_ref[pl.ds(near_ks, near_w), :]
    near_v = v_ref[pl.ds(near_ks, near_w), :]
    near_key = near_ks + lax.broadcasted_iota(jnp.int32, (1, near_w), 1)
    near_ck = jnp.where(near_key < base, ck_ref[0, :, pl.ds(near_ks, near_w)], jnp.inf)
    t_near = _dot_nt(q_ref[...], near_k) - near_ck
    for hh, rs in enumerate(halves):
        nk = (hh + 1) * half
        qh = q_ref[rs, :]
        t_diag = _dot_nt(qh, k_ref[pl.ds(base, nk), :]) - ck_ref[0, :, pl.ds(base, nk)]
        row = lax.broadcasted_iota(jnp.int32, t_diag.shape, 0)
        col = lax.broadcasted_iota(jnp.int32, t_diag.shape, 1)
        t_diag = jnp.where(col <= row + hh * half, t_diag, -jnp.inf)
        t = jnp.concatenate([t_near[rs], t_diag], axis=1)
        m0 = jnp.max(t, axis=1, keepdims=True) + cqb[rs]
        p = jnp.exp2(t - _lane_tile(m0 - cqb[rs], (near_w + nk) // LANES)).astype(BF16)
        m_ref[rs, :] = m0
        acc = _dot(p[:, :near_w], near_v) + _dot(p[:, near_w:], v_ref[pl.ds(base, nk), :])
        acc_ref[rs, :] = acc
        o_ref[rs, :] = (acc[:, :HEAD_DIM] / acc[:, HEAD_DIM:]).astype(BF16)

    def body(j, carry):
        ks = pl.multiple_of(j * tq, tq)
        kb = k_ref[pl.ds(ks, tq), :]
        vb = v_ref[pl.ds(ks, tq), :]
        t_all = _dot_nt(q_ref[...], kb) - ck_ref[0, :, pl.ds(ks, tq)]
        for rs in halves:
            t = t_all[rs]
            m_prev = m_ref[rs, :]
            m_new = jnp.maximum(m_prev, jnp.max(t, axis=1, keepdims=True) + cqb[rs])
            alpha = jnp.exp2(m_prev - m_new)
            p = jnp.exp2(t - _lane_tile(m_new - cqb[rs], tq // LANES))
            acc_ref[rs, :] = _lane_tile(alpha, 2) * acc_ref[rs, :] + _dot(p.astype(BF16), vb)
            m_ref[rs, :] = m_new
        return carry

    @pl.when(j0 < first_near)
    def _():
        lax.fori_loop(j0, first_near, body, 0)
        acc = acc_ref[...]
        o_ref[...] = (acc[:, :HEAD_DIM] / acc[:, HEAD_DIM:]).astype(BF16)


def _attention(q, k, v_ones, cum_col, cum_row, k_norm_sq, w_cast, tq=512, near=2, splits=2):
    s = q.shape[0]
    nq = s // tq
    slab, w_bf16 = _cast_slab(w_cast, N_HEADS * nq)
    slab_spec = pl.BlockSpec((slab, w_cast.shape[1]), lambda h, i: (h * nq + i, 0))
    return pl.pallas_call(
        functools.partial(_attn_kernel, tq=tq, near=near, splits=splits),
        grid=(N_HEADS, nq),
        in_specs=[pl.BlockSpec((tq, HEAD_DIM), lambda h, i: (i, h)),
                  pl.BlockSpec((s, HEAD_DIM), lambda h, i: (0, h)),
                  pl.BlockSpec((s, 2 * HEAD_DIM), lambda h, i: (0, h)),
                  pl.BlockSpec((tq, LANES), lambda h, i: (i, 0)),
                  pl.BlockSpec((1, 1, s), lambda h, i: (h, 0, 0)),
                  pl.BlockSpec((N_HEADS, LANES), lambda h, i: (0, 0)),
                  slab_spec],
        out_specs=[pl.BlockSpec((tq, HEAD_DIM), lambda h, i: (i, h)), slab_spec],
        out_shape=[jax.ShapeDtypeStruct((s, N_HEADS * HEAD_DIM), BF16), w_bf16],
        scratch_shapes=[pltpu.VMEM((tq, LANES), F32), pltpu.VMEM((tq, 2 * HEAD_DIM), F32)],
        compiler_params=_cparams(("arbitrary", "arbitrary"), 40),
        name="fox_attention",
    )(q, k, v_ones, cum_col, cum_row, k_norm_sq, w_cast)


def _cmul(are, aim, bre, bim):
    return are * bre - aim * bim, are * bim + aim * bre


def _step_pair(u_ref, a, n_chunks):
    x0 = u_ref[pl.ds(2 * a, n_chunks, stride=SSM_T), :]
    x1 = u_ref[pl.ds(2 * a + 1, n_chunks, stride=SSM_T), :]
    return jnp.concatenate([x0, x1], axis=1).astype(BF16)


def _ssm_in_kernel(u_ref, ldt_ref, are_ref, aim_ref, btr_ref, bti_ref, cr_ref, ci_ref, d_ref,
                   wsrc_ref, w2_ref, pmt2_ref, x0_ref, wdst_ref,
                   cp_ref, wl_ref, q2_ref, v_ref, w_ref, e_ref):
    wdst_ref[...] = wsrc_ref[...].astype(BF16)
    n_chunks = u_ref.shape[0] // SSM_T
    half_w = SSM_TILE_G * SSM_STATE
    pw_parts = []
    lane = lax.broadcasted_iota(jnp.int32, (SSM_GROUP, LANES), 1)
    first = lane < SSM_STATE

    @pl.when(pl.program_id(0) == 0)
    def _():
        q2_ref[...] = jnp.zeros(q2_ref.shape, BF16)

    pmt2_ref[...] = jnp.zeros(pmt2_ref.shape, BF16)

    def place(ref, lead, step, gi, re_part, im_part):
        r0 = (step % 2) * LANES + gi * SSM_GROUP
        rows = slice(r0, r0 + SSM_GROUP)
        mine = first if gi % 2 == 0 else jnp.logical_not(first)
        c_re = (gi // 2) * LANES
        c_im = half_w + c_re
        ref[lead + (step // 2, rows, slice(c_re, c_re + LANES))] = (
            jnp.where(mine, re_part, 0.0).astype(BF16))
        ref[lead + (step // 2, rows, slice(c_im, c_im + LANES))] = (
            jnp.where(mine, im_part, 0.0).astype(BF16))

    for gi in range(SSM_TILE_G):
        dt = jnp.exp(ldt_ref[gi])
        are, aim = are_ref[gi], aim_ref[gi]
        mag = jnp.exp(dt * are)
        abre, abim = mag * jnp.cos(dt * aim), mag * jnp.sin(dt * aim)
        nre, nim = abre - 1.0, abim
        den = are * are + aim * aim
        zre = (nre * are + nim * aim) / den
        zim = (nim * are - nre * aim) / den
        bbre, bbim = _cmul(zre, zim, btr_ref[gi], bti_ref[gi])
        bbcat = jnp.where(first, bbre, bbim)

        cpre, cpim = cr_ref[gi], ci_ref[gi]
        qre, qim = bbre, bbim
        pwre, pwim = jnp.ones_like(abre), jnp.zeros_like(abim)
        for t in range(SSM_T):
            cp_ref[t * SSM_GROUP:(t + 1) * SSM_GROUP, :] = jnp.where(first, cpre, -cpim)
            if t > 0:
                place(pmt2_ref, (0,), t - 1, gi, cpre, -cpim)
            place(q2_ref, (), SSM_T - 1 - t, gi, qre, qim)
            cpre, cpim = _cmul(cpre, cpim, abre, abim)
            qre, qim = _cmul(qre, qim, abre, abim)
            pwre, pwim = _cmul(pwre, pwim, abre, abim)
        place(pmt2_ref, (0,), SSM_T - 1, gi, cpre, -cpim)
        pw_parts.append((pwre, pwim))

        krow = lax.dot_general(bbcat, cp_ref[...], (((1,), (1,)), ((), ())),
                               preferred_element_type=F32, precision=lax.Precision.HIGHEST)
        own = (lane >= gi * SSM_GROUP) & (lane < (gi + 1) * SSM_GROUP)
        for tau in range(SSM_T):
            half = krow[:, (tau // SSM_TILE_G) * LANES:(tau // SSM_TILE_G + 1) * LANES]
            shift = ((gi - tau % SSM_TILE_G) * SSM_GROUP) % LANES
            moved = half if shift == 0 else pltpu.roll(half, shift, axis=1)
            wl_ref[tau, gi * SSM_GROUP:(gi + 1) * SSM_GROUP, :] = jnp.where(own, moved, 0.0)

    r = lax.broadcasted_iota(jnp.int32, (LANES, LANES), 0)
    c = lax.broadcasted_iota(jnp.int32, (LANES, LANES), 1)
    wl_ref[0] = wl_ref[0] + jnp.where(r == c, d_ref[0], 0.0)
    for dl in range(SSM_PAIRS):
        diag = wl_ref[2 * dl].astype(BF16)
        w2_ref[0, dl, 0:LANES, 0:LANES] = diag
        w2_ref[0, dl, LANES:, LANES:] = diag
        w2_ref[0, dl, 0:LANES, LANES:] = wl_ref[2 * dl + 1].astype(BF16)
        below = jnp.zeros((LANES, LANES), BF16) if dl == 0 else wl_ref[2 * dl - 1].astype(BF16)
        w2_ref[0, dl, LANES:, 0:LANES] = below

    v = _dot(_step_pair(u_ref, 0, n_chunks), q2_ref[0])
    for a in range(1, SSM_PAIRS):
        v = v + _dot(_step_pair(u_ref, a, n_chunks), q2_ref[a])
    n_re = half_w // LANES
    for k in range(2 * n_re):
        v_ref[k] = v[:, k * LANES:(k + 1) * LANES]

    pair = lambda k, part: jnp.where(first[0:1], pw_parts[2 * k][part], pw_parts[2 * k + 1][part])
    ars = [pair(k, 0) for k in range(n_re)]
    ais = [pair(k, 1) for k in range(n_re)]
    n_half = n_chunks // 2
    even, odd = pl.ds(0, n_half, stride=2), pl.ds(1, n_half, stride=2)
    for k in range(n_re):
        ve_re, ve_im = v_ref[k, even, :], v_ref[n_re + k, even, :]
        w_ref[k] = ars[k] * ve_re - ais[k] * ve_im + v_ref[k, odd, :]
        w_ref[n_re + k] = ars[k] * ve_im + ais[k] * ve_re + v_ref[n_re + k, odd, :]
    ar = jnp.concatenate(ars, axis=1)
    ai = jnp.concatenate(ais, axis=1)
    ar2, ai2 = ar * ar - ai * ai, 2.0 * ar * ai

    def scan_body(c, carry):
        xre, xim = carry
        for k in range(n_re):
            e_ref[k, pl.ds(c, 1), :] = xre[:, k * LANES:(k + 1) * LANES]
            e_ref[n_re + k, pl.ds(c, 1), :] = xim[:, k * LANES:(k + 1) * LANES]
        wre = jnp.concatenate([w_ref[k, pl.ds(c, 1), :] for k in range(n_re)], axis=1)
        wim = jnp.concatenate([w_ref[n_re + k, pl.ds(c, 1), :] for k in range(n_re)], axis=1)
        return ar2 * xre - ai2 * xim + wre, ar2 * xim + ai2 * xre + wim

    zero = jnp.zeros(ar.shape, F32)
    lax.fori_loop(0, n_half, scan_body, (zero, zero), unroll=8)
    for k in range(n_re):
        e_re, e_im = e_ref[k], e_ref[n_re + k]
        x0_ref[k, even, :] = e_re
        x0_ref[n_re + k, even, :] = e_im
        x0_ref[k, odd, :] = ars[k] * e_re - ais[k] * e_im + v_ref[k, even, :]
        x0_ref[n_re + k, odd, :] = ars[k] * e_im + ais[k] * e_re + v_ref[n_re + k, even, :]


def _ssm_in(u, ldt, are2, aim2, bt_re2, bt_im2, c_re2, c_im2, d_rows, w_cast):
    s, width = u.shape
    n_tiles = width // LANES
    c = s // SSM_T
    kw = SSM_TILE_G * LANES
    per_tile = lambda *shape: pl.BlockSpec((SSM_TILE_G,) + shape,
                                           lambda j: (j,) + (0,) * len(shape))
    slab, w_bf16 = _cast_slab(w_cast, n_tiles)
    slab_spec = pl.BlockSpec((slab, w_cast.shape[1]), lambda j: (j, 0))
    return pl.pallas_call(
        _ssm_in_kernel,
        grid=(n_tiles,),
        in_specs=[pl.BlockSpec((s, LANES), lambda j: (0, j)),
                  per_tile(1, LANES), per_tile(1, LANES), per_tile(1, LANES),
                  per_tile(SSM_GROUP, LANES), per_tile(SSM_GROUP, LANES),
                  per_tile(SSM_GROUP, LANES), per_tile(SSM_GROUP, LANES),
                  pl.BlockSpec((1, 1, LANES), lambda j: (j, 0, 0)), slab_spec],
        out_specs=[pl.BlockSpec((1, SSM_PAIRS, 2 * LANES, 2 * LANES), lambda j: (j, 0, 0, 0)),
                   pl.BlockSpec((1, SSM_PAIRS, 2 * LANES, kw), lambda j: (j, 0, 0, 0)),
                   pl.BlockSpec((SSM_TILE_G, c, LANES), lambda j: (j, 0, 0)), slab_spec],
        out_shape=[jax.ShapeDtypeStruct((n_tiles, SSM_PAIRS, 2 * LANES, 2 * LANES), BF16),
                   jax.ShapeDtypeStruct((n_tiles, SSM_PAIRS, 2 * LANES, kw), BF16),
                   jax.ShapeDtypeStruct((n_tiles * SSM_TILE_G, c, LANES), F32), w_bf16],
        scratch_shapes=[pltpu.VMEM((SSM_TW, LANES), F32),
                        pltpu.VMEM((SSM_T, LANES, LANES), F32),
                        pltpu.VMEM((SSM_PAIRS, 2 * LANES, kw), BF16),
                        pltpu.VMEM((SSM_TILE_G, c, LANES), F32),
                        pltpu.VMEM((SSM_TILE_G, c // 2, LANES), F32),
                        pltpu.VMEM((SSM_TILE_G, c // 2, LANES), F32)],
        compiler_params=_cparams(("arbitrary",), 60),
        name="ssm_state_in",
    )(u, ldt, are2, aim2, bt_re2, bt_im2, c_re2, c_im2, d_rows, w_cast)


def _ssm_out_kernel(u_ref, w2_ref, pmt2_ref, x0_ref, wsrc_a_ref, wsrc_b_ref,
                    y_ref, wdst_a_ref, wdst_b_ref):
    wdst_a_ref[...] = wsrc_a_ref[...].astype(BF16)
    wdst_b_ref[...] = wsrc_b_ref[...].astype(BF16)
    n_chunks = u_ref.shape[0] // SSM_T
    x0 = jnp.concatenate([x0_ref[k] for k in range(x0_ref.shape[0])], axis=1).astype(BF16)
    pairs = [_step_pair(u_ref, a, n_chunks) for a in range(SSM_PAIRS)]
    for b in range(SSM_PAIRS):
        acc = _dot_nt(x0, pmt2_ref[0, b])
        for a in range(b + 1):
            acc = acc + _dot(pairs[a], w2_ref[0, b - a])
        y_ref[pl.ds(2 * b, n_chunks, stride=SSM_T), :] = acc[:, :LANES]
        y_ref[pl.ds(2 * b + 1, n_chunks, stride=SSM_T), :] = acc[:, LANES:]


def _ssm_out(u, w2, pmt2, x0, w_cast_a, w_cast_b):
    s, width = u.shape
    n_tiles = width // LANES
    c = s // SSM_T
    kw = SSM_TILE_G * LANES
    slab_a, a_bf16 = _cast_slab(w_cast_a, n_tiles)
    slab_b, b_bf16 = _cast_slab(w_cast_b, n_tiles)
    spec_a = pl.BlockSpec((slab_a, w_cast_a.shape[1]), lambda j: (j, 0))
    spec_b = pl.BlockSpec((slab_b, w_cast_b.shape[1]), lambda j: (j, 0))
    return pl.pallas_call(
        _ssm_out_kernel,
        grid=(n_tiles,),
        in_specs=[pl.BlockSpec((s, LANES), lambda j: (0, j)),
                  pl.BlockSpec((1, SSM_PAIRS, 2 * LANES, 2 * LANES), lambda j: (j, 0, 0, 0)),
                  pl.BlockSpec((1, SSM_PAIRS, 2 * LANES, kw), lambda j: (j, 0, 0, 0)),
                  pl.BlockSpec((SSM_TILE_G, c, LANES), lambda j: (j, 0, 0)), spec_a, spec_b],
        out_specs=[pl.BlockSpec((s, LANES), lambda j: (0, j)), spec_a, spec_b],
        out_shape=[jax.ShapeDtypeStruct((s, width), F32), a_bf16, b_bf16],
        compiler_params=_cparams(("arbitrary",), 48),
        name="ssm_out",
    )(u, w2, pmt2, x0, w_cast_a, w_cast_b)


def _s5(u, a_re, a_im, log_dt, b_re, b_im, c_re, c_im, d_skip, w_cast_in, w_cast_a, w_cast_b):
    g, p = a_re.shape
    dup = lambda a: jnp.concatenate([a, a], axis=-1)
    ldt = jnp.broadcast_to(log_dt.reshape(g, 1, 1), (g, 1, LANES))
    are2, aim2 = dup(a_re).reshape(g, 1, LANES), dup(a_im).reshape(g, 1, LANES)
    bt_re2 = dup(jnp.swapaxes(b_re, 1, 2))
    bt_im2 = dup(jnp.swapaxes(b_im, 1, 2))
    w2, pmt2, x0, w_in_bf16 = _ssm_in(u, ldt, are2, aim2, bt_re2, bt_im2, dup(c_re), dup(c_im),
                                      d_skip.reshape(-1, 1, LANES), w_cast_in)
    y, w_a_bf16, w_b_bf16 = _ssm_out(u, w2, pmt2, x0, w_cast_a, w_cast_b)
    return y, w_in_bf16, w_a_bf16, w_b_bf16


def _gelu_tanh(x):
    return 0.5 * x * (1.0 + jnp.tanh(math.sqrt(2.0 / math.pi) * (x + 0.044715 * (x * x * x))))


def _mixout_kernel(x_ref, attn_ref, y_ref, wglu_ref, bglu_ref, ga_ref, gs_ref, wo_ref,
                   gt_ref, gf_ref, sc_ref, sh_ref, h_ref, hn_ref):
    aw = attn_ref.shape[1]
    y = _gelu_tanh(y_ref[...])
    gate = jax.nn.sigmoid(_dot(y.astype(BF16), wglu_ref[...]) + bglu_ref[...])
    ns = _rms(y * gate, gs_ref[...]).astype(BF16)
    na = _rms(attn_ref[...].astype(F32), ga_ref[...]).astype(BF16)
    mixed = _dot(na, wo_ref[0:aw, :]) + _dot(ns, wo_ref[aw:, :])
    h = x_ref[...] + gt_ref[...] * mixed
    h_ref[...] = h
    hn_ref[...] = (_rms(h, gf_ref[...]) * (1.0 + sc_ref[...]) + sh_ref[...]).astype(BF16)


def _mixout(x, attn, y, w_glu, b_glu, g_attn, g_ssm, w_out, gt1, g_ffn, sc2, sh2, tm=512):
    s, d = x.shape
    w = attn.shape[1]
    rows = lambda c: pl.BlockSpec((tm, c), lambda i: (i, 0))
    return pl.pallas_call(
        _mixout_kernel,
        grid=(s // tm,),
        in_specs=[rows(d), rows(w), rows(w), _resident((w, w)), _resident((1, w)),
                  _resident((1, w)), _resident((1, w)), _resident(w_out.shape),
                  _resident((1, d)), _resident((1, d)), _resident((1, d)), _resident((1, d))],
        out_specs=[rows(d), rows(d)],
        out_shape=[jax.ShapeDtypeStruct((s, d), F32), jax.ShapeDtypeStruct((s, d), BF16)],
        compiler_params=_cparams(("arbitrary",), 56),
        name="mixer_out",
    )(x, attn, y, w_glu, b_glu, g_attn, g_ssm, w_out, gt1, g_ffn, sc2, sh2)


FFN_HALO = 16


def _ffn_kernel(hn_ref, halo_ref, h_hbm, wa_ref, wb_ref, cw_ref, cb_ref, wd_ref, gt_ref, gfin_ref,
                o_ref, h_buf, h_sem):
    i, j = pl.program_id(0), pl.program_id(1)
    tm = o_ref.shape[0]

    def residual_copy():
        return pltpu.make_async_copy(h_hbm.at[pl.ds(i * tm, tm), :], h_buf, h_sem)

    @pl.when(j == 0)
    def _():
        residual_copy().start()
        o_ref[...] = jnp.zeros(o_ref.shape, F32)

    hn = hn_ref[...]
    a_ext = _dot(jnp.concatenate([halo_ref[...], hn], axis=0), wa_ref[...])
    a = a_ext[FFN_HALO:]
    b = _dot(hn, wb_ref[...])
    halo = a_ext[FFN_HALO - 8:FFN_HALO] * (i > 0).astype(F32)
    row = lax.broadcasted_iota(jnp.int32, a.shape, 0)
    prev1 = jnp.where(row == 0, halo[7:8, :], pltpu.roll(a, 1, axis=0))
    prev2 = jnp.where(row == 0, halo[6:7, :],
                      jnp.where(row == 1, halo[7:8, :], pltpu.roll(a, 2, axis=0)))
    cw = cw_ref[...]
    conv = cb_ref[...] + cw[0:1, :] * prev2 + cw[1:2, :] * prev1 + cw[2:3, :] * a
    act = (conv * jax.nn.sigmoid(conv) * b).astype(BF16)
    o_ref[...] += _dot(act, wd_ref[...])

    @pl.when(j == pl.num_programs(1) - 1)
    def _():
        residual_copy().wait()
        h = h_buf[...] + gt_ref[...] * o_ref[...]
        o_ref[...] = _rms(h, gfin_ref[...])


def _ffn(hn, h, w_up, conv_w, conv_b, w_down, gt2, g_final, tm=1024, tn=512):
    s, d = h.shape
    d_ff = w_down.shape[0]
    nf = d_ff // tn
    halo_blocks = tm // FFN_HALO
    return pl.pallas_call(
        _ffn_kernel,
        grid=(s // tm, nf),
        in_specs=[pl.BlockSpec((tm, d), lambda i, j: (i, 0)),
                  pl.BlockSpec((FFN_HALO, d),
                               lambda i, j: (jnp.maximum(i * halo_blocks - 1, 0), 0)),
                  pl.BlockSpec(memory_space=pl.ANY),
                  pl.BlockSpec((d, tn), lambda i, j: (0, j)),
                  pl.BlockSpec((d, tn), lambda i, j: (0, nf + j)),
                  pl.BlockSpec((3, tn), lambda i, j: (0, j)),
                  pl.BlockSpec((1, tn), lambda i, j: (0, j)),
                  pl.BlockSpec((tn, d), lambda i, j: (j, 0)),
                  _resident((1, d)), _resident((1, d))],
        out_specs=pl.BlockSpec((tm, d), lambda i, j: (i, 0)),
        out_shape=jax.ShapeDtypeStruct((s, d), F32),
        scratch_shapes=[pltpu.VMEM((tm, d), F32), pltpu.SemaphoreType.DMA(())],
        compiler_params=_cparams(("arbitrary", "arbitrary"), 62),
        name="conv_ffn",
    )(hn, hn, h, w_up, w_up, conv_w, conv_b, w_down, gt2, g_final)


def _layer(h, mod, g_mix, w_in, b_f, a_re, a_im, log_dt, ssm_b_re, ssm_b_im, ssm_c_re, ssm_c_im,
           ssm_d, w_glu, b_glu, g_attn_out, g_ssm_out, w_out, g_ffn, w_up, conv_w, conv_b, w_down):
    s, d = h.shape
    aw = N_HEADS * HEAD_DIM
    sh1, sc1, gt1, sh2, sc2, gt2 = [mod[:, i * d:(i + 1) * d] for i in range(N_MOD)]
    row = lambda a: a.reshape(1, -1)

    q, k, v, u, f, k_norm_sq = _inproj(h, row(g_mix), sc1, sh1, w_in,
                                       w_in[:, 3 * aw + N_HEADS:])

    cum_row, cum_col = _forget_cumsum(f, b_f)
    attn, w_up_bf16 = _attention(q, k, v, cum_col, cum_row.reshape(N_HEADS, 1, s), k_norm_sq,
                                 w_up)

    y, w_down_bf16, w_out_bf16, w_glu_bf16 = _s5(
        u, a_re, a_im, log_dt, ssm_b_re, ssm_b_im, ssm_c_re, ssm_c_im, ssm_d,
        w_down, w_out, w_glu)

    h1, hn2 = _mixout(h, attn, y, w_glu_bf16, row(b_glu), row(g_attn_out),
                      row(g_ssm_out), w_out_bf16, gt1, row(g_ffn), sc2, sh2)
    return hn2, h1, (w_up_bf16, conv_w, row(conv_b), w_down_bf16, gt2)


def kernel(x, c, w_ada, b_ada, g_mix, w_in, b_f, a_re, a_im, log_dt, ssm_b_re, ssm_b_im, ssm_c_re,
           ssm_c_im, ssm_d, w_glu, b_glu, g_attn_out, g_ssm_out, w_out, g_ffn, w_up, conv_w,
           conv_b, w_down, g_final):
    batch, s, d = x.shape
    assert w_ada.shape[0] == 1, "only DEPTH == 1 is supported"
    l = 0
    outs = []
    for bi in range(batch):
        mod = _adaln(c[bi:bi + 1], w_ada[l], b_ada[l])
        hn2, h1, ffn_args = _layer(
            x[bi], mod, g_mix[l], w_in[l].astype(BF16), b_f[l], a_re[l], a_im[l], log_dt[l], ssm_b_re[l],
            ssm_b_im[l], ssm_c_re[l], ssm_c_im[l], ssm_d[l], w_glu[l], b_glu[l],
            g_attn_out[l], g_ssm_out[l], w_out[l], g_ffn[l], w_up[l], conv_w[l], conv_b[l],
            w_down[l])
        outs.append(_ffn(hn2, h1, *ffn_args, g_final.reshape(1, d)))
    return jnp.stack(outs, axis=0)
```

```python
import functools
import math

import jax
import jax.numpy as jnp
from jax import lax
from jax.experimental import pallas as pl
from jax.experimental.pallas import tpu as pltpu

F32 = jnp.float32
BF16 = jnp.bfloat16

EPS = 1e-6
HEAD_DIM = 128
N_HEADS = 8
SSM_GROUP = 16
SSM_STATE = 64
N_MOD = 6
LANES = 128
SSM_T = 16
SSM_TW = SSM_T * SSM_GROUP
SSM_PAIRS = SSM_T // 2
SSM_TILE_G = LANES // SSM_GROUP
LOG2E = 1.4426950408889634
SKIP_LOG2 = 151.0
NORM_SLACK = 1.01

_MIB = 1024 * 1024


def _cparams(semantics, vmem_mib):
    return pltpu.CompilerParams(dimension_semantics=semantics, vmem_limit_bytes=vmem_mib * _MIB)


def _resident(shape):
    return pl.BlockSpec(shape, lambda *_: (0,) * len(shape), pipeline_mode=pl.Buffered(1))


def _dot(a, b):
    return jnp.dot(a, b, preferred_element_type=F32)


def _dot_nt(a, b):
    return lax.dot_general(a, b, (((1,), (1,)), ((), ())), preferred_element_type=F32)


def _rms(x, g):
    return x * lax.rsqrt(jnp.mean(x * x, axis=-1, keepdims=True) + EPS) * g


def _lane_tile(x, reps):
    return jnp.concatenate([x] * reps, axis=1)


def _cast_slab(w, steps):
    rows = w.shape[0] // steps
    assert rows * steps == w.shape[0] and rows % 16 == 0, (w.shape, steps)
    return rows, jax.ShapeDtypeStruct(w.shape, BF16)


def _adaln_kernel(c_ref, w_ref, b_ref, o_ref):
    c = c_ref[...]
    cond = c * jax.nn.sigmoid(c)
    cond8 = jnp.broadcast_to(cond, (8, c.shape[1])).astype(BF16)
    acc = _dot(cond8, w_ref[...].astype(BF16))
    o_ref[...] = acc[0:1, :] + b_ref[...]


def _adaln(c, w, b, tn=1536):
    d, n = w.shape
    return pl.pallas_call(
        _adaln_kernel,
        grid=(n // tn,),
        in_specs=[pl.BlockSpec((1, d), lambda j: (0, 0)),
                  pl.BlockSpec((d, tn), lambda j: (0, j)),
                  pl.BlockSpec((1, tn), lambda j: (0, j))],
        out_specs=pl.BlockSpec((1, tn), lambda j: (0, j)),
        out_shape=jax.ShapeDtypeStruct((1, n), F32),
        compiler_params=_cparams(("arbitrary",), 48),
        name="adaln",
    )(c, w, b.reshape(1, n))


def _inproj_kernel(x_ref, g_ref, sc_ref, sh_ref, wq_ref, wk_ref, wv_ref, wu_ref, wf_ref,
                   q_ref, k_ref, v_ref, u_ref, f_ref, kn_ref, *, q_scale):
    hn = (_rms(x_ref[...], g_ref[...]) * (1.0 + sc_ref[...]) + sh_ref[...]).astype(BF16)
    f_ref[...] = _dot(hn, wf_ref[...])
    q_ref[...] = (_dot(hn, wq_ref[...]) * q_scale).astype(BF16)
    k = _dot(hn, wk_ref[...]).astype(BF16)
    k_ref[...] = k
    u_ref[...] = _dot(hn, wu_ref[...])

    k32 = k.astype(F32)
    ksq = k32 * k32
    tile_max = jnp.concatenate(
        [jnp.broadcast_to(
            jnp.max(jnp.sum(ksq[:, h * HEAD_DIM:(h + 1) * HEAD_DIM], axis=1, keepdims=True),
                    axis=0, keepdims=True), (1, LANES)) for h in range(N_HEADS)], axis=0)

    @pl.when(pl.program_id(0) == 0)
    def _():
        kn_ref[...] = tile_max

    @pl.when(pl.program_id(0) > 0)
    def _():
        kn_ref[...] = jnp.maximum(kn_ref[...], tile_max)

    v = _dot(hn, wv_ref[...]).astype(BF16)
    ones = jnp.ones((v.shape[0], HEAD_DIM), BF16)
    for h in range(N_HEADS):
        v_ref[:, 2 * h * HEAD_DIM:(2 * h + 1) * HEAD_DIM] = v[:, h * HEAD_DIM:(h + 1) * HEAD_DIM]
        v_ref[:, (2 * h + 1) * HEAD_DIM:(2 * h + 2) * HEAD_DIM] = ones


def _inproj(x, g, sc, sh, w_in, wu, tm=512):
    s, d = x.shape
    aw, sw = N_HEADS * HEAD_DIM, wu.shape[1]
    rows = lambda c: pl.BlockSpec((tm, c), lambda i: (i, 0))
    cols = lambda width, blk: pl.BlockSpec((d, width), lambda i: (0, blk),
                                           pipeline_mode=pl.Buffered(1))
    return pl.pallas_call(
        functools.partial(_inproj_kernel, q_scale=HEAD_DIM ** -0.5 * LOG2E),
        grid=(s // tm,),
        in_specs=[rows(d), _resident((1, d)), _resident((1, d)), _resident((1, d)),
                  cols(aw, 0), cols(aw, 1), cols(aw, 2), _resident((d, sw)),
                  cols(LANES, 3 * aw // LANES)],
        out_specs=[rows(aw), rows(aw), rows(2 * aw), rows(sw), rows(LANES),
                   pl.BlockSpec((N_HEADS, LANES), lambda i: (0, 0))],
        out_shape=[jax.ShapeDtypeStruct((s, aw), BF16), jax.ShapeDtypeStruct((s, aw), BF16),
                   jax.ShapeDtypeStruct((s, 2 * aw), BF16), jax.ShapeDtypeStruct((s, sw), F32),
                   jax.ShapeDtypeStruct((s, LANES), F32),
                   jax.ShapeDtypeStruct((N_HEADS, LANES), F32)],
        compiler_params=_cparams(("arbitrary",), 56),
        name="inproj",
    )(x, g, sc, sh, w_in, w_in, w_in, wu, w_in)


def _log2_forget_cumsum(z, axis):
    x = jnp.minimum(z, 0.0) - jnp.log1p(jnp.exp(-jnp.abs(z)))
    pos = lax.broadcasted_iota(jnp.int32, x.shape, axis)
    shift = 1
    while shift < x.shape[axis]:
        x = x + jnp.where(pos >= shift, pltpu.roll(x, shift, axis=axis), 0.0)
        shift *= 2
    return x * LOG2E


def _cum_kernel(ft_ref, bcol_ref, f_ref, brow_ref, row_ref, col_ref):
    row_ref[...] = _log2_forget_cumsum(ft_ref[...] + bcol_ref[...], 1)
    col_ref[...] = _log2_forget_cumsum(f_ref[...] + brow_ref[...], 0)


def _forget_cumsum(f, b_f):
    s, w = f.shape
    h = b_f.shape[0]
    return pl.pallas_call(
        _cum_kernel,
        out_shape=[jax.ShapeDtypeStruct((h, s), F32), jax.ShapeDtypeStruct((s, w), F32)],
        compiler_params=_cparams(None, 48),
        name="forget_cumsum",
    )(f[:, :h].T, b_f.reshape(h, 1), f, jnp.pad(b_f, (0, w - h)).reshape(1, w))


def _attn_kernel(q_ref, k_ref, v_ref, cq_ref, ck_ref, kn_ref, wsrc_ref, o_ref, wdst_ref,
                 m_ref, acc_ref, *, tq, near, splits):
    i = pl.program_id(1)
    wdst_ref[...] = wsrc_ref[...].astype(BF16)
    kmax = jnp.sqrt(kn_ref[pl.ds(pl.program_id(0), 1), :]) * NORM_SLACK

    head_lane = lax.broadcasted_iota(jnp.int32, (tq, LANES), 1) == pl.program_id(0)
    cqb = jnp.broadcast_to(
        jnp.sum(jnp.where(head_lane, cq_ref[...], 0.0), axis=1, keepdims=True), (tq, LANES))

    half = tq // splits
    halves = [slice(hh * half, (hh + 1) * half) for hh in range(splits)]

    base = pl.multiple_of(i * tq, tq)
    qf = q_ref[...].astype(F32)
    own = jnp.sum(qf * k_ref[pl.ds(base, tq), :].astype(F32), axis=1, keepdims=True)
    qn = jnp.sqrt(jnp.sum(qf * qf, axis=1, keepdims=True))
    bound = jnp.max(qn * kmax - (own - cqb))
    ck_all = ck_ref[0]
    pos = lax.broadcasted_iota(jnp.int32, ck_all.shape, 1)
    dead = jnp.where((pos < i * tq) & (ck_all > bound + SKIP_LOG2), 1.0, 0.0)
    j0 = jnp.sum(dead).astype(jnp.int32) // tq

    first_near = jnp.maximum(i - near, 0)
    near_w = near * tq
    near_ks = pl.multiple_of(first_near * tq, tq)
    near_k = k_ref[pl.ds(near_ks, near_w), :]
    near_v = v_ref[pl.ds(near_ks, near_w), :]
    near_key = near_ks + lax.broadcasted_iota(jnp.int32, (1, near_w), 1)
    near_ck = jnp.where(near_key < base, ck_ref[0, :, pl.ds(near_ks, near_w)], jnp.inf)
    t_near = _dot_nt(q_ref[...], near_k) - near_ck
    for hh, rs in enumerate(halves):
        nk = (hh + 1) * half
        qh = q_ref[rs, :]
        t_diag = _dot_nt(qh, k_ref[pl.ds(base, nk), :]) - ck_ref[0, :, pl.ds(base, nk)]
        row = lax.broadcasted_iota(jnp.int32, t_diag.shape, 0)
        col = lax.broadcasted_iota(jnp.int32, t_diag.shape, 1)
        t_diag = jnp.where(col <= row + hh * half, t_diag, -jnp.inf)
        t = jnp.concatenate([t_near[rs], t_diag], axis=1)
        m0 = jnp.max(t, axis=1, keepdims=True) + cqb[rs]
        p = jnp.exp2(t - _lane_tile(m0 - cqb[rs], (near_w + nk) // LANES)).astype(BF16)
        m_ref[rs, :] = m0
        acc = _dot(p[:, :near_w], near_v) + _dot(p[:, near_w:], v_ref[pl.ds(base, nk), :])
        acc_ref[rs, :] = acc
        o_ref[rs, :] = (acc[:, :HEAD_DIM] / acc[:, HEAD_DIM:]).astype(BF16)

    def body(j, carry):
        ks = pl.multiple_of(j * tq, tq)
        kb = k_ref[pl.ds(ks, tq), :]
        vb = v_ref[pl.ds(ks, tq), :]
        t_all = _dot_nt(q_ref[...], kb) - ck_ref[0, :, pl.ds(ks, tq)]
        for rs in halves:
            t = t_all[rs]
            m_prev = m_ref[rs, :]
            m_new = jnp.maximum(m_prev, jnp.max(t, axis=1, keepdims=True) + cqb[rs])
            alpha = jnp.exp2(m_prev - m_new)
            p = jnp.exp2(t - _lane_tile(m_new - cqb[rs], tq // LANES))
            acc_ref[rs, :] = _lane_tile(alpha, 2) * acc_ref[rs, :] + _dot(p.astype(BF16), vb)
            m_ref[rs, :] = m_new
        return carry

    @pl.when(j0 < first_near)
    def _():
        lax.fori_loop(j0, first_near, body, 0)
        acc = acc_ref[...]
        o_ref[...] = (acc[:, :HEAD_DIM] / acc[:, HEAD_DIM:]).astype(BF16)


def _attention(q, k, v_ones, cum_col, cum_row, k_norm_sq, w_cast, tq=512, near=2, splits=2):
    s = q.shape[0]
    nq = s // tq
    slab, w_bf16 = _cast_slab(w_cast, N_HEADS * nq)
    slab_spec = pl.BlockSpec((slab, w_cast.shape[1]), lambda h, i: (h * nq + i, 0))
    return pl.pallas_call(
        functools.partial(_attn_kernel, tq=tq, near=near, splits=splits),
        grid=(N_HEADS, nq),
        in_specs=[pl.BlockSpec((tq, HEAD_DIM), lambda h, i: (i, h)),
                  pl.BlockSpec((s, HEAD_DIM), lambda h, i: (0, h)),
                  pl.BlockSpec((s, 2 * HEAD_DIM), lambda h, i: (0, h)),
                  pl.BlockSpec((tq, LANES), lambda h, i: (i, 0)),
                  pl.BlockSpec((1, 1, s), lambda h, i: (h, 0, 0)),
                  pl.BlockSpec((N_HEADS, LANES), lambda h, i: (0, 0)),
                  slab_spec],
        out_specs=[pl.BlockSpec((tq, HEAD_DIM), lambda h, i: (i, h)), slab_spec],
        out_shape=[jax.ShapeDtypeStruct((s, N_HEADS * HEAD_DIM), BF16), w_bf16],
        scratch_shapes=[pltpu.VMEM((tq, LANES), F32), pltpu.VMEM((tq, 2 * HEAD_DIM), F32)],
        compiler_params=_cparams(("arbitrary", "arbitrary"), 40),
        name="fox_attention",
    )(q, k, v_ones, cum_col, cum_row, k_norm_sq, w_cast)


def _cmul(are, aim, bre, bim):
    return are * bre - aim * bim, are * bim + aim * bre


def _step_pair(u_ref, a, n_chunks):
    x0 = u_ref[pl.ds(2 * a, n_chunks, stride=SSM_T), :]
    x1 = u_ref[pl.ds(2 * a + 1, n_chunks, stride=SSM_T), :]
    return jnp.concatenate([x0, x1], axis=1).astype(BF16)


def _ssm_in_kernel(u_ref, ldt_ref, are_ref, aim_ref, btr_ref, bti_ref, cr_ref, ci_ref, d_ref,
                   wsrc_ref, w2_ref, pmt2_ref, x0_ref, wdst_ref,
                   cp_ref, wl_ref, q2_ref, v_ref, w_ref, e_ref):
    wdst_ref[...] = wsrc_ref[...].astype(BF16)
    n_chunks = u_ref.shape[0] // SSM_T
    half_w = SSM_TILE_G * SSM_STATE
    pw_parts = []
    lane = lax.broadcasted_iota(jnp.int32, (SSM_GROUP, LANES), 1)
    first = lane < SSM_STATE

    @pl.when(pl.program_id(0) == 0)
    def _():
        q2_ref[...] = jnp.zeros(q2_ref.shape, BF16)

    pmt2_ref[...] = jnp.zeros(pmt2_ref.shape, BF16)

    def place(ref, lead, step, gi, re_part, im_part):
        r0 = (step % 2) * LANES + gi * SSM_GROUP
        rows = slice(r0, r0 + SSM_GROUP)
        mine = first if gi % 2 == 0 else jnp.logical_not(first)
        c_re = (gi // 2) * LANES
        c_im = half_w + c_re
        ref[lead + (step // 2, rows, slice(c_re, c_re + LANES))] = (
            jnp.where(mine, re_part, 0.0).astype(BF16))
        ref[lead + (step // 2, rows, slice(c_im, c_im + LANES))] = (
            jnp.where(mine, im_part, 0.0).astype(BF16))

    dup = lambda p: jnp.concatenate([p, p], axis=-1)
    for gi in range(SSM_TILE_G):
        dt = jnp.exp(ldt_ref[gi])
        are, aim = dup(are_ref[gi]), dup(aim_ref[gi])
        mag = jnp.exp(dt * are)
        abre, abim = mag * jnp.cos(dt * aim), mag * jnp.sin(dt * aim)
        nre, nim = abre - 1.0, abim
        den = are * are + aim * aim
        zre = (nre * are + nim * aim) / den
        zim = (nim * are - nre * aim) / den
        bbre, bbim = _cmul(zre, zim, dup(btr_ref[gi]), dup(bti_ref[gi]))
        bbcat = jnp.where(first, bbre, bbim)

        cpre, cpim = dup(cr_ref[gi]), dup(ci_ref[gi])
        qre, qim = bbre, bbim
        pwre, pwim = jnp.ones_like(abre), jnp.zeros_like(abim)
        for t in range(SSM_T):
            cp_ref[t * SSM_GROUP:(t + 1) * SSM_GROUP, :] = jnp.where(first, cpre, -cpim)
            if t > 0:
                place(pmt2_ref, (0,), t - 1, gi, cpre, -cpim)
            place(q2_ref, (), SSM_T - 1 - t, gi, qre, qim)
            cpre, cpim = _cmul(cpre, cpim, abre, abim)
            qre, qim = _cmul(qre, qim, abre, abim)
            pwre, pwim = _cmul(pwre, pwim, abre, abim)
        place(pmt2_ref, (0,), SSM_T - 1, gi, cpre, -cpim)
        pw_parts.append((pwre, pwim))

        krow = lax.dot_general(bbcat, cp_ref[...], (((1,), (1,)), ((), ())),
                               preferred_element_type=F32, precision=lax.Precision.HIGHEST)
        own = (lane >= gi * SSM_GROUP) & (lane < (gi + 1) * SSM_GROUP)
        for tau in range(SSM_T):
            half = krow[:, (tau // SSM_TILE_G) * LANES:(tau // SSM_TILE_G + 1) * LANES]
            shift = ((gi - tau % SSM_TILE_G) * SSM_GROUP) % LANES
            moved = half if shift == 0 else pltpu.roll(half, shift, axis=1)
            wl_ref[tau, gi * SSM_GROUP:(gi + 1) * SSM_GROUP, :] = jnp.where(own, moved, 0.0)

    r = lax.broadcasted_iota(jnp.int32, (LANES, LANES), 0)
    c = lax.broadcasted_iota(jnp.int32, (LANES, LANES), 1)
    wl_ref[0] = wl_ref[0] + jnp.where(r == c, d_ref[0], 0.0)
    for dl in range(SSM_PAIRS):
        diag = wl_ref[2 * dl].astype(BF16)
        w2_ref[0, dl, 0:LANES, 0:LANES] = diag
        w2_ref[0, dl, LANES:, LANES:] = diag
        w2_ref[0, dl, 0:LANES, LANES:] = wl_ref[2 * dl + 1].astype(BF16)
        below = jnp.zeros((LANES, LANES), BF16) if dl == 0 else wl_ref[2 * dl - 1].astype(BF16)
        w2_ref[0, dl, LANES:, 0:LANES] = below

    v = _dot(_step_pair(u_ref, 0, n_chunks), q2_ref[0])
    for a in range(1, SSM_PAIRS):
        v = v + _dot(_step_pair(u_ref, a, n_chunks), q2_ref[a])
    n_re = half_w // LANES
    for k in range(2 * n_re):
        v_ref[k] = v[:, k * LANES:(k + 1) * LANES]

    pair = lambda k, part: jnp.where(first[0:1], pw_parts[2 * k][part], pw_parts[2 * k + 1][part])
    ars = [pair(k, 0) for k in range(n_re)]
    ais = [pair(k, 1) for k in range(n_re)]
    n_half = n_chunks // 2
    even, odd = pl.ds(0, n_half, stride=2), pl.ds(1, n_half, stride=2)
    for k in range(n_re):
        ve_re, ve_im = v_ref[k, even, :], v_ref[n_re + k, even, :]
        w_ref[k] = ars[k] * ve_re - ais[k] * ve_im + v_ref[k, odd, :]
        w_ref[n_re + k] = ars[k] * ve_im + ais[k] * ve_re + v_ref[n_re + k, odd, :]
    ar = jnp.concatenate(ars, axis=1)
    ai = jnp.concatenate(ais, axis=1)
    ar2, ai2 = ar * ar - ai * ai, 2.0 * ar * ai

    def scan_body(c, carry):
        xre, xim = carry
        for k in range(n_re):
            e_ref[k, pl.ds(c, 1), :] = xre[:, k * LANES:(k + 1) * LANES]
            e_ref[n_re + k, pl.ds(c, 1), :] = xim[:, k * LANES:(k + 1) * LANES]
        wre = jnp.concatenate([w_ref[k, pl.ds(c, 1), :] for k in range(n_re)], axis=1)
        wim = jnp.concatenate([w_ref[n_re + k, pl.ds(c, 1), :] for k in range(n_re)], axis=1)
        return ar2 * xre - ai2 * xim + wre, ar2 * xim + ai2 * xre + wim

    zero = jnp.zeros(ar.shape, F32)
    lax.fori_loop(0, n_half, scan_body, (zero, zero), unroll=8)
    for k in range(n_re):
        e_re, e_im = e_ref[k], e_ref[n_re + k]
        x0_ref[k, even, :] = e_re
        x0_ref[n_re + k, even, :] = e_im
        x0_ref[k, odd, :] = ars[k] * e_re - ais[k] * e_im + v_ref[k, even, :]
        x0_ref[n_re + k, odd, :] = ars[k] * e_im + ais[k] * e_re + v_ref[n_re + k, even, :]


def _ssm_in(u, ldt, are2, aim2, bt_re2, bt_im2, c_re2, c_im2, d_rows, w_cast):
    s, width = u.shape
    n_tiles = width // LANES
    c = s // SSM_T
    kw = SSM_TILE_G * LANES
    per_tile = lambda *shape: pl.BlockSpec((SSM_TILE_G,) + shape,
                                           lambda j: (j,) + (0,) * len(shape))
    slab, w_bf16 = _cast_slab(w_cast, n_tiles)
    slab_spec = pl.BlockSpec((slab, w_cast.shape[1]), lambda j: (j, 0))
    return pl.pallas_call(
        _ssm_in_kernel,
        grid=(n_tiles,),
        in_specs=[pl.BlockSpec((s, LANES), lambda j: (0, j)),
                  per_tile(1, LANES), per_tile(1, SSM_STATE), per_tile(1, SSM_STATE),
                  per_tile(SSM_GROUP, SSM_STATE), per_tile(SSM_GROUP, SSM_STATE),
                  per_tile(SSM_GROUP, SSM_STATE), per_tile(SSM_GROUP, SSM_STATE),
                  pl.BlockSpec((1, 1, LANES), lambda j: (j, 0, 0)), slab_spec],
        out_specs=[pl.BlockSpec((1, SSM_PAIRS, 2 * LANES, 2 * LANES), lambda j: (j, 0, 0, 0)),
                   pl.BlockSpec((1, SSM_PAIRS, 2 * LANES, kw), lambda j: (j, 0, 0, 0)),
                   pl.BlockSpec((SSM_TILE_G, c, LANES), lambda j: (j, 0, 0)), slab_spec],
        out_shape=[jax.ShapeDtypeStruct((n_tiles, SSM_PAIRS, 2 * LANES, 2 * LANES), BF16),
                   jax.ShapeDtypeStruct((n_tiles, SSM_PAIRS, 2 * LANES, kw), BF16),
                   jax.ShapeDtypeStruct((n_tiles * SSM_TILE_G, c, LANES), F32), w_bf16],
        scratch_shapes=[pltpu.VMEM((SSM_TW, LANES), F32),
                        pltpu.VMEM((SSM_T, LANES, LANES), F32),
                        pltpu.VMEM((SSM_PAIRS, 2 * LANES, kw), BF16),
                        pltpu.VMEM((SSM_TILE_G, c, LANES), F32),
                        pltpu.VMEM((SSM_TILE_G, c // 2, LANES), F32),
                        pltpu.VMEM((SSM_TILE_G, c // 2, LANES), F32)],
        compiler_params=_cparams(("arbitrary",), 60),
        name="ssm_state_in",
    )(u, ldt, are2, aim2, bt_re2, bt_im2, c_re2, c_im2, d_rows, w_cast)


def _ssm_out_kernel(u_ref, w2_ref, pmt2_ref, x0_ref, wsrc_a_ref, wsrc_b_ref,
                    y_ref, wdst_a_ref, wdst_b_ref):
    wdst_a_ref[...] = wsrc_a_ref[...].astype(BF16)
    wdst_b_ref[...] = wsrc_b_ref[...].astype(BF16)
    n_chunks = u_ref.shape[0] // SSM_T
    x0 = jnp.concatenate([x0_ref[k] for k in range(x0_ref.shape[0])], axis=1).astype(BF16)
    pairs = [_step_pair(u_ref, a, n_chunks) for a in range(SSM_PAIRS)]
    for b in range(SSM_PAIRS):
        acc = _dot_nt(x0, pmt2_ref[0, b])
        for a in range(b + 1):
            acc = acc + _dot(pairs[a], w2_ref[0, b - a])
        y_ref[pl.ds(2 * b, n_chunks, stride=SSM_T), :] = acc[:, :LANES]
        y_ref[pl.ds(2 * b + 1, n_chunks, stride=SSM_T), :] = acc[:, LANES:]


def _ssm_out(u, w2, pmt2, x0, w_cast_a, w_cast_b):
    s, width = u.shape
    n_tiles = width // LANES
    c = s // SSM_T
    kw = SSM_TILE_G * LANES
    slab_a, a_bf16 = _cast_slab(w_cast_a, n_tiles)
    slab_b, b_bf16 = _cast_slab(w_cast_b, n_tiles)
    spec_a = pl.BlockSpec((slab_a, w_cast_a.shape[1]), lambda j: (j, 0))
    spec_b = pl.BlockSpec((slab_b, w_cast_b.shape[1]), lambda j: (j, 0))
    return pl.pallas_call(
        _ssm_out_kernel,
        grid=(n_tiles,),
        in_specs=[pl.BlockSpec((s, LANES), lambda j: (0, j)),
                  pl.BlockSpec((1, SSM_PAIRS, 2 * LANES, 2 * LANES), lambda j: (j, 0, 0, 0)),
                  pl.BlockSpec((1, SSM_PAIRS, 2 * LANES, kw), lambda j: (j, 0, 0, 0)),
                  pl.BlockSpec((SSM_TILE_G, c, LANES), lambda j: (j, 0, 0)), spec_a, spec_b],
        out_specs=[pl.BlockSpec((s, LANES), lambda j: (0, j)), spec_a, spec_b],
        out_shape=[jax.ShapeDtypeStruct((s, width), F32), a_bf16, b_bf16],
        compiler_params=_cparams(("arbitrary",), 48),
        name="ssm_out",
    )(u, w2, pmt2, x0, w_cast_a, w_cast_b)


def _s5(u, a_re, a_im, log_dt, b_re, b_im, c_re, c_im, d_skip, w_cast_in, w_cast_a, w_cast_b):
    g, p = a_re.shape
    ldt = jnp.broadcast_to(log_dt.reshape(g, 1, 1), (g, 1, LANES))
    w2, pmt2, x0, w_in_bf16 = _ssm_in(
        u, ldt, a_re.reshape(g, 1, p), a_im.reshape(g, 1, p),
        jnp.swapaxes(b_re, 1, 2), jnp.swapaxes(b_im, 1, 2), c_re, c_im,
        d_skip.reshape(-1, 1, LANES), w_cast_in)
    y, w_a_bf16, w_b_bf16 = _ssm_out(u, w2, pmt2, x0, w_cast_a, w_cast_b)
    return y, w_in_bf16, w_a_bf16, w_b_bf16


def _gelu_tanh(x):
    return 0.5 * x * (1.0 + jnp.tanh(math.sqrt(2.0 / math.pi) * (x + 0.044715 * (x * x * x))))


def _mixout_kernel(x_ref, attn_ref, y_ref, wglu_ref, bglu_ref, ga_ref, gs_ref, wo_ref,
                   gt_ref, gf_ref, sc_ref, sh_ref, h_ref, hn_ref):
    y = _gelu_tanh(y_ref[...])
    gate = jax.nn.sigmoid(_dot(y.astype(BF16), wglu_ref[...]) + bglu_ref[...])
    ns = _rms(y * gate, gs_ref[...]).astype(BF16)
    na = _rms(attn_ref[...].astype(F32), ga_ref[...]).astype(BF16)
    mixed = _dot(jnp.concatenate([na, ns], axis=1), wo_ref[...])
    h = x_ref[...] + gt_ref[...] * mixed
    h_ref[...] = h
    hn_ref[...] = (_rms(h, gf_ref[...]) * (1.0 + sc_ref[...]) + sh_ref[...]).astype(BF16)


def _mixout(x, attn, y, w_glu, b_glu, g_attn, g_ssm, w_out, gt1, g_ffn, sc2, sh2, tm=512):
    s, d = x.shape
    w = attn.shape[1]
    rows = lambda c: pl.BlockSpec((tm, c), lambda i: (i, 0))
    return pl.pallas_call(
        _mixout_kernel,
        grid=(s // tm,),
        in_specs=[rows(d), rows(w), rows(w), _resident((w, w)), _resident((1, w)),
                  _resident((1, w)), _resident((1, w)), _resident(w_out.shape),
                  _resident((1, d)), _resident((1, d)), _resident((1, d)), _resident((1, d))],
        out_specs=[rows(d), rows(d)],
        out_shape=[jax.ShapeDtypeStruct((s, d), F32), jax.ShapeDtypeStruct((s, d), BF16)],
        compiler_params=_cparams(("arbitrary",), 56),
        name="mixer_out",
    )(x, attn, y, w_glu, b_glu, g_attn, g_ssm, w_out, gt1, g_ffn, sc2, sh2)


FFN_HALO = 16


def _ffn_kernel(hn_ref, halo_ref, h_hbm, wa_ref, wb_ref, cw_ref, cb_ref, wd_ref, gt_ref, gfin_ref,
                o_ref, h_buf, h_sem):
    i, j = pl.program_id(0), pl.program_id(1)
    tm = o_ref.shape[0]

    def residual_copy():
        return pltpu.make_async_copy(h_hbm.at[pl.ds(i * tm, tm), :], h_buf, h_sem)

    @pl.when(j == 0)
    def _():
        residual_copy().start()
        o_ref[...] = jnp.zeros(o_ref.shape, F32)

    hn = hn_ref[...]
    a_ext = _dot(jnp.concatenate([halo_ref[...], hn], axis=0), wa_ref[...])
    a = a_ext[FFN_HALO:]
    b = _dot(hn, wb_ref[...])
    halo = a_ext[FFN_HALO - 8:FFN_HALO] * (i > 0).astype(F32)
    row = lax.broadcasted_iota(jnp.int32, a.shape, 0)
    prev1 = jnp.where(row == 0, halo[7:8, :], pltpu.roll(a, 1, axis=0))
    prev2 = jnp.where(row == 0, halo[6:7, :],
                      jnp.where(row == 1, halo[7:8, :], pltpu.roll(a, 2, axis=0)))
    cw = cw_ref[...]
    conv = cb_ref[...] + cw[0:1, :] * prev2 + cw[1:2, :] * prev1 + cw[2:3, :] * a
    act = (conv * jax.nn.sigmoid(conv) * b).astype(BF16)
    o_ref[...] += _dot(act, wd_ref[...])

    @pl.when(j == pl.num_programs(1) - 1)
    def _():
        residual_copy().wait()
        h = h_buf[...] + gt_ref[...] * o_ref[...]
        o_ref[...] = _rms(h, gfin_ref[...])


def _ffn(hn, h, w_up, conv_w, conv_b, w_down, gt2, g_final, tm=1024, tn=512):
    s, d = h.shape
    d_ff = w_down.shape[0]
    nf = d_ff // tn
    halo_blocks = tm // FFN_HALO
    return pl.pallas_call(
        _ffn_kernel,
        grid=(s // tm, nf),
        in_specs=[pl.BlockSpec((tm, d), lambda i, j: (i, 0)),
                  pl.BlockSpec((FFN_HALO, d),
                               lambda i, j: (jnp.maximum(i * halo_blocks - 1, 0), 0)),
                  pl.BlockSpec(memory_space=pl.ANY),
                  pl.BlockSpec((d, tn), lambda i, j: (0, j)),
                  pl.BlockSpec((d, tn), lambda i, j: (0, nf + j)),
                  pl.BlockSpec((3, tn), lambda i, j: (0, j)),
                  pl.BlockSpec((1, tn), lambda i, j: (0, j)),
                  pl.BlockSpec((tn, d), lambda i, j: (j, 0)),
                  _resident((1, d)), _resident((1, d))],
        out_specs=pl.BlockSpec((tm, d), lambda i, j: (i, 0)),
        out_shape=jax.ShapeDtypeStruct((s, d), F32),
        scratch_shapes=[pltpu.VMEM((tm, d), F32), pltpu.SemaphoreType.DMA(())],
        compiler_params=_cparams(("arbitrary", "arbitrary"), 62),
        name="conv_ffn",
    )(hn, hn, h, w_up, w_up, conv_w, conv_b, w_down, gt2, g_final)


def _layer(h, mod, g_mix, w_in, b_f, a_re, a_im, log_dt, ssm_b_re, ssm_b_im, ssm_c_re, ssm_c_im,
           ssm_d, w_glu, b_glu, g_attn_out, g_ssm_out, w_out, g_ffn, w_up, conv_w, conv_b, w_down):
    s, d = h.shape
    aw = N_HEADS * HEAD_DIM
    sh1, sc1, gt1, sh2, sc2, gt2 = [mod[:, i * d:(i + 1) * d] for i in range(N_MOD)]
    row = lambda a: a.reshape(1, -1)

    q, k, v, u, f, k_norm_sq = _inproj(h, row(g_mix), sc1, sh1, w_in,
                                       w_in[:, 3 * aw + N_HEADS:])

    cum_row, cum_col = _forget_cumsum(f, b_f)
    attn, w_up_bf16 = _attention(q, k, v, cum_col, cum_row.reshape(N_HEADS, 1, s), k_norm_sq,
                                 w_up)

    y, w_down_bf16, w_out_bf16, w_glu_bf16 = _s5(
        u, a_re, a_im, log_dt, ssm_b_re, ssm_b_im, ssm_c_re, ssm_c_im, ssm_d,
        w_down, w_out, w_glu)

    h1, hn2 = _mixout(h, attn, y, w_glu_bf16, row(b_glu), row(g_attn_out),
                      row(g_ssm_out), w_out_bf16, gt1, row(g_ffn), sc2, sh2)
    return hn2, h1, (w_up_bf16, conv_w, row(conv_b), w_down_bf16, gt2)


def kernel(x, c, w_ada, b_ada, g_mix, w_in, b_f, a_re, a_im, log_dt, ssm_b_re, ssm_b_im, ssm_c_re,
           ssm_c_im, ssm_d, w_glu, b_glu, g_attn_out, g_ssm_out, w_out, g_ffn, w_up, conv_w,
           conv_b, w_down, g_final):
    batch, s, d = x.shape
    assert w_ada.shape[0] == 1, "only DEPTH == 1 is supported"
    l = 0
    outs = []
    for bi in range(batch):
        mod = _adaln(c[bi:bi + 1], w_ada[l], b_ada[l])
        hn2, h1, ffn_args = _layer(
            x[bi], mod, g_mix[l], w_in[l].astype(BF16), b_f[l], a_re[l], a_im[l], log_dt[l], ssm_b_re[l],
            ssm_b_im[l], ssm_c_re[l], ssm_c_im[l], ssm_d[l], w_glu[l], b_glu[l],
            g_attn_out[l], g_ssm_out[l], w_out[l], g_ffn[l], w_up[l], conv_w[l], conv_b[l],
            w_down[l])
        outs.append(_ffn(hn2, h1, *ffn_args, g_final.reshape(1, d)))
    return jnp.stack(outs, axis=0)
```

```python
import functools
import math

import jax
import jax.numpy as jnp
from jax import lax
from jax.experimental import pallas as pl
from jax.experimental.pallas import tpu as pltpu

F32 = jnp.float32
BF16 = jnp.bfloat16

EPS = 1e-6
HEAD_DIM = 128
N_HEADS = 8
SSM_GROUP = 16
SSM_STATE = 64
N_MOD = 6
LANES = 128
SSM_T = 16
SSM_TW = SSM_T * SSM_GROUP
SSM_PAIRS = SSM_T // 2
SSM_TILE_G = LANES // SSM_GROUP
LOG2E = 1.4426950408889634
SKIP_LOG2 = 151.0
NORM_SLACK = 1.01

_MIB = 1024 * 1024


def _cparams(semantics, vmem_mib):
    return pltpu.CompilerParams(dimension_semantics=semantics, vmem_limit_bytes=vmem_mib * _MIB)


def _resident(shape):
    return pl.BlockSpec(shape, lambda *_: (0,) * len(shape), pipeline_mode=pl.Buffered(1))


def _dot(a, b):
    return jnp.dot(a, b, preferred_element_type=F32)


def _dot_nt(a, b):
    return lax.dot_general(a, b, (((1,), (1,)), ((), ())), preferred_element_type=F32)


def _rms(x, g):
    return x * lax.rsqrt(jnp.mean(x * x, axis=-1, keepdims=True) + EPS) * g


def _lane_tile(x, reps):
    return jnp.concatenate([x] * reps, axis=1)


def _cast_slab(w, steps):
    rows = w.shape[0] // steps
    assert rows * steps == w.shape[0] and rows % 16 == 0, (w.shape, steps)
    return rows, jax.ShapeDtypeStruct(w.shape, BF16)


def _adaln_kernel(c_ref, w_ref, b_ref, o_ref):
    c = c_ref[...]
    cond = c * jax.nn.sigmoid(c)
    cond8 = jnp.broadcast_to(cond, (8, c.shape[1])).astype(BF16)
    acc = _dot(cond8, w_ref[...].astype(BF16))
    o_ref[...] = acc[0:1, :] + b_ref[...]


def _adaln(c, w, b, tn=1536):
    d, n = w.shape
    return pl.pallas_call(
        _adaln_kernel,
        grid=(n // tn,),
        in_specs=[pl.BlockSpec((1, d), lambda j: (0, 0)),
                  pl.BlockSpec((d, tn), lambda j: (0, j)),
                  pl.BlockSpec((1, tn), lambda j: (0, j))],
        out_specs=pl.BlockSpec((1, tn), lambda j: (0, j)),
        out_shape=jax.ShapeDtypeStruct((1, n), F32),
        compiler_params=_cparams(("arbitrary",), 48),
        name="adaln",
    )(c, w, b.reshape(1, n))


def _inproj_kernel(x_ref, g_ref, sc_ref, sh_ref, wq_ref, wk_ref, wv_ref, wu_ref, wf_ref,
                   q_ref, k_ref, v_ref, u_ref, f_ref, kn_ref, *, q_scale):
    hn = (_rms(x_ref[...], g_ref[...]) * (1.0 + sc_ref[...]) + sh_ref[...]).astype(BF16)
    f_ref[...] = _dot(hn, wf_ref[...])
    q_ref[...] = (_dot(hn, wq_ref[...]) * q_scale).astype(BF16)
    k = _dot(hn, wk_ref[...]).astype(BF16)
    k_ref[...] = k
    u_ref[...] = _dot(hn, wu_ref[...])

    k32 = k.astype(F32)
    ksq = k32 * k32
    tile_max = jnp.concatenate(
        [jnp.broadcast_to(
            jnp.max(jnp.sum(ksq[:, h * HEAD_DIM:(h + 1) * HEAD_DIM], axis=1, keepdims=True),
                    axis=0, keepdims=True), (1, LANES)) for h in range(N_HEADS)], axis=0)

    @pl.when(pl.program_id(0) == 0)
    def _():
        kn_ref[...] = tile_max

    @pl.when(pl.program_id(0) > 0)
    def _():
        kn_ref[...] = jnp.maximum(kn_ref[...], tile_max)

    v = _dot(hn, wv_ref[...]).astype(BF16)
    ones = jnp.ones((v.shape[0], HEAD_DIM), BF16)
    for h in range(N_HEADS):
        v_ref[:, 2 * h * HEAD_DIM:(2 * h + 1) * HEAD_DIM] = v[:, h * HEAD_DIM:(h + 1) * HEAD_DIM]
        v_ref[:, (2 * h + 1) * HEAD_DIM:(2 * h + 2) * HEAD_DIM] = ones


def _inproj(x, g, sc, sh, w_in, wu, tm=512):
    s, d = x.shape
    aw, sw = N_HEADS * HEAD_DIM, wu.shape[1]
    rows = lambda c: pl.BlockSpec((tm, c), lambda i: (i, 0))
    cols = lambda width, blk: pl.BlockSpec((d, width), lambda i: (0, blk),
                                           pipeline_mode=pl.Buffered(1))
    return pl.pallas_call(
        functools.partial(_inproj_kernel, q_scale=HEAD_DIM ** -0.5 * LOG2E),
        grid=(s // tm,),
        in_specs=[rows(d), _resident((1, d)), _resident((1, d)), _resident((1, d)),
                  cols(aw, 0), cols(aw, 1), cols(aw, 2), _resident((d, sw)),
                  cols(LANES, 3 * aw // LANES)],
        out_specs=[rows(aw), rows(aw), rows(2 * aw), rows(sw), rows(LANES),
                   pl.BlockSpec((N_HEADS, LANES), lambda i: (0, 0))],
        out_shape=[jax.ShapeDtypeStruct((s, aw), BF16), jax.ShapeDtypeStruct((s, aw), BF16),
                   jax.ShapeDtypeStruct((s, 2 * aw), BF16), jax.ShapeDtypeStruct((s, sw), F32),
                   jax.ShapeDtypeStruct((s, LANES), F32),
                   jax.ShapeDtypeStruct((N_HEADS, LANES), F32)],
        compiler_params=_cparams(("arbitrary",), 56),
        name="inproj",
    )(x, g, sc, sh, w_in, w_in, w_in, wu, w_in)


def _cum_kernel(f_ref, brow_ref, row_ref, col_ref):
    n_heads, s_len = row_ref.shape
    z = (f_ref[...] + brow_ref[...]).T[0:n_heads, :]
    x = jnp.minimum(z, 0.0) - jnp.log1p(jnp.exp(-jnp.abs(z)))
    pos = lax.broadcasted_iota(jnp.int32, x.shape, 1)
    shift = 1
    while shift < s_len:
        x = x + jnp.where(pos >= shift, pltpu.roll(x, shift, axis=1), 0.0)
        shift *= 2
    x = x * LOG2E
    row_ref[...] = x
    pad = jnp.zeros((col_ref.shape[1] - n_heads, s_len), F32)
    col_ref[...] = jnp.concatenate([x, pad], axis=0).T


def _forget_cumsum(f, b_f):
    s, w = f.shape
    h = b_f.shape[0]
    return pl.pallas_call(
        _cum_kernel,
        out_shape=[jax.ShapeDtypeStruct((h, s), F32), jax.ShapeDtypeStruct((s, w), F32)],
        compiler_params=_cparams(None, 48),
        name="forget_cumsum",
    )(f, jnp.pad(b_f, (0, w - h)).reshape(1, w))


def _attn_kernel(q_ref, k_ref, v_ref, cq_ref, ck_ref, kn_ref, wsrc_ref, o_ref, wdst_ref,
                 m_ref, acc_ref, *, tq, near, splits):
    i = pl.program_id(1)
    wdst_ref[...] = wsrc_ref[...].astype(BF16)
    kmax = jnp.sqrt(kn_ref[pl.ds(pl.program_id(0), 1), :]) * NORM_SLACK

    head_lane = lax.broadcasted_iota(jnp.int32, (tq, LANES), 1) == pl.program_id(0)
    cqb = jnp.broadcast_to(
        jnp.sum(jnp.where(head_lane, cq_ref[...], 0.0), axis=1, keepdims=True), (tq, LANES))

    half = tq // splits
    halves = [slice(hh * half, (hh + 1) * half) for hh in range(splits)]

    base = pl.multiple_of(i * tq, tq)
    qf = q_ref[...].astype(F32)
    own = jnp.sum(qf * k_ref[pl.ds(base, tq), :].astype(F32), axis=1, keepdims=True)
    qn = jnp.sqrt(jnp.sum(qf * qf, axis=1, keepdims=True))
    bound = jnp.max(qn * kmax - (own - cqb))
    ck_all = ck_ref[0]
    pos = lax.broadcasted_iota(jnp.int32, ck_all.shape, 1)
    dead = jnp.where((pos < i * tq) & (ck_all > bound + SKIP_LOG2), 1.0, 0.0)
    j0 = jnp.sum(dead).astype(jnp.int32) // tq

    first_near = jnp.maximum(i - near, 0)
    near_w = near * tq
    near_ks = pl.multiple_of(first_near * tq, tq)
    near_k = k_ref[pl.ds(near_ks, near_w), :]
    near_v = v_ref[pl.ds(near_ks, near_w), :]
    near_key = near_ks + lax.broadcasted_iota(jnp.int32, (1, near_w), 1)
    near_ck = jnp.where(near_key < base, ck_ref[0, :, pl.ds(near_ks, near_w)], jnp.inf)
    t_near = _dot_nt(q_ref[...], near_k) - near_ck
    for hh, rs in enumerate(halves):
        nk = (hh + 1) * half
        qh = q_ref[rs, :]
        t_diag = _dot_nt(qh, k_ref[pl.ds(base, nk), :]) - ck_ref[0, :, pl.ds(base, nk)]
        row = lax.broadcasted_iota(jnp.int32, t_diag.shape, 0)
        col = lax.broadcasted_iota(jnp.int32, t_diag.shape, 1)
        t_diag = jnp.where(col <= row + hh * half, t_diag, -jnp.inf)
        t = jnp.concatenate([t_near[rs], t_diag], axis=1)
        m0 = jnp.max(t, axis=1, keepdims=True) + cqb[rs]
        p = jnp.exp2(t - _lane_tile(m0 - cqb[rs], (near_w + nk) // LANES)).astype(BF16)
        m_ref[rs, :] = m0
        acc = _dot(p[:, :near_w], near_v) + _dot(p[:, near_w:], v_ref[pl.ds(base, nk), :])
        acc_ref[rs, :] = acc
        o_ref[rs, :] = (acc[:, :HEAD_DIM] / acc[:, HEAD_DIM:]).astype(BF16)

    def body(j, carry):
        ks = pl.multiple_of(j * tq, tq)
        kb = k_ref[pl.ds(ks, tq), :]
        vb = v_ref[pl.ds(ks, tq), :]
        t_all = _dot_nt(q_ref[...], kb) - ck_ref[0, :, pl.ds(ks, tq)]
        for rs in halves:
            t = t_all[rs]
            m_prev = m_ref[rs, :]
            m_new = jnp.maximum(m_prev, jnp.max(t, axis=1, keepdims=True) + cqb[rs])
            alpha = jnp.exp2(m_prev - m_new)
            p = jnp.exp2(t - _lane_tile(m_new - cqb[rs], tq // LANES))
            acc_ref[rs, :] = _lane_tile(alpha, 2) * acc_ref[rs, :] + _dot(p.astype(BF16), vb)
            m_ref[rs, :] = m_new
        return carry

    @pl.when(j0 < first_near)
    def _():
        lax.fori_loop(j0, first_near, body, 0)
        acc = acc_ref[...]
        o_ref[...] = (acc[:, :HEAD_DIM] / acc[:, HEAD_DIM:]).astype(BF16)


def _attention(q, k, v_ones, cum_col, cum_row, k_norm_sq, w_cast, tq=512, near=2, splits=2):
    s = q.shape[0]
    nq = s // tq
    slab, w_bf16 = _cast_slab(w_cast, N_HEADS * nq)
    slab_spec = pl.BlockSpec((slab, w_cast.shape[1]), lambda h, i: (h * nq + i, 0))
    return pl.pallas_call(
        functools.partial(_attn_kernel, tq=tq, near=near, splits=splits),
        grid=(N_HEADS, nq),
        in_specs=[pl.BlockSpec((tq, HEAD_DIM), lambda h, i: (i, h)),
                  pl.BlockSpec((s, HEAD_DIM), lambda h, i: (0, h)),
                  pl.BlockSpec((s, 2 * HEAD_DIM), lambda h, i: (0, h)),
                  pl.BlockSpec((tq, LANES), lambda h, i: (i, 0)),
                  pl.BlockSpec((1, 1, s), lambda h, i: (h, 0, 0)),
                  pl.BlockSpec((N_HEADS, LANES), lambda h, i: (0, 0)),
                  slab_spec],
        out_specs=[pl.BlockSpec((tq, HEAD_DIM), lambda h, i: (i, h)), slab_spec],
        out_shape=[jax.ShapeDtypeStruct((s, N_HEADS * HEAD_DIM), BF16), w_bf16],
        scratch_shapes=[pltpu.VMEM((tq, LANES), F32), pltpu.VMEM((tq, 2 * HEAD_DIM), F32)],
        compiler_params=_cparams(("arbitrary", "arbitrary"), 40),
        name="fox_attention",
    )(q, k, v_ones, cum_col, cum_row, k_norm_sq, w_cast)


def _cmul(are, aim, bre, bim):
    return are * bre - aim * bim, are * bim + aim * bre


def _chunk_scan(src_ref, dst_ref, n, ars, ais, scratch):
    n_re = len(ars)
    if not scratch:
        ar = jnp.concatenate(ars, axis=1)
        ai = jnp.concatenate(ais, axis=1)

        def body(c, carry):
            xre, xim = carry
            for k in range(n_re):
                dst_ref[k, pl.ds(c, 1), :] = xre[:, k * LANES:(k + 1) * LANES]
                dst_ref[n_re + k, pl.ds(c, 1), :] = xim[:, k * LANES:(k + 1) * LANES]
            sre = jnp.concatenate([src_ref[k, pl.ds(c, 1), :] for k in range(n_re)], axis=1)
            sim = jnp.concatenate([src_ref[n_re + k, pl.ds(c, 1), :] for k in range(n_re)], axis=1)
            return ar * xre - ai * xim + sre, ar * xim + ai * xre + sim

        zero = jnp.zeros(ar.shape, F32)
        lax.fori_loop(0, n, body, (zero, zero), unroll=8)
        return
    (w_ref, e_ref), deeper = scratch[0], scratch[1:]
    half = n // 2
    even, odd = pl.ds(0, half, stride=2), pl.ds(1, half, stride=2)
    for k in range(n_re):
        se_re, se_im = src_ref[k, even, :], src_ref[n_re + k, even, :]
        w_ref[k] = ars[k] * se_re - ais[k] * se_im + src_ref[k, odd, :]
        w_ref[n_re + k] = ars[k] * se_im + ais[k] * se_re + src_ref[n_re + k, odd, :]
    _chunk_scan(w_ref, e_ref, half, [a * a - b * b for a, b in zip(ars, ais)],
                [2.0 * a * b for a, b in zip(ars, ais)], deeper)
    for k in range(n_re):
        e_re, e_im = e_ref[k], e_ref[n_re + k]
        dst_ref[k, even, :] = e_re
        dst_ref[n_re + k, even, :] = e_im
        dst_ref[k, odd, :] = ars[k] * e_re - ais[k] * e_im + src_ref[k, even, :]
        dst_ref[n_re + k, odd, :] = ars[k] * e_im + ais[k] * e_re + src_ref[n_re + k, even, :]


def _step_pair(u_ref, a, n_chunks):
    x0 = u_ref[pl.ds(2 * a, n_chunks, stride=SSM_T), :]
    x1 = u_ref[pl.ds(2 * a + 1, n_chunks, stride=SSM_T), :]
    return jnp.concatenate([x0, x1], axis=1).astype(BF16)


def _ssm_in_kernel(u_ref, ldt_ref, are_ref, aim_ref, btr_ref, bti_ref, cr_ref, ci_ref, d_ref,
                   wsrc_ref, w2_ref, pmt2_ref, x0_ref, wdst_ref,
                   cp_ref, wl_ref, q2_ref, v_ref, sw1_ref, se1_ref, sw2_ref, se2_ref):
    wdst_ref[...] = wsrc_ref[...].astype(BF16)
    n_chunks = u_ref.shape[0] // SSM_T
    half_w = SSM_TILE_G * SSM_STATE
    pw_parts = []
    lane = lax.broadcasted_iota(jnp.int32, (SSM_GROUP, LANES), 1)
    first = lane < SSM_STATE

    @pl.when(pl.program_id(0) == 0)
    def _():
        q2_ref[...] = jnp.zeros(q2_ref.shape, BF16)

    pmt2_ref[...] = jnp.zeros(pmt2_ref.shape, BF16)

    def place(ref, lead, step, gi, re_part, im_part):
        r0 = (step % 2) * LANES + gi * SSM_GROUP
        rows = slice(r0, r0 + SSM_GROUP)
        mine = first if gi % 2 == 0 else jnp.logical_not(first)
        c_re = (gi // 2) * LANES
        c_im = half_w + c_re
        ref[lead + (step // 2, rows, slice(c_re, c_re + LANES))] = (
            jnp.where(mine, re_part, 0.0).astype(BF16))
        ref[lead + (step // 2, rows, slice(c_im, c_im + LANES))] = (
            jnp.where(mine, im_part, 0.0).astype(BF16))

    dup = lambda p: jnp.concatenate([p, p], axis=-1)
    for gi in range(SSM_TILE_G):
        dt = jnp.exp(ldt_ref[gi])
        are, aim = dup(are_ref[gi]), dup(aim_ref[gi])
        mag = jnp.exp(dt * are)
        abre, abim = mag * jnp.cos(dt * aim), mag * jnp.sin(dt * aim)
        nre, nim = abre - 1.0, abim
        den = are * are + aim * aim
        zre = (nre * are + nim * aim) / den
        zim = (nim * are - nre * aim) / den
        bbre, bbim = _cmul(zre, zim, dup(btr_ref[gi]), dup(bti_ref[gi]))
        bbcat = jnp.where(first, bbre, bbim)

        cpre, cpim = dup(cr_ref[gi]), dup(ci_ref[gi])
        qre, qim = bbre, bbim
        pwre, pwim = jnp.ones_like(abre), jnp.zeros_like(abim)
        for t in range(SSM_T):
            cp_ref[t * SSM_GROUP:(t + 1) * SSM_GROUP, :] = jnp.where(first, cpre, -cpim)
            if t > 0:
                place(pmt2_ref, (0,), t - 1, gi, cpre, -cpim)
            place(q2_ref, (), SSM_T - 1 - t, gi, qre, qim)
            cpre, cpim = _cmul(cpre, cpim, abre, abim)
            qre, qim = _cmul(qre, qim, abre, abim)
            pwre, pwim = _cmul(pwre, pwim, abre, abim)
        place(pmt2_ref, (0,), SSM_T - 1, gi, cpre, -cpim)
        pw_parts.append((pwre, pwim))

        krow = lax.dot_general(bbcat, cp_ref[...], (((1,), (1,)), ((), ())),
                               preferred_element_type=F32, precision=lax.Precision.HIGHEST)
        own = (lane >= gi * SSM_GROUP) & (lane < (gi + 1) * SSM_GROUP)
        for tau in range(SSM_T):
            half = krow[:, (tau // SSM_TILE_G) * LANES:(tau // SSM_TILE_G + 1) * LANES]
            shift = ((gi - tau % SSM_TILE_G) * SSM_GROUP) % LANES
            moved = half if shift == 0 else pltpu.roll(half, shift, axis=1)
            wl_ref[tau, gi * SSM_GROUP:(gi + 1) * SSM_GROUP, :] = jnp.where(own, moved, 0.0)

    r = lax.broadcasted_iota(jnp.int32, (LANES, LANES), 0)
    c = lax.broadcasted_iota(jnp.int32, (LANES, LANES), 1)
    wl_ref[0] = wl_ref[0] + jnp.where(r == c, d_ref[0], 0.0)
    for dl in range(SSM_PAIRS):
        diag = wl_ref[2 * dl].astype(BF16)
        w2_ref[0, dl, 0:LANES, 0:LANES] = diag
        w2_ref[0, dl, LANES:, LANES:] = diag
        w2_ref[0, dl, 0:LANES, LANES:] = wl_ref[2 * dl + 1].astype(BF16)
        below = jnp.zeros((LANES, LANES), BF16) if dl == 0 else wl_ref[2 * dl - 1].astype(BF16)
        w2_ref[0, dl, LANES:, 0:LANES] = below

    v = _dot(_step_pair(u_ref, 0, n_chunks), q2_ref[0])
    for a in range(1, SSM_PAIRS):
        v = v + _dot(_step_pair(u_ref, a, n_chunks), q2_ref[a])
    n_re = half_w // LANES
    for k in range(2 * n_re):
        v_ref[k] = v[:, k * LANES:(k + 1) * LANES]

    pair = lambda k, part: jnp.where(first[0:1], pw_parts[2 * k][part], pw_parts[2 * k + 1][part])
    _chunk_scan(v_ref, x0_ref, n_chunks, [pair(k, 0) for k in range(n_re)],
                [pair(k, 1) for k in range(n_re)], [(sw1_ref, se1_ref), (sw2_ref, se2_ref)])


def _ssm_in(u, ldt, are2, aim2, bt_re2, bt_im2, c_re2, c_im2, d_rows, w_cast):
    s, width = u.shape
    n_tiles = width // LANES
    c = s // SSM_T
    kw = SSM_TILE_G * LANES
    per_tile = lambda *shape: pl.BlockSpec((SSM_TILE_G,) + shape,
                                           lambda j: (j,) + (0,) * len(shape))
    slab, w_bf16 = _cast_slab(w_cast, n_tiles)
    slab_spec = pl.BlockSpec((slab, w_cast.shape[1]), lambda j: (j, 0))
    return pl.pallas_call(
        _ssm_in_kernel,
        grid=(n_tiles,),
        in_specs=[pl.BlockSpec((s, LANES), lambda j: (0, j)),
                  per_tile(1, 1), per_tile(1, SSM_STATE), per_tile(1, SSM_STATE),
                  per_tile(SSM_GROUP, SSM_STATE), per_tile(SSM_GROUP, SSM_STATE),
                  per_tile(SSM_GROUP, SSM_STATE), per_tile(SSM_GROUP, SSM_STATE),
                  pl.BlockSpec((1, 1, LANES), lambda j: (j, 0, 0)), slab_spec],
        out_specs=[pl.BlockSpec((1, SSM_PAIRS, 2 * LANES, 2 * LANES), lambda j: (j, 0, 0, 0)),
                   pl.BlockSpec((1, SSM_PAIRS, 2 * LANES, kw), lambda j: (j, 0, 0, 0)),
                   pl.BlockSpec((SSM_TILE_G, c, LANES), lambda j: (j, 0, 0)), slab_spec],
        out_shape=[jax.ShapeDtypeStruct((n_tiles, SSM_PAIRS, 2 * LANES, 2 * LANES), BF16),
                   jax.ShapeDtypeStruct((n_tiles, SSM_PAIRS, 2 * LANES, kw), BF16),
                   jax.ShapeDtypeStruct((n_tiles * SSM_TILE_G, c, LANES), F32), w_bf16],
        scratch_shapes=[pltpu.VMEM((SSM_TW, LANES), F32),
                        pltpu.VMEM((SSM_T, LANES, LANES), F32),
                        pltpu.VMEM((SSM_PAIRS, 2 * LANES, kw), BF16),
                        pltpu.VMEM((SSM_TILE_G, c, LANES), F32),
                        pltpu.VMEM((SSM_TILE_G, c // 2, LANES), F32),
                        pltpu.VMEM((SSM_TILE_G, c // 2, LANES), F32),
                        pltpu.VMEM((SSM_TILE_G, c // 4, LANES), F32),
                        pltpu.VMEM((SSM_TILE_G, c // 4, LANES), F32)],
        compiler_params=_cparams(("arbitrary",), 60),
        name="ssm_state_in",
    )(u, ldt, are2, aim2, bt_re2, bt_im2, c_re2, c_im2, d_rows, w_cast)


def _ssm_out_kernel(u_ref, w2_ref, pmt2_ref, x0_ref, wsrc_a_ref, wsrc_b_ref,
                    y_ref, wdst_a_ref, wdst_b_ref):
    wdst_a_ref[...] = wsrc_a_ref[...].astype(BF16)
    wdst_b_ref[...] = wsrc_b_ref[...].astype(BF16)
    n_chunks = u_ref.shape[0] // SSM_T
    x0 = jnp.concatenate([x0_ref[k] for k in range(x0_ref.shape[0])], axis=1).astype(BF16)
    pairs = [_step_pair(u_ref, a, n_chunks) for a in range(SSM_PAIRS)]
    for b in range(SSM_PAIRS):
        acc = _dot_nt(x0, pmt2_ref[0, b])
        for a in range(b + 1):
            acc = acc + _dot(pairs[a], w2_ref[0, b - a])
        y_ref[pl.ds(2 * b, n_chunks, stride=SSM_T), :] = acc[:, :LANES]
        y_ref[pl.ds(2 * b + 1, n_chunks, stride=SSM_T), :] = acc[:, LANES:]


def _ssm_out(u, w2, pmt2, x0, w_cast_a, w_cast_b):
    s, width = u.shape
    n_tiles = width // LANES
    c = s // SSM_T
    kw = SSM_TILE_G * LANES
    slab_a, a_bf16 = _cast_slab(w_cast_a, n_tiles)
    slab_b, b_bf16 = _cast_slab(w_cast_b, n_tiles)
    spec_a = pl.BlockSpec((slab_a, w_cast_a.shape[1]), lambda j: (j, 0))
    spec_b = pl.BlockSpec((slab_b, w_cast_b.shape[1]), lambda j: (j, 0))
    return pl.pallas_call(
        _ssm_out_kernel,
        grid=(n_tiles,),
        in_specs=[pl.BlockSpec((s, LANES), lambda j: (0, j)),
                  pl.BlockSpec((1, SSM_PAIRS, 2 * LANES, 2 * LANES), lambda j: (j, 0, 0, 0)),
                  pl.BlockSpec((1, SSM_PAIRS, 2 * LANES, kw), lambda j: (j, 0, 0, 0)),
                  pl.BlockSpec((SSM_TILE_G, c, LANES), lambda j: (j, 0, 0)), spec_a, spec_b],
        out_specs=[pl.BlockSpec((s, LANES), lambda j: (0, j)), spec_a, spec_b],
        out_shape=[jax.ShapeDtypeStruct((s, width), F32), a_bf16, b_bf16],
        compiler_params=_cparams(("arbitrary",), 48),
        name="ssm_out",
    )(u, w2, pmt2, x0, w_cast_a, w_cast_b)


def _s5(u, a_re, a_im, log_dt, b_re, b_im, c_re, c_im, d_skip, w_cast_in, w_cast_a, w_cast_b):
    g, p = a_re.shape
    w2, pmt2, x0, w_in_bf16 = _ssm_in(
        u, log_dt.reshape(g, 1, 1), a_re.reshape(g, 1, p), a_im.reshape(g, 1, p),
        jnp.swapaxes(b_re, 1, 2), jnp.swapaxes(b_im, 1, 2), c_re, c_im,
        d_skip.reshape(-1, 1, LANES), w_cast_in)
    y, w_a_bf16, w_b_bf16 = _ssm_out(u, w2, pmt2, x0, w_cast_a, w_cast_b)
    return y, w_in_bf16, w_a_bf16, w_b_bf16


def _gelu_tanh(x):
    return 0.5 * x * (1.0 + jnp.tanh(math.sqrt(2.0 / math.pi) * (x + 0.044715 * (x * x * x))))


def _mixout_kernel(x_ref, attn_ref, y_ref, wglu_ref, bglu_ref, ga_ref, gs_ref, wo_ref,
                   gt_ref, gf_ref, sc_ref, sh_ref, h_ref, hn_ref):
    y = _gelu_tanh(y_ref[...])
    gate = jax.nn.sigmoid(_dot(y.astype(BF16), wglu_ref[...]) + bglu_ref[...])
    ns = _rms(y * gate, gs_ref[...]).astype(BF16)
    na = _rms(attn_ref[...].astype(F32), ga_ref[...]).astype(BF16)
    mixed = _dot(jnp.concatenate([na, ns], axis=1), wo_ref[...])
    h = x_ref[...] + gt_ref[...] * mixed
    h_ref[...] = h
    hn_ref[...] = (_rms(h, gf_ref[...]) * (1.0 + sc_ref[...]) + sh_ref[...]).astype(BF16)


def _mixout(x, attn, y, w_glu, b_glu, g_attn, g_ssm, w_out, gt1, g_ffn, sc2, sh2, tm=512):
    s, d = x.shape
    w = attn.shape[1]
    rows = lambda c: pl.BlockSpec((tm, c), lambda i: (i, 0))
    return pl.pallas_call(
        _mixout_kernel,
        grid=(s // tm,),
        in_specs=[rows(d), rows(w), rows(w), _resident((w, w)), _resident((1, w)),
                  _resident((1, w)), _resident((1, w)), _resident(w_out.shape),
                  _resident((1, d)), _resident((1, d)), _resident((1, d)), _resident((1, d))],
        out_specs=[rows(d), rows(d)],
        out_shape=[jax.ShapeDtypeStruct((s, d), F32), jax.ShapeDtypeStruct((s, d), BF16)],
        compiler_params=_cparams(("arbitrary",), 56),
        name="mixer_out",
    )(x, attn, y, w_glu, b_glu, g_attn, g_ssm, w_out, gt1, g_ffn, sc2, sh2)


FFN_HALO = 16


def _ffn_kernel(hn_ref, halo_ref, h_hbm, wa_ref, wb_ref, cw_ref, cb_ref, wd_ref, gt_ref, gfin_ref,
                o_ref, h_buf, h_sem):
    i, j = pl.program_id(0), pl.program_id(1)
    tm = o_ref.shape[0]

    def residual_copy():
        return pltpu.make_async_copy(h_hbm.at[pl.ds(i * tm, tm), :], h_buf, h_sem)

    @pl.when(j == 0)
    def _():
        residual_copy().start()
        o_ref[...] = jnp.zeros(o_ref.shape, F32)

    hn = hn_ref[...]
    a_ext = _dot(jnp.concatenate([halo_ref[...], hn], axis=0), wa_ref[...])
    a = a_ext[FFN_HALO:]
    b = _dot(hn, wb_ref[...])
    halo = a_ext[FFN_HALO - 8:FFN_HALO] * (i > 0).astype(F32)
    row = lax.broadcasted_iota(jnp.int32, a.shape, 0)
    prev1 = jnp.where(row == 0, halo[7:8, :], pltpu.roll(a, 1, axis=0))
    prev2 = jnp.where(row == 0, halo[6:7, :],
                      jnp.where(row == 1, halo[7:8, :], pltpu.roll(a, 2, axis=0)))
    cw = cw_ref[...]
    conv = cb_ref[...] + cw[0:1, :] * prev2 + cw[1:2, :] * prev1 + cw[2:3, :] * a
    act = (conv * jax.nn.sigmoid(conv) * b).astype(BF16)
    o_ref[...] += _dot(act, wd_ref[...])

    @pl.when(j == pl.num_programs(1) - 1)
    def _():
        residual_copy().wait()
        h = h_buf[...] + gt_ref[...] * o_ref[...]
        o_ref[...] = _rms(h, gfin_ref[...])


def _ffn(hn, h, w_up, conv_w, conv_b, w_down, gt2, g_final, tm=1024, tn=512):
    s, d = h.shape
    d_ff = w_down.shape[0]
    nf = d_ff // tn
    halo_blocks = tm // FFN_HALO
    return pl.pallas_call(
        _ffn_kernel,
        grid=(s // tm, nf),
        in_specs=[pl.BlockSpec((tm, d), lambda i, j: (i, 0)),
                  pl.BlockSpec((FFN_HALO, d),
                               lambda i, j: (jnp.maximum(i * halo_blocks - 1, 0), 0)),
                  pl.BlockSpec(memory_space=pl.ANY),
                  pl.BlockSpec((d, tn), lambda i, j: (0, j)),
                  pl.BlockSpec((d, tn), lambda i, j: (0, nf + j)),
                  pl.BlockSpec((3, tn), lambda i, j: (0, j)),
                  pl.BlockSpec((1, tn), lambda i, j: (0, j)),
                  pl.BlockSpec((tn, d), lambda i, j: (j, 0)),
                  _resident((1, d)), _resident((1, d))],
        out_specs=pl.BlockSpec((tm, d), lambda i, j: (i, 0)),
        out_shape=jax.ShapeDtypeStruct((s, d), F32),
        scratch_shapes=[pltpu.VMEM((tm, d), F32), pltpu.SemaphoreType.DMA(())],
        compiler_params=_cparams(("arbitrary", "arbitrary"), 62),
        name="conv_ffn",
    )(hn, hn, h, w_up, w_up, conv_w, conv_b, w_down, gt2, g_final)


def _layer(h, mod, g_mix, w_in, b_f, a_re, a_im, log_dt, ssm_b_re, ssm_b_im, ssm_c_re, ssm_c_im,
           ssm_d, w_glu, b_glu, g_attn_out, g_ssm_out, w_out, g_ffn, w_up, conv_w, conv_b, w_down):
    s, d = h.shape
    aw = N_HEADS * HEAD_DIM
    sh1, sc1, gt1, sh2, sc2, gt2 = [mod[:, i * d:(i + 1) * d] for i in range(N_MOD)]
    row = lambda a: a.reshape(1, -1)

    q, k, v, u, f, k_norm_sq = _inproj(h, row(g_mix), sc1, sh1, w_in,
                                       w_in[:, 3 * aw + N_HEADS:])

    cum_row, cum_col = _forget_cumsum(f, b_f)
    attn, w_up_bf16 = _attention(q, k, v, cum_col, cum_row.reshape(N_HEADS, 1, s), k_norm_sq,
                                 w_up)

    y, w_down_bf16, w_out_bf16, w_glu_bf16 = _s5(
        u, a_re, a_im, log_dt, ssm_b_re, ssm_b_im, ssm_c_re, ssm_c_im, ssm_d,
        w_down, w_out, w_glu)

    h1, hn2 = _mixout(h, attn, y, w_glu_bf16, row(b_glu), row(g_attn_out),
                      row(g_ssm_out), w_out_bf16, gt1, row(g_ffn), sc2, sh2)
    return hn2, h1, (w_up_bf16, conv_w, row(conv_b), w_down_bf16, gt2)


def kernel(x, c, w_ada, b_ada, g_mix, w_in, b_f, a_re, a_im, log_dt, ssm_b_re, ssm_b_im, ssm_c_re,
           ssm_c_im, ssm_d, w_glu, b_glu, g_attn_out, g_ssm_out, w_out, g_ffn, w_up, conv_w,
           conv_b, w_down, g_final):
    batch, s, d = x.shape
    assert w_ada.shape[0] == 1, "only DEPTH == 1 is supported"
    l = 0
    outs = []
    for bi in range(batch):
        mod = _adaln(c[bi:bi + 1], w_ada[l], b_ada[l])
        hn2, h1, ffn_args = _layer(
            x[bi], mod, g_mix[l], w_in[l].astype(BF16), b_f[l], a_re[l], a_im[l], log_dt[l], ssm_b_re[l],
            ssm_b_im[l], ssm_c_re[l], ssm_c_im[l], ssm_d[l], w_glu[l], b_glu[l],
            g_attn_out[l], g_ssm_out[l], w_out[l], g_ffn[l], w_up[l], conv_w[l], conv_b[l],
            w_down[l])
        outs.append(_ffn(hn2, h1, *ffn_args, g_final.reshape(1, d)))
    return jnp.stack(outs, axis=0)
```

```python
import functools
import math

import jax
import jax.numpy as jnp
from jax import lax
from jax.experimental import pallas as pl
from jax.experimental.pallas import tpu as pltpu

F32 = jnp.float32
BF16 = jnp.bfloat16

EPS = 1e-6
HEAD_DIM = 128
N_HEADS = 8
SSM_GROUP = 16
SSM_STATE = 64
N_MOD = 6
LANES = 128
SSM_T = 16
SSM_TW = SSM_T * SSM_GROUP
SSM_PAIRS = SSM_T // 2
SSM_TILE_G = LANES // SSM_GROUP
LOG2E = 1.4426950408889634
SKIP_LOG2 = 151.0
NORM_SLACK = 1.01

_MIB = 1024 * 1024


def _cparams(semantics, vmem_mib):
    return pltpu.CompilerParams(dimension_semantics=semantics, vmem_limit_bytes=vmem_mib * _MIB)


def _resident(shape):
    return pl.BlockSpec(shape, lambda *_: (0,) * len(shape), pipeline_mode=pl.Buffered(1))


def _dot(a, b):
    return jnp.dot(a, b, preferred_element_type=F32)


def _dot_nt(a, b):
    return lax.dot_general(a, b, (((1,), (1,)), ((), ())), preferred_element_type=F32)


def _rms(x, g):
    return x * lax.rsqrt(jnp.mean(x * x, axis=-1, keepdims=True) + EPS) * g


def _lane_tile(x, reps):
    return jnp.concatenate([x] * reps, axis=1)


def _cast_slab(w, steps):
    rows = w.shape[0] // steps
    assert rows * steps == w.shape[0] and rows % 16 == 0, (w.shape, steps)
    return rows, jax.ShapeDtypeStruct(w.shape, BF16)


def _adaln_kernel(c_ref, w_ref, b_ref, o_ref):
    c = c_ref[...]
    cond = c * jax.nn.sigmoid(c)
    cond8 = jnp.broadcast_to(cond, (8, c.shape[1])).astype(BF16)
    acc = _dot(cond8, w_ref[...].astype(BF16))
    o_ref[...] = acc[0:1, :] + b_ref[...]


def _adaln(c, w, b, tn=1536):
    d, n = w.shape
    return pl.pallas_call(
        _adaln_kernel,
        grid=(n // tn,),
        in_specs=[pl.BlockSpec((1, d), lambda j: (0, 0)),
                  pl.BlockSpec((d, tn), lambda j: (0, j)),
                  pl.BlockSpec((1, tn), lambda j: (0, j))],
        out_specs=pl.BlockSpec((1, tn), lambda j: (0, j)),
        out_shape=jax.ShapeDtypeStruct((1, n), F32),
        compiler_params=_cparams(("arbitrary",), 48),
        name="adaln",
    )(c, w, b.reshape(1, n))


def _inproj_kernel(x_ref, g_ref, sc_ref, sh_ref, wq_ref, wk_ref, wv_ref, wu_ref, wf_ref,
                   q_ref, k_ref, v_ref, u_ref, f_ref, kn_ref, *, q_scale):
    hn = (_rms(x_ref[...], g_ref[...] * (1.0 + sc_ref[...])) + sh_ref[...]).astype(BF16)
    f_ref[...] = _dot(hn, wf_ref[...])
    q_ref[...] = (_dot(hn, wq_ref[...]) * q_scale).astype(BF16)
    k = _dot(hn, wk_ref[...]).astype(BF16)
    k_ref[...] = k
    u_ref[...] = _dot(hn, wu_ref[...])

    k32 = k.astype(F32)
    ksq = k32 * k32
    tile_max = jnp.concatenate(
        [jnp.broadcast_to(
            jnp.max(jnp.sum(ksq[:, h * HEAD_DIM:(h + 1) * HEAD_DIM], axis=1, keepdims=True),
                    axis=0, keepdims=True), (1, LANES)) for h in range(N_HEADS)], axis=0)

    @pl.when(pl.program_id(0) == 0)
    def _():
        kn_ref[...] = tile_max

    @pl.when(pl.program_id(0) > 0)
    def _():
        kn_ref[...] = jnp.maximum(kn_ref[...], tile_max)

    v = _dot(hn, wv_ref[...]).astype(BF16)
    ones = jnp.ones((v.shape[0], HEAD_DIM), BF16)
    for h in range(N_HEADS):
        v_ref[:, 2 * h * HEAD_DIM:(2 * h + 1) * HEAD_DIM] = v[:, h * HEAD_DIM:(h + 1) * HEAD_DIM]
        v_ref[:, (2 * h + 1) * HEAD_DIM:(2 * h + 2) * HEAD_DIM] = ones


def _inproj(x, g, sc, sh, w_in, wu, tm=512):
    s, d = x.shape
    aw, sw = N_HEADS * HEAD_DIM, wu.shape[1]
    rows = lambda c: pl.BlockSpec((tm, c), lambda i: (i, 0))
    cols = lambda width, blk: pl.BlockSpec((d, width), lambda i: (0, blk),
                                           pipeline_mode=pl.Buffered(1))
    return pl.pallas_call(
        functools.partial(_inproj_kernel, q_scale=HEAD_DIM ** -0.5 * LOG2E),
        grid=(s // tm,),
        in_specs=[rows(d), _resident((1, d)), _resident((1, d)), _resident((1, d)),
                  cols(aw, 0), cols(aw, 1), cols(aw, 2), _resident((d, sw)),
                  cols(LANES, 3 * aw // LANES)],
        out_specs=[rows(aw), rows(aw), rows(2 * aw), rows(sw), rows(LANES),
                   pl.BlockSpec((N_HEADS, LANES), lambda i: (0, 0))],
        out_shape=[jax.ShapeDtypeStruct((s, aw), BF16), jax.ShapeDtypeStruct((s, aw), BF16),
                   jax.ShapeDtypeStruct((s, 2 * aw), BF16), jax.ShapeDtypeStruct((s, sw), F32),
                   jax.ShapeDtypeStruct((s, LANES), F32),
                   jax.ShapeDtypeStruct((N_HEADS, LANES), F32)],
        compiler_params=_cparams(("arbitrary",), 56),
        name="inproj",
    )(x, g, sc, sh, w_in, w_in, w_in, wu, w_in)


def _cum_kernel(f_ref, brow_ref, row_ref, col_ref):
    n_heads, s_len = row_ref.shape
    z = (f_ref[...] + brow_ref[...]).T[0:n_heads, :]
    x = jnp.minimum(z, 0.0) - jnp.log1p(jnp.exp(-jnp.abs(z)))
    pos = lax.broadcasted_iota(jnp.int32, x.shape, 1)
    shift = 1
    while shift < s_len:
        x = x + jnp.where(pos >= shift, pltpu.roll(x, shift, axis=1), 0.0)
        shift *= 2
    x = x * LOG2E
    row_ref[...] = x
    pad = jnp.zeros((col_ref.shape[1] - n_heads, s_len), F32)
    col_ref[...] = jnp.concatenate([x, pad], axis=0).T


def _forget_cumsum(f, b_f):
    s, w = f.shape
    h = b_f.shape[0]
    return pl.pallas_call(
        _cum_kernel,
        out_shape=[jax.ShapeDtypeStruct((h, s), F32), jax.ShapeDtypeStruct((s, w), F32)],
        compiler_params=_cparams(None, 48),
        name="forget_cumsum",
    )(f, jnp.pad(b_f, (0, w - h)).reshape(1, w))


def _attn_kernel(q_ref, k_ref, v_ref, cq_ref, ck_ref, kn_ref, wsrc_ref, o_ref, wdst_ref,
                 m_ref, acc_ref, *, tq, near, splits):
    i = pl.program_id(1)
    wdst_ref[...] = wsrc_ref[...].astype(BF16)
    kmax = jnp.sqrt(kn_ref[pl.ds(pl.program_id(0), 1), :]) * NORM_SLACK

    head_lane = lax.broadcasted_iota(jnp.int32, (tq, LANES), 1) == pl.program_id(0)
    cqb = jnp.broadcast_to(
        jnp.sum(jnp.where(head_lane, cq_ref[...], 0.0), axis=1, keepdims=True), (tq, LANES))

    half = tq // splits
    halves = [slice(hh * half, (hh + 1) * half) for hh in range(splits)]

    base = pl.multiple_of(i * tq, tq)
    qf = q_ref[...].astype(F32)
    own = jnp.sum(qf * k_ref[pl.ds(base, tq), :].astype(F32), axis=1, keepdims=True)
    qn = jnp.sqrt(jnp.sum(qf * qf, axis=1, keepdims=True))
    bound = jnp.max(qn * kmax - (own - cqb))
    ck_all = ck_ref[0]
    pos = lax.broadcasted_iota(jnp.int32, ck_all.shape, 1)
    dead = jnp.where((pos < i * tq) & (ck_all > bound + SKIP_LOG2), 1.0, 0.0)
    j0 = jnp.sum(dead).astype(jnp.int32) // tq

    first_near = jnp.maximum(i - near, 0)
    near_w = near * tq
    near_ks = pl.multiple_of(first_near * tq, tq)
    near_k = k_ref[pl.ds(near_ks, near_w), :]
    near_v = v_ref[pl.ds(near_ks, near_w), :]
    near_key = near_ks + lax.broadcasted_iota(jnp.int32, (1, near_w), 1)
    near_ck = jnp.where(near_key < base, ck_ref[0, :, pl.ds(near_ks, near_w)], jnp.inf)
    t_near = _dot_nt(q_ref[...], near_k) - near_ck
    for hh, rs in enumerate(halves):
        nk = (hh + 1) * half
        qh = q_ref[rs, :]
        t_diag = _dot_nt(qh, k_ref[pl.ds(base, nk), :]) - ck_ref[0, :, pl.ds(base, nk)]
        row = lax.broadcasted_iota(jnp.int32, t_diag.shape, 0)
        col = lax.broadcasted_iota(jnp.int32, t_diag.shape, 1)
        t_diag = jnp.where(col <= row + hh * half, t_diag, -jnp.inf)
        t = jnp.concatenate([t_near[rs], t_diag], axis=1)
        m0 = jnp.max(t, axis=1, keepdims=True) + cqb[rs]
        p = jnp.exp2(t - _lane_tile(m0 - cqb[rs], (near_w + nk) // LANES)).astype(BF16)
        m_ref[rs, :] = m0
        acc = _dot(p[:, :near_w], near_v) + _dot(p[:, near_w:], v_ref[pl.ds(base, nk), :])
        acc_ref[rs, :] = acc
        o_ref[rs, :] = (acc[:, :HEAD_DIM] / acc[:, HEAD_DIM:]).astype(BF16)

    def body(j, carry):
        ks = pl.multiple_of(j * tq, tq)
        kb = k_ref[pl.ds(ks, tq), :]
        vb = v_ref[pl.ds(ks, tq), :]
        t_all = _dot_nt(q_ref[...], kb) - ck_ref[0, :, pl.ds(ks, tq)]
        for rs in halves:
            t = t_all[rs]
            m_prev = m_ref[rs, :]
            m_new = jnp.maximum(m_prev, jnp.max(t, axis=1, keepdims=True) + cqb[rs])
            alpha = jnp.exp2(m_prev - m_new)
            p = jnp.exp2(t - _lane_tile(m_new - cqb[rs], tq // LANES))
            acc_ref[rs, :] = _lane_tile(alpha, 2) * acc_ref[rs, :] + _dot(p.astype(BF16), vb)
            m_ref[rs, :] = m_new
        return carry

    @pl.when(j0 < first_near)
    def _():
        lax.fori_loop(j0, first_near, body, 0)
        acc = acc_ref[...]
        o_ref[...] = (acc[:, :HEAD_DIM] / acc[:, HEAD_DIM:]).astype(BF16)


def _attention(q, k, v_ones, cum_col, cum_row, k_norm_sq, w_cast, tq=512, near=2, splits=2):
    s = q.shape[0]
    nq = s // tq
    slab, w_bf16 = _cast_slab(w_cast, N_HEADS * nq)
    slab_spec = pl.BlockSpec((slab, w_cast.shape[1]), lambda h, i: (h * nq + i, 0))
    return pl.pallas_call(
        functools.partial(_attn_kernel, tq=tq, near=near, splits=splits),
        grid=(N_HEADS, nq),
        in_specs=[pl.BlockSpec((tq, HEAD_DIM), lambda h, i: (i, h)),
                  pl.BlockSpec((s, HEAD_DIM), lambda h, i: (0, h)),
                  pl.BlockSpec((s, 2 * HEAD_DIM), lambda h, i: (0, h)),
                  pl.BlockSpec((tq, LANES), lambda h, i: (i, 0)),
                  pl.BlockSpec((1, 1, s), lambda h, i: (h, 0, 0)),
                  pl.BlockSpec((N_HEADS, LANES), lambda h, i: (0, 0)),
                  slab_spec],
        out_specs=[pl.BlockSpec((tq, HEAD_DIM), lambda h, i: (i, h)), slab_spec],
        out_shape=[jax.ShapeDtypeStruct((s, N_HEADS * HEAD_DIM), BF16), w_bf16],
        scratch_shapes=[pltpu.VMEM((tq, LANES), F32), pltpu.VMEM((tq, 2 * HEAD_DIM), F32)],
        compiler_params=_cparams(("arbitrary", "arbitrary"), 40),
        name="fox_attention",
    )(q, k, v_ones, cum_col, cum_row, k_norm_sq, w_cast)


def _cmul(are, aim, bre, bim):
    return are * bre - aim * bim, are * bim + aim * bre


def _chunk_scan(src_ref, dst_ref, n, ars, ais, scratch):
    n_re = len(ars)
    if not scratch:
        ar = jnp.concatenate(ars, axis=1)
        ai = jnp.concatenate(ais, axis=1)

        def body(c, carry):
            xre, xim = carry
            for k in range(n_re):
                dst_ref[k, pl.ds(c, 1), :] = xre[:, k * LANES:(k + 1) * LANES]
                dst_ref[n_re + k, pl.ds(c, 1), :] = xim[:, k * LANES:(k + 1) * LANES]
            sre = jnp.concatenate([src_ref[k, pl.ds(c, 1), :] for k in range(n_re)], axis=1)
            sim = jnp.concatenate([src_ref[n_re + k, pl.ds(c, 1), :] for k in range(n_re)], axis=1)
            return ar * xre - ai * xim + sre, ar * xim + ai * xre + sim

        zero = jnp.zeros(ar.shape, F32)
        lax.fori_loop(0, n, body, (zero, zero), unroll=8)
        return
    (w_ref, e_ref), deeper = scratch[0], scratch[1:]
    half = n // 2
    even, odd = pl.ds(0, half, stride=2), pl.ds(1, half, stride=2)
    for k in range(n_re):
        se_re, se_im = src_ref[k, even, :], src_ref[n_re + k, even, :]
        w_ref[k] = ars[k] * se_re - ais[k] * se_im + src_ref[k, odd, :]
        w_ref[n_re + k] = ars[k] * se_im + ais[k] * se_re + src_ref[n_re + k, odd, :]
    _chunk_scan(w_ref, e_ref, half, [a * a - b * b for a, b in zip(ars, ais)],
                [2.0 * a * b for a, b in zip(ars, ais)], deeper)
    for k in range(n_re):
        e_re, e_im = e_ref[k], e_ref[n_re + k]
        dst_ref[k, even, :] = e_re
        dst_ref[n_re + k, even, :] = e_im
        dst_ref[k, odd, :] = ars[k] * e_re - ais[k] * e_im + src_ref[k, even, :]
        dst_ref[n_re + k, odd, :] = ars[k] * e_im + ais[k] * e_re + src_ref[n_re + k, even, :]


def _step_pair(u_ref, a, n_chunks):
    x0 = u_ref[pl.ds(2 * a, n_chunks, stride=SSM_T), :]
    x1 = u_ref[pl.ds(2 * a + 1, n_chunks, stride=SSM_T), :]
    return jnp.concatenate([x0, x1], axis=1).astype(BF16)


def _ssm_in_kernel(u_ref, ldt_ref, are_ref, aim_ref, btr_ref, bti_ref, cr_ref, ci_ref, d_ref,
                   wsrc_ref, w2_ref, pmt2_ref, x0_ref, wdst_ref,
                   cp_ref, wl_ref, q2_ref, v_ref, sw1_ref, se1_ref, sw2_ref, se2_ref):
    wdst_ref[...] = wsrc_ref[...].astype(BF16)
    n_chunks = u_ref.shape[0] // SSM_T
    half_w = SSM_TILE_G * SSM_STATE
    pw_parts = []
    lane = lax.broadcasted_iota(jnp.int32, (SSM_GROUP, LANES), 1)
    first = lane < SSM_STATE

    @pl.when(pl.program_id(0) == 0)
    def _():
        q2_ref[...] = jnp.zeros(q2_ref.shape, BF16)

    pmt2_ref[...] = jnp.zeros(pmt2_ref.shape, BF16)

    def place(ref, lead, step, gi, re_part, im_part):
        r0 = (step % 2) * LANES + gi * SSM_GROUP
        rows = slice(r0, r0 + SSM_GROUP)
        mine = first if gi % 2 == 0 else jnp.logical_not(first)
        c_re = (gi // 2) * LANES
        c_im = half_w + c_re
        ref[lead + (step // 2, rows, slice(c_re, c_re + LANES))] = (
            jnp.where(mine, re_part, 0.0).astype(BF16))
        ref[lead + (step // 2, rows, slice(c_im, c_im + LANES))] = (
            jnp.where(mine, im_part, 0.0).astype(BF16))

    dup = lambda p: jnp.concatenate([p, p], axis=-1)
    for gi in range(SSM_TILE_G):
        dt = jnp.exp(ldt_ref[gi])
        are, aim = dup(are_ref[gi]), dup(aim_ref[gi])
        mag = jnp.exp(dt * are)
        abre, abim = mag * jnp.cos(dt * aim), mag * jnp.sin(dt * aim)
        nre, nim = abre - 1.0, abim
        den = are * are + aim * aim
        zre = (nre * are + nim * aim) / den
        zim = (nim * are - nre * aim) / den
        bbre, bbim = _cmul(zre, zim, dup(btr_ref[gi]), dup(bti_ref[gi]))
        bbcat = jnp.where(first, bbre, bbim)

        cpre, cpim = dup(cr_ref[gi]), dup(ci_ref[gi])
        qre, qim = bbre, bbim
        pwre, pwim = jnp.ones_like(abre), jnp.zeros_like(abim)
        for t in range(SSM_T):
            cp_ref[t * SSM_GROUP:(t + 1) * SSM_GROUP, :] = jnp.where(first, cpre, -cpim)
            if t > 0:
                place(pmt2_ref, (0,), t - 1, gi, cpre, -cpim)
            place(q2_ref, (), SSM_T - 1 - t, gi, qre, qim)
            cpre, cpim = _cmul(cpre, cpim, abre, abim)
            qre, qim = _cmul(qre, qim, abre, abim)
            pwre, pwim = _cmul(pwre, pwim, abre, abim)
        place(pmt2_ref, (0,), SSM_T - 1, gi, cpre, -cpim)
        pw_parts.append((pwre, pwim))

        krow = lax.dot_general(bbcat, cp_ref[...], (((1,), (1,)), ((), ())),
                               preferred_element_type=F32, precision=lax.Precision.HIGHEST)
        own = (lane >= gi * SSM_GROUP) & (lane < (gi + 1) * SSM_GROUP)
        for tau in range(SSM_T):
            half = krow[:, (tau // SSM_TILE_G) * LANES:(tau // SSM_TILE_G + 1) * LANES]
            shift = ((gi - tau % SSM_TILE_G) * SSM_GROUP) % LANES
            moved = half if shift == 0 else pltpu.roll(half, shift, axis=1)
            wl_ref[tau, gi * SSM_GROUP:(gi + 1) * SSM_GROUP, :] = jnp.where(own, moved, 0.0)

    r = lax.broadcasted_iota(jnp.int32, (LANES, LANES), 0)
    c = lax.broadcasted_iota(jnp.int32, (LANES, LANES), 1)
    wl_ref[0] = wl_ref[0] + jnp.where(r == c, d_ref[0], 0.0)
    for dl in range(SSM_PAIRS):
        diag = wl_ref[2 * dl].astype(BF16)
        w2_ref[0, dl, 0:LANES, 0:LANES] = diag
        w2_ref[0, dl, LANES:, LANES:] = diag
        w2_ref[0, dl, 0:LANES, LANES:] = wl_ref[2 * dl + 1].astype(BF16)
        below = jnp.zeros((LANES, LANES), BF16) if dl == 0 else wl_ref[2 * dl - 1].astype(BF16)
        w2_ref[0, dl, LANES:, 0:LANES] = below

    v = _dot(_step_pair(u_ref, 0, n_chunks), q2_ref[0])
    for a in range(1, SSM_PAIRS):
        v = v + _dot(_step_pair(u_ref, a, n_chunks), q2_ref[a])
    n_re = half_w // LANES
    for k in range(2 * n_re):
        v_ref[k] = v[:, k * LANES:(k + 1) * LANES]

    pair = lambda k, part: jnp.where(first[0:1], pw_parts[2 * k][part], pw_parts[2 * k + 1][part])
    _chunk_scan(v_ref, x0_ref, n_chunks, [pair(k, 0) for k in range(n_re)],
                [pair(k, 1) for k in range(n_re)], [(sw1_ref, se1_ref), (sw2_ref, se2_ref)])


def _ssm_in(u, ldt, are2, aim2, bt_re2, bt_im2, c_re2, c_im2, d_rows, w_cast):
    s, width = u.shape
    n_tiles = width // LANES
    c = s // SSM_T
    kw = SSM_TILE_G * LANES
    per_tile = lambda *shape: pl.BlockSpec((SSM_TILE_G,) + shape,
                                           lambda j: (j,) + (0,) * len(shape))
    slab, w_bf16 = _cast_slab(w_cast, n_tiles)
    slab_spec = pl.BlockSpec((slab, w_cast.shape[1]), lambda j: (j, 0))
    return pl.pallas_call(
        _ssm_in_kernel,
        grid=(n_tiles,),
        in_specs=[pl.BlockSpec((s, LANES), lambda j: (0, j)),
                  per_tile(1, 1), per_tile(1, SSM_STATE), per_tile(1, SSM_STATE),
                  per_tile(SSM_GROUP, SSM_STATE), per_tile(SSM_GROUP, SSM_STATE),
                  per_tile(SSM_GROUP, SSM_STATE), per_tile(SSM_GROUP, SSM_STATE),
                  pl.BlockSpec((1, 1, LANES), lambda j: (j, 0, 0)), slab_spec],
        out_specs=[pl.BlockSpec((1, SSM_PAIRS, 2 * LANES, 2 * LANES), lambda j: (j, 0, 0, 0)),
                   pl.BlockSpec((1, SSM_PAIRS, 2 * LANES, kw), lambda j: (j, 0, 0, 0)),
                   pl.BlockSpec((SSM_TILE_G, c, LANES), lambda j: (j, 0, 0)), slab_spec],
        out_shape=[jax.ShapeDtypeStruct((n_tiles, SSM_PAIRS, 2 * LANES, 2 * LANES), BF16),
                   jax.ShapeDtypeStruct((n_tiles, SSM_PAIRS, 2 * LANES, kw), BF16),
                   jax.ShapeDtypeStruct((n_tiles * SSM_TILE_G, c, LANES), F32), w_bf16],
        scratch_shapes=[pltpu.VMEM((SSM_TW, LANES), F32),
                        pltpu.VMEM((SSM_T, LANES, LANES), F32),
                        pltpu.VMEM((SSM_PAIRS, 2 * LANES, kw), BF16),
                        pltpu.VMEM((SSM_TILE_G, c, LANES), F32),
                        pltpu.VMEM((SSM_TILE_G, c // 2, LANES), F32),
                        pltpu.VMEM((SSM_TILE_G, c // 2, LANES), F32),
                        pltpu.VMEM((SSM_TILE_G, c // 4, LANES), F32),
                        pltpu.VMEM((SSM_TILE_G, c // 4, LANES), F32)],
        compiler_params=_cparams(("arbitrary",), 60),
        name="ssm_state_in",
    )(u, ldt, are2, aim2, bt_re2, bt_im2, c_re2, c_im2, d_rows, w_cast)


def _ssm_out_kernel(u_ref, w2_ref, pmt2_ref, x0_ref, wsrc_a_ref, wsrc_b_ref,
                    y_ref, wdst_a_ref, wdst_b_ref):
    wdst_a_ref[...] = wsrc_a_ref[...].astype(BF16)
    wdst_b_ref[...] = wsrc_b_ref[...].astype(BF16)
    n_chunks = u_ref.shape[0] // SSM_T
    x0 = jnp.concatenate([x0_ref[k] for k in range(x0_ref.shape[0])], axis=1).astype(BF16)
    pairs = [_step_pair(u_ref, a, n_chunks) for a in range(SSM_PAIRS)]
    for b in range(SSM_PAIRS):
        acc = _dot_nt(x0, pmt2_ref[0, b])
        for a in range(b + 1):
            acc = acc + _dot(pairs[a], w2_ref[0, b - a])
        y_ref[pl.ds(2 * b, n_chunks, stride=SSM_T), :] = acc[:, :LANES]
        y_ref[pl.ds(2 * b + 1, n_chunks, stride=SSM_T), :] = acc[:, LANES:]


def _ssm_out(u, w2, pmt2, x0, w_cast_a, w_cast_b):
    s, width = u.shape
    n_tiles = width // LANES
    c = s // SSM_T
    kw = SSM_TILE_G * LANES
    slab_a, a_bf16 = _cast_slab(w_cast_a, n_tiles)
    slab_b, b_bf16 = _cast_slab(w_cast_b, n_tiles)
    spec_a = pl.BlockSpec((slab_a, w_cast_a.shape[1]), lambda j: (j, 0))
    spec_b = pl.BlockSpec((slab_b, w_cast_b.shape[1]), lambda j: (j, 0))
    return pl.pallas_call(
        _ssm_out_kernel,
        grid=(n_tiles,),
        in_specs=[pl.BlockSpec((s, LANES), lambda j: (0, j)),
                  pl.BlockSpec((1, SSM_PAIRS, 2 * LANES, 2 * LANES), lambda j: (j, 0, 0, 0)),
                  pl.BlockSpec((1, SSM_PAIRS, 2 * LANES, kw), lambda j: (j, 0, 0, 0)),
                  pl.BlockSpec((SSM_TILE_G, c, LANES), lambda j: (j, 0, 0)), spec_a, spec_b],
        out_specs=[pl.BlockSpec((s, LANES), lambda j: (0, j)), spec_a, spec_b],
        out_shape=[jax.ShapeDtypeStruct((s, width), F32), a_bf16, b_bf16],
        compiler_params=_cparams(("arbitrary",), 48),
        name="ssm_out",
    )(u, w2, pmt2, x0, w_cast_a, w_cast_b)


def _s5(u, a_re, a_im, log_dt, b_re, b_im, c_re, c_im, d_skip, w_cast_in, w_cast_a, w_cast_b):
    g, p = a_re.shape
    w2, pmt2, x0, w_in_bf16 = _ssm_in(
        u, log_dt.reshape(g, 1, 1), a_re.reshape(g, 1, p), a_im.reshape(g, 1, p),
        jnp.swapaxes(b_re, 1, 2), jnp.swapaxes(b_im, 1, 2), c_re, c_im,
        d_skip.reshape(-1, 1, LANES), w_cast_in)
    y, w_a_bf16, w_b_bf16 = _ssm_out(u, w2, pmt2, x0, w_cast_a, w_cast_b)
    return y, w_in_bf16, w_a_bf16, w_b_bf16


def _gelu_tanh(x):
    return 0.5 * x * (1.0 + jnp.tanh(math.sqrt(2.0 / math.pi) * (x + 0.044715 * (x * x * x))))


def _mixout_kernel(x_ref, attn_ref, y_ref, wglu_ref, bglu_ref, ga_ref, gs_ref, wo_ref,
                   gt_ref, gf_ref, sc_ref, sh_ref, h_ref, hn_ref):
    y = _gelu_tanh(y_ref[...])
    gate = jax.nn.sigmoid(_dot(y.astype(BF16), wglu_ref[...]) + bglu_ref[...])
    ns = _rms(y * gate, gs_ref[...]).astype(BF16)
    na = _rms(attn_ref[...].astype(F32), ga_ref[...]).astype(BF16)
    mixed = _dot(jnp.concatenate([na, ns], axis=1), wo_ref[...])
    h = x_ref[...] + gt_ref[...] * mixed
    h_ref[...] = h
    hn_ref[...] = (_rms(h, gf_ref[...] * (1.0 + sc_ref[...])) + sh_ref[...]).astype(BF16)


def _mixout(x, attn, y, w_glu, b_glu, g_attn, g_ssm, w_out, gt1, g_ffn, sc2, sh2, tm=512):
    s, d = x.shape
    w = attn.shape[1]
    rows = lambda c: pl.BlockSpec((tm, c), lambda i: (i, 0))
    return pl.pallas_call(
        _mixout_kernel,
        grid=(s // tm,),
        in_specs=[rows(d), rows(w), rows(w), _resident((w, w)), _resident((1, w)),
                  _resident((1, w)), _resident((1, w)), _resident(w_out.shape),
                  _resident((1, d)), _resident((1, d)), _resident((1, d)), _resident((1, d))],
        out_specs=[rows(d), rows(d)],
        out_shape=[jax.ShapeDtypeStruct((s, d), F32), jax.ShapeDtypeStruct((s, d), BF16)],
        compiler_params=_cparams(("arbitrary",), 56),
        name="mixer_out",
    )(x, attn, y, w_glu, b_glu, g_attn, g_ssm, w_out, gt1, g_ffn, sc2, sh2)


FFN_HALO = 16


def _ffn_kernel(hn_ref, halo_ref, h_hbm, wa_ref, wb_ref, cw_ref, cb_ref, wd_ref, gt_ref, gfin_ref,
                o_ref, h_buf, h_sem):
    i, j = pl.program_id(0), pl.program_id(1)
    tm = o_ref.shape[0]

    def residual_copy():
        return pltpu.make_async_copy(h_hbm.at[pl.ds(i * tm, tm), :], h_buf, h_sem)

    @pl.when(j == 0)
    def _():
        residual_copy().start()
        o_ref[...] = jnp.zeros(o_ref.shape, F32)

    hn = hn_ref[...]
    a_ext = _dot(jnp.concatenate([halo_ref[...], hn], axis=0), wa_ref[...])
    a = a_ext[FFN_HALO:]
    b = _dot(hn, wb_ref[...])
    halo = a_ext[FFN_HALO - 8:FFN_HALO] * (i > 0).astype(F32)
    row = lax.broadcasted_iota(jnp.int32, a.shape, 0)
    prev1 = jnp.where(row == 0, halo[7:8, :], pltpu.roll(a, 1, axis=0))
    prev2 = jnp.where(row == 0, halo[6:7, :],
                      jnp.where(row == 1, halo[7:8, :], pltpu.roll(a, 2, axis=0)))
    cw = cw_ref[...]
    conv = cb_ref[...] + cw[0:1, :] * prev2 + cw[1:2, :] * prev1 + cw[2:3, :] * a
    act = (conv * jax.nn.sigmoid(conv) * b).astype(BF16)
    o_ref[...] += _dot(act, wd_ref[...])

    @pl.when(j == pl.num_programs(1) - 1)
    def _():
        residual_copy().wait()
        h = h_buf[...] + gt_ref[...] * o_ref[...]
        o_ref[...] = _rms(h, gfin_ref[...])


def _ffn(hn, h, w_up, conv_w, conv_b, w_down, gt2, g_final, tm=1024, tn=512):
    s, d = h.shape
    d_ff = w_down.shape[0]
    nf = d_ff // tn
    halo_blocks = tm // FFN_HALO
    return pl.pallas_call(
        _ffn_kernel,
        grid=(s // tm, nf),
        in_specs=[pl.BlockSpec((tm, d), lambda i, j: (i, 0)),
                  pl.BlockSpec((FFN_HALO, d),
                               lambda i, j: (jnp.maximum(i * halo_blocks - 1, 0), 0)),
                  pl.BlockSpec(memory_space=pl.ANY),
                  pl.BlockSpec((d, tn), lambda i, j: (0, j)),
                  pl.BlockSpec((d, tn), lambda i, j: (0, nf + j)),
                  pl.BlockSpec((3, tn), lambda i, j: (0, j)),
                  pl.BlockSpec((1, tn), lambda i, j: (0, j)),
                  pl.BlockSpec((tn, d), lambda i, j: (j, 0)),
                  _resident((1, d)), _resident((1, d))],
        out_specs=pl.BlockSpec((tm, d), lambda i, j: (i, 0)),
        out_shape=jax.ShapeDtypeStruct((s, d), F32),
        scratch_shapes=[pltpu.VMEM((tm, d), F32), pltpu.SemaphoreType.DMA(())],
        compiler_params=_cparams(("arbitrary", "arbitrary"), 62),
        name="conv_ffn",
    )(hn, hn, h, w_up, w_up, conv_w, conv_b, w_down, gt2, g_final)


def _layer(h, mod, g_mix, w_in, b_f, a_re, a_im, log_dt, ssm_b_re, ssm_b_im, ssm_c_re, ssm_c_im,
           ssm_d, w_glu, b_glu, g_attn_out, g_ssm_out, w_out, g_ffn, w_up, conv_w, conv_b, w_down):
    s, d = h.shape
    aw = N_HEADS * HEAD_DIM
    sh1, sc1, gt1, sh2, sc2, gt2 = [mod[:, i * d:(i + 1) * d] for i in range(N_MOD)]
    row = lambda a: a.reshape(1, -1)

    q, k, v, u, f, k_norm_sq = _inproj(h, row(g_mix), sc1, sh1, w_in,
                                       w_in[:, 3 * aw + N_HEADS:])

    cum_row, cum_col = _forget_cumsum(f, b_f)
    attn, w_up_bf16 = _attention(q, k, v, cum_col, cum_row.reshape(N_HEADS, 1, s), k_norm_sq,
                                 w_up)

    y, w_down_bf16, w_out_bf16, w_glu_bf16 = _s5(
        u, a_re, a_im, log_dt, ssm_b_re, ssm_b_im, ssm_c_re, ssm_c_im, ssm_d,
        w_down, w_out, w_glu)

    h1, hn2 = _mixout(h, attn, y, w_glu_bf16, row(b_glu), row(g_attn_out),
                      row(g_ssm_out), w_out_bf16, gt1, row(g_ffn), sc2, sh2)
    return hn2, h1, (w_up_bf16, conv_w, row(conv_b), w_down_bf16, gt2)


def kernel(x, c, w_ada, b_ada, g_mix, w_in, b_f, a_re, a_im, log_dt, ssm_b_re, ssm_b_im, ssm_c_re,
           ssm_c_im, ssm_d, w_glu, b_glu, g_attn_out, g_ssm_out, w_out, g_ffn, w_up, conv_w,
           conv_b, w_down, g_final):
    batch, s, d = x.shape
    assert w_ada.shape[0] == 1, "only DEPTH == 1 is supported"
    l = 0
    outs = []
    for bi in range(batch):
        mod = _adaln(c[bi:bi + 1], w_ada[l], b_ada[l])
        hn2, h1, ffn_args = _layer(
            x[bi], mod, g_mix[l], w_in[l].astype(BF16), b_f[l], a_re[l], a_im[l], log_dt[l], ssm_b_re[l],
            ssm_b_im[l], ssm_c_re[l], ssm_c_im[l], ssm_d[l], w_glu[l], b_glu[l],
            g_attn_out[l], g_ssm_out[l], w_out[l], g_ffn[l], w_up[l], conv_w[l], conv_b[l],
            w_down[l])
        outs.append(_ffn(hn2, h1, *ffn_args, g_final.reshape(1, d)))
    return jnp.stack(outs, axis=0)
```

```python
import functools
import math

import jax
import jax.numpy as jnp
from jax import lax
from jax.experimental import pallas as pl
from jax.experimental.pallas import tpu as pltpu

F32 = jnp.float32
BF16 = jnp.bfloat16

EPS = 1e-6
HEAD_DIM = 128
N_HEADS = 8
SSM_GROUP = 16
SSM_STATE = 64
N_MOD = 6
LANES = 128
SSM_T = 16
SSM_TW = SSM_T * SSM_GROUP
SSM_PAIRS = SSM_T // 2
SSM_TILE_G = LANES // SSM_GROUP
LOG2E = 1.4426950408889634
SKIP_LOG2 = 151.0
NORM_SLACK = 1.01

_MIB = 1024 * 1024


def _cparams(semantics, vmem_mib):
    return pltpu.CompilerParams(dimension_semantics=semantics, vmem_limit_bytes=vmem_mib * _MIB)


def _resident(shape):
    return pl.BlockSpec(shape, lambda *_: (0,) * len(shape), pipeline_mode=pl.Buffered(1))


def _dot(a, b):
    return jnp.dot(a, b, preferred_element_type=F32)


def _dot_nt(a, b):
    return lax.dot_general(a, b, (((1,), (1,)), ((), ())), preferred_element_type=F32)


def _rms(x, g):
    return x * lax.rsqrt(jnp.mean(x * x, axis=-1, keepdims=True) + EPS) * g


def _lane_tile(x, reps):
    return jnp.concatenate([x] * reps, axis=1)


def _cast_slab(w, steps):
    rows = w.shape[0] // steps
    assert rows * steps == w.shape[0] and rows % 16 == 0, (w.shape, steps)
    return rows, jax.ShapeDtypeStruct(w.shape, BF16)


def _adaln_kernel(c_ref, w_ref, b_ref, o_ref):
    c = c_ref[...]
    cond = c * jax.nn.sigmoid(c)
    cond8 = jnp.broadcast_to(cond, (8, c.shape[1])).astype(BF16)
    acc = _dot(cond8, w_ref[...].astype(BF16))
    o_ref[...] = acc[0:1, :] + b_ref[...]


def _adaln(c, w, b, tn=1536):
    d, n = w.shape
    return pl.pallas_call(
        _adaln_kernel,
        grid=(n // tn,),
        in_specs=[pl.BlockSpec((1, d), lambda j: (0, 0)),
                  pl.BlockSpec((d, tn), lambda j: (0, j)),
                  pl.BlockSpec((1, tn), lambda j: (0, j))],
        out_specs=pl.BlockSpec((1, tn), lambda j: (0, j)),
        out_shape=jax.ShapeDtypeStruct((1, n), F32),
        compiler_params=_cparams(("arbitrary",), 48),
        name="adaln",
    )(c, w, b.reshape(1, n))


def _inproj_kernel(x_ref, g_ref, sc_ref, sh_ref, wq_ref, wk_ref, wv_ref, wu_ref, wf_ref,
                   q_ref, k_ref, v_ref, u_ref, f_ref, kn_ref, *, q_scale):
    hn = (_rms(x_ref[...], g_ref[...] * (1.0 + sc_ref[...])) + sh_ref[...]).astype(BF16)
    f_ref[...] = _dot(hn, wf_ref[...])
    q_ref[...] = (_dot(hn, wq_ref[...]) * q_scale).astype(BF16)
    k = _dot(hn, wk_ref[...]).astype(BF16)
    k_ref[...] = k
    u_ref[...] = _dot(hn, wu_ref[...])

    k32 = k.astype(F32)
    ksq = k32 * k32
    tile_max = jnp.concatenate(
        [jnp.broadcast_to(
            jnp.max(jnp.sum(ksq[:, h * HEAD_DIM:(h + 1) * HEAD_DIM], axis=1, keepdims=True),
                    axis=0, keepdims=True), (1, LANES)) for h in range(N_HEADS)], axis=0)

    @pl.when(pl.program_id(0) == 0)
    def _():
        kn_ref[...] = tile_max

    @pl.when(pl.program_id(0) > 0)
    def _():
        kn_ref[...] = jnp.maximum(kn_ref[...], tile_max)

    v = _dot(hn, wv_ref[...]).astype(BF16)
    ones = jnp.ones((v.shape[0], HEAD_DIM), BF16)
    for h in range(N_HEADS):
        v_ref[:, 2 * h * HEAD_DIM:(2 * h + 1) * HEAD_DIM] = v[:, h * HEAD_DIM:(h + 1) * HEAD_DIM]
        v_ref[:, (2 * h + 1) * HEAD_DIM:(2 * h + 2) * HEAD_DIM] = ones


def _inproj(x, g, sc, sh, w_in, wu, tm=512):
    s, d = x.shape
    aw, sw = N_HEADS * HEAD_DIM, wu.shape[1]
    rows = lambda c: pl.BlockSpec((tm, c), lambda i: (i, 0))
    cols = lambda width, blk: pl.BlockSpec((d, width), lambda i: (0, blk),
                                           pipeline_mode=pl.Buffered(1))
    return pl.pallas_call(
        functools.partial(_inproj_kernel, q_scale=HEAD_DIM ** -0.5 * LOG2E),
        grid=(s // tm,),
        in_specs=[rows(d), _resident((1, d)), _resident((1, d)), _resident((1, d)),
                  cols(aw, 0), cols(aw, 1), cols(aw, 2), _resident((d, sw)),
                  cols(LANES, 3 * aw // LANES)],
        out_specs=[rows(aw), rows(aw), rows(2 * aw), rows(sw), rows(LANES),
                   pl.BlockSpec((N_HEADS, LANES), lambda i: (0, 0))],
        out_shape=[jax.ShapeDtypeStruct((s, aw), BF16), jax.ShapeDtypeStruct((s, aw), BF16),
                   jax.ShapeDtypeStruct((s, 2 * aw), BF16), jax.ShapeDtypeStruct((s, sw), F32),
                   jax.ShapeDtypeStruct((s, LANES), F32),
                   jax.ShapeDtypeStruct((N_HEADS, LANES), F32)],
        compiler_params=_cparams(("arbitrary",), 56),
        name="inproj",
    )(x, g, sc, sh, w_in, w_in, w_in, wu, w_in)


def _cum_kernel(f_ref, brow_ref, row_ref, col_ref):
    n_heads, s_len = row_ref.shape
    z = (f_ref[...] + brow_ref[...]).T[0:n_heads, :]
    x = jnp.minimum(z, 0.0) - jnp.log1p(jnp.exp(-jnp.abs(z)))
    pos = lax.broadcasted_iota(jnp.int32, x.shape, 1)
    shift = 1
    while shift < s_len:
        x = x + jnp.where(pos >= shift, pltpu.roll(x, shift, axis=1), 0.0)
        shift *= 2
    x = x * LOG2E
    row_ref[...] = x
    pad = jnp.zeros((col_ref.shape[1] - n_heads, s_len), F32)
    col_ref[...] = jnp.concatenate([x, pad], axis=0).T


def _forget_cumsum(f, b_f):
    s, w = f.shape
    h = b_f.shape[0]
    return pl.pallas_call(
        _cum_kernel,
        out_shape=[jax.ShapeDtypeStruct((h, s), F32), jax.ShapeDtypeStruct((s, w), F32)],
        compiler_params=_cparams(None, 48),
        name="forget_cumsum",
    )(f, jnp.pad(b_f, (0, w - h)).reshape(1, w))


def _attn_kernel(q_ref, k_ref, v_ref, cq_ref, ck_ref, kn_ref, wsrc_ref, o_ref, wdst_ref,
                 m_ref, acc_ref, *, tq, near, splits, heads_per_step):
    wdst_ref[...] = wsrc_ref[...].astype(BF16)
    for hl in range(heads_per_step):
        _attn_head(hl, pl.program_id(0) * heads_per_step + hl, q_ref, k_ref, v_ref, cq_ref,
                   ck_ref, kn_ref, o_ref, m_ref, acc_ref, tq=tq, near=near, splits=splits)


def _attn_head(hl, head, q_ref, k_ref, v_ref, cq_ref, ck_ref, kn_ref, o_ref, m_ref, acc_ref, *,
               tq, near, splits):
    i = pl.program_id(1)
    qc = slice(hl * HEAD_DIM, (hl + 1) * HEAD_DIM)
    vc = slice(2 * hl * HEAD_DIM, 2 * (hl + 1) * HEAD_DIM)
    kmax = jnp.sqrt(kn_ref[pl.ds(head, 1), :]) * NORM_SLACK

    head_lane = lax.broadcasted_iota(jnp.int32, (tq, LANES), 1) == head
    cqb = jnp.broadcast_to(
        jnp.sum(jnp.where(head_lane, cq_ref[...], 0.0), axis=1, keepdims=True), (tq, LANES))

    half = tq // splits
    halves = [slice(hh * half, (hh + 1) * half) for hh in range(splits)]

    base = pl.multiple_of(i * tq, tq)
    qf = q_ref[:, qc].astype(F32)
    own = jnp.sum(qf * k_ref[pl.ds(base, tq), qc].astype(F32), axis=1, keepdims=True)
    qn = jnp.sqrt(jnp.sum(qf * qf, axis=1, keepdims=True))
    bound = jnp.max(qn * kmax - (own - cqb))
    ck_all = ck_ref[hl]
    pos = lax.broadcasted_iota(jnp.int32, ck_all.shape, 1)
    dead = jnp.where((pos < i * tq) & (ck_all > bound + SKIP_LOG2), 1.0, 0.0)
    j0 = jnp.sum(dead).astype(jnp.int32) // tq

    first_near = jnp.maximum(i - near, 0)
    near_w = near * tq
    near_ks = pl.multiple_of(first_near * tq, tq)
    near_k = k_ref[pl.ds(near_ks, near_w), qc]
    near_v = v_ref[pl.ds(near_ks, near_w), vc]
    near_key = near_ks + lax.broadcasted_iota(jnp.int32, (1, near_w), 1)
    near_ck = jnp.where(near_key < base, ck_ref[hl, :, pl.ds(near_ks, near_w)], jnp.inf)
    t_near = _dot_nt(q_ref[:, qc], near_k) - near_ck
    for hh, rs in enumerate(halves):
        nk = (hh + 1) * half
        qh = q_ref[rs, qc]
        t_diag = _dot_nt(qh, k_ref[pl.ds(base, nk), qc]) - ck_ref[hl, :, pl.ds(base, nk)]
        row = lax.broadcasted_iota(jnp.int32, t_diag.shape, 0)
        col = lax.broadcasted_iota(jnp.int32, t_diag.shape, 1)
        t_diag = jnp.where(col <= row + hh * half, t_diag, -jnp.inf)
        t = jnp.concatenate([t_near[rs], t_diag], axis=1)
        m0 = jnp.max(t, axis=1, keepdims=True) + cqb[rs]
        p = jnp.exp2(t - _lane_tile(m0 - cqb[rs], (near_w + nk) // LANES)).astype(BF16)
        m_ref[hl, rs, :] = m0
        acc = _dot(p[:, :near_w], near_v) + _dot(p[:, near_w:], v_ref[pl.ds(base, nk), vc])
        acc_ref[hl, rs, :] = acc
        o_ref[rs, qc] = (acc[:, :HEAD_DIM] / acc[:, HEAD_DIM:]).astype(BF16)

    def body(j, carry):
        ks = pl.multiple_of(j * tq, tq)
        kb = k_ref[pl.ds(ks, tq), qc]
        vb = v_ref[pl.ds(ks, tq), vc]
        t_all = _dot_nt(q_ref[:, qc], kb) - ck_ref[hl, :, pl.ds(ks, tq)]
        for rs in halves:
            t = t_all[rs]
            m_prev = m_ref[hl, rs, :]
            m_new = jnp.maximum(m_prev, jnp.max(t, axis=1, keepdims=True) + cqb[rs])
            alpha = jnp.exp2(m_prev - m_new)
            p = jnp.exp2(t - _lane_tile(m_new - cqb[rs], tq // LANES))
            acc_ref[hl, rs, :] = _lane_tile(alpha, 2) * acc_ref[hl, rs, :] + _dot(p.astype(BF16), vb)
            m_ref[hl, rs, :] = m_new
        return carry

    @pl.when(j0 < first_near)
    def _():
        lax.fori_loop(j0, first_near, body, 0)
        acc = acc_ref[hl]
        o_ref[:, qc] = (acc[:, :HEAD_DIM] / acc[:, HEAD_DIM:]).astype(BF16)


def _attention(q, k, v_ones, cum_col, cum_row, k_norm_sq, w_cast, tq=512, near=2, splits=2,
               hp=2):
    s = q.shape[0]
    nq = s // tq
    ng = N_HEADS // hp
    slab, w_bf16 = _cast_slab(w_cast, ng * nq)
    slab_spec = pl.BlockSpec((slab, w_cast.shape[1]), lambda h, i: (h * nq + i, 0))
    return pl.pallas_call(
        functools.partial(_attn_kernel, tq=tq, near=near, splits=splits, heads_per_step=hp),
        grid=(ng, nq),
        in_specs=[pl.BlockSpec((tq, hp * HEAD_DIM), lambda h, i: (i, h)),
                  pl.BlockSpec((s, hp * HEAD_DIM), lambda h, i: (0, h)),
                  pl.BlockSpec((s, 2 * hp * HEAD_DIM), lambda h, i: (0, h)),
                  pl.BlockSpec((tq, LANES), lambda h, i: (i, 0)),
                  pl.BlockSpec((hp, 1, s), lambda h, i: (h, 0, 0)),
                  pl.BlockSpec((N_HEADS, LANES), lambda h, i: (0, 0)),
                  slab_spec],
        out_specs=[pl.BlockSpec((tq, hp * HEAD_DIM), lambda h, i: (i, h)), slab_spec],
        out_shape=[jax.ShapeDtypeStruct((s, N_HEADS * HEAD_DIM), BF16), w_bf16],
        scratch_shapes=[pltpu.VMEM((hp, tq, LANES), F32),
                        pltpu.VMEM((hp, tq, 2 * HEAD_DIM), F32)],
        compiler_params=_cparams(("arbitrary", "arbitrary"), 56),
        name="fox_attention",
    )(q, k, v_ones, cum_col, cum_row, k_norm_sq, w_cast)


def _cmul(are, aim, bre, bim):
    return are * bre - aim * bim, are * bim + aim * bre


def _chunk_scan(src_ref, dst_ref, n, ars, ais, scratch):
    n_re = len(ars)
    if not scratch:
        ar = jnp.concatenate(ars, axis=1)
        ai = jnp.concatenate(ais, axis=1)

        def body(c, carry):
            xre, xim = carry
            for k in range(n_re):
                dst_ref[k, pl.ds(c, 1), :] = xre[:, k * LANES:(k + 1) * LANES]
                dst_ref[n_re + k, pl.ds(c, 1), :] = xim[:, k * LANES:(k + 1) * LANES]
            sre = jnp.concatenate([src_ref[k, pl.ds(c, 1), :] for k in range(n_re)], axis=1)
            sim = jnp.concatenate([src_ref[n_re + k, pl.ds(c, 1), :] for k in range(n_re)], axis=1)
            return ar * xre - ai * xim + sre, ar * xim + ai * xre + sim

        zero = jnp.zeros(ar.shape, F32)
        lax.fori_loop(0, n, body, (zero, zero), unroll=8)
        return
    (w_ref, e_ref), deeper = scratch[0], scratch[1:]
    half = n // 2
    even, odd = pl.ds(0, half, stride=2), pl.ds(1, half, stride=2)
    for k in range(n_re):
        se_re, se_im = src_ref[k, even, :], src_ref[n_re + k, even, :]
        w_ref[k] = ars[k] * se_re - ais[k] * se_im + src_ref[k, odd, :]
        w_ref[n_re + k] = ars[k] * se_im + ais[k] * se_re + src_ref[n_re + k, odd, :]
    _chunk_scan(w_ref, e_ref, half, [a * a - b * b for a, b in zip(ars, ais)],
                [2.0 * a * b for a, b in zip(ars, ais)], deeper)
    for k in range(n_re):
        e_re, e_im = e_ref[k], e_ref[n_re + k]
        dst_ref[k, even, :] = e_re
        dst_ref[n_re + k, even, :] = e_im
        dst_ref[k, odd, :] = ars[k] * e_re - ais[k] * e_im + src_ref[k, even, :]
        dst_ref[n_re + k, odd, :] = ars[k] * e_im + ais[k] * e_re + src_ref[n_re + k, even, :]


def _step_pair(u_ref, a, n_chunks):
    x0 = u_ref[pl.ds(2 * a, n_chunks, stride=SSM_T), :]
    x1 = u_ref[pl.ds(2 * a + 1, n_chunks, stride=SSM_T), :]
    return jnp.concatenate([x0, x1], axis=1).astype(BF16)


def _ssm_in_kernel(u_ref, ldt_ref, are_ref, aim_ref, btr_ref, bti_ref, cr_ref, ci_ref, d_ref,
                   wsrc_ref, w2_ref, pmt2_ref, x0_ref, wdst_ref,
                   cp_ref, wl_ref, q2_ref, v_ref, sw1_ref, se1_ref, sw2_ref, se2_ref):
    wdst_ref[...] = wsrc_ref[...].astype(BF16)
    n_chunks = u_ref.shape[0] // SSM_T
    half_w = SSM_TILE_G * SSM_STATE
    pw_parts = []
    lane = lax.broadcasted_iota(jnp.int32, (SSM_GROUP, LANES), 1)
    first = lane < SSM_STATE

    @pl.when(pl.program_id(0) == 0)
    def _():
        q2_ref[...] = jnp.zeros(q2_ref.shape, BF16)

    pmt2_ref[...] = jnp.zeros(pmt2_ref.shape, BF16)

    def place(ref, lead, step, gi, re_part, im_part):
        r0 = (step % 2) * LANES + gi * SSM_GROUP
        rows = slice(r0, r0 + SSM_GROUP)
        mine = first if gi % 2 == 0 else jnp.logical_not(first)
        c_re = (gi // 2) * LANES
        c_im = half_w + c_re
        ref[lead + (step // 2, rows, slice(c_re, c_re + LANES))] = (
            jnp.where(mine, re_part, 0.0).astype(BF16))
        ref[lead + (step // 2, rows, slice(c_im, c_im + LANES))] = (
            jnp.where(mine, im_part, 0.0).astype(BF16))

    dup = lambda p: jnp.concatenate([p, p], axis=-1)
    for gi in range(SSM_TILE_G):
        dt = jnp.exp(ldt_ref[gi])
        are, aim = dup(are_ref[gi]), dup(aim_ref[gi])
        mag = jnp.exp(dt * are)
        abre, abim = mag * jnp.cos(dt * aim), mag * jnp.sin(dt * aim)
        nre, nim = abre - 1.0, abim
        den = are * are + aim * aim
        zre = (nre * are + nim * aim) / den
        zim = (nim * are - nre * aim) / den
        bbre, bbim = _cmul(zre, zim, dup(btr_ref[gi]), dup(bti_ref[gi]))
        bbcat = jnp.where(first, bbre, bbim)

        cpre, cpim = dup(cr_ref[gi]), dup(ci_ref[gi])
        qre, qim = bbre, bbim
        pwre, pwim = jnp.ones_like(abre), jnp.zeros_like(abim)
        for t in range(SSM_T):
            cp_ref[t * SSM_GROUP:(t + 1) * SSM_GROUP, :] = jnp.where(first, cpre, -cpim)
            if t > 0:
                place(pmt2_ref, (0,), t - 1, gi, cpre, -cpim)
            place(q2_ref, (), SSM_T - 1 - t, gi, qre, qim)
            cpre, cpim = _cmul(cpre, cpim, abre, abim)
            qre, qim = _cmul(qre, qim, abre, abim)
            pwre, pwim = _cmul(pwre, pwim, abre, abim)
        place(pmt2_ref, (0,), SSM_T - 1, gi, cpre, -cpim)
        pw_parts.append((pwre, pwim))

        krow = lax.dot_general(bbcat, cp_ref[...], (((1,), (1,)), ((), ())),
                               preferred_element_type=F32, precision=lax.Precision.HIGHEST)
        own = (lane >= gi * SSM_GROUP) & (lane < (gi + 1) * SSM_GROUP)
        for tau in range(SSM_T):
            half = krow[:, (tau // SSM_TILE_G) * LANES:(tau // SSM_TILE_G + 1) * LANES]
            shift = ((gi - tau % SSM_TILE_G) * SSM_GROUP) % LANES
            moved = half if shift == 0 else pltpu.roll(half, shift, axis=1)
            wl_ref[tau, gi * SSM_GROUP:(gi + 1) * SSM_GROUP, :] = jnp.where(own, moved, 0.0)

    r = lax.broadcasted_iota(jnp.int32, (LANES, LANES), 0)
    c = lax.broadcasted_iota(jnp.int32, (LANES, LANES), 1)
    wl_ref[0] = wl_ref[0] + jnp.where(r == c, d_ref[0], 0.0)
    for dl in range(SSM_PAIRS):
        diag = wl_ref[2 * dl].astype(BF16)
        w2_ref[0, dl, 0:LANES, 0:LANES] = diag
        w2_ref[0, dl, LANES:, LANES:] = diag
        w2_ref[0, dl, 0:LANES, LANES:] = wl_ref[2 * dl + 1].astype(BF16)
        below = jnp.zeros((LANES, LANES), BF16) if dl == 0 else wl_ref[2 * dl - 1].astype(BF16)
        w2_ref[0, dl, LANES:, 0:LANES] = below

    v = _dot(_step_pair(u_ref, 0, n_chunks), q2_ref[0])
    for a in range(1, SSM_PAIRS):
        v = v + _dot(_step_pair(u_ref, a, n_chunks), q2_ref[a])
    n_re = half_w // LANES
    for k in range(2 * n_re):
        v_ref[k] = v[:, k * LANES:(k + 1) * LANES]

    pair = lambda k, part: jnp.where(first[0:1], pw_parts[2 * k][part], pw_parts[2 * k + 1][part])
    _chunk_scan(v_ref, x0_ref, n_chunks, [pair(k, 0) for k in range(n_re)],
                [pair(k, 1) for k in range(n_re)], [(sw1_ref, se1_ref), (sw2_ref, se2_ref)])


def _ssm_in(u, ldt, are2, aim2, bt_re2, bt_im2, c_re2, c_im2, d_rows, w_cast):
    s, width = u.shape
    n_tiles = width // LANES
    c = s // SSM_T
    kw = SSM_TILE_G * LANES
    per_tile = lambda *shape: pl.BlockSpec((SSM_TILE_G,) + shape,
                                           lambda j: (j,) + (0,) * len(shape))
    slab, w_bf16 = _cast_slab(w_cast, n_tiles)
    slab_spec = pl.BlockSpec((slab, w_cast.shape[1]), lambda j: (j, 0))
    return pl.pallas_call(
        _ssm_in_kernel,
        grid=(n_tiles,),
        in_specs=[pl.BlockSpec((s, LANES), lambda j: (0, j)),
                  per_tile(1, 1), per_tile(1, SSM_STATE), per_tile(1, SSM_STATE),
                  per_tile(SSM_GROUP, SSM_STATE), per_tile(SSM_GROUP, SSM_STATE),
                  per_tile(SSM_GROUP, SSM_STATE), per_tile(SSM_GROUP, SSM_STATE),
                  pl.BlockSpec((1, 1, LANES), lambda j: (j, 0, 0)), slab_spec],
        out_specs=[pl.BlockSpec((1, SSM_PAIRS, 2 * LANES, 2 * LANES), lambda j: (j, 0, 0, 0)),
                   pl.BlockSpec((1, SSM_PAIRS, 2 * LANES, kw), lambda j: (j, 0, 0, 0)),
                   pl.BlockSpec((SSM_TILE_G, c, LANES), lambda j: (j, 0, 0)), slab_spec],
        out_shape=[jax.ShapeDtypeStruct((n_tiles, SSM_PAIRS, 2 * LANES, 2 * LANES), BF16),
                   jax.ShapeDtypeStruct((n_tiles, SSM_PAIRS, 2 * LANES, kw), BF16),
                   jax.ShapeDtypeStruct((n_tiles * SSM_TILE_G, c, LANES), F32), w_bf16],
        scratch_shapes=[pltpu.VMEM((SSM_TW, LANES), F32),
                        pltpu.VMEM((SSM_T, LANES, LANES), F32),
                        pltpu.VMEM((SSM_PAIRS, 2 * LANES, kw), BF16),
                        pltpu.VMEM((SSM_TILE_G, c, LANES), F32),
                        pltpu.VMEM((SSM_TILE_G, c // 2, LANES), F32),
                        pltpu.VMEM((SSM_TILE_G, c // 2, LANES), F32),
                        pltpu.VMEM((SSM_TILE_G, c // 4, LANES), F32),
                        pltpu.VMEM((SSM_TILE_G, c // 4, LANES), F32)],
        compiler_params=_cparams(("arbitrary",), 60),
        name="ssm_state_in",
    )(u, ldt, are2, aim2, bt_re2, bt_im2, c_re2, c_im2, d_rows, w_cast)


def _ssm_out_kernel(u_ref, w2_ref, pmt2_ref, x0_ref, wsrc_a_ref, wsrc_b_ref,
                    y_ref, wdst_a_ref, wdst_b_ref):
    wdst_a_ref[...] = wsrc_a_ref[...].astype(BF16)
    wdst_b_ref[...] = wsrc_b_ref[...].astype(BF16)
    n_chunks = u_ref.shape[0] // SSM_T
    x0 = jnp.concatenate([x0_ref[k] for k in range(x0_ref.shape[0])], axis=1).astype(BF16)
    pairs = [_step_pair(u_ref, a, n_chunks) for a in range(SSM_PAIRS)]
    for b in range(SSM_PAIRS):
        acc = _dot_nt(x0, pmt2_ref[0, b])
        for a in range(b + 1):
            acc = acc + _dot(pairs[a], w2_ref[0, b - a])
        y_ref[pl.ds(2 * b, n_chunks, stride=SSM_T), :] = acc[:, :LANES]
        y_ref[pl.ds(2 * b + 1, n_chunks, stride=SSM_T), :] = acc[:, LANES:]


def _ssm_out(u, w2, pmt2, x0, w_cast_a, w_cast_b):
    s, width = u.shape
    n_tiles = width // LANES
    c = s // SSM_T
    kw = SSM_TILE_G * LANES
    slab_a, a_bf16 = _cast_slab(w_cast_a, n_tiles)
    slab_b, b_bf16 = _cast_slab(w_cast_b, n_tiles)
    spec_a = pl.BlockSpec((slab_a, w_cast_a.shape[1]), lambda j: (j, 0))
    spec_b = pl.BlockSpec((slab_b, w_cast_b.shape[1]), lambda j: (j, 0))
    return pl.pallas_call(
        _ssm_out_kernel,
        grid=(n_tiles,),
        in_specs=[pl.BlockSpec((s, LANES), lambda j: (0, j)),
                  pl.BlockSpec((1, SSM_PAIRS, 2 * LANES, 2 * LANES), lambda j: (j, 0, 0, 0)),
                  pl.BlockSpec((1, SSM_PAIRS, 2 * LANES, kw), lambda j: (j, 0, 0, 0)),
                  pl.BlockSpec((SSM_TILE_G, c, LANES), lambda j: (j, 0, 0)), spec_a, spec_b],
        out_specs=[pl.BlockSpec((s, LANES), lambda j: (0, j)), spec_a, spec_b],
        out_shape=[jax.ShapeDtypeStruct((s, width), F32), a_bf16, b_bf16],
        compiler_params=_cparams(("arbitrary",), 48),
        name="ssm_out",
    )(u, w2, pmt2, x0, w_cast_a, w_cast_b)


def _s5(u, a_re, a_im, log_dt, b_re, b_im, c_re, c_im, d_skip, w_cast_in, w_cast_a, w_cast_b):
    g, p = a_re.shape
    w2, pmt2, x0, w_in_bf16 = _ssm_in(
        u, log_dt.reshape(g, 1, 1), a_re.reshape(g, 1, p), a_im.reshape(g, 1, p),
        jnp.swapaxes(b_re, 1, 2), jnp.swapaxes(b_im, 1, 2), c_re, c_im,
        d_skip.reshape(-1, 1, LANES), w_cast_in)
    y, w_a_bf16, w_b_bf16 = _ssm_out(u, w2, pmt2, x0, w_cast_a, w_cast_b)
    return y, w_in_bf16, w_a_bf16, w_b_bf16


def _gelu_tanh(x):
    return 0.5 * x * (1.0 + jnp.tanh(math.sqrt(2.0 / math.pi) * (x + 0.044715 * (x * x * x))))


def _mixout_kernel(x_ref, attn_ref, y_ref, wglu_ref, bglu_ref, ga_ref, gs_ref, wo_ref,
                   gt_ref, gf_ref, sc_ref, sh_ref, h_ref, hn_ref):
    y = _gelu_tanh(y_ref[...])
    gate = jax.nn.sigmoid(_dot(y.astype(BF16), wglu_ref[...]) + bglu_ref[...])
    ns = _rms(y * gate, gs_ref[...]).astype(BF16)
    na = _rms(attn_ref[...].astype(F32), ga_ref[...]).astype(BF16)
    mixed = _dot(jnp.concatenate([na, ns], axis=1), wo_ref[...])
    h = x_ref[...] + gt_ref[...] * mixed
    h_ref[...] = h
    hn_ref[...] = (_rms(h, gf_ref[...] * (1.0 + sc_ref[...])) + sh_ref[...]).astype(BF16)


def _mixout(x, attn, y, w_glu, b_glu, g_attn, g_ssm, w_out, gt1, g_ffn, sc2, sh2, tm=512):
    s, d = x.shape
    w = attn.shape[1]
    rows = lambda c: pl.BlockSpec((tm, c), lambda i: (i, 0))
    return pl.pallas_call(
        _mixout_kernel,
        grid=(s // tm,),
        in_specs=[rows(d), rows(w), rows(w), _resident((w, w)), _resident((1, w)),
                  _resident((1, w)), _resident((1, w)), _resident(w_out.shape),
                  _resident((1, d)), _resident((1, d)), _resident((1, d)), _resident((1, d))],
        out_specs=[rows(d), rows(d)],
        out_shape=[jax.ShapeDtypeStruct((s, d), F32), jax.ShapeDtypeStruct((s, d), BF16)],
        compiler_params=_cparams(("arbitrary",), 56),
        name="mixer_out",
    )(x, attn, y, w_glu, b_glu, g_attn, g_ssm, w_out, gt1, g_ffn, sc2, sh2)


FFN_HALO = 16


def _ffn_kernel(hn_ref, halo_ref, h_hbm, wa_ref, wb_ref, cw_ref, cb_ref, wd_ref, gt_ref, gfin_ref,
                o_ref, h_buf, h_sem):
    i, j = pl.program_id(0), pl.program_id(1)
    tm = o_ref.shape[0]

    def residual_copy():
        return pltpu.make_async_copy(h_hbm.at[pl.ds(i * tm, tm), :], h_buf, h_sem)

    @pl.when(j == 0)
    def _():
        residual_copy().start()
        o_ref[...] = jnp.zeros(o_ref.shape, F32)

    hn = hn_ref[...]
    a_ext = _dot(jnp.concatenate([halo_ref[...], hn], axis=0), wa_ref[...])
    a = a_ext[FFN_HALO:]
    b = _dot(hn, wb_ref[...])
    halo = a_ext[FFN_HALO - 8:FFN_HALO] * (i > 0).astype(F32)
    row = lax.broadcasted_iota(jnp.int32, a.shape, 0)
    prev1 = jnp.where(row == 0, halo[7:8, :], pltpu.roll(a, 1, axis=0))
    prev2 = jnp.where(row == 0, halo[6:7, :],
                      jnp.where(row == 1, halo[7:8, :], pltpu.roll(a, 2, axis=0)))
    cw = cw_ref[...]
    conv = cb_ref[...] + cw[0:1, :] * prev2 + cw[1:2, :] * prev1 + cw[2:3, :] * a
    act = (conv * jax.nn.sigmoid(conv) * b).astype(BF16)
    o_ref[...] += _dot(act, wd_ref[...])

    @pl.when(j == pl.num_programs(1) - 1)
    def _():
        residual_copy().wait()
        h = h_buf[...] + gt_ref[...] * o_ref[...]
        o_ref[...] = _rms(h, gfin_ref[...])


def _ffn(hn, h, w_up, conv_w, conv_b, w_down, gt2, g_final, tm=1024, tn=512):
    s, d = h.shape
    d_ff = w_down.shape[0]
    nf = d_ff // tn
    halo_blocks = tm // FFN_HALO
    return pl.pallas_call(
        _ffn_kernel,
        grid=(s // tm, nf),
        in_specs=[pl.BlockSpec((tm, d), lambda i, j: (i, 0)),
                  pl.BlockSpec((FFN_HALO, d),
                               lambda i, j: (jnp.maximum(i * halo_blocks - 1, 0), 0)),
                  pl.BlockSpec(memory_space=pl.ANY),
                  pl.BlockSpec((d, tn), lambda i, j: (0, j)),
                  pl.BlockSpec((d, tn), lambda i, j: (0, nf + j)),
                  pl.BlockSpec((3, tn), lambda i, j: (0, j)),
                  pl.BlockSpec((1, tn), lambda i, j: (0, j)),
                  pl.BlockSpec((tn, d), lambda i, j: (j, 0)),
                  _resident((1, d)), _resident((1, d))],
        out_specs=pl.BlockSpec((tm, d), lambda i, j: (i, 0)),
        out_shape=jax.ShapeDtypeStruct((s, d), F32),
        scratch_shapes=[pltpu.VMEM((tm, d), F32), pltpu.SemaphoreType.DMA(())],
        compiler_params=_cparams(("arbitrary", "arbitrary"), 62),
        name="conv_ffn",
    )(hn, hn, h, w_up, w_up, conv_w, conv_b, w_down, gt2, g_final)


def _layer(h, mod, g_mix, w_in, b_f, a_re, a_im, log_dt, ssm_b_re, ssm_b_im, ssm_c_re, ssm_c_im,
           ssm_d, w_glu, b_glu, g_attn_out, g_ssm_out, w_out, g_ffn, w_up, conv_w, conv_b, w_down):
    s, d = h.shape
    aw = N_HEADS * HEAD_DIM
    sh1, sc1, gt1, sh2, sc2, gt2 = [mod[:, i * d:(i + 1) * d] for i in range(N_MOD)]
    row = lambda a: a.reshape(1, -1)

    q, k, v, u, f, k_norm_sq = _inproj(h, row(g_mix), sc1, sh1, w_in,
                                       w_in[:, 3 * aw + N_HEADS:])

    cum_row, cum_col = _forget_cumsum(f, b_f)
    attn, w_up_bf16 = _attention(q, k, v, cum_col, cum_row.reshape(N_HEADS, 1, s), k_norm_sq,
                                 w_up)

    y, w_down_bf16, w_out_bf16, w_glu_bf16 = _s5(
        u, a_re, a_im, log_dt, ssm_b_re, ssm_b_im, ssm_c_re, ssm_c_im, ssm_d,
        w_down, w_out, w_glu)

    h1, hn2 = _mixout(h, attn, y, w_glu_bf16, row(b_glu), row(g_attn_out),
                      row(g_ssm_out), w_out_bf16, gt1, row(g_ffn), sc2, sh2)
    return hn2, h1, (w_up_bf16, conv_w, row(conv_b), w_down_bf16, gt2)


def kernel(x, c, w_ada, b_ada, g_mix, w_in, b_f, a_re, a_im, log_dt, ssm_b_re, ssm_b_im, ssm_c_re,
           ssm_c_im, ssm_d, w_glu, b_glu, g_attn_out, g_ssm_out, w_out, g_ffn, w_up, conv_w,
           conv_b, w_down, g_final):
    batch, s, d = x.shape
    assert w_ada.shape[0] == 1, "only DEPTH == 1 is supported"
    l = 0
    outs = []
    for bi in range(batch):
        mod = _adaln(c[bi:bi + 1], w_ada[l], b_ada[l])
        hn2, h1, ffn_args = _layer(
            x[bi], mod, g_mix[l], w_in[l].astype(BF16), b_f[l], a_re[l], a_im[l], log_dt[l], ssm_b_re[l],
            ssm_b_im[l], ssm_c_re[l], ssm_c_im[l], ssm_d[l], w_glu[l], b_glu[l],
            g_attn_out[l], g_ssm_out[l], w_out[l], g_ffn[l], w_up[l], conv_w[l], conv_b[l],
            w_down[l])
        outs.append(_ffn(hn2, h1, *ffn_args, g_final.reshape(1, d)))
    return jnp.stack(outs, axis=0)
```

```python
import functools
import math

import jax
import jax.numpy as jnp
from jax import lax
from jax.experimental import pallas as pl
from jax.experimental.pallas import tpu as pltpu

F32 = jnp.float32
BF16 = jnp.bfloat16

EPS = 1e-6
HEAD_DIM = 128
N_HEADS = 8
SSM_GROUP = 16
SSM_STATE = 64
N_MOD = 6
LANES = 128
SSM_T = 16
SSM_TW = SSM_T * SSM_GROUP
SSM_PAIRS = SSM_T // 2
SSM_TILE_G = LANES // SSM_GROUP
LOG2E = 1.4426950408889634
SKIP_LOG2 = 151.0
NORM_SLACK = 1.01

_MIB = 1024 * 1024


def _cparams(semantics, vmem_mib):
    return pltpu.CompilerParams(dimension_semantics=semantics, vmem_limit_bytes=vmem_mib * _MIB)


def _resident(shape):
    return pl.BlockSpec(shape, lambda *_: (0,) * len(shape), pipeline_mode=pl.Buffered(1))


def _dot(a, b):
    return jnp.dot(a, b, preferred_element_type=F32)


def _dot_nt(a, b):
    return lax.dot_general(a, b, (((1,), (1,)), ((), ())), preferred_element_type=F32)


def _rms(x, g):
    return x * lax.rsqrt(jnp.mean(x * x, axis=-1, keepdims=True) + EPS) * g


def _lane_tile(x, reps):
    return jnp.concatenate([x] * reps, axis=1)


def _cast_slab(w, steps):
    rows = w.shape[0] // steps
    assert rows * steps == w.shape[0] and rows % 16 == 0, (w.shape, steps)
    return rows, jax.ShapeDtypeStruct(w.shape, BF16)


def _adaln_kernel(c_ref, wa_ref, wb_ref, b_ref, o_ref):
    c = c_ref[...]
    cond = c * jax.nn.sigmoid(c)
    cond8 = jnp.broadcast_to(cond, (8, c.shape[1])).astype(BF16)
    half = wa_ref.shape[0]
    acc = _dot(cond8[:, :half], wa_ref[...].astype(BF16))
    acc += _dot(cond8[:, half:], wb_ref[...].astype(BF16))
    o_ref[...] = acc[0:1, :] + b_ref[...]


def _adaln(c, w, b, tn=1536):
    d, n = w.shape
    return pl.pallas_call(
        _adaln_kernel,
        grid=(n // tn,),
        in_specs=[pl.BlockSpec((1, d), lambda j: (0, 0)),
                  pl.BlockSpec((d // 2, tn), lambda j: (0, j)),
                  pl.BlockSpec((d // 2, tn), lambda j: (1, j)),
                  pl.BlockSpec((1, tn), lambda j: (0, j))],
        out_specs=pl.BlockSpec((1, tn), lambda j: (0, j)),
        out_shape=jax.ShapeDtypeStruct((1, n), F32),
        compiler_params=_cparams(("arbitrary",), 48),
        name="adaln",
    )(c, w, w, b.reshape(1, n))


def _inproj_kernel(x_ref, g_ref, sc_ref, sh_ref, wq_ref, wk_ref, wv_ref, wu_ref, wf_ref,
                   q_ref, k_ref, v_ref, u_ref, f_ref, kn_ref, *, q_scale):
    hn = (_rms(x_ref[...], g_ref[...] * (1.0 + sc_ref[...])) + sh_ref[...]).astype(BF16)
    f_ref[...] = _dot(hn, wf_ref[...])
    q_ref[...] = (_dot(hn, wq_ref[...]) * q_scale).astype(BF16)
    k = _dot(hn, wk_ref[...]).astype(BF16)
    k_ref[...] = k
    u_ref[...] = _dot(hn, wu_ref[...])

    k32 = k.astype(F32)
    ksq = k32 * k32
    tile_max = jnp.concatenate(
        [jnp.broadcast_to(
            jnp.max(jnp.sum(ksq[:, h * HEAD_DIM:(h + 1) * HEAD_DIM], axis=1, keepdims=True),
                    axis=0, keepdims=True), (1, LANES)) for h in range(N_HEADS)], axis=0)

    @pl.when(pl.program_id(0) == 0)
    def _():
        kn_ref[...] = tile_max

    @pl.when(pl.program_id(0) > 0)
    def _():
        kn_ref[...] = jnp.maximum(kn_ref[...], tile_max)

    v = _dot(hn, wv_ref[...]).astype(BF16)
    ones = jnp.ones((v.shape[0], HEAD_DIM), BF16)
    for h in range(N_HEADS):
        v_ref[:, 2 * h * HEAD_DIM:(2 * h + 1) * HEAD_DIM] = v[:, h * HEAD_DIM:(h + 1) * HEAD_DIM]
        v_ref[:, (2 * h + 1) * HEAD_DIM:(2 * h + 2) * HEAD_DIM] = ones


def _inproj(x, g, sc, sh, w_in, wu, tm=512):
    s, d = x.shape
    aw, sw = N_HEADS * HEAD_DIM, wu.shape[1]
    rows = lambda c: pl.BlockSpec((tm, c), lambda i: (i, 0))
    cols = lambda width, blk: pl.BlockSpec((d, width), lambda i: (0, blk),
                                           pipeline_mode=pl.Buffered(1))
    return pl.pallas_call(
        functools.partial(_inproj_kernel, q_scale=HEAD_DIM ** -0.5 * LOG2E),
        grid=(s // tm,),
        in_specs=[rows(d), _resident((1, d)), _resident((1, d)), _resident((1, d)),
                  cols(aw, 0), cols(aw, 1), cols(aw, 2), _resident((d, sw)),
                  cols(LANES, 3 * aw // LANES)],
        out_specs=[rows(aw), rows(aw), rows(2 * aw), rows(sw), rows(LANES),
                   pl.BlockSpec((N_HEADS, LANES), lambda i: (0, 0))],
        out_shape=[jax.ShapeDtypeStruct((s, aw), BF16), jax.ShapeDtypeStruct((s, aw), BF16),
                   jax.ShapeDtypeStruct((s, 2 * aw), BF16), jax.ShapeDtypeStruct((s, sw), F32),
                   jax.ShapeDtypeStruct((s, LANES), F32),
                   jax.ShapeDtypeStruct((N_HEADS, LANES), F32)],
        compiler_params=_cparams(("arbitrary",), 56),
        name="inproj",
    )(x, g, sc, sh, w_in, w_in, w_in, wu, w_in)


def _cum_kernel(f_ref, brow_ref, row_ref, col_ref):
    n_heads, s_len = row_ref.shape
    z = (f_ref[...] + brow_ref[...]).T[0:n_heads, :]
    x = jnp.minimum(z, 0.0) - jnp.log1p(jnp.exp(-jnp.abs(z)))
    pos = lax.broadcasted_iota(jnp.int32, x.shape, 1)
    shift = 1
    while shift < s_len:
        x = x + jnp.where(pos >= shift, pltpu.roll(x, shift, axis=1), 0.0)
        shift *= 2
    x = x * LOG2E
    row_ref[...] = x
    pad = jnp.zeros((col_ref.shape[1] - n_heads, s_len), F32)
    col_ref[...] = jnp.concatenate([x, pad], axis=0).T


def _forget_cumsum(f, b_f):
    s, w = f.shape
    h = b_f.shape[0]
    return pl.pallas_call(
        _cum_kernel,
        out_shape=[jax.ShapeDtypeStruct((h, s), F32), jax.ShapeDtypeStruct((s, w), F32)],
        compiler_params=_cparams(None, 48),
        name="forget_cumsum",
    )(f, jnp.pad(b_f, (0, w - h)).reshape(1, w))


def _attn_kernel(q_ref, k_ref, v_ref, cq_ref, ck_ref, kn_ref, wsrc_ref, o_ref, wdst_ref,
                 m_ref, acc_ref, *, tq, near, splits, heads_per_step):
    wdst_ref[...] = wsrc_ref[...].astype(BF16)
    for hl in range(heads_per_step):
        _attn_head(hl, pl.program_id(0) * heads_per_step + hl, q_ref, k_ref, v_ref, cq_ref,
                   ck_ref, kn_ref, o_ref, m_ref, acc_ref, tq=tq, near=near, splits=splits)


def _attn_head(hl, head, q_ref, k_ref, v_ref, cq_ref, ck_ref, kn_ref, o_ref, m_ref, acc_ref, *,
               tq, near, splits):
    i = pl.program_id(1)
    qc = slice(hl * HEAD_DIM, (hl + 1) * HEAD_DIM)
    vc = slice(2 * hl * HEAD_DIM, 2 * (hl + 1) * HEAD_DIM)
    kmax = jnp.sqrt(kn_ref[pl.ds(head, 1), :]) * NORM_SLACK

    head_lane = lax.broadcasted_iota(jnp.int32, (tq, LANES), 1) == head
    cqb = jnp.broadcast_to(
        jnp.sum(jnp.where(head_lane, cq_ref[...], 0.0), axis=1, keepdims=True), (tq, LANES))

    half = tq // splits
    halves = [slice(hh * half, (hh + 1) * half) for hh in range(splits)]

    base = pl.multiple_of(i * tq, tq)
    qf = q_ref[:, qc].astype(F32)
    own = jnp.sum(qf * k_ref[pl.ds(base, tq), qc].astype(F32), axis=1, keepdims=True)
    qn = jnp.sqrt(jnp.sum(qf * qf, axis=1, keepdims=True))
    bound = jnp.max(qn * kmax - (own - cqb))
    ck_all = ck_ref[hl]
    pos = lax.broadcasted_iota(jnp.int32, ck_all.shape, 1)
    dead = jnp.where((pos < i * tq) & (ck_all > bound + SKIP_LOG2), 1.0, 0.0)
    j0 = jnp.sum(dead).astype(jnp.int32) // tq

    first_near = jnp.maximum(i - near, 0)
    near_w = near * tq
    near_ks = pl.multiple_of(first_near * tq, tq)
    near_k = k_ref[pl.ds(near_ks, near_w), qc]
    near_v = v_ref[pl.ds(near_ks, near_w), vc]
    near_key = near_ks + lax.broadcasted_iota(jnp.int32, (1, near_w), 1)
    near_ck = jnp.where(near_key < base, ck_ref[hl, :, pl.ds(near_ks, near_w)], jnp.inf)
    t_near = _dot_nt(q_ref[:, qc], near_k) - near_ck
    for hh, rs in enumerate(halves):
        nk = (hh + 1) * half
        qh = q_ref[rs, qc]
        t_diag = _dot_nt(qh, k_ref[pl.ds(base, nk), qc]) - ck_ref[hl, :, pl.ds(base, nk)]
        row = lax.broadcasted_iota(jnp.int32, t_diag.shape, 0)
        col = lax.broadcasted_iota(jnp.int32, t_diag.shape, 1)
        t_diag = jnp.where(col <= row + hh * half, t_diag, -jnp.inf)
        t = jnp.concatenate([t_near[rs], t_diag], axis=1)
        m0 = jnp.max(t, axis=1, keepdims=True) + cqb[rs]
        p = jnp.exp2(t - _lane_tile(m0 - cqb[rs], (near_w + nk) // LANES)).astype(BF16)
        m_ref[hl, rs, :] = m0
        acc = _dot(p[:, :near_w], near_v) + _dot(p[:, near_w:], v_ref[pl.ds(base, nk), vc])
        acc_ref[hl, rs, :] = acc
        o_ref[rs, qc] = (acc[:, :HEAD_DIM] / acc[:, HEAD_DIM:]).astype(BF16)

    def body(j, carry):
        ks = pl.multiple_of(j * tq, tq)
        kb = k_ref[pl.ds(ks, tq), qc]
        vb = v_ref[pl.ds(ks, tq), vc]
        t_all = _dot_nt(q_ref[:, qc], kb) - ck_ref[hl, :, pl.ds(ks, tq)]
        for rs in halves:
            t = t_all[rs]
            m_prev = m_ref[hl, rs, :]
            m_new = jnp.maximum(m_prev, jnp.max(t, axis=1, keepdims=True) + cqb[rs])
            alpha = jnp.exp2(m_prev - m_new)
            p = jnp.exp2(t - _lane_tile(m_new - cqb[rs], tq // LANES))
            acc_ref[hl, rs, :] = _lane_tile(alpha, 2) * acc_ref[hl, rs, :] + _dot(p.astype(BF16), vb)
            m_ref[hl, rs, :] = m_new
        return carry

    @pl.when(j0 < first_near)
    def _():
        lax.fori_loop(j0, first_near, body, 0)
        acc = acc_ref[hl]
        o_ref[:, qc] = (acc[:, :HEAD_DIM] / acc[:, HEAD_DIM:]).astype(BF16)


def _attention(q, k, v_ones, cum_col, cum_row, k_norm_sq, w_cast, tq=512, near=2, splits=2,
               hp=2):
    s = q.shape[0]
    nq = s // tq
    ng = N_HEADS // hp
    slab, w_bf16 = _cast_slab(w_cast, ng * nq)
    slab_spec = pl.BlockSpec((slab, w_cast.shape[1]), lambda h, i: (h * nq + i, 0))
    return pl.pallas_call(
        functools.partial(_attn_kernel, tq=tq, near=near, splits=splits, heads_per_step=hp),
        grid=(ng, nq),
        in_specs=[pl.BlockSpec((tq, hp * HEAD_DIM), lambda h, i: (i, h)),
                  pl.BlockSpec((s, hp * HEAD_DIM), lambda h, i: (0, h)),
                  pl.BlockSpec((s, 2 * hp * HEAD_DIM), lambda h, i: (0, h)),
                  pl.BlockSpec((tq, LANES), lambda h, i: (i, 0)),
                  pl.BlockSpec((hp, 1, s), lambda h, i: (h, 0, 0)),
                  pl.BlockSpec((N_HEADS, LANES), lambda h, i: (0, 0)),
                  slab_spec],
        out_specs=[pl.BlockSpec((tq, hp * HEAD_DIM), lambda h, i: (i, h)), slab_spec],
        out_shape=[jax.ShapeDtypeStruct((s, N_HEADS * HEAD_DIM), BF16), w_bf16],
        scratch_shapes=[pltpu.VMEM((hp, tq, LANES), F32),
                        pltpu.VMEM((hp, tq, 2 * HEAD_DIM), F32)],
        compiler_params=_cparams(("arbitrary", "arbitrary"), 56),
        name="fox_attention",
    )(q, k, v_ones, cum_col, cum_row, k_norm_sq, w_cast)


def _cmul(are, aim, bre, bim):
    return are * bre - aim * bim, are * bim + aim * bre


def _chunk_scan(src_ref, dst_ref, n, ars, ais, scratch):
    n_re = len(ars)
    if not scratch:
        ar = jnp.concatenate(ars, axis=1)
        ai = jnp.concatenate(ais, axis=1)

        def body(c, carry):
            xre, xim = carry
            for k in range(n_re):
                dst_ref[k, pl.ds(c, 1), :] = xre[:, k * LANES:(k + 1) * LANES]
                dst_ref[n_re + k, pl.ds(c, 1), :] = xim[:, k * LANES:(k + 1) * LANES]
            sre = jnp.concatenate([src_ref[k, pl.ds(c, 1), :] for k in range(n_re)], axis=1)
            sim = jnp.concatenate([src_ref[n_re + k, pl.ds(c, 1), :] for k in range(n_re)], axis=1)
            return ar * xre - ai * xim + sre, ar * xim + ai * xre + sim

        zero = jnp.zeros(ar.shape, F32)
        lax.fori_loop(0, n, body, (zero, zero), unroll=8)
        return
    (w_ref, e_ref), deeper = scratch[0], scratch[1:]
    half = n // 2
    even, odd = pl.ds(0, half, stride=2), pl.ds(1, half, stride=2)
    for k in range(n_re):
        se_re, se_im = src_ref[k, even, :], src_ref[n_re + k, even, :]
        w_ref[k] = ars[k] * se_re - ais[k] * se_im + src_ref[k, odd, :]
        w_ref[n_re + k] = ars[k] * se_im + ais[k] * se_re + src_ref[n_re + k, odd, :]
    _chunk_scan(w_ref, e_ref, half, [a * a - b * b for a, b in zip(ars, ais)],
                [2.0 * a * b for a, b in zip(ars, ais)], deeper)
    for k in range(n_re):
        e_re, e_im = e_ref[k], e_ref[n_re + k]
        dst_ref[k, even, :] = e_re
        dst_ref[n_re + k, even, :] = e_im
        dst_ref[k, odd, :] = ars[k] * e_re - ais[k] * e_im + src_ref[k, even, :]
        dst_ref[n_re + k, odd, :] = ars[k] * e_im + ais[k] * e_re + src_ref[n_re + k, even, :]


def _step_pair(u_ref, a, n_chunks):
    x0 = u_ref[pl.ds(2 * a, n_chunks, stride=SSM_T), :]
    x1 = u_ref[pl.ds(2 * a + 1, n_chunks, stride=SSM_T), :]
    return jnp.concatenate([x0, x1], axis=1).astype(BF16)


def _ssm_in_kernel(u_ref, ldt_ref, are_ref, aim_ref, btr_ref, bti_ref, cr_ref, ci_ref, d_ref,
                   wsrc_ref, w2_ref, pmt2_ref, x0_ref, wdst_ref,
                   cp_ref, wl_ref, q2_ref, v_ref, sw1_ref, se1_ref, sw2_ref, se2_ref):
    wdst_ref[...] = wsrc_ref[...].astype(BF16)
    n_chunks = u_ref.shape[0] // SSM_T
    half_w = SSM_TILE_G * SSM_STATE
    pw_parts = []
    lane = lax.broadcasted_iota(jnp.int32, (SSM_GROUP, LANES), 1)
    first = lane < SSM_STATE

    @pl.when(pl.program_id(0) == 0)
    def _():
        q2_ref[...] = jnp.zeros(q2_ref.shape, BF16)

    pmt2_ref[...] = jnp.zeros(pmt2_ref.shape, BF16)

    def place(ref, lead, step, gi, re_part, im_part):
        r0 = (step % 2) * LANES + gi * SSM_GROUP
        rows = slice(r0, r0 + SSM_GROUP)
        mine = first if gi % 2 == 0 else jnp.logical_not(first)
        c_re = (gi // 2) * LANES
        c_im = half_w + c_re
        ref[lead + (step // 2, rows, slice(c_re, c_re + LANES))] = (
            jnp.where(mine, re_part, 0.0).astype(BF16))
        ref[lead + (step // 2, rows, slice(c_im, c_im + LANES))] = (
            jnp.where(mine, im_part, 0.0).astype(BF16))

    dup = lambda p: jnp.concatenate([p, p], axis=-1)
    for gi in range(SSM_TILE_G):
        dt = jnp.exp(ldt_ref[gi])
        are, aim = dup(are_ref[gi]), dup(aim_ref[gi])
        mag = jnp.exp(dt * are)
        abre, abim = mag * jnp.cos(dt * aim), mag * jnp.sin(dt * aim)
        nre, nim = abre - 1.0, abim
        den = are * are + aim * aim
        zre = (nre * are + nim * aim) / den
        zim = (nim * are - nre * aim) / den
        bbre, bbim = _cmul(zre, zim, dup(btr_ref[gi]), dup(bti_ref[gi]))
        bbcat = jnp.where(first, bbre, bbim)

        cpre, cpim = dup(cr_ref[gi]), dup(ci_ref[gi])
        qre, qim = bbre, bbim
        pwre, pwim = jnp.ones_like(abre), jnp.zeros_like(abim)
        for t in range(SSM_T):
            cp_ref[t * SSM_GROUP:(t + 1) * SSM_GROUP, :] = jnp.where(first, cpre, -cpim)
            if t > 0:
                place(pmt2_ref, (0,), t - 1, gi, cpre, -cpim)
            place(q2_ref, (), SSM_T - 1 - t, gi, qre, qim)
            cpre, cpim = _cmul(cpre, cpim, abre, abim)
            qre, qim = _cmul(qre, qim, abre, abim)
            pwre, pwim = _cmul(pwre, pwim, abre, abim)
        place(pmt2_ref, (0,), SSM_T - 1, gi, cpre, -cpim)
        pw_parts.append((pwre, pwim))

        krow = lax.dot_general(bbcat, cp_ref[...], (((1,), (1,)), ((), ())),
                               preferred_element_type=F32, precision=lax.Precision.HIGHEST)
        own = (lane >= gi * SSM_GROUP) & (lane < (gi + 1) * SSM_GROUP)
        for tau in range(SSM_T):
            half = krow[:, (tau // SSM_TILE_G) * LANES:(tau // SSM_TILE_G + 1) * LANES]
            shift = ((gi - tau % SSM_TILE_G) * SSM_GROUP) % LANES
            moved = half if shift == 0 else pltpu.roll(half, shift, axis=1)
            wl_ref[tau, gi * SSM_GROUP:(gi + 1) * SSM_GROUP, :] = jnp.where(own, moved, 0.0)

    r = lax.broadcasted_iota(jnp.int32, (LANES, LANES), 0)
    c = lax.broadcasted_iota(jnp.int32, (LANES, LANES), 1)
    wl_ref[0] = wl_ref[0] + jnp.where(r == c, d_ref[0], 0.0)
    for dl in range(SSM_PAIRS):
        diag = wl_ref[2 * dl].astype(BF16)
        w2_ref[0, dl, 0:LANES, 0:LANES] = diag
        w2_ref[0, dl, LANES:, LANES:] = diag
        w2_ref[0, dl, 0:LANES, LANES:] = wl_ref[2 * dl + 1].astype(BF16)
        below = jnp.zeros((LANES, LANES), BF16) if dl == 0 else wl_ref[2 * dl - 1].astype(BF16)
        w2_ref[0, dl, LANES:, 0:LANES] = below

    v = _dot(_step_pair(u_ref, 0, n_chunks), q2_ref[0])
    for a in range(1, SSM_PAIRS):
        v = v + _dot(_step_pair(u_ref, a, n_chunks), q2_ref[a])
    n_re = half_w // LANES
    for k in range(2 * n_re):
        v_ref[k] = v[:, k * LANES:(k + 1) * LANES]

    pair = lambda k, part: jnp.where(first[0:1], pw_parts[2 * k][part], pw_parts[2 * k + 1][part])
    _chunk_scan(v_ref, x0_ref, n_chunks, [pair(k, 0) for k in range(n_re)],
                [pair(k, 1) for k in range(n_re)], [(sw1_ref, se1_ref), (sw2_ref, se2_ref)])


def _ssm_in(u, ldt, are2, aim2, bt_re2, bt_im2, c_re2, c_im2, d_rows, w_cast):
    s, width = u.shape
    n_tiles = width // LANES
    c = s // SSM_T
    kw = SSM_TILE_G * LANES
    per_tile = lambda *shape: pl.BlockSpec((SSM_TILE_G,) + shape,
                                           lambda j: (j,) + (0,) * len(shape))
    slab, w_bf16 = _cast_slab(w_cast, n_tiles)
    slab_spec = pl.BlockSpec((slab, w_cast.shape[1]), lambda j: (j, 0))
    return pl.pallas_call(
        _ssm_in_kernel,
        grid=(n_tiles,),
        in_specs=[pl.BlockSpec((s, LANES), lambda j: (0, j)),
                  per_tile(1, 1), per_tile(1, SSM_STATE), per_tile(1, SSM_STATE),
                  per_tile(SSM_GROUP, SSM_STATE), per_tile(SSM_GROUP, SSM_STATE),
                  per_tile(SSM_GROUP, SSM_STATE), per_tile(SSM_GROUP, SSM_STATE),
                  pl.BlockSpec((1, 1, LANES), lambda j: (j, 0, 0)), slab_spec],
        out_specs=[pl.BlockSpec((1, SSM_PAIRS, 2 * LANES, 2 * LANES), lambda j: (j, 0, 0, 0)),
                   pl.BlockSpec((1, SSM_PAIRS, 2 * LANES, kw), lambda j: (j, 0, 0, 0)),
                   pl.BlockSpec((SSM_TILE_G, c, LANES), lambda j: (j, 0, 0)), slab_spec],
        out_shape=[jax.ShapeDtypeStruct((n_tiles, SSM_PAIRS, 2 * LANES, 2 * LANES), BF16),
                   jax.ShapeDtypeStruct((n_tiles, SSM_PAIRS, 2 * LANES, kw), BF16),
                   jax.ShapeDtypeStruct((n_tiles * SSM_TILE_G, c, LANES), F32), w_bf16],
        scratch_shapes=[pltpu.VMEM((SSM_TW, LANES), F32),
                        pltpu.VMEM((SSM_T, LANES, LANES), F32),
                        pltpu.VMEM((SSM_PAIRS, 2 * LANES, kw), BF16),
                        pltpu.VMEM((SSM_TILE_G, c, LANES), F32),
                        pltpu.VMEM((SSM_TILE_G, c // 2, LANES), F32),
                        pltpu.VMEM((SSM_TILE_G, c // 2, LANES), F32),
                        pltpu.VMEM((SSM_TILE_G, c // 4, LANES), F32),
                        pltpu.VMEM((SSM_TILE_G, c // 4, LANES), F32)],
        compiler_params=_cparams(("arbitrary",), 60),
        name="ssm_state_in",
    )(u, ldt, are2, aim2, bt_re2, bt_im2, c_re2, c_im2, d_rows, w_cast)


def _ssm_out_kernel(u_ref, w2_ref, pmt2_ref, x0_ref, wsrc_a_ref, wsrc_b_ref,
                    y_ref, wdst_a_ref, wdst_b_ref):
    wdst_a_ref[...] = wsrc_a_ref[...].astype(BF16)
    wdst_b_ref[...] = wsrc_b_ref[...].astype(BF16)
    n_chunks = u_ref.shape[0] // SSM_T
    x0 = jnp.concatenate([x0_ref[k] for k in range(x0_ref.shape[0])], axis=1).astype(BF16)
    pairs = [_step_pair(u_ref, a, n_chunks) for a in range(SSM_PAIRS)]
    for b in range(SSM_PAIRS):
        acc = _dot_nt(x0, pmt2_ref[0, b])
        for a in range(b + 1):
            acc = acc + _dot(pairs[a], w2_ref[0, b - a])
        y_ref[pl.ds(2 * b, n_chunks, stride=SSM_T), :] = acc[:, :LANES]
        y_ref[pl.ds(2 * b + 1, n_chunks, stride=SSM_T), :] = acc[:, LANES:]


def _ssm_out(u, w2, pmt2, x0, w_cast_a, w_cast_b):
    s, width = u.shape
    n_tiles = width // LANES
    c = s // SSM_T
    kw = SSM_TILE_G * LANES
    slab_a, a_bf16 = _cast_slab(w_cast_a, n_tiles)
    slab_b, b_bf16 = _cast_slab(w_cast_b, n_tiles)
    spec_a = pl.BlockSpec((slab_a, w_cast_a.shape[1]), lambda j: (j, 0))
    spec_b = pl.BlockSpec((slab_b, w_cast_b.shape[1]), lambda j: (j, 0))
    return pl.pallas_call(
        _ssm_out_kernel,
        grid=(n_tiles,),
        in_specs=[pl.BlockSpec((s, LANES), lambda j: (0, j)),
                  pl.BlockSpec((1, SSM_PAIRS, 2 * LANES, 2 * LANES), lambda j: (j, 0, 0, 0)),
                  pl.BlockSpec((1, SSM_PAIRS, 2 * LANES, kw), lambda j: (j, 0, 0, 0)),
                  pl.BlockSpec((SSM_TILE_G, c, LANES), lambda j: (j, 0, 0)), spec_a, spec_b],
        out_specs=[pl.BlockSpec((s, LANES), lambda j: (0, j)), spec_a, spec_b],
        out_shape=[jax.ShapeDtypeStruct((s, width), F32), a_bf16, b_bf16],
        compiler_params=_cparams(("arbitrary",), 48),
        name="ssm_out",
    )(u, w2, pmt2, x0, w_cast_a, w_cast_b)


def _s5(u, a_re, a_im, log_dt, b_re, b_im, c_re, c_im, d_skip, w_cast_in, w_cast_a, w_cast_b):
    g, p = a_re.shape
    w2, pmt2, x0, w_in_bf16 = _ssm_in(
        u, log_dt.reshape(g, 1, 1), a_re.reshape(g, 1, p), a_im.reshape(g, 1, p),
        jnp.swapaxes(b_re, 1, 2), jnp.swapaxes(b_im, 1, 2), c_re, c_im,
        d_skip.reshape(-1, 1, LANES), w_cast_in)
    y, w_a_bf16, w_b_bf16 = _ssm_out(u, w2, pmt2, x0, w_cast_a, w_cast_b)
    return y, w_in_bf16, w_a_bf16, w_b_bf16


def _gelu_tanh(x):
    return 0.5 * x * (1.0 + jnp.tanh(math.sqrt(2.0 / math.pi) * (x + 0.044715 * (x * x * x))))


def _mixout_kernel(x_ref, attn_ref, y_ref, wglu_ref, bglu_ref, ga_ref, gs_ref, wo_ref,
                   gt_ref, gf_ref, sc_ref, sh_ref, h_ref, hn_ref):
    y = _gelu_tanh(y_ref[...])
    gate = jax.nn.sigmoid(_dot(y.astype(BF16), wglu_ref[...]) + bglu_ref[...])
    ns = _rms(y * gate, gs_ref[...]).astype(BF16)
    na = _rms(attn_ref[...].astype(F32), ga_ref[...]).astype(BF16)
    mixed = _dot(jnp.concatenate([na, ns], axis=1), wo_ref[...])
    h = x_ref[...] + gt_ref[...] * mixed
    h_ref[...] = h
    hn_ref[...] = (_rms(h, gf_ref[...] * (1.0 + sc_ref[...])) + sh_ref[...]).astype(BF16)


def _mixout(x, attn, y, w_glu, b_glu, g_attn, g_ssm, w_out, gt1, g_ffn, sc2, sh2, tm=512):
    s, d = x.shape
    w = attn.shape[1]
    rows = lambda c: pl.BlockSpec((tm, c), lambda i: (i, 0))
    return pl.pallas_call(
        _mixout_kernel,
        grid=(s // tm,),
        in_specs=[rows(d), rows(w), rows(w), _resident((w, w)), _resident((1, w)),
                  _resident((1, w)), _resident((1, w)), _resident(w_out.shape),
                  _resident((1, d)), _resident((1, d)), _resident((1, d)), _resident((1, d))],
        out_specs=[rows(d), rows(d)],
        out_shape=[jax.ShapeDtypeStruct((s, d), F32), jax.ShapeDtypeStruct((s, d), BF16)],
        compiler_params=_cparams(("arbitrary",), 56),
        name="mixer_out",
    )(x, attn, y, w_glu, b_glu, g_attn, g_ssm, w_out, gt1, g_ffn, sc2, sh2)


FFN_HALO = 16


def _ffn_kernel(hn_ref, halo_ref, h_hbm, wa_ref, wb_ref, cw_ref, cb_ref, wd_ref, gt_ref, gfin_ref,
                o_ref, h_buf, h_sem):
    i, j = pl.program_id(0), pl.program_id(1)
    tm = o_ref.shape[0]

    def residual_copy():
        return pltpu.make_async_copy(h_hbm.at[pl.ds(i * tm, tm), :], h_buf, h_sem)

    @pl.when(j == 0)
    def _():
        residual_copy().start()
        o_ref[...] = jnp.zeros(o_ref.shape, F32)

    hn = hn_ref[...]
    a_ext = _dot(jnp.concatenate([halo_ref[...], hn], axis=0), wa_ref[...])
    a = a_ext[FFN_HALO:]
    b = _dot(hn, wb_ref[...])
    halo = a_ext[FFN_HALO - 8:FFN_HALO] * (i > 0).astype(F32)
    row = lax.broadcasted_iota(jnp.int32, a.shape, 0)
    prev1 = jnp.where(row == 0, halo[7:8, :], pltpu.roll(a, 1, axis=0))
    prev2 = jnp.where(row == 0, halo[6:7, :],
                      jnp.where(row == 1, halo[7:8, :], pltpu.roll(a, 2, axis=0)))
    cw = cw_ref[...]
    conv = cb_ref[...] + cw[0:1, :] * prev2 + cw[1:2, :] * prev1 + cw[2:3, :] * a
    act = (conv * jax.nn.sigmoid(conv) * b).astype(BF16)
    o_ref[...] += _dot(act, wd_ref[...])

    @pl.when(j == pl.num_programs(1) - 1)
    def _():
        residual_copy().wait()
        h = h_buf[...] + gt_ref[...] * o_ref[...]
        o_ref[...] = _rms(h, gfin_ref[...])


def _ffn(hn, h, w_up, conv_w, conv_b, w_down, gt2, g_final, tm=1024, tn=512):
    s, d = h.shape
    d_ff = w_down.shape[0]
    nf = d_ff // tn
    halo_blocks = tm // FFN_HALO
    return pl.pallas_call(
        _ffn_kernel,
        grid=(s // tm, nf),
        in_specs=[pl.BlockSpec((tm, d), lambda i, j: (i, 0)),
                  pl.BlockSpec((FFN_HALO, d),
                               lambda i, j: (jnp.maximum(i * halo_blocks - 1, 0), 0)),
                  pl.BlockSpec(memory_space=pl.ANY),
                  pl.BlockSpec((d, tn), lambda i, j: (0, j)),
                  pl.BlockSpec((d, tn), lambda i, j: (0, nf + j)),
                  pl.BlockSpec((3, tn), lambda i, j: (0, j)),
                  pl.BlockSpec((1, tn), lambda i, j: (0, j)),
                  pl.BlockSpec((tn, d), lambda i, j: (j, 0)),
                  _resident((1, d)), _resident((1, d))],
        out_specs=pl.BlockSpec((tm, d), lambda i, j: (i, 0)),
        out_shape=jax.ShapeDtypeStruct((s, d), F32),
        scratch_shapes=[pltpu.VMEM((tm, d), F32), pltpu.SemaphoreType.DMA(())],
        compiler_params=_cparams(("arbitrary", "arbitrary"), 62),
        name="conv_ffn",
    )(hn, hn, h, w_up, w_up, conv_w, conv_b, w_down, gt2, g_final)


def _layer(h, mod, g_mix, w_in, b_f, a_re, a_im, log_dt, ssm_b_re, ssm_b_im, ssm_c_re, ssm_c_im,
           ssm_d, w_glu, b_glu, g_attn_out, g_ssm_out, w_out, g_ffn, w_up, conv_w, conv_b, w_down):
    s, d = h.shape
    aw = N_HEADS * HEAD_DIM
    sh1, sc1, gt1, sh2, sc2, gt2 = [mod[:, i * d:(i + 1) * d] for i in range(N_MOD)]
    row = lambda a: a.reshape(1, -1)

    q, k, v, u, f, k_norm_sq = _inproj(h, row(g_mix), sc1, sh1, w_in,
                                       w_in[:, 3 * aw + N_HEADS:])

    cum_row, cum_col = _forget_cumsum(f, b_f)
    attn, w_up_bf16 = _attention(q, k, v, cum_col, cum_row.reshape(N_HEADS, 1, s), k_norm_sq,
                                 w_up)

    y, w_down_bf16, w_out_bf16, w_glu_bf16 = _s5(
        u, a_re, a_im, log_dt, ssm_b_re, ssm_b_im, ssm_c_re, ssm_c_im, ssm_d,
        w_down, w_out, w_glu)

    h1, hn2 = _mixout(h, attn, y, w_glu_bf16, row(b_glu), row(g_attn_out),
                      row(g_ssm_out), w_out_bf16, gt1, row(g_ffn), sc2, sh2)
    return hn2, h1, (w_up_bf16, conv_w, row(conv_b), w_down_bf16, gt2)


def kernel(x, c, w_ada, b_ada, g_mix, w_in, b_f, a_re, a_im, log_dt, ssm_b_re, ssm_b_im, ssm_c_re,
           ssm_c_im, ssm_d, w_glu, b_glu, g_attn_out, g_ssm_out, w_out, g_ffn, w_up, conv_w,
           conv_b, w_down, g_final):
    batch, s, d = x.shape
    assert w_ada.shape[0] == 1, "only DEPTH == 1 is supported"
    l = 0
    outs = []
    for bi in range(batch):
        mod = _adaln(c[bi:bi + 1], w_ada[l], b_ada[l])
        hn2, h1, ffn_args = _layer(
            x[bi], mod, g_mix[l], w_in[l].astype(BF16), b_f[l], a_re[l], a_im[l], log_dt[l], ssm_b_re[l],
            ssm_b_im[l], ssm_c_re[l], ssm_c_im[l], ssm_d[l], w_glu[l], b_glu[l],
            g_attn_out[l], g_ssm_out[l], w_out[l], g_ffn[l], w_up[l], conv_w[l], conv_b[l],
            w_down[l])
        outs.append(_ffn(hn2, h1, *ffn_args, g_final.reshape(1, d)))
    return jnp.stack(outs, axis=0)
```
